```python
import jax
import jax.numpy as jnp
from jax import lax
import numpy as np

D_MODEL = 2048
BATCH = 2
SEQ = 4096
DEPTH = 2
DEC_BATCH = 8
DEC_SEQ = 1
PAST_LEN = 16384
PAGE_SIZE = 128

N_ATT_HEADS = 8
N_KV_HEADS = 2
HEAD_DIM = 128
ATT_WIDTH = N_ATT_HEADS * HEAD_DIM
KV_WIDTH = N_KV_HEADS * HEAD_DIM
N_IDX_HEADS = 16
IDX_DIM = 64
TOPK_MAX = 256
Q_BLOCK = 128
N_HGRN_HEADS = 8
HGRN_DK = 128
HGRN_DV = 128
HGRN_WIDTH = N_HGRN_HEADS * HGRN_DK
HGRN_V_WIDTH = N_HGRN_HEADS * HGRN_DV
HGRN_CHUNK = 64
N_EXPERTS = 16
N_GROUPS = 4
EXPERTS_PER_GROUP = N_EXPERTS // N_GROUPS
TOP_K = 2
EXPERT_DFF = 1024
ALPHA = (2 * DEPTH) ** 0.25
BETA = (8 * DEPTH) ** -0.25
LN_EPS = 1e-5
MASK_VALUE = -1e30
IN_SPLITS = (ATT_WIDTH, KV_WIDTH, KV_WIDTH, N_IDX_HEADS * IDX_DIM, IDX_DIM, N_IDX_HEADS,
             HGRN_WIDTH, HGRN_WIDTH, HGRN_V_WIDTH, HGRN_V_WIDTH, D_MODEL, D_MODEL)
IN_WIDTH = sum(IN_SPLITS)
SPLIT_POINTS = tuple(sum(IN_SPLITS[:i + 1]) for i in range(len(IN_SPLITS) - 1))

kernel_name = 'dsa_hgrn2_gated_moe_decoder_step'


def layer_norm(x):
    x32 = x.astype(jnp.float32)
    mu = jnp.mean(x32, -1, keepdims=True)
    var = jnp.mean(jnp.square(x32 - mu), -1, keepdims=True)
    return (x32 - mu) * lax.rsqrt(var + LN_EPS)


def modulate(x, shift, scale):
    return (layer_norm(x) * (1.0 + scale) + shift).astype(x.dtype)


def post_norm(x, g, b):
    return (layer_norm(x) * g + b).astype(x.dtype)


def indexer_topk(iq, iw, ik, qpos, n_sel):
    L = ik.shape[1]
    logits = jnp.einsum('bqhd,bkd->bqkh', iq, ik).astype(jnp.float32) * IDX_DIM ** -0.5
    score = jnp.einsum('bqkh,bqh->bqk', jax.nn.relu(logits), iw.astype(jnp.float32)) * N_IDX_HEADS ** -0.5
    causal = jnp.arange(L)[None, :] <= qpos[:, None]
    score = jnp.where(causal[None], score, MASK_VALUE)
    _, idx = lax.top_k(score, n_sel)
    valid = idx <= qpos[None, :, None]
    return idx, valid


def attend_selected(q, k_sel, v_sel, valid):
    B, Tq = q.shape[:2]
    qg = q.reshape(B, Tq, N_KV_HEADS, N_ATT_HEADS // N_KV_HEADS, HEAD_DIM)
    s = jnp.einsum('bqgrd,bqngd->bqgrn', qg, k_sel).astype(jnp.float32) * HEAD_DIM ** -0.5
    s = jnp.where(valid[:, :, None, None, :], s, MASK_VALUE)
    p = jax.nn.softmax(s, axis=-1).astype(v_sel.dtype)
    o = jnp.einsum('bqgrn,bqngd->bqgrd', p, v_sel)
    return o.reshape(B, Tq, ATT_WIDTH)


def dsa_prompt(q, k, v, iq, ik, iw):
    B, T = q.shape[:2]
    n_sel = min(TOPK_MAX, T // 4)

    def block(start):
        sl = lambda a: lax.dynamic_slice_in_dim(a, start, Q_BLOCK, axis=1)
        qpos = start + jnp.arange(Q_BLOCK)
        idx, valid = indexer_topk(sl(iq), sl(iw), ik, qpos, n_sel)
        gather = lambda a: jax.vmap(lambda aa, ii: aa[ii])(a, idx)
        return attend_selected(sl(q), gather(k), gather(v), valid)

    o = lax.map(block, jnp.arange(0, T, Q_BLOCK))
    return o.transpose(1, 0, 2, 3).reshape(B, T, ATT_WIDTH)


def dsa_decode(q, k_new, v_new, iq, ik_new, iw, k_pool, v_pool, ik_pool, page_table):
    DB, T = q.shape[:2]
    page = k_pool.shape[1]
    past = page_table.shape[1] * page
    n_sel = min(TOPK_MAX, (past + T) // 4)
    ik_past = ik_pool[page_table].reshape(DB, past, IDX_DIM)
    ik_all = jnp.concatenate([ik_past, ik_new], axis=1)
    qpos = past + jnp.arange(T)
    idx, valid = indexer_topk(iq, iw, ik_all, qpos, n_sel)
    is_past = idx < past
    pidx = jnp.minimum(idx, past - 1)
    phys = jax.vmap(lambda pt, ii: pt[ii // page])(page_table, pidx)
    off = pidx % page
    nidx = jnp.clip(idx - past, 0, T - 1)

    def pick(pool, new):
        from_past = pool[phys, off]
        from_new = jax.vmap(lambda a, ii: a[ii])(new, nidx)
        return jnp.where(is_past[..., None, None], from_past, from_new)

    return attend_selected(q, pick(k_pool, k_new), pick(v_pool, v_new), valid)


def hgrn2_chunked(q, log_f, k, i, s0):
    B, T, H, dk = q.shape
    dv = i.shape[-1]
    C = min(HGRN_CHUNK, T)
    n = -(-T // C)
    pad = n * C - T

    def to_chunks(a):
        a = jnp.pad(a, ((0, 0), (0, pad), (0, 0), (0, 0)))
        return a.reshape(B, n, C, H, a.shape[-1]).transpose(1, 0, 3, 2, 4)

    tri = jnp.tril(jnp.ones((C, C), bool))[None, None, :, :, None]

    def step(S, xs):
        qq, ff, kk, ii = xs
        b = jnp.cumsum(ff, axis=2)
        diff = b[:, :, :, None, :] - b[:, :, None, :, :]
        decay = jnp.where(tri, jnp.exp(jnp.where(tri, diff, 0.0)), 0.0)
        A = jnp.einsum('bhtd,bhtsd,bhsd->bhts', qq, decay, kk)
        o = jnp.einsum('bhts,bhsv->bhtv', A, ii) + jnp.einsum('bhtd,bhdv->bhtv', qq * jnp.exp(b), S)
        b_last = b[:, :, -1:, :]
        S_new = jnp.exp(b_last[:, :, 0, :, None]) * S + jnp.einsum('bhsd,bhsv->bhdv', kk * jnp.exp(b_last - b), ii)
        return S_new, o

    S, o = lax.scan(step, s0, (to_chunks(q), to_chunks(log_f), to_chunks(k), to_chunks(i)))
    o = o.transpose(1, 0, 3, 2, 4).reshape(B, n * C, H, dv)[:, :T]
    return o, S


def token_mixer(l, h, attn_fn, s0, lb, w_in, w_up_a, w_up_b, w_o, norm_g):
    B, T, _ = h.shape
    f32 = jnp.float32
    proj = jnp.einsum('btd,de->bte', h, w_in)
    q, k, v, iq, ik, iw, hq, hf, hi, hg, ga, gb = jnp.split(proj, SPLIT_POINTS, axis=-1)
    q = q.reshape(B, T, N_ATT_HEADS, HEAD_DIM)
    k = k.reshape(B, T, N_KV_HEADS, HEAD_DIM)
    v = v.reshape(B, T, N_KV_HEADS, HEAD_DIM)
    iq = iq.reshape(B, T, N_IDX_HEADS, IDX_DIM)
    o_a = attn_fn(l, q, k, v, iq, ik, iw)
    hf = hf.astype(f32)
    f_gate = lb + (1.0 - lb) * jax.nn.sigmoid(hf)
    log_f = jnp.log(f_gate)
    k_gate = (1.0 - lb) * jax.nn.sigmoid(-hf)
    heads = lambda a, d: a.reshape(B, T, N_HGRN_HEADS, d)
    o_b, s_new = hgrn2_chunked(heads(jax.nn.silu(hq.astype(f32)), HGRN_DK), heads(log_f, HGRN_DK),
                               heads(k_gate, HGRN_DK), heads(hi.astype(f32), HGRN_DV), s0.astype(f32))
    o_b = o_b * lax.rsqrt(jnp.mean(jnp.square(o_b), -1, keepdims=True) + LN_EPS)
    o_b = (o_b.reshape(B, T, HGRN_V_WIDTH) * norm_g * jax.nn.silu(hg.astype(f32))).astype(h.dtype)
    merged = jax.nn.sigmoid(ga) * (o_a @ w_up_a) + jax.nn.sigmoid(gb) * (o_b @ w_up_b)
    return merged @ w_o, k, v, ik, s_new.astype(h.dtype)


def moe(h, w_router, b_router, w1, w3, w2):
    f32 = jnp.float32
    aff = jax.nn.sigmoid(jnp.einsum('nd,de->ne', h, w_router).astype(f32))
    sel = (aff + b_router.astype(f32)).reshape(-1, N_GROUPS, EXPERTS_PER_GROUP)
    grp_score = jnp.sum(lax.top_k(sel, TOP_K)[0], axis=-1)
    g_idx = jnp.argmax(grp_score, axis=-1)
    in_grp = jnp.take_along_axis(sel, g_idx[:, None, None], axis=1)[:, 0]
    _, e_loc = lax.top_k(in_grp, TOP_K)
    e_idx = g_idx[:, None] * EXPERTS_PER_GROUP + e_loc
    w_sel = jnp.take_along_axis(aff, e_idx, axis=1)
    w_sel = w_sel / jnp.sum(w_sel, axis=-1, keepdims=True)
    gates = jnp.einsum('nk,nke->ne', w_sel, jax.nn.one_hot(e_idx, N_EXPERTS, dtype=f32))
    a = jnp.einsum('nd,edf->nef', h, w1)
    b = jnp.einsum('nd,edf->nef', h, w3)
    u = jax.nn.silu(a) * b * gates[:, :, None].astype(h.dtype)
    return jnp.einsum('nef,efd->nd', u, w2)


def trunk(x, c, attn_fn, s0, lower_bounds, w_ada, b_ada, w_in, w_up_a, w_up_b, w_o, hgrn_norm_g,
          ln1_g, ln1_b, w_router, b_router, w1, w3, w2, ln2_g, ln2_b):
    B, T, D = x.shape
    new_k, new_v, new_ik, new_s = [], [], [], []
    for l in range(DEPTH):
        mod = (jax.nn.silu(c) @ w_ada[l] + b_ada[l])[:, None, :]
        sh1, sc1, g1, sh2, sc2, g2 = jnp.split(mod, 6, axis=-1)
        h = modulate(x, sh1, sc1)
        y, k, v, ik, s = token_mixer(l, h, attn_fn, s0[l], lower_bounds[l], w_in[l], w_up_a[l],
                                     w_up_b[l], w_o[l], hgrn_norm_g[l])
        x = post_norm(ALPHA * x + g1 * y, ln1_g[l], ln1_b[l])
        h = modulate(x, sh2, sc2)
        y = moe(h.reshape(B * T, D), w_router, b_router, w1[l], w3[l], w2[l]).reshape(B, T, D)
        x = post_norm(ALPHA * x + g2 * y, ln2_g[l], ln2_b[l])
        new_k.append(k)
        new_v.append(v)
        new_ik.append(ik)
        new_s.append(s)
    return x, jnp.stack(new_k), jnp.stack(new_v), jnp.stack(new_ik), jnp.stack(new_s)


def setup_inputs(seed: int = 0) -> dict:
    key = jax.random.key(seed)
    ks = jax.random.split(key, 32)
    f32 = jnp.float32

    def nrm(k, shape, scale):
        return jax.random.normal(k, shape, f32) * scale

    n_pages = PAST_LEN // PAGE_SIZE
    n_used = DEC_BATCH * n_pages
    n_phys = n_used + n_used // 4
    page_table = jax.random.permutation(ks[8], n_phys)[:n_used].reshape(DEC_BATCH, n_pages).astype(jnp.int32)
    return {
        'x_prompt': nrm(ks[0], (BATCH, SEQ, D_MODEL), 1.0),
        'x_sample': nrm(ks[1], (DEC_BATCH, DEC_SEQ, D_MODEL), 1.0),
        'c_prompt': nrm(ks[2], (BATCH, D_MODEL), 1.0),
        'c_sample': nrm(ks[3], (DEC_BATCH, D_MODEL), 1.0),
        'cache_k': nrm(ks[4], (DEPTH, n_phys, PAGE_SIZE, N_KV_HEADS, HEAD_DIM), 1.0),
        'cache_v': nrm(ks[5], (DEPTH, n_phys, PAGE_SIZE, N_KV_HEADS, HEAD_DIM), 1.0),
        'cache_idx_k': nrm(ks[6], (DEPTH, n_phys, PAGE_SIZE, IDX_DIM), 1.0),
        'state_hgrn': nrm(ks[7], (DEPTH, DEC_BATCH, N_HGRN_HEADS, HGRN_DK, HGRN_DV), 0.5),
        'page_table': page_table,
        'w_ada': nrm(ks[9], (DEPTH, D_MODEL, 6 * D_MODEL), 0.5 * D_MODEL ** -0.5),
        'b_ada': nrm(ks[10], (DEPTH, 6 * D_MODEL), 0.02),
        'w_in': nrm(ks[11], (DEPTH, D_MODEL, IN_WIDTH), D_MODEL ** -0.5),
        'w_up_a': nrm(ks[12], (DEPTH, ATT_WIDTH, D_MODEL), ATT_WIDTH ** -0.5),
        'w_up_b': nrm(ks[13], (DEPTH, HGRN_V_WIDTH, D_MODEL), HGRN_V_WIDTH ** -0.5),
        'w_o': nrm(ks[14], (DEPTH, D_MODEL, D_MODEL), BETA * D_MODEL ** -0.5),
        'hgrn_norm_g': 1.0 + nrm(ks[15], (DEPTH, HGRN_V_WIDTH), 0.02),
        'hgrn_lb_logits': nrm(ks[16], (DEPTH, HGRN_WIDTH), 0.5),
        'ln1_g': 1.0 + nrm(ks[17], (DEPTH, D_MODEL), 0.02),
        'ln1_b': nrm(ks[18], (DEPTH, D_MODEL), 0.02),
        'w_router': nrm(ks[19], (D_MODEL, N_EXPERTS), D_MODEL ** -0.5),
        'b_router': nrm(ks[20], (N_EXPERTS,), 0.01),
        'w1': nrm(ks[21], (DEPTH, N_EXPERTS, D_MODEL, EXPERT_DFF), D_MODEL ** -0.5),
        'w3': nrm(ks[22], (DEPTH, N_EXPERTS, D_MODEL, EXPERT_DFF), D_MODEL ** -0.5),
        'w2': nrm(ks[23], (DEPTH, N_EXPERTS, EXPERT_DFF, D_MODEL), BETA * EXPERT_DFF ** -0.5),
        'ln2_g': 1.0 + nrm(ks[24], (DEPTH, D_MODEL), 0.02),
        'ln2_b': nrm(ks[25], (DEPTH, D_MODEL), 0.02),
    }


def reference(x_prompt, x_sample, c_prompt, c_sample, cache_k, cache_v, cache_idx_k, state_hgrn, page_table,
              w_ada, b_ada, w_in, w_up_a, w_up_b, w_o, hgrn_norm_g, hgrn_lb_logits, ln1_g, ln1_b,
              w_router, b_router, w1, w3, w2, ln2_g, ln2_b):
    lbp = jax.nn.softmax(hgrn_lb_logits.astype(jnp.float32), axis=0)
    lower_bounds = jnp.cumsum(lbp, axis=0) - lbp[0]

    def prompt_attn(l, q, k, v, iq, ik, iw):
        return dsa_prompt(q, k, v, iq, ik, iw)

    def sample_attn(l, q, k, v, iq, ik, iw):
        return dsa_decode(q, k, v, iq, ik, iw, cache_k[l], cache_v[l], cache_idx_k[l], page_table)

    s0_prompt = jnp.zeros((DEPTH, x_prompt.shape[0], N_HGRN_HEADS, HGRN_DK, HGRN_DV), x_prompt.dtype)
    y_prompt, k_p, v_p, ik_p, s_p = trunk(x_prompt, c_prompt, prompt_attn, s0_prompt, lower_bounds,
                                          w_ada, b_ada, w_in, w_up_a, w_up_b, w_o, hgrn_norm_g,
                                          ln1_g, ln1_b, w_router, b_router, w1, w3, w2, ln2_g, ln2_b)
    y_sample, k_s, v_s, ik_s, s_s = trunk(x_sample, c_sample, sample_attn, state_hgrn, lower_bounds,
                                          w_ada, b_ada, w_in, w_up_a, w_up_b, w_o, hgrn_norm_g,
                                          ln1_g, ln1_b, w_router, b_router, w1, w3, w2, ln2_g, ln2_b)
    return (y_prompt, y_sample, k_p, v_p, ik_p, s_p, k_s, v_s, ik_s, s_s)
```

```python
import functools

import jax
import jax.numpy as jnp
from jax import lax
from jax.experimental import pallas as pl
from jax.experimental.pallas import tpu as pltpu

F32 = jnp.float32
BF16 = jnp.bfloat16
I32 = jnp.int32

DEPTH = 2
D_MODEL = 2048
N_ATT_HEADS = 8
N_KV_HEADS = 2
KV_GROUP = N_ATT_HEADS // N_KV_HEADS
HEAD_DIM = 128
ATT_WIDTH = N_ATT_HEADS * HEAD_DIM
KV_WIDTH = N_KV_HEADS * HEAD_DIM
N_IDX_HEADS = 16
IDX_DIM = 64
TOPK_MAX = 256
N_HGRN_HEADS = 8
HGRN_DK = 128
HGRN_DV = 128
HGRN_WIDTH = N_HGRN_HEADS * HGRN_DK
HGRN_CHUNK = 64
HGRN_SUB = 16
N_EXPERTS = 16
N_GROUPS = 4
EXPERTS_PER_GROUP = N_EXPERTS // N_GROUPS
EXPERT_DFF = 1024
ALPHA = (2 * DEPTH) ** 0.25
LN_EPS = 1e-5
MASK_VALUE = -1e30
INT_MIN = -2 ** 31

IN_SPLITS = (ATT_WIDTH, KV_WIDTH, KV_WIDTH, N_IDX_HEADS * IDX_DIM, IDX_DIM, N_IDX_HEADS,
             HGRN_WIDTH, HGRN_WIDTH, HGRN_WIDTH, HGRN_WIDTH, D_MODEL, D_MODEL)

LANES = 128
OFF_GA = 0
OFF_GB = OFF_GA + D_MODEL
OFF_Q = OFF_GB + D_MODEL
OFF_IQ = OFF_Q + ATT_WIDTH
OFF_HQ = OFF_IQ + N_IDX_HEADS * IDX_DIM
OFF_HF = OFF_HQ + HGRN_WIDTH
OFF_HI = OFF_HF + HGRN_WIDTH
OFF_HG = OFF_HI + HGRN_WIDTH
OFF_K = OFF_HG + HGRN_WIDTH
OFF_V = OFF_K + KV_WIDTH
OFF_IKW = OFF_V + KV_WIDTH
PACK_WIDTH = OFF_IKW + LANES

VMEM_LIMIT = 56 * 1024 * 1024


def _params(semantics):
    return pltpu.CompilerParams(dimension_semantics=semantics, vmem_limit_bytes=VMEM_LIMIT)


def _tile(n, pref):
    t = min(n, pref)
    while n % t:
        t //= 2
    return t


def _ln(x):
    mu = jnp.mean(x, axis=-1, keepdims=True)
    xc = x - mu
    var = jnp.mean(xc * xc, axis=-1, keepdims=True)
    return xc * lax.rsqrt(var + LN_EPS)


def _dot(a, b):
    return jnp.dot(a, b, preferred_element_type=F32)


def _dot_nt(a, b):
    return lax.dot_general(a, b, (((1,), (1,)), ((), ())), preferred_element_type=F32)


def _split3(x):
    x1 = x.astype(BF16)
    r1 = x - x1.astype(F32)
    x2 = r1.astype(BF16)
    x3 = (r1 - x2.astype(F32)).astype(BF16)
    return x1, x2, x3


def _sort_key(s):
    bits = pltpu.bitcast(s, I32)
    return bits ^ ((bits >> 31) & 0x7FFFFFFF)


def _kth_largest_key(count_ge, n_sel, shape):
    def body(i, ans):
        bit = lax.shift_left(jnp.int32(1), jnp.int32(31) - i)
        cand = ans | bit
        cnt = count_ge(cand ^ INT_MIN)
        return jnp.where(cnt >= n_sel, cand, ans)
    ans = lax.fori_loop(0, 32, body, jnp.zeros(shape, I32))
    return ans ^ INT_MIN


def _ada_kernel(c_ref, w_ref, b_ref, o_ref):
    c = c_ref[...]
    a = (c * jax.nn.sigmoid(c)).astype(BF16)
    o_ref[...] = _dot(a, w_ref[...].astype(BF16)) + b_ref[...]


def _ada(c_all, w_ada, b_ada):
    rows = c_all.shape[0]
    width = w_ada.shape[-1]
    tn = _tile(width, 1024)
    return pl.pallas_call(
        _ada_kernel,
        out_shape=jax.ShapeDtypeStruct((DEPTH, rows, width), F32),
        grid=(DEPTH, width // tn),
        in_specs=[pl.BlockSpec((rows, D_MODEL), lambda l, j: (0, 0)),
                  pl.BlockSpec((None, D_MODEL, tn), lambda l, j: (l, 0, j)),
                  pl.BlockSpec((None, 1, tn), lambda l, j: (l, 0, j))],
        out_specs=pl.BlockSpec((None, rows, tn), lambda l, j: (l, 0, j)),
        compiler_params=_params(("parallel", "parallel")),
        name="ada",
    )(c_all, w_ada, b_ada.reshape(DEPTH, 1, width))


def _proj_kernel(x_ref, sh_ref, sc_ref, w_ref, o_ref, h_scr):
    @pl.when(pl.program_id(1) == 0)
    def _():
        h_scr[...] = (_ln(x_ref[...]) * (1.0 + sc_ref[0]) + sh_ref[0]).astype(BF16)
    o_ref[...] = _dot(h_scr[...], w_ref[...])


def _proj(x, shift, scale, w_pack, l, tm):
    n = x.shape[0]
    tiles_per_mod = (n // tm) // shift.shape[0]
    r = shift.shape[1]
    tn = 640
    mod_spec = pl.BlockSpec((1, r, D_MODEL), lambda i, j: (i // tiles_per_mod, 0, 0))
    return pl.pallas_call(
        _proj_kernel,
        out_shape=jax.ShapeDtypeStruct((n, PACK_WIDTH), F32),
        grid=(n // tm, PACK_WIDTH // tn),
        in_specs=[pl.BlockSpec((tm, D_MODEL), lambda i, j: (i, 0)),
                  mod_spec, mod_spec,
                  pl.BlockSpec((None, D_MODEL, tn), lambda i, j: (l, 0, j))],
        out_specs=pl.BlockSpec((tm, tn), lambda i, j: (i, j)),
        scratch_shapes=[pltpu.VMEM((tm, D_MODEL), BF16)],
        compiler_params=_params(("parallel", "arbitrary")),
        name="proj",
    )(x, shift, scale, w_pack)


def _attn_prompt_kernel(q_ref, iq_ref, iwq_ref, k_ref, v_ref, ikw_ref, o_ref,
                        kb_scr, vb_scr, ikb_scr, iqs_scr, qs_scr, key_scr, m_scr, l_scr, acc_scr,
                        *, tq, n_sel):
    qi = pl.program_id(1)
    tk = tq

    @pl.when(qi == 0)
    def _():
        kb_scr[...] = k_ref[...].astype(BF16)
        vb_scr[...] = v_ref[...].astype(BF16)
        ikb_scr[...] = ikw_ref[:, :IDX_DIM].astype(BF16)

    iq = iq_ref[...]
    for h in range(N_IDX_HEADS):
        iqs_scr[h * tq:(h + 1) * tq, :] = iq[:, h * IDX_DIM:(h + 1) * IDX_DIM].astype(BF16)
    q = q_ref[...]
    for h in range(N_ATT_HEADS):
        qs_scr[h] = q[:, h * HEAD_DIM:(h + 1) * HEAD_DIM].astype(BF16)
    w = iwq_ref[:, IDX_DIM:IDX_DIM + N_IDX_HEADS] * (IDX_DIM ** -0.5 * N_IDX_HEADS ** -0.5)
    row = qi * tq + lax.broadcasted_iota(I32, (tq, tk), 0)
    col0 = lax.broadcasted_iota(I32, (tq, tk), 1)

    def score_block(kb, carry):
        start = pl.multiple_of(kb * tk, tk)
        lg = _dot_nt(iqs_scr[...], ikb_scr[pl.ds(start, tk), :])
        s = jnp.zeros((tq, tk), F32)
        for h in range(N_IDX_HEADS):
            s = s + w[:, h:h + 1] * jnp.maximum(lg[h * tq:(h + 1) * tq, :], 0.0)
        s = jnp.where(kb * tk + col0 <= row, s, MASK_VALUE)
        key_scr[kb] = _sort_key(s)
        return carry

    lax.fori_loop(0, qi + 1, score_block, 0)

    def count_ge(cand):
        def body(kb, cnt):
            hit = jnp.where(key_scr[kb] >= cand, 1.0, 0.0)
            return cnt + jnp.sum(hit, axis=1, keepdims=True)
        return lax.fori_loop(0, qi + 1, body, jnp.zeros((tq, 1), F32))

    thr = _kth_largest_key(count_ge, float(n_sel), (tq, 1))

    m_scr[...] = jnp.full(m_scr.shape, MASK_VALUE, F32)
    l_scr[...] = jnp.zeros(l_scr.shape, F32)
    acc_scr[...] = jnp.zeros(acc_scr.shape, F32)

    def att_block(kb, carry):
        start = pl.multiple_of(kb * tk, tk)
        sel = (key_scr[kb] >= thr) & (kb * tk + col0 <= row)
        for g in range(N_KV_HEADS):
            kg = kb_scr[pl.ds(start, tk), g * HEAD_DIM:(g + 1) * HEAD_DIM]
            vg = vb_scr[pl.ds(start, tk), g * HEAD_DIM:(g + 1) * HEAD_DIM]
            for r in range(KV_GROUP):
                hh = g * KV_GROUP + r
                s = _dot_nt(qs_scr[hh], kg) * HEAD_DIM ** -0.5
                s = jnp.where(sel, s, MASK_VALUE)
                m_old = m_scr[hh]
                m_new = jnp.maximum(m_old, jnp.max(s, axis=1, keepdims=True))
                p = jnp.where(sel, jnp.exp(s - m_new), 0.0)
                a = jnp.exp(m_old - m_new)
                l_scr[hh] = a * l_scr[hh] + jnp.sum(p, axis=1, keepdims=True)
                acc_scr[hh] = a * acc_scr[hh] + _dot(p.astype(BF16), vg)
                m_scr[hh] = m_new
        return carry

    lax.fori_loop(0, qi + 1, att_block, 0)
    for h in range(N_ATT_HEADS):
        o_ref[:, h * HEAD_DIM:(h + 1) * HEAD_DIM] = (acc_scr[h] / l_scr[h]).astype(o_ref.dtype)


def _attn_prompt(proj, b, t):
    tq = _tile(t, 256)
    n_sel = min(TOPK_MAX, t // 4)
    proj3 = proj.reshape(b, t, PACK_WIDTH)
    kern = functools.partial(_attn_prompt_kernel, tq=tq, n_sel=n_sel)
    out = pl.pallas_call(
        kern,
        out_shape=jax.ShapeDtypeStruct((b, t, ATT_WIDTH), BF16),
        grid=(b, t // tq),
        in_specs=[pl.BlockSpec((None, tq, ATT_WIDTH), lambda bi, qi: (bi, qi, OFF_Q // ATT_WIDTH)),
                  pl.BlockSpec((None, tq, ATT_WIDTH), lambda bi, qi: (bi, qi, OFF_IQ // ATT_WIDTH)),
                  pl.BlockSpec((None, tq, LANES), lambda bi, qi: (bi, qi, OFF_IKW // LANES)),
                  pl.BlockSpec((None, t, KV_WIDTH), lambda bi, qi: (bi, 0, OFF_K // KV_WIDTH)),
                  pl.BlockSpec((None, t, KV_WIDTH), lambda bi, qi: (bi, 0, OFF_V // KV_WIDTH)),
                  pl.BlockSpec((None, t, LANES), lambda bi, qi: (bi, 0, OFF_IKW // LANES))],
        out_specs=pl.BlockSpec((None, tq, ATT_WIDTH), lambda bi, qi: (bi, qi, 0)),
        scratch_shapes=[pltpu.VMEM((t, KV_WIDTH), BF16),
                        pltpu.VMEM((t, KV_WIDTH), BF16),
                        pltpu.VMEM((t, IDX_DIM), BF16),
                        pltpu.VMEM((N_IDX_HEADS * tq, IDX_DIM), BF16),
                        pltpu.VMEM((N_ATT_HEADS, tq, HEAD_DIM), BF16),
                        pltpu.VMEM((t // tq, tq, tq), I32),
                        pltpu.VMEM((N_ATT_HEADS, tq, 1), F32),
                        pltpu.VMEM((N_ATT_HEADS, tq, 1), F32),
                        pltpu.VMEM((N_ATT_HEADS, tq, HEAD_DIM), F32)],
        compiler_params=_params(("parallel", "arbitrary")),
        name="attn_prompt",
    )(proj3, proj3, proj3, proj3, proj3, proj3)
    return out.reshape(b * t, ATT_WIDTH)


def _hgrn_gates(hq, hf, lb):
    q = hq * jax.nn.sigmoid(hq)
    f = lb + (1.0 - lb) * jax.nn.sigmoid(hf)
    kk = (1.0 - lb) * jax.nn.sigmoid(-hf)
    return q, f, kk


def _hgrn_finish(o, hg, ng):
    o = o * lax.rsqrt(jnp.mean(o * o, axis=-1, keepdims=True) + LN_EPS)
    return o * ng * (hg * jax.nn.sigmoid(hg))


def _hgrn_prompt_kernel(hq_ref, hf_ref, hi_ref, hg_ref, lb_ref, ng_ref, o_ref, s_ref, st_scr, oc_scr,
                        *, n_chunks):
    c_len = HGRN_CHUNK
    sub = HGRN_SUB
    ti = pl.program_id(2)

    @pl.when(ti == 0)
    def _():
        st_scr[...] = jnp.zeros(st_scr.shape, F32)

    lb = lb_ref[...]
    ng = ng_ref[...]
    r_i = lax.broadcasted_iota(I32, (c_len, c_len), 0)
    c_i = lax.broadcasted_iota(I32, (c_len, c_len), 1)
    tri = jnp.where(c_i <= r_i, 1.0, 0.0).astype(BF16)
    ones = jnp.ones((HGRN_DK, HGRN_DV), BF16)

    def chunk(c, carry):
        sl = pl.ds(pl.multiple_of(c * c_len, c_len), c_len)
        q, f, kk = _hgrn_gates(hq_ref[sl, :], hf_ref[sl, :], lb)
        v = hi_ref[sl, :]
        l1, l2, l3 = _split3(jnp.log(f))
        b = _dot(tri, l1) + _dot(tri, l2) + _dot(tri, l3)
        st = st_scr[...]
        oc_scr[...] = _dot_nt((q * jnp.exp(b)).astype(BF16), st.astype(BF16))
        for j in range(c_len // sub):
            lo = j * sub
            nt = c_len - lo
            qt, bt = q[lo:], b[lo:]
            t_idx = lax.broadcasted_iota(I32, (nt, 1), 0)
            ps = []
            for s in range(sub):
                valid = t_idx >= s
                dec = jnp.exp(jnp.where(valid, bt - b[lo + s:lo + s + 1, :], 0.0))
                ps.append(jnp.where(valid, qt * kk[lo + s:lo + s + 1, :] * dec, 0.0).astype(BF16))
            rs = _dot(jnp.concatenate(ps, axis=0), ones)
            acc = jnp.zeros((nt, HGRN_DV), F32)
            for s in range(sub):
                acc = acc + rs[s * nt:(s + 1) * nt, :] * v[lo + s:lo + s + 1, :]
            oc_scr[lo:, :] += acc
        b_last = b[c_len - 1:c_len, :]
        kd = (kk * jnp.exp(b_last - b)).astype(BF16)
        upd = lax.dot_general(v.astype(BF16), kd, (((0,), (0,)), ((), ())), preferred_element_type=F32)
        st_scr[...] = st * jnp.exp(b_last) + upd
        o_ref[sl, :] = _hgrn_finish(oc_scr[...], hg_ref[sl, :], ng).astype(o_ref.dtype)
        return carry

    lax.fori_loop(0, n_chunks, chunk, 0)

    @pl.when(ti == pl.num_programs(2) - 1)
    def _():
        s_ref[...] = st_scr[...].T


def _hgrn_prompt(proj, lb, ng, b, t):
    tc = _tile(t, 512)
    n_chunks = tc // HGRN_CHUNK
    proj3 = proj.reshape(b, t, PACK_WIDTH)

    def col(off):
        return pl.BlockSpec((None, tc, HGRN_DK), lambda bi, h, ti: (bi, ti, off // HGRN_DK + h))

    vec = pl.BlockSpec((None, 1, HGRN_DK), lambda bi, h, ti: (h, 0, 0))
    o, s = pl.pallas_call(
        functools.partial(_hgrn_prompt_kernel, n_chunks=n_chunks),
        out_shape=(jax.ShapeDtypeStruct((b, t, HGRN_WIDTH), BF16),
                   jax.ShapeDtypeStruct((b, N_HGRN_HEADS, HGRN_DK, HGRN_DV), F32)),
        grid=(b, N_HGRN_HEADS, t // tc),
        in_specs=[col(OFF_HQ), col(OFF_HF), col(OFF_HI), col(OFF_HG), vec, vec],
        out_specs=(pl.BlockSpec((None, tc, HGRN_DV), lambda bi, h, ti: (bi, ti, h)),
                   pl.BlockSpec((None, None, HGRN_DK, HGRN_DV), lambda bi, h, ti: (bi, h, 0, 0))),
        scratch_shapes=[pltpu.VMEM((HGRN_DV, HGRN_DK), F32),
                        pltpu.VMEM((HGRN_CHUNK, HGRN_DV), F32)],
        compiler_params=_params(("parallel", "parallel", "arbitrary")),
        name="hgrn_prompt",
    )(proj3, proj3, proj3, proj3, lb.reshape(N_HGRN_HEADS, 1, HGRN_DK), ng.reshape(N_HGRN_HEADS, 1, HGRN_DV))
    return o.reshape(b * t, HGRN_WIDTH), s


def _merge_kernel(oa_ref, ob_ref, ga_ref, gb_ref, wa_ref, wb_ref, o_ref):
    ya = _dot(oa_ref[...].astype(BF16), wa_ref[...])
    yb = _dot(ob_ref[...].astype(BF16), wb_ref[...])
    o_ref[...] = (jax.nn.sigmoid(ga_ref[...]) * ya + jax.nn.sigmoid(gb_ref[...]) * yb).astype(o_ref.dtype)


def _merge(o_a, o_b, proj, w_up_a, w_up_b, l, tm):
    n = o_a.shape[0]
    return pl.pallas_call(
        _merge_kernel,
        out_shape=jax.ShapeDtypeStruct((n, D_MODEL), BF16),
        grid=(n // tm,),
        in_specs=[pl.BlockSpec((tm, ATT_WIDTH), lambda i: (i, 0)),
                  pl.BlockSpec((tm, HGRN_WIDTH), lambda i: (i, 0)),
                  pl.BlockSpec((tm, D_MODEL), lambda i: (i, OFF_GA // D_MODEL)),
                  pl.BlockSpec((tm, D_MODEL), lambda i: (i, OFF_GB // D_MODEL)),
                  pl.BlockSpec((None, ATT_WIDTH, D_MODEL), lambda i: (l, 0, 0)),
                  pl.BlockSpec((None, HGRN_WIDTH, D_MODEL), lambda i: (l, 0, 0))],
        out_specs=pl.BlockSpec((tm, D_MODEL), lambda i: (i, 0)),
        compiler_params=_params(("parallel",)),
        name="merge",
    )(o_a, o_b, proj, proj, w_up_a, w_up_b)


def _post_norm(x, gate, y, g, b):
    return _ln(ALPHA * x + gate * y) * g + b


def _out_kernel(m_ref, w_ref, x_ref, gate_ref, lng_ref, lnb_ref, o_ref):
    y = _dot(m_ref[...], w_ref[...])
    o_ref[...] = _post_norm(x_ref[...], gate_ref[0], y, lng_ref[...], lnb_ref[...])


def _out_proj(merged, w_o, x, gate, ln_g, ln_b, l, tm):
    n = x.shape[0]
    tiles_per_mod = (n // tm) // gate.shape[0]
    vec = pl.BlockSpec((None, 1, D_MODEL), lambda i: (l, 0, 0))
    return pl.pallas_call(
        _out_kernel,
        out_shape=jax.ShapeDtypeStruct((n, D_MODEL), F32),
        grid=(n // tm,),
        in_specs=[pl.BlockSpec((tm, D_MODEL), lambda i: (i, 0)),
                  pl.BlockSpec((None, D_MODEL, D_MODEL), lambda i: (l, 0, 0)),
                  pl.BlockSpec((tm, D_MODEL), lambda i: (i, 0)),
                  pl.BlockSpec((1, gate.shape[1], D_MODEL), lambda i: (i // tiles_per_mod, 0, 0)),
                  vec, vec],
        out_specs=pl.BlockSpec((tm, D_MODEL), lambda i: (i, 0)),
        compiler_params=_params(("parallel",)),
        name="out_proj",
    )(merged, w_o, x, gate, ln_g.reshape(DEPTH, 1, D_MODEL), ln_b.reshape(DEPTH, 1, D_MODEL))


def _router_kernel(x_ref, sh_ref, sc_ref, wr_ref, br_ref, h_ref, g_ref):
    h = _ln(x_ref[...]) * (1.0 + sc_ref[0]) + sh_ref[0]
    h_ref[...] = h.astype(BF16)
    h1, h2, _ = _split3(h)
    w1, w2, _ = _split3(wr_ref[...])
    logits = _dot_nt(w1, h1) + _dot_nt(w1, h2) + _dot_nt(w2, h1)
    aff = jax.nn.sigmoid(logits)
    sel = aff + br_ref[...]
    rows = [sel[e:e + 1, :] for e in range(N_EXPERTS)]
    grp = []
    for g in range(N_GROUPS):
        a, b, c, d = rows[g * EXPERTS_PER_GROUP:(g + 1) * EXPERTS_PER_GROUP]
        hi1, lo1 = jnp.maximum(a, b), jnp.minimum(a, b)
        hi2, lo2 = jnp.maximum(c, d), jnp.minimum(c, d)
        grp.append(jnp.maximum(hi1, hi2) + jnp.maximum(jnp.minimum(hi1, hi2), jnp.maximum(lo1, lo2)))
    best = functools.reduce(jnp.maximum, grp)
    taken = jnp.zeros_like(best)
    picked = []
    for g in range(N_GROUPS):
        is_g = jnp.where(grp[g] == best, 1.0, 0.0) * (1.0 - taken)
        taken = taken + is_g
        for e in range(g * EXPERTS_PER_GROUP, (g + 1) * EXPERTS_PER_GROUP):
            rank = jnp.zeros_like(best)
            for o in range(g * EXPERTS_PER_GROUP, (g + 1) * EXPERTS_PER_GROUP):
                if o < e:
                    rank = rank + jnp.where(rows[o] >= rows[e], 1.0, 0.0)
                elif o > e:
                    rank = rank + jnp.where(rows[o] > rows[e], 1.0, 0.0)
            picked.append(is_g * jnp.where(rank < 2.0, 1.0, 0.0))
    gate = jnp.concatenate(picked, axis=0) * aff
    g_ref[...] = gate / jnp.sum(gate, axis=0, keepdims=True)


def _router(x, shift, scale, w_router_t, b_router, tm):
    n = x.shape[0]
    tiles_per_mod = (n // tm) // shift.shape[0]
    mod_spec = pl.BlockSpec((1, shift.shape[1], D_MODEL), lambda i: (i // tiles_per_mod, 0, 0))
    return pl.pallas_call(
        _router_kernel,
        out_shape=(jax.ShapeDtypeStruct((n, D_MODEL), BF16),
                   jax.ShapeDtypeStruct((N_EXPERTS, n), F32)),
        grid=(n // tm,),
        in_specs=[pl.BlockSpec((tm, D_MODEL), lambda i: (i, 0)),
                  mod_spec, mod_spec,
                  pl.BlockSpec((N_EXPERTS, D_MODEL), lambda i: (0, 0)),
                  pl.BlockSpec((N_EXPERTS, 1), lambda i: (0, 0))],
        out_specs=(pl.BlockSpec((tm, D_MODEL), lambda i: (i, 0)),
                   pl.BlockSpec((N_EXPERTS, tm), lambda i: (0, i))),
        compiler_params=_params(("parallel",)),
        name="router",
    )(x, shift, scale, w_router_t, b_router.reshape(N_EXPERTS, 1))


def _moe_kernel(h_ref, g_ref, w1_ref, w3_ref, w2_ref, x_ref, gate_ref, lng_ref, lnb_ref, o_ref, acc_scr):
    e = pl.program_id(1)

    @pl.when(e == 0)
    def _():
        acc_scr[...] = jnp.zeros(acc_scr.shape, F32)

    h = h_ref[...]
    a = _dot(h, w1_ref[...])
    b = _dot(h, w3_ref[...])
    gates = g_ref[...]
    lane = lax.broadcasted_iota(I32, gates.shape, 1)
    gcol = jnp.sum(jnp.where(lane == e, gates, 0.0), axis=1, keepdims=True)
    u = (a * jax.nn.sigmoid(a) * b * gcol).astype(BF16)
    acc_scr[...] += _dot(u, w2_ref[...])

    @pl.when(e == pl.num_programs(1) - 1)
    def _():
        o_ref[...] = _post_norm(x_ref[...], gate_ref[0], acc_scr[...], lng_ref[...], lnb_ref[...])


def _moe(h, gates, w1, w3, w2, x, gate, ln_g, ln_b, l, tm):
    n = x.shape[0]
    tiles_per_mod = (n // tm) // gate.shape[0]
    vec = pl.BlockSpec((None, 1, D_MODEL), lambda i, e: (l, 0, 0))
    return pl.pallas_call(
        _moe_kernel,
        out_shape=jax.ShapeDtypeStruct((n, D_MODEL), F32),
        grid=(n // tm, N_EXPERTS),
        in_specs=[pl.BlockSpec((tm, D_MODEL), lambda i, e: (i, 0)),
                  pl.BlockSpec((tm, N_EXPERTS), lambda i, e: (i, 0)),
                  pl.BlockSpec((None, None, D_MODEL, EXPERT_DFF), lambda i, e: (l, e, 0, 0)),
                  pl.BlockSpec((None, None, D_MODEL, EXPERT_DFF), lambda i, e: (l, e, 0, 0)),
                  pl.BlockSpec((None, None, EXPERT_DFF, D_MODEL), lambda i, e: (l, e, 0, 0)),
                  pl.BlockSpec((tm, D_MODEL), lambda i, e: (i, 0)),
                  pl.BlockSpec((1, gate.shape[1], D_MODEL), lambda i, e: (i // tiles_per_mod, 0, 0)),
                  vec, vec],
        out_specs=pl.BlockSpec((tm, D_MODEL), lambda i, e: (i, 0)),
        scratch_shapes=[pltpu.VMEM((tm, D_MODEL), F32)],
        compiler_params=_params(("parallel", "arbitrary")),
        name="moe",
    )(h, gates, w1, w3, w2, x, gate, ln_g.reshape(DEPTH, 1, D_MODEL), ln_b.reshape(DEPTH, 1, D_MODEL))


def _row_to_col(v):
    n = v.shape[1]
    r = lax.broadcasted_iota(I32, (n, n), 0)
    c = lax.broadcasted_iota(I32, (n, n), 1)
    return jnp.sum(jnp.where(r == c, jnp.broadcast_to(v, (n, n)), 0.0), axis=1, keepdims=True)


def _attn_decode_kernel(pt_ref, proj_ref, ck_ref, cv_ref, cik_ref, o_ref,
                        ik_buf, k_buf, v_buf, key_scr, sem_ik, sem_k, sem_v,
                        *, l, n_pages, page, chunk_pages, n_sel):
    b = pl.program_id(0)
    n_chunks = n_pages // chunk_pages
    chunk = chunk_pages * page

    def ik_copy(j):
        return pltpu.make_async_copy(cik_ref.at[l, pt_ref[b, j]], ik_buf.at[pl.ds(j * page, page)], sem_ik)

    def kv_copies(c, j, slot):
        pg = pt_ref[b, c * chunk_pages + j]
        dst = pl.ds(j * page, page)
        return (pltpu.make_async_copy(ck_ref.at[l, pg], k_buf.at[slot, dst], sem_k.at[slot]),
                pltpu.make_async_copy(cv_ref.at[l, pg], v_buf.at[slot, dst], sem_v.at[slot]))

    def start_chunk(c, slot):
        def body(j, carry):
            ck, cv = kv_copies(c, j, slot)
            ck.start()
            cv.start()
            return carry
        lax.fori_loop(0, chunk_pages, body, 0)

    def wait_chunk(c, slot):
        def body(j, carry):
            ck, cv = kv_copies(c, j, slot)
            ck.wait()
            cv.wait()
            return carry
        lax.fori_loop(0, chunk_pages, body, 0)

    def start_ik(j, carry):
        ik_copy(j).start()
        return carry

    def wait_ik(j, carry):
        ik_copy(j).wait()
        return carry

    lax.fori_loop(0, n_pages, start_ik, 0)
    start_chunk(0, 0)
    lax.fori_loop(0, n_pages, wait_ik, 0)

    rowsl = slice(None)
    iq = proj_ref[rowsl, OFF_IQ:OFF_IQ + N_IDX_HEADS * IDX_DIM]
    iq_h = jnp.concatenate([iq[:, h * IDX_DIM:(h + 1) * IDX_DIM] for h in range(N_IDX_HEADS)], axis=0)
    w_row = proj_ref[rowsl, OFF_IKW + IDX_DIM:OFF_IKW + IDX_DIM + N_IDX_HEADS]
    w_col = _row_to_col(w_row) * (IDX_DIM ** -0.5 * N_IDX_HEADS ** -0.5)
    ik_new = proj_ref[rowsl, OFF_IKW:OFF_IKW + IDX_DIM]

    iq_b = iq_h.astype(BF16)
    for c in range(n_chunks):
        lg = _dot_nt(iq_b, ik_buf[c * chunk:(c + 1) * chunk, :].astype(BF16))
        s = jnp.sum(w_col * jnp.maximum(lg, 0.0), axis=0, keepdims=True)
        key_scr[c] = _sort_key(s)
    lg_new = jnp.sum(iq_b.astype(F32) * ik_new.astype(BF16).astype(F32), axis=1, keepdims=True)
    key_new = _sort_key(jnp.sum(w_col * jnp.maximum(lg_new, 0.0), axis=0, keepdims=True))

    def count_ge(cand):
        cnt = jnp.where(key_new >= cand, 1.0, 0.0)
        for c in range(n_chunks):
            cnt = cnt + jnp.sum(jnp.where(key_scr[c] >= cand, 1.0, 0.0), axis=1, keepdims=True)
        return cnt

    thr = _kth_largest_key(count_ge, float(n_sel), (1, 1))

    q = proj_ref[rowsl, OFF_Q:OFF_Q + ATT_WIDTH]
    q_h = jnp.concatenate([q[:, h * HEAD_DIM:(h + 1) * HEAD_DIM] for h in range(N_ATT_HEADS)], axis=0)
    q_b = q_h.astype(BF16)
    k_new = proj_ref[rowsl, OFF_K:OFF_K + KV_WIDTH]
    v_new = proj_ref[rowsl, OFF_V:OFF_V + KV_WIDTH]
    scale = HEAD_DIM ** -0.5

    def att_chunk(c, carry):
        slot = c % 2

        @pl.when(c + 1 < n_chunks)
        def _():
            start_chunk(c + 1, 1 - slot)

        wait_chunk(c, slot)
        sel = key_scr[c] >= thr
        out = []
        for g in range(N_KV_HEADS):
            m_old, l_old, acc_old = carry[g]
            kg = k_buf[slot, :, g * HEAD_DIM:(g + 1) * HEAD_DIM].astype(BF16)
            vg = v_buf[slot, :, g * HEAD_DIM:(g + 1) * HEAD_DIM].astype(BF16)
            s = _dot_nt(q_b[g * KV_GROUP:(g + 1) * KV_GROUP], kg) * scale
            s = jnp.where(sel, s, MASK_VALUE)
            m_new = jnp.maximum(m_old, jnp.max(s, axis=1, keepdims=True))
            p = jnp.where(sel, jnp.exp(s - m_new), 0.0)
            a = jnp.exp(m_old - m_new)
            out.append((m_new, a * l_old + jnp.sum(p, axis=1, keepdims=True),
                        a * acc_old + _dot(p.astype(BF16), vg)))
        return tuple(out)

    init = tuple((jnp.full((KV_GROUP, 1), MASK_VALUE, F32), jnp.zeros((KV_GROUP, 1), F32),
                  jnp.zeros((KV_GROUP, HEAD_DIM), F32)) for _ in range(N_KV_HEADS))
    res = lax.fori_loop(0, n_chunks, att_chunk, init)

    sel_new = key_new >= thr
    for g in range(N_KV_HEADS):
        m_old, l_old, acc_old = res[g]
        kg = k_new[:, g * HEAD_DIM:(g + 1) * HEAD_DIM].astype(BF16).astype(F32)
        vg = v_new[:, g * HEAD_DIM:(g + 1) * HEAD_DIM].astype(BF16).astype(F32)
        qg = q_b[g * KV_GROUP:(g + 1) * KV_GROUP].astype(F32)
        s = jnp.sum(qg * kg, axis=1, keepdims=True) * scale
        s = jnp.where(sel_new, s, MASK_VALUE)
        m_new = jnp.maximum(m_old, s)
        p = jnp.where(sel_new, jnp.exp(s - m_new), 0.0)
        a = jnp.exp(m_old - m_new)
        l_new = a * l_old + p
        acc = a * acc_old + p.astype(BF16).astype(F32) * vg
        o = acc / l_new
        for r in range(KV_GROUP):
            hh = g * KV_GROUP + r
            o_ref[rowsl, hh * HEAD_DIM:(hh + 1) * HEAD_DIM] = o[r:r + 1, :].astype(o_ref.dtype)


def _attn_decode(proj, page_table, cache_k, cache_v, cache_idx_k, l):
    db, n_pages = page_table.shape
    n_phys, page = cache_k.shape[1], cache_k.shape[2]
    past = n_pages * page
    n_sel = min(TOPK_MAX, (past + 1) // 4)
    chunk_pages = _tile(n_pages, 16)
    ck = cache_k.reshape(DEPTH, n_phys, page, KV_WIDTH)
    cv = cache_v.reshape(DEPTH, n_phys, page, KV_WIDTH)
    kern = functools.partial(_attn_decode_kernel, l=l, n_pages=n_pages, page=page,
                             chunk_pages=chunk_pages, n_sel=n_sel)
    grid_spec = pltpu.PrefetchScalarGridSpec(
        num_scalar_prefetch=1,
        grid=(db,),
        in_specs=[pl.BlockSpec((None, 1, PACK_WIDTH), lambda b, pt: (b, 0, 0)),
                  pl.BlockSpec(memory_space=pl.ANY),
                  pl.BlockSpec(memory_space=pl.ANY),
                  pl.BlockSpec(memory_space=pl.ANY)],
        out_specs=pl.BlockSpec((None, 1, ATT_WIDTH), lambda b, pt: (b, 0, 0)),
        scratch_shapes=[pltpu.VMEM((past, IDX_DIM), F32),
                        pltpu.VMEM((2, chunk_pages * page, KV_WIDTH), F32),
                        pltpu.VMEM((2, chunk_pages * page, KV_WIDTH), F32),
                        pltpu.VMEM((n_pages // chunk_pages, 1, chunk_pages * page), I32),
                        pltpu.SemaphoreType.DMA,
                        pltpu.SemaphoreType.DMA((2,)),
                        pltpu.SemaphoreType.DMA((2,))])
    return pl.pallas_call(
        kern,
        out_shape=jax.ShapeDtypeStruct((db, 1, ATT_WIDTH), F32),
        grid_spec=grid_spec,
        compiler_params=_params(("arbitrary",)),
        name="attn_decode",
    )(page_table, proj.reshape(db, 1, PACK_WIDTH), ck, cv, cache_idx_k).reshape(db, ATT_WIDTH)


def _hgrn_decode_kernel(proj_ref, s0_ref, lb_ref, ng_ref, o_ref, s_ref):
    b = pl.program_id(0)
    rowsl = slice(None)
    for h in range(N_HGRN_HEADS):
        cols = lambda off: slice(off + h * HGRN_DK, off + (h + 1) * HGRN_DK)
        lb = lb_ref[:, h * HGRN_DK:(h + 1) * HGRN_DK]
        q, f, kk = _hgrn_gates(proj_ref[rowsl, cols(OFF_HQ)], proj_ref[rowsl, cols(OFF_HF)], lb)
        v = proj_ref[rowsl, cols(OFF_HI)]
        s_new = _row_to_col(f) * s0_ref[h] + _row_to_col(kk) * v
        s_ref[h] = s_new
        o = jnp.sum(_row_to_col(q) * s_new, axis=0, keepdims=True)
        ng = ng_ref[:, h * HGRN_DV:(h + 1) * HGRN_DV]
        o_ref[rowsl, h * HGRN_DV:(h + 1) * HGRN_DV] = _hgrn_finish(
            o, proj_ref[rowsl, cols(OFF_HG)], ng).astype(o_ref.dtype)


def _hgrn_decode(proj, state, lb, ng, l):
    db = proj.shape[0]
    st_spec_in = pl.BlockSpec((None, None, N_HGRN_HEADS, HGRN_DK, HGRN_DV), lambda b: (l, b, 0, 0, 0))
    o, s = pl.pallas_call(
        _hgrn_decode_kernel,
        out_shape=(jax.ShapeDtypeStruct((db, 1, HGRN_WIDTH), F32),
                   jax.ShapeDtypeStruct((db, N_HGRN_HEADS, HGRN_DK, HGRN_DV), F32)),
        grid=(db,),
        in_specs=[pl.BlockSpec((None, 1, PACK_WIDTH), lambda b: (b, 0, 0)),
                  st_spec_in,
                  pl.BlockSpec((1, HGRN_WIDTH), lambda b: (0, 0)),
                  pl.BlockSpec((1, HGRN_WIDTH), lambda b: (0, 0))],
        out_specs=(pl.BlockSpec((None, 1, HGRN_WIDTH), lambda b: (b, 0, 0)),
                   pl.BlockSpec((None, N_HGRN_HEADS, HGRN_DK, HGRN_DV), lambda b: (b, 0, 0, 0))),
        compiler_params=_params(("parallel",)),
        name="hgrn_decode",
    )(proj.reshape(db, 1, PACK_WIDTH), state, lb.reshape(1, HGRN_WIDTH), ng.reshape(1, HGRN_WIDTH))
    return o.reshape(db, HGRN_WIDTH), s


def _pack_w_in(w_in):
    offs = [0]
    for s in IN_SPLITS:
        offs.append(offs[-1] + s)
    q, k, v, iq, ik, iw, hq, hf, hi, hg, ga, gb = [w_in[:, :, offs[i]:offs[i + 1]] for i in range(len(IN_SPLITS))]
    pad = jnp.zeros(w_in.shape[:2] + (LANES - IDX_DIM - N_IDX_HEADS,), w_in.dtype)
    return jnp.concatenate([ga, gb, q, iq, hq, hf, hi, hg, k, v, ik, iw, pad], axis=-1).astype(BF16)


def _mods(mod_l, rows, per_row):
    m = mod_l[rows]
    parts = jnp.split(m, 6, axis=-1)
    if per_row:
        return [p[None, :, :] for p in parts]
    return [p[:, None, :] for p in parts]


def kernel(x_prompt, x_sample, c_prompt, c_sample, cache_k, cache_v, cache_idx_k, state_hgrn, page_table,
           w_ada, b_ada, w_in, w_up_a, w_up_b, w_o, hgrn_norm_g, hgrn_lb_logits, ln1_g, ln1_b,
           w_router, b_router, w1, w3, w2, ln2_g, ln2_b):
    bp, t, _ = x_prompt.shape
    db = x_sample.shape[0]
    lbp = jax.nn.softmax(hgrn_lb_logits.astype(F32), axis=0)
    lower_bounds = jnp.cumsum(lbp, axis=0) - lbp[0]

    w_pack = _pack_w_in(w_in)
    w_up_a_b, w_up_b_b, w_o_b = w_up_a.astype(BF16), w_up_b.astype(BF16), w_o.astype(BF16)
    w1_b, w3_b, w2_b = w1.astype(BF16), w3.astype(BF16), w2.astype(BF16)
    w_router_t = w_router.T

    n_c = bp + db
    c_rows = -(-n_c // 8) * 8
    c_all = jnp.concatenate([c_prompt, c_sample, jnp.zeros((c_rows - n_c, D_MODEL), F32)], axis=0)
    mod = _ada(c_all, w_ada, b_ada)

    xp = x_prompt.reshape(bp * t, D_MODEL)
    xs = x_sample.reshape(db, D_MODEL)
    tm_p = _tile(bp * t, 512)
    tm_moe = _tile(bp * t, 256)
    outs_p = {"k": [], "v": [], "ik": [], "s": []}
    outs_s = {"k": [], "v": [], "ik": [], "s": []}
    for l in range(DEPTH):
        sh1, sc1, g1, sh2, sc2, g2 = _mods(mod[l], slice(0, bp), per_row=False)
        proj = _proj(xp, sh1, sc1, w_pack, l, tm_p)
        o_a = _attn_prompt(proj, bp, t)
        o_b, s_new = _hgrn_prompt(proj, lower_bounds[l], hgrn_norm_g[l], bp, t)
        merged = _merge(o_a, o_b, proj, w_up_a_b, w_up_b_b, l, tm_p)
        xp = _out_proj(merged, w_o_b, xp, g1, ln1_g, ln1_b, l, tm_p)
        h2, gates = _router(xp, sh2, sc2, w_router_t, b_router, tm_p)
        xp = _moe(h2, gates.T, w1_b, w3_b, w2_b, xp, g2, ln2_g, ln2_b, l, tm_moe)
        outs_p["k"].append(proj[:, OFF_K:OFF_K + KV_WIDTH].reshape(bp, t, N_KV_HEADS, HEAD_DIM))
        outs_p["v"].append(proj[:, OFF_V:OFF_V + KV_WIDTH].reshape(bp, t, N_KV_HEADS, HEAD_DIM))
        outs_p["ik"].append(proj[:, OFF_IKW:OFF_IKW + IDX_DIM].reshape(bp, t, IDX_DIM))
        outs_p["s"].append(s_new)

        sh1, sc1, g1, sh2, sc2, g2 = _mods(mod[l], slice(bp, bp + db), per_row=True)
        proj = _proj(xs, sh1, sc1, w_pack, l, db)
        o_a = _attn_decode(proj, page_table, cache_k, cache_v, cache_idx_k, l)
        o_b, s_new = _hgrn_decode(proj, state_hgrn, lower_bounds[l], hgrn_norm_g[l], l)
        merged = _merge(o_a, o_b, proj, w_up_a_b, w_up_b_b, l, db)
        xs = _out_proj(merged, w_o_b, xs, g1, ln1_g, ln1_b, l, db)
        h2, gates = _router(xs, sh2, sc2, w_router_t, b_router, db)
        xs = _moe(h2, gates.T, w1_b, w3_b, w2_b, xs, g2, ln2_g, ln2_b, l, db)
        outs_s["k"].append(proj[:, OFF_K:OFF_K + KV_WIDTH].reshape(db, 1, N_KV_HEADS, HEAD_DIM))
        outs_s["v"].append(proj[:, OFF_V:OFF_V + KV_WIDTH].reshape(db, 1, N_KV_HEADS, HEAD_DIM))
        outs_s["ik"].append(proj[:, OFF_IKW:OFF_IKW + IDX_DIM].reshape(db, 1, IDX_DIM))
        outs_s["s"].append(s_new)

    return (xp.reshape(bp, t, D_MODEL), xs.reshape(db, 1, D_MODEL),
            jnp.stack(outs_p["k"]), jnp.stack(outs_p["v"]), jnp.stack(outs_p["ik"]), jnp.stack(outs_p["s"]),
            jnp.stack(outs_s["k"]), jnp.stack(outs_s["v"]), jnp.stack(outs_s["ik"]), jnp.stack(outs_s["s"]))
```

```python
import functools

import jax
import jax.numpy as jnp
from jax import lax
from jax.experimental import pallas as pl
from jax.experimental.pallas import tpu as pltpu

F32 = jnp.float32
BF16 = jnp.bfloat16
I32 = jnp.int32

DEPTH = 2
D_MODEL = 2048
N_ATT_HEADS = 8
N_KV_HEADS = 2
KV_GROUP = N_ATT_HEADS // N_KV_HEADS
HEAD_DIM = 128
ATT_WIDTH = N_ATT_HEADS * HEAD_DIM
KV_WIDTH = N_KV_HEADS * HEAD_DIM
N_IDX_HEADS = 16
IDX_DIM = 64
TOPK_MAX = 256
N_HGRN_HEADS = 8
HGRN_DK = 128
HGRN_DV = 128
HGRN_WIDTH = N_HGRN_HEADS * HGRN_DK
HGRN_CHUNK = 64
HGRN_SUB = 16
N_EXPERTS = 16
N_GROUPS = 4
EXPERTS_PER_GROUP = N_EXPERTS // N_GROUPS
EXPERT_DFF = 1024
ALPHA = (2 * DEPTH) ** 0.25
LN_EPS = 1e-5
MASK_VALUE = -1e30
INT_MIN = -2 ** 31

IN_SPLITS = (ATT_WIDTH, KV_WIDTH, KV_WIDTH, N_IDX_HEADS * IDX_DIM, IDX_DIM, N_IDX_HEADS,
             HGRN_WIDTH, HGRN_WIDTH, HGRN_WIDTH, HGRN_WIDTH, D_MODEL, D_MODEL)

LANES = 128
OFF_GA = 0
OFF_GB = OFF_GA + D_MODEL
OFF_Q = OFF_GB + D_MODEL
OFF_IQ = OFF_Q + ATT_WIDTH
OFF_HQ = OFF_IQ + N_IDX_HEADS * IDX_DIM
OFF_HF = OFF_HQ + HGRN_WIDTH
OFF_HI = OFF_HF + HGRN_WIDTH
OFF_HG = OFF_HI + HGRN_WIDTH
OFF_K = OFF_HG + HGRN_WIDTH
OFF_V = OFF_K + KV_WIDTH
OFF_IKW = OFF_V + KV_WIDTH
PACK_WIDTH = OFF_IKW + LANES

VMEM_LIMIT = 56 * 1024 * 1024


def _params(semantics):
    return pltpu.CompilerParams(dimension_semantics=semantics, vmem_limit_bytes=VMEM_LIMIT)


def _tile(n, pref):
    t = min(n, pref)
    while n % t:
        t //= 2
    return t


def _ln(x):
    mu = jnp.mean(x, axis=-1, keepdims=True)
    xc = x - mu
    var = jnp.mean(xc * xc, axis=-1, keepdims=True)
    return xc * lax.rsqrt(var + LN_EPS)


def _dot(a, b):
    return jnp.dot(a, b, preferred_element_type=F32)


def _dot_nt(a, b):
    return lax.dot_general(a, b, (((1,), (1,)), ((), ())), preferred_element_type=F32)


def _split3(x):
    x1 = x.astype(BF16)
    r1 = x - x1.astype(F32)
    x2 = r1.astype(BF16)
    x3 = (r1 - x2.astype(F32)).astype(BF16)
    return x1, x2, x3


def _sort_key(s):
    bits = pltpu.bitcast(s, I32)
    return bits ^ ((bits >> 31) & 0x7FFFFFFF)


def _kth_largest_key(count_ge, n_sel, shape):
    def body(i, ans):
        bit = lax.shift_left(jnp.int32(1), jnp.int32(31) - i)
        cand = ans | bit
        cnt = count_ge(cand ^ INT_MIN)
        return jnp.where(cnt >= n_sel, cand, ans)
    ans = lax.fori_loop(0, 32, body, jnp.zeros(shape, I32))
    return ans ^ INT_MIN


def _ada_kernel(c_ref, w_ref, b_ref, o_ref):
    c = c_ref[...]
    a = (c * jax.nn.sigmoid(c)).astype(BF16)
    o_ref[...] = _dot(a, w_ref[...].astype(BF16)) + b_ref[...]


def _ada(c_all, w_ada, b_ada):
    rows = c_all.shape[0]
    width = w_ada.shape[-1]
    tn = _tile(width, 1024)
    return pl.pallas_call(
        _ada_kernel,
        out_shape=jax.ShapeDtypeStruct((DEPTH, rows, width), F32),
        grid=(DEPTH, width // tn),
        in_specs=[pl.BlockSpec((rows, D_MODEL), lambda l, j: (0, 0)),
                  pl.BlockSpec((None, D_MODEL, tn), lambda l, j: (l, 0, j)),
                  pl.BlockSpec((None, 1, tn), lambda l, j: (l, 0, j))],
        out_specs=pl.BlockSpec((None, rows, tn), lambda l, j: (l, 0, j)),
        compiler_params=_params(("parallel", "parallel")),
        name="ada",
    )(c_all, w_ada, b_ada.reshape(DEPTH, 1, width))


def _proj_kernel(x_ref, sh_ref, sc_ref, w_ref, o_ref, h_scr):
    @pl.when(pl.program_id(1) == 0)
    def _():
        h_scr[...] = (_ln(x_ref[...]) * (1.0 + sc_ref[0]) + sh_ref[0]).astype(BF16)
    o_ref[...] = _dot(h_scr[...], w_ref[...])


def _proj(x, shift, scale, w_pack, l, tm):
    n = x.shape[0]
    tiles_per_mod = (n // tm) // shift.shape[0]
    r = shift.shape[1]
    tn = 640
    mod_spec = pl.BlockSpec((1, r, D_MODEL), lambda i, j: (i // tiles_per_mod, 0, 0))
    return pl.pallas_call(
        _proj_kernel,
        out_shape=jax.ShapeDtypeStruct((n, PACK_WIDTH), F32),
        grid=(n // tm, PACK_WIDTH // tn),
        in_specs=[pl.BlockSpec((tm, D_MODEL), lambda i, j: (i, 0)),
                  mod_spec, mod_spec,
                  pl.BlockSpec((None, D_MODEL, tn), lambda i, j: (l, 0, j))],
        out_specs=pl.BlockSpec((tm, tn), lambda i, j: (i, j)),
        scratch_shapes=[pltpu.VMEM((tm, D_MODEL), BF16)],
        compiler_params=_params(("parallel", "arbitrary")),
        name="proj",
    )(x, shift, scale, w_pack)


ROW_CHUNK = 64
ATT_ROWS = 128


def _lane_fold(x, op):
    acc = x[:, :LANES]
    for i in range(1, x.shape[1] // LANES):
        acc = op(acc, x[:, i * LANES:(i + 1) * LANES])
    return acc


def _attn_prompt_kernel(q_ref, iq_ref, iwq_ref, k_ref, v_ref, ikw_ref, o_ref,
                        kb_scr, va_scr, ikb_scr, iqs_scr, qs_scr, key_scr, m_scr, acc_scr, xb_scr,
                        *, tq, n_sel, col_bits):
    qi = pl.program_id(1)
    tk = tq
    n_lane_tiles = tk // LANES

    @pl.when(qi == 0)
    def _():
        kb_scr[...] = k_ref[...].astype(BF16)
        for g in range(N_KV_HEADS):
            va_scr[:, 2 * g * HEAD_DIM:(2 * g + 1) * HEAD_DIM] = (
                v_ref[:, g * HEAD_DIM:(g + 1) * HEAD_DIM].astype(BF16))
            va_scr[:, (2 * g + 1) * HEAD_DIM:(2 * g + 2) * HEAD_DIM] = jnp.ones((va_scr.shape[0], HEAD_DIM), BF16)
        ikb_scr[...] = ikw_ref[:, :IDX_DIM].astype(BF16)

    iq = iq_ref[...]
    for h in range(N_IDX_HEADS):
        iqs_scr[h * tq:(h + 1) * tq, :] = iq[:, h * IDX_DIM:(h + 1) * IDX_DIM].astype(BF16)
    q = q_ref[...]
    for h in range(N_ATT_HEADS):
        qs_scr[h] = q[:, h * HEAD_DIM:(h + 1) * HEAD_DIM].astype(BF16)
    w = iwq_ref[:, IDX_DIM:IDX_DIM + N_IDX_HEADS] * (IDX_DIM ** -0.5 * N_IDX_HEADS ** -0.5)
    row = qi * tq + lax.broadcasted_iota(I32, (tq, tk), 0)
    col0 = lax.broadcasted_iota(I32, (tq, tk), 1)

    def score_block(kb, carry):
        start = pl.multiple_of(kb * tk, tk)
        lg = _dot_nt(iqs_scr[...], ikb_scr[pl.ds(start, tk), :])
        s = jnp.zeros((tq, tk), F32)
        for h in range(N_IDX_HEADS):
            s = s + w[:, h:h + 1] * jnp.maximum(lg[h * tq:(h + 1) * tq, :], 0.0)
        s = jnp.where(kb * tk + col0 <= row, s, MASK_VALUE)
        key_scr[kb] = _sort_key(s)
        return carry

    lax.fori_loop(0, qi + 1, score_block, 0)

    chunks = [slice(c * ROW_CHUNK, (c + 1) * ROW_CHUNK) for c in range(tq // ROW_CHUNK)]

    def count(make_pred):
        parts = []
        for rows in chunks:
            pred = make_pred(rows)

            def body(kb, part, pred=pred, rows=rows):
                hit = jnp.where(pred(kb, key_scr[kb, rows, :]), 1.0, 0.0)
                return part + _lane_fold(hit, jnp.add)

            parts.append(lax.fori_loop(0, qi + 1, body, jnp.zeros((ROW_CHUNK, LANES), F32)))
        part = jnp.concatenate(parts, axis=0)
        return jnp.broadcast_to(jnp.sum(part, axis=1, keepdims=True), (tq, LANES))

    def wide(x):
        return jnp.concatenate([x] * n_lane_tiles, axis=1)

    def bcast(x, rows):
        return wide(x[rows])

    def count_ge(cand):
        return count(lambda rows: (lambda kb, key, c=bcast(cand, rows): key >= c))

    thr = _kth_largest_key(count_ge, float(n_sel), (tq, LANES))

    cnt_gt = count(lambda rows: (lambda kb, key, c=bcast(thr, rows): key > c))
    cnt_ge = count_ge(thr)
    need = float(n_sel) - cnt_gt
    xb_scr[...] = jnp.full(xb_scr.shape, 2 ** 31 - 1, I32)

    @pl.when(jnp.max(jnp.abs(cnt_ge - float(n_sel))) > 0.0)
    def _():
        def body(i, x):
            cand = x | lax.shift_left(jnp.int32(1), jnp.int32(col_bits - 1) - i)

            def make_pred(rows):
                c_thr, c_cand = bcast(thr, rows), bcast(cand, rows)
                c_col = lax.broadcasted_iota(I32, (ROW_CHUNK, tk), 1)
                return lambda kb, key: (key == c_thr) & (kb * tk + c_col < c_cand)

            return jnp.where(count(make_pred) < need, cand, x)

        xb_scr[...] = lax.fori_loop(0, col_bits, body, jnp.zeros((tq, LANES), I32))

    xb_w, thr_w = wide(xb_scr[...]), wide(thr)

    def bias_block(kb, carry):
        key = key_scr[kb]
        col = kb * tk + col0
        sel = ((key > thr_w) | ((key == thr_w) & (col <= xb_w))) & (col <= row)
        key_scr[kb] = pltpu.bitcast(jnp.where(sel, 0.0, -jnp.inf), I32)
        return carry

    lax.fori_loop(0, qi + 1, bias_block, 0)

    m_scr[...] = jnp.full(m_scr.shape, MASK_VALUE, F32)
    acc_scr[...] = jnp.zeros(acc_scr.shape, F32)
    scale = HEAD_DIM ** -0.5

    def att_block(kb, carry):
        start = pl.multiple_of(kb * tk, tk)
        for g in range(N_KV_HEADS):
            kg = kb_scr[pl.ds(start, tk), g * HEAD_DIM:(g + 1) * HEAD_DIM]
            va = va_scr[pl.ds(start, tk), 2 * g * HEAD_DIM:(2 * g + 2) * HEAD_DIM]
            for r in range(KV_GROUP):
                hh = g * KV_GROUP + r
                for c in range(tq // ATT_ROWS):
                    rows = slice(c * ATT_ROWS, (c + 1) * ATT_ROWS)
                    bias = pltpu.bitcast(key_scr[kb, rows, :], F32)
                    s = _dot_nt(qs_scr[hh, rows, :], kg) * scale + bias
                    m_old = m_scr[hh, rows, :]
                    m_new = jnp.maximum(m_old, jnp.max(_lane_fold(s, jnp.maximum), axis=1, keepdims=True))
                    p = jnp.exp(s - jnp.concatenate([m_new] * n_lane_tiles, axis=1))
                    a = jnp.exp(m_old - m_new)
                    acc_scr[hh, rows, :] = (jnp.concatenate([a, a], axis=1) * acc_scr[hh, rows, :]
                                            + _dot(p.astype(BF16), va))
                    m_scr[hh, rows, :] = m_new
        return carry

    lax.fori_loop(0, qi + 1, att_block, 0)
    for h in range(N_ATT_HEADS):
        acc = acc_scr[h]
        o_ref[:, h * HEAD_DIM:(h + 1) * HEAD_DIM] = (acc[:, :HEAD_DIM] / acc[:, HEAD_DIM:]).astype(o_ref.dtype)


def _attn_prompt(proj, b, t):
    tq = _tile(t, 256)
    n_sel = min(TOPK_MAX, t // 4)
    proj3 = proj.reshape(b, t, PACK_WIDTH)
    kern = functools.partial(_attn_prompt_kernel, tq=tq, n_sel=n_sel, col_bits=t.bit_length())
    out = pl.pallas_call(
        kern,
        out_shape=jax.ShapeDtypeStruct((b, t, ATT_WIDTH), BF16),
        grid=(b, t // tq),
        in_specs=[pl.BlockSpec((None, tq, ATT_WIDTH), lambda bi, qi: (bi, qi, OFF_Q // ATT_WIDTH)),
                  pl.BlockSpec((None, tq, ATT_WIDTH), lambda bi, qi: (bi, qi, OFF_IQ // ATT_WIDTH)),
                  pl.BlockSpec((None, tq, LANES), lambda bi, qi: (bi, qi, OFF_IKW // LANES)),
                  pl.BlockSpec((None, t, KV_WIDTH), lambda bi, qi: (bi, 0, OFF_K // KV_WIDTH)),
                  pl.BlockSpec((None, t, KV_WIDTH), lambda bi, qi: (bi, 0, OFF_V // KV_WIDTH)),
                  pl.BlockSpec((None, t, LANES), lambda bi, qi: (bi, 0, OFF_IKW // LANES))],
        out_specs=pl.BlockSpec((None, tq, ATT_WIDTH), lambda bi, qi: (bi, qi, 0)),
        scratch_shapes=[pltpu.VMEM((t, KV_WIDTH), BF16),
                        pltpu.VMEM((t, 2 * KV_WIDTH), BF16),
                        pltpu.VMEM((t, IDX_DIM), BF16),
                        pltpu.VMEM((N_IDX_HEADS * tq, IDX_DIM), BF16),
                        pltpu.VMEM((N_ATT_HEADS, tq, HEAD_DIM), BF16),
                        pltpu.VMEM((t // tq, tq, tq), I32),
                        pltpu.VMEM((N_ATT_HEADS, tq, LANES), F32),
                        pltpu.VMEM((N_ATT_HEADS, tq, 2 * HEAD_DIM), F32),
                        pltpu.VMEM((tq, LANES), I32)],
        compiler_params=_params(("parallel", "arbitrary")),
        name="attn_prompt",
    )(proj3, proj3, proj3, proj3, proj3, proj3)
    return out.reshape(b * t, ATT_WIDTH)


def _hgrn_gates(hq, hf, lb):
    q = hq * jax.nn.sigmoid(hq)
    f = lb + (1.0 - lb) * jax.nn.sigmoid(hf)
    kk = (1.0 - lb) * jax.nn.sigmoid(-hf)
    return q, f, kk


def _hgrn_finish(o, hg, ng):
    o = o * lax.rsqrt(jnp.mean(o * o, axis=-1, keepdims=True) + LN_EPS)
    return o * ng * (hg * jax.nn.sigmoid(hg))


def _hgrn_prompt_kernel(hq_ref, hf_ref, hi_ref, hg_ref, lb_ref, ng_ref, o_ref, s_ref, st_scr, oc_scr,
                        *, n_chunks):
    c_len = HGRN_CHUNK
    sub = HGRN_SUB
    ti = pl.program_id(2)

    @pl.when(ti == 0)
    def _():
        st_scr[...] = jnp.zeros(st_scr.shape, F32)

    lb = lb_ref[...]
    ng = ng_ref[...]
    r_i = lax.broadcasted_iota(I32, (c_len, c_len), 0)
    c_i = lax.broadcasted_iota(I32, (c_len, c_len), 1)
    tri = jnp.where(c_i <= r_i, 1.0, 0.0).astype(BF16)
    ones = jnp.ones((HGRN_DK, HGRN_DV), BF16)

    def chunk(c, carry):
        sl = pl.ds(pl.multiple_of(c * c_len, c_len), c_len)
        q, f, kk = _hgrn_gates(hq_ref[sl, :], hf_ref[sl, :], lb)
        v = hi_ref[sl, :]
        l1, l2, l3 = _split3(jnp.log(f))
        b = _dot(tri, l1) + _dot(tri, l2) + _dot(tri, l3)
        st = st_scr[...]
        oc_scr[...] = _dot_nt((q * jnp.exp(b)).astype(BF16), st.astype(BF16))
        for j in range(c_len // sub):
            lo = j * sub
            nt = c_len - lo
            qt, bt = q[lo:], b[lo:]
            t_idx = lax.broadcasted_iota(I32, (nt, 1), 0)
            ps = []
            for s in range(sub):
                valid = t_idx >= s
                dec = jnp.exp(jnp.where(valid, bt - b[lo + s:lo + s + 1, :], 0.0))
                ps.append(jnp.where(valid, qt * kk[lo + s:lo + s + 1, :] * dec, 0.0).astype(BF16))
            rs = _dot(jnp.concatenate(ps, axis=0), ones)
            acc = jnp.zeros((nt, HGRN_DV), F32)
            for s in range(sub):
                acc = acc + rs[s * nt:(s + 1) * nt, :] * v[lo + s:lo + s + 1, :]
            oc_scr[lo:, :] += acc
        b_last = b[c_len - 1:c_len, :]
        kd = (kk * jnp.exp(b_last - b)).astype(BF16)
        upd = lax.dot_general(v.astype(BF16), kd, (((0,), (0,)), ((), ())), preferred_element_type=F32)
        st_scr[...] = st * jnp.exp(b_last) + upd
        o_ref[sl, :] = _hgrn_finish(oc_scr[...], hg_ref[sl, :], ng).astype(o_ref.dtype)
        return carry

    lax.fori_loop(0, n_chunks, chunk, 0)

    @pl.when(ti == pl.num_programs(2) - 1)
    def _():
        s_ref[...] = st_scr[...].T


def _hgrn_prompt(proj, lb, ng, b, t):
    tc = _tile(t, 512)
    n_chunks = tc // HGRN_CHUNK
    proj3 = proj.reshape(b, t, PACK_WIDTH)

    def col(off):
        return pl.BlockSpec((None, tc, HGRN_DK), lambda bi, h, ti: (bi, ti, off // HGRN_DK + h))

    vec = pl.BlockSpec((None, 1, HGRN_DK), lambda bi, h, ti: (h, 0, 0))
    o, s = pl.pallas_call(
        functools.partial(_hgrn_prompt_kernel, n_chunks=n_chunks),
        out_shape=(jax.ShapeDtypeStruct((b, t, HGRN_WIDTH), BF16),
                   jax.ShapeDtypeStruct((b, N_HGRN_HEADS, HGRN_DK, HGRN_DV), F32)),
        grid=(b, N_HGRN_HEADS, t // tc),
        in_specs=[col(OFF_HQ), col(OFF_HF), col(OFF_HI), col(OFF_HG), vec, vec],
        out_specs=(pl.BlockSpec((None, tc, HGRN_DV), lambda bi, h, ti: (bi, ti, h)),
                   pl.BlockSpec((None, None, HGRN_DK, HGRN_DV), lambda bi, h, ti: (bi, h, 0, 0))),
        scratch_shapes=[pltpu.VMEM((HGRN_DV, HGRN_DK), F32),
                        pltpu.VMEM((HGRN_CHUNK, HGRN_DV), F32)],
        compiler_params=_params(("parallel", "parallel", "arbitrary")),
        name="hgrn_prompt",
    )(proj3, proj3, proj3, proj3, lb.reshape(N_HGRN_HEADS, 1, HGRN_DK), ng.reshape(N_HGRN_HEADS, 1, HGRN_DV))
    return o.reshape(b * t, HGRN_WIDTH), s


def _merge_kernel(oa_ref, ob_ref, ga_ref, gb_ref, wa_ref, wb_ref, o_ref):
    ya = _dot(oa_ref[...].astype(BF16), wa_ref[...])
    yb = _dot(ob_ref[...].astype(BF16), wb_ref[...])
    o_ref[...] = (jax.nn.sigmoid(ga_ref[...]) * ya + jax.nn.sigmoid(gb_ref[...]) * yb).astype(o_ref.dtype)


def _merge(o_a, o_b, proj, w_up_a, w_up_b, l, tm):
    n = o_a.shape[0]
    return pl.pallas_call(
        _merge_kernel,
        out_shape=jax.ShapeDtypeStruct((n, D_MODEL), BF16),
        grid=(n // tm,),
        in_specs=[pl.BlockSpec((tm, ATT_WIDTH), lambda i: (i, 0)),
                  pl.BlockSpec((tm, HGRN_WIDTH), lambda i: (i, 0)),
                  pl.BlockSpec((tm, D_MODEL), lambda i: (i, OFF_GA // D_MODEL)),
                  pl.BlockSpec((tm, D_MODEL), lambda i: (i, OFF_GB // D_MODEL)),
                  pl.BlockSpec((None, ATT_WIDTH, D_MODEL), lambda i: (l, 0, 0)),
                  pl.BlockSpec((None, HGRN_WIDTH, D_MODEL), lambda i: (l, 0, 0))],
        out_specs=pl.BlockSpec((tm, D_MODEL), lambda i: (i, 0)),
        compiler_params=_params(("parallel",)),
        name="merge",
    )(o_a, o_b, proj, proj, w_up_a, w_up_b)


def _post_norm(x, gate, y, g, b):
    return _ln(ALPHA * x + gate * y) * g + b


def _out_kernel(m_ref, w_ref, x_ref, gate_ref, lng_ref, lnb_ref, o_ref):
    y = _dot(m_ref[...], w_ref[...])
    o_ref[...] = _post_norm(x_ref[...], gate_ref[0], y, lng_ref[...], lnb_ref[...])


def _out_proj(merged, w_o, x, gate, ln_g, ln_b, l, tm):
    n = x.shape[0]
    tiles_per_mod = (n // tm) // gate.shape[0]
    vec = pl.BlockSpec((None, 1, D_MODEL), lambda i: (l, 0, 0))
    return pl.pallas_call(
        _out_kernel,
        out_shape=jax.ShapeDtypeStruct((n, D_MODEL), F32),
        grid=(n // tm,),
        in_specs=[pl.BlockSpec((tm, D_MODEL), lambda i: (i, 0)),
                  pl.BlockSpec((None, D_MODEL, D_MODEL), lambda i: (l, 0, 0)),
                  pl.BlockSpec((tm, D_MODEL), lambda i: (i, 0)),
                  pl.BlockSpec((1, gate.shape[1], D_MODEL), lambda i: (i // tiles_per_mod, 0, 0)),
                  vec, vec],
        out_specs=pl.BlockSpec((tm, D_MODEL), lambda i: (i, 0)),
        compiler_params=_params(("parallel",)),
        name="out_proj",
    )(merged, w_o, x, gate, ln_g.reshape(DEPTH, 1, D_MODEL), ln_b.reshape(DEPTH, 1, D_MODEL))


def _router_kernel(x_ref, sh_ref, sc_ref, wr_ref, br_ref, h_ref, g_ref):
    h = _ln(x_ref[...]) * (1.0 + sc_ref[0]) + sh_ref[0]
    h_ref[...] = h.astype(BF16)
    h1, h2, _ = _split3(h)
    w1, w2, _ = _split3(wr_ref[...])
    logits = _dot_nt(w1, h1) + _dot_nt(w1, h2) + _dot_nt(w2, h1)
    aff = jax.nn.sigmoid(logits)
    sel = aff + br_ref[...]
    rows = [sel[e:e + 1, :] for e in range(N_EXPERTS)]
    grp = []
    for g in range(N_GROUPS):
        a, b, c, d = rows[g * EXPERTS_PER_GROUP:(g + 1) * EXPERTS_PER_GROUP]
        hi1, lo1 = jnp.maximum(a, b), jnp.minimum(a, b)
        hi2, lo2 = jnp.maximum(c, d), jnp.minimum(c, d)
        grp.append(jnp.maximum(hi1, hi2) + jnp.maximum(jnp.minimum(hi1, hi2), jnp.maximum(lo1, lo2)))
    best = functools.reduce(jnp.maximum, grp)
    taken = jnp.zeros_like(best)
    picked = []
    for g in range(N_GROUPS):
        is_g = jnp.where(grp[g] == best, 1.0, 0.0) * (1.0 - taken)
        taken = taken + is_g
        for e in range(g * EXPERTS_PER_GROUP, (g + 1) * EXPERTS_PER_GROUP):
            rank = jnp.zeros_like(best)
            for o in range(g * EXPERTS_PER_GROUP, (g + 1) * EXPERTS_PER_GROUP):
                if o < e:
                    rank = rank + jnp.where(rows[o] >= rows[e], 1.0, 0.0)
                elif o > e:
                    rank = rank + jnp.where(rows[o] > rows[e], 1.0, 0.0)
            picked.append(is_g * jnp.where(rank < 2.0, 1.0, 0.0))
    gate = jnp.concatenate(picked, axis=0) * aff
    g_ref[...] = gate / jnp.sum(gate, axis=0, keepdims=True)


def _router(x, shift, scale, w_router_t, b_router, tm):
    n = x.shape[0]
    tiles_per_mod = (n // tm) // shift.shape[0]
    mod_spec = pl.BlockSpec((1, shift.shape[1], D_MODEL), lambda i: (i // tiles_per_mod, 0, 0))
    return pl.pallas_call(
        _router_kernel,
        out_shape=(jax.ShapeDtypeStruct((n, D_MODEL), BF16),
                   jax.ShapeDtypeStruct((N_EXPERTS, n), F32)),
        grid=(n // tm,),
        in_specs=[pl.BlockSpec((tm, D_MODEL), lambda i: (i, 0)),
                  mod_spec, mod_spec,
                  pl.BlockSpec((N_EXPERTS, D_MODEL), lambda i: (0, 0)),
                  pl.BlockSpec((N_EXPERTS, 1), lambda i: (0, 0))],
        out_specs=(pl.BlockSpec((tm, D_MODEL), lambda i: (i, 0)),
                   pl.BlockSpec((N_EXPERTS, tm), lambda i: (0, i))),
        compiler_params=_params(("parallel",)),
        name="router",
    )(x, shift, scale, w_router_t, b_router.reshape(N_EXPERTS, 1))


def _moe_kernel(h_ref, g_ref, w1_ref, w3_ref, w2_ref, x_ref, gate_ref, lng_ref, lnb_ref, o_ref, acc_scr):
    e = pl.program_id(1)

    @pl.when(e == 0)
    def _():
        acc_scr[...] = jnp.zeros(acc_scr.shape, F32)

    h = h_ref[...]
    a = _dot(h, w1_ref[...])
    b = _dot(h, w3_ref[...])
    gates = g_ref[...]
    lane = lax.broadcasted_iota(I32, gates.shape, 1)
    gcol = jnp.sum(jnp.where(lane == e, gates, 0.0), axis=1, keepdims=True)
    u = (a * jax.nn.sigmoid(a) * b * gcol).astype(BF16)
    acc_scr[...] += _dot(u, w2_ref[...])

    @pl.when(e == pl.num_programs(1) - 1)
    def _():
        o_ref[...] = _post_norm(x_ref[...], gate_ref[0], acc_scr[...], lng_ref[...], lnb_ref[...])


def _moe(h, gates, w1, w3, w2, x, gate, ln_g, ln_b, l, tm):
    n = x.shape[0]
    tiles_per_mod = (n // tm) // gate.shape[0]
    vec = pl.BlockSpec((None, 1, D_MODEL), lambda i, e: (l, 0, 0))
    return pl.pallas_call(
        _moe_kernel,
        out_shape=jax.ShapeDtypeStruct((n, D_MODEL), F32),
        grid=(n // tm, N_EXPERTS),
        in_specs=[pl.BlockSpec((tm, D_MODEL), lambda i, e: (i, 0)),
                  pl.BlockSpec((tm, N_EXPERTS), lambda i, e: (i, 0)),
                  pl.BlockSpec((None, None, D_MODEL, EXPERT_DFF), lambda i, e: (l, e, 0, 0)),
                  pl.BlockSpec((None, None, D_MODEL, EXPERT_DFF), lambda i, e: (l, e, 0, 0)),
                  pl.BlockSpec((None, None, EXPERT_DFF, D_MODEL), lambda i, e: (l, e, 0, 0)),
                  pl.BlockSpec((tm, D_MODEL), lambda i, e: (i, 0)),
                  pl.BlockSpec((1, gate.shape[1], D_MODEL), lambda i, e: (i // tiles_per_mod, 0, 0)),
                  vec, vec],
        out_specs=pl.BlockSpec((tm, D_MODEL), lambda i, e: (i, 0)),
        scratch_shapes=[pltpu.VMEM((tm, D_MODEL), F32)],
        compiler_params=_params(("parallel", "arbitrary")),
        name="moe",
    )(h, gates, w1, w3, w2, x, gate, ln_g.reshape(DEPTH, 1, D_MODEL), ln_b.reshape(DEPTH, 1, D_MODEL))


def _row_to_col(v):
    n = v.shape[1]
    r = lax.broadcasted_iota(I32, (n, n), 0)
    c = lax.broadcasted_iota(I32, (n, n), 1)
    return jnp.sum(jnp.where(r == c, jnp.broadcast_to(v, (n, n)), 0.0), axis=1, keepdims=True)


def _attn_decode_kernel(pt_ref, proj_ref, ck_ref, cv_ref, cik_ref, o_ref,
                        ik_buf, k_buf, v_buf, key_scr, sem_ik, sem_k, sem_v,
                        *, l, n_pages, page, chunk_pages, n_sel):
    b = pl.program_id(0)
    n_chunks = n_pages // chunk_pages
    chunk = chunk_pages * page

    def ik_copy(j):
        return pltpu.make_async_copy(cik_ref.at[l, pt_ref[b, j]], ik_buf.at[pl.ds(j * page, page)], sem_ik)

    def kv_copies(c, j, slot):
        pg = pt_ref[b, c * chunk_pages + j]
        dst = pl.ds(j * page, page)
        copies = []
        for g in range(N_KV_HEADS):
            copies.append(pltpu.make_async_copy(ck_ref.at[l, pg, :, g, :], k_buf.at[slot, g, dst], sem_k.at[slot]))
            copies.append(pltpu.make_async_copy(cv_ref.at[l, pg, :, g, :], v_buf.at[slot, g, dst], sem_v.at[slot]))
        return copies

    def start_chunk(c, slot):
        def body(j, carry):
            for cp in kv_copies(c, j, slot):
                cp.start()
            return carry
        lax.fori_loop(0, chunk_pages, body, 0)

    def wait_chunk(c, slot):
        def body(j, carry):
            for cp in kv_copies(c, j, slot):
                cp.wait()
            return carry
        lax.fori_loop(0, chunk_pages, body, 0)

    def start_ik(j, carry):
        ik_copy(j).start()
        return carry

    def wait_ik(j, carry):
        ik_copy(j).wait()
        return carry

    lax.fori_loop(0, n_pages, start_ik, 0)
    start_chunk(0, 0)
    lax.fori_loop(0, n_pages, wait_ik, 0)

    rowsl = slice(None)
    iq = proj_ref[rowsl, OFF_IQ:OFF_IQ + N_IDX_HEADS * IDX_DIM]
    iq_h = jnp.concatenate([iq[:, h * IDX_DIM:(h + 1) * IDX_DIM] for h in range(N_IDX_HEADS)], axis=0)
    w_row = proj_ref[rowsl, OFF_IKW + IDX_DIM:OFF_IKW + IDX_DIM + N_IDX_HEADS]
    w_col = _row_to_col(w_row) * (IDX_DIM ** -0.5 * N_IDX_HEADS ** -0.5)
    ik_new = proj_ref[rowsl, OFF_IKW:OFF_IKW + IDX_DIM]

    iq_b = iq_h.astype(BF16)
    for c in range(n_chunks):
        lg = _dot_nt(iq_b, ik_buf[c * chunk:(c + 1) * chunk, :].astype(BF16))
        s = jnp.sum(w_col * jnp.maximum(lg, 0.0), axis=0, keepdims=True)
        key_scr[c] = _sort_key(s)
    lg_new = jnp.sum(iq_b.astype(F32) * ik_new.astype(BF16).astype(F32), axis=1, keepdims=True)
    key_new = _sort_key(jnp.sum(w_col * jnp.maximum(lg_new, 0.0), axis=0, keepdims=True))

    def count_ge(cand):
        cnt = jnp.where(key_new >= cand, 1.0, 0.0)
        for c in range(n_chunks):
            cnt = cnt + jnp.sum(jnp.where(key_scr[c] >= cand, 1.0, 0.0), axis=1, keepdims=True)
        return cnt

    thr = _kth_largest_key(count_ge, float(n_sel), (1, 1))

    q = proj_ref[rowsl, OFF_Q:OFF_Q + ATT_WIDTH]
    q_h = jnp.concatenate([q[:, h * HEAD_DIM:(h + 1) * HEAD_DIM] for h in range(N_ATT_HEADS)], axis=0)
    q_b = q_h.astype(BF16)
    k_new = proj_ref[rowsl, OFF_K:OFF_K + KV_WIDTH]
    v_new = proj_ref[rowsl, OFF_V:OFF_V + KV_WIDTH]
    scale = HEAD_DIM ** -0.5

    def att_chunk(c, carry):
        slot = c % 2

        @pl.when(c + 1 < n_chunks)
        def _():
            start_chunk(c + 1, 1 - slot)

        wait_chunk(c, slot)
        sel = key_scr[c] >= thr
        out = []
        for g in range(N_KV_HEADS):
            m_old, l_old, acc_old = carry[g]
            kg = k_buf[slot, g].astype(BF16)
            vg = v_buf[slot, g].astype(BF16)
            s = _dot_nt(q_b[g * KV_GROUP:(g + 1) * KV_GROUP], kg) * scale
            s = jnp.where(sel, s, MASK_VALUE)
            m_new = jnp.maximum(m_old, jnp.max(s, axis=1, keepdims=True))
            p = jnp.where(sel, jnp.exp(s - m_new), 0.0)
            a = jnp.exp(m_old - m_new)
            out.append((m_new, a * l_old + jnp.sum(p, axis=1, keepdims=True),
                        a * acc_old + _dot(p.astype(BF16), vg)))
        return tuple(out)

    init = tuple((jnp.full((KV_GROUP, 1), MASK_VALUE, F32), jnp.zeros((KV_GROUP, 1), F32),
                  jnp.zeros((KV_GROUP, HEAD_DIM), F32)) for _ in range(N_KV_HEADS))
    res = lax.fori_loop(0, n_chunks, att_chunk, init)

    sel_new = key_new >= thr
    for g in range(N_KV_HEADS):
        m_old, l_old, acc_old = res[g]
        kg = k_new[:, g * HEAD_DIM:(g + 1) * HEAD_DIM].astype(BF16).astype(F32)
        vg = v_new[:, g * HEAD_DIM:(g + 1) * HEAD_DIM].astype(BF16).astype(F32)
        qg = q_b[g * KV_GROUP:(g + 1) * KV_GROUP].astype(F32)
        s = jnp.sum(qg * kg, axis=1, keepdims=True) * scale
        s = jnp.where(sel_new, s, MASK_VALUE)
        m_new = jnp.maximum(m_old, s)
        p = jnp.where(sel_new, jnp.exp(s - m_new), 0.0)
        a = jnp.exp(m_old - m_new)
        l_new = a * l_old + p
        acc = a * acc_old + p.astype(BF16).astype(F32) * vg
        o = acc / l_new
        for r in range(KV_GROUP):
            hh = g * KV_GROUP + r
            o_ref[rowsl, hh * HEAD_DIM:(hh + 1) * HEAD_DIM] = o[r:r + 1, :].astype(o_ref.dtype)


def _attn_decode(proj, page_table, cache_k, cache_v, cache_idx_k, l):
    db, n_pages = page_table.shape
    page = cache_k.shape[2]
    past = n_pages * page
    n_sel = min(TOPK_MAX, (past + 1) // 4)
    chunk_pages = _tile(n_pages, 16)
    kern = functools.partial(_attn_decode_kernel, l=l, n_pages=n_pages, page=page,
                             chunk_pages=chunk_pages, n_sel=n_sel)
    grid_spec = pltpu.PrefetchScalarGridSpec(
        num_scalar_prefetch=1,
        grid=(db,),
        in_specs=[pl.BlockSpec((None, 1, PACK_WIDTH), lambda b, pt: (b, 0, 0)),
                  pl.BlockSpec(memory_space=pl.ANY),
                  pl.BlockSpec(memory_space=pl.ANY),
                  pl.BlockSpec(memory_space=pl.ANY)],
        out_specs=pl.BlockSpec((None, 1, ATT_WIDTH), lambda b, pt: (b, 0, 0)),
        scratch_shapes=[pltpu.VMEM((past, IDX_DIM), F32),
                        pltpu.VMEM((2, N_KV_HEADS, chunk_pages * page, HEAD_DIM), F32),
                        pltpu.VMEM((2, N_KV_HEADS, chunk_pages * page, HEAD_DIM), F32),
                        pltpu.VMEM((n_pages // chunk_pages, 1, chunk_pages * page), I32),
                        pltpu.SemaphoreType.DMA,
                        pltpu.SemaphoreType.DMA((2,)),
                        pltpu.SemaphoreType.DMA((2,))])
    return pl.pallas_call(
        kern,
        out_shape=jax.ShapeDtypeStruct((db, 1, ATT_WIDTH), F32),
        grid_spec=grid_spec,
        compiler_params=_params(("arbitrary",)),
        name="attn_decode",
    )(page_table, proj.reshape(db, 1, PACK_WIDTH), cache_k, cache_v, cache_idx_k).reshape(db, ATT_WIDTH)


def _hgrn_decode_kernel(proj_ref, s0_ref, lb_ref, ng_ref, o_ref, s_ref):
    rowsl = slice(None)
    for h in range(N_HGRN_HEADS):
        cols = lambda off: slice(off + h * HGRN_DK, off + (h + 1) * HGRN_DK)
        lb = lb_ref[:, h * HGRN_DK:(h + 1) * HGRN_DK]
        q, f, kk = _hgrn_gates(proj_ref[rowsl, cols(OFF_HQ)], proj_ref[rowsl, cols(OFF_HF)], lb)
        v = proj_ref[rowsl, cols(OFF_HI)]
        s_new = _row_to_col(f) * s0_ref[h] + _row_to_col(kk) * v
        s_ref[h] = s_new
        o = jnp.sum(_row_to_col(q) * s_new, axis=0, keepdims=True)
        ng = ng_ref[:, h * HGRN_DV:(h + 1) * HGRN_DV]
        o_ref[rowsl, h * HGRN_DV:(h + 1) * HGRN_DV] = _hgrn_finish(
            o, proj_ref[rowsl, cols(OFF_HG)], ng).astype(o_ref.dtype)


def _hgrn_decode(proj, state, lb, ng, l):
    db = proj.shape[0]
    st_spec_in = pl.BlockSpec((None, None, N_HGRN_HEADS, HGRN_DK, HGRN_DV), lambda b: (l, b, 0, 0, 0))
    o, s = pl.pallas_call(
        _hgrn_decode_kernel,
        out_shape=(jax.ShapeDtypeStruct((db, 1, HGRN_WIDTH), F32),
                   jax.ShapeDtypeStruct((db, N_HGRN_HEADS, HGRN_DK, HGRN_DV), F32)),
        grid=(db,),
        in_specs=[pl.BlockSpec((None, 1, PACK_WIDTH), lambda b: (b, 0, 0)),
                  st_spec_in,
                  pl.BlockSpec((1, HGRN_WIDTH), lambda b: (0, 0)),
                  pl.BlockSpec((1, HGRN_WIDTH), lambda b: (0, 0))],
        out_specs=(pl.BlockSpec((None, 1, HGRN_WIDTH), lambda b: (b, 0, 0)),
                   pl.BlockSpec((None, N_HGRN_HEADS, HGRN_DK, HGRN_DV), lambda b: (b, 0, 0, 0))),
        compiler_params=_params(("parallel",)),
        name="hgrn_decode",
    )(proj.reshape(db, 1, PACK_WIDTH), state, lb.reshape(1, HGRN_WIDTH), ng.reshape(1, HGRN_WIDTH))
    return o.reshape(db, HGRN_WIDTH), s


def _pack_w_in(w_in):
    offs = [0]
    for s in IN_SPLITS:
        offs.append(offs[-1] + s)
    q, k, v, iq, ik, iw, hq, hf, hi, hg, ga, gb = [w_in[:, :, offs[i]:offs[i + 1]] for i in range(len(IN_SPLITS))]
    pad = jnp.zeros(w_in.shape[:2] + (LANES - IDX_DIM - N_IDX_HEADS,), w_in.dtype)
    return jnp.concatenate([ga, gb, q, iq, hq, hf, hi, hg, k, v, ik, iw, pad], axis=-1).astype(BF16)


def _mods(mod_l, rows, per_row):
    m = mod_l[rows]
    parts = jnp.split(m, 6, axis=-1)
    if per_row:
        return [p[None, :, :] for p in parts]
    return [p[:, None, :] for p in parts]


def kernel(x_prompt, x_sample, c_prompt, c_sample, cache_k, cache_v, cache_idx_k, state_hgrn, page_table,
           w_ada, b_ada, w_in, w_up_a, w_up_b, w_o, hgrn_norm_g, hgrn_lb_logits, ln1_g, ln1_b,
           w_router, b_router, w1, w3, w2, ln2_g, ln2_b):
    bp, t, _ = x_prompt.shape
    db = x_sample.shape[0]
    lbp = jax.nn.softmax(hgrn_lb_logits.astype(F32), axis=0)
    lower_bounds = jnp.cumsum(lbp, axis=0) - lbp[0]

    w_pack = _pack_w_in(w_in)
    w_up_a_b, w_up_b_b, w_o_b = w_up_a.astype(BF16), w_up_b.astype(BF16), w_o.astype(BF16)
    w1_b, w3_b, w2_b = w1.astype(BF16), w3.astype(BF16), w2.astype(BF16)
    w_router_t = w_router.T

    n_c = bp + db
    c_rows = -(-n_c // 8) * 8
    c_all = jnp.concatenate([c_prompt, c_sample, jnp.zeros((c_rows - n_c, D_MODEL), F32)], axis=0)
    mod = _ada(c_all, w_ada, b_ada)

    xp = x_prompt.reshape(bp * t, D_MODEL)
    xs = x_sample.reshape(db, D_MODEL)
    tm_p = _tile(bp * t, 512)
    tm_moe = _tile(bp * t, 256)
    outs_p = {"k": [], "v": [], "ik": [], "s": []}
    outs_s = {"k": [], "v": [], "ik": [], "s": []}
    for l in range(DEPTH):
        sh1, sc1, g1, sh2, sc2, g2 = _mods(mod[l], slice(0, bp), per_row=False)
        proj = _proj(xp, sh1, sc1, w_pack, l, tm_p)
        o_a = _attn_prompt(proj, bp, t)
        o_b, s_new = _hgrn_prompt(proj, lower_bounds[l], hgrn_norm_g[l], bp, t)
        merged = _merge(o_a, o_b, proj, w_up_a_b, w_up_b_b, l, tm_p)
        xp = _out_proj(merged, w_o_b, xp, g1, ln1_g, ln1_b, l, tm_p)
        h2, gates = _router(xp, sh2, sc2, w_router_t, b_router, tm_p)
        xp = _moe(h2, gates.T, w1_b, w3_b, w2_b, xp, g2, ln2_g, ln2_b, l, tm_moe)
        outs_p["k"].append(proj[:, OFF_K:OFF_K + KV_WIDTH].reshape(bp, t, N_KV_HEADS, HEAD_DIM))
        outs_p["v"].append(proj[:, OFF_V:OFF_V + KV_WIDTH].reshape(bp, t, N_KV_HEADS, HEAD_DIM))
        outs_p["ik"].append(proj[:, OFF_IKW:OFF_IKW + IDX_DIM].reshape(bp, t, IDX_DIM))
        outs_p["s"].append(s_new)

        sh1, sc1, g1, sh2, sc2, g2 = _mods(mod[l], slice(bp, bp + db), per_row=True)
        proj = _proj(xs, sh1, sc1, w_pack, l, db)
        o_a = _attn_decode(proj, page_table, cache_k, cache_v, cache_idx_k, l)
        o_b, s_new = _hgrn_decode(proj, state_hgrn, lower_bounds[l], hgrn_norm_g[l], l)
        merged = _merge(o_a, o_b, proj, w_up_a_b, w_up_b_b, l, db)
        xs = _out_proj(merged, w_o_b, xs, g1, ln1_g, ln1_b, l, db)
        h2, gates = _router(xs, sh2, sc2, w_router_t, b_router, db)
        xs = _moe(h2, gates.T, w1_b, w3_b, w2_b, xs, g2, ln2_g, ln2_b, l, db)
        outs_s["k"].append(proj[:, OFF_K:OFF_K + KV_WIDTH].reshape(db, 1, N_KV_HEADS, HEAD_DIM))
        outs_s["v"].append(proj[:, OFF_V:OFF_V + KV_WIDTH].reshape(db, 1, N_KV_HEADS, HEAD_DIM))
        outs_s["ik"].append(proj[:, OFF_IKW:OFF_IKW + IDX_DIM].reshape(db, 1, IDX_DIM))
        outs_s["s"].append(s_new)

    return (xp.reshape(bp, t, D_MODEL), xs.reshape(db, 1, D_MODEL),
            jnp.stack(outs_p["k"]), jnp.stack(outs_p["v"]), jnp.stack(outs_p["ik"]), jnp.stack(outs_p["s"]),
            jnp.stack(outs_s["k"]), jnp.stack(outs_s["v"]), jnp.stack(outs_s["ik"]), jnp.stack(outs_s["s"]))
```

```python
import functools

import jax
import jax.numpy as jnp
from jax import lax
from jax.experimental import pallas as pl
from jax.experimental.pallas import tpu as pltpu

F32 = jnp.float32
BF16 = jnp.bfloat16
I32 = jnp.int32

DEPTH = 2
D_MODEL = 2048
N_ATT_HEADS = 8
N_KV_HEADS = 2
KV_GROUP = N_ATT_HEADS // N_KV_HEADS
HEAD_DIM = 128
ATT_WIDTH = N_ATT_HEADS * HEAD_DIM
KV_WIDTH = N_KV_HEADS * HEAD_DIM
N_IDX_HEADS = 16
IDX_DIM = 64
TOPK_MAX = 256
N_HGRN_HEADS = 8
HGRN_DK = 128
HGRN_DV = 128
HGRN_WIDTH = N_HGRN_HEADS * HGRN_DK
HGRN_CHUNK = 64
HGRN_SUB = 16
N_EXPERTS = 16
N_GROUPS = 4
EXPERTS_PER_GROUP = N_EXPERTS // N_GROUPS
EXPERT_DFF = 1024
ALPHA = (2 * DEPTH) ** 0.25
LN_EPS = 1e-5
MASK_VALUE = -1e30
INT_MIN = -2 ** 31

IN_SPLITS = (ATT_WIDTH, KV_WIDTH, KV_WIDTH, N_IDX_HEADS * IDX_DIM, IDX_DIM, N_IDX_HEADS,
             HGRN_WIDTH, HGRN_WIDTH, HGRN_WIDTH, HGRN_WIDTH, D_MODEL, D_MODEL)

LANES = 128
OFF_GA = 0
OFF_GB = OFF_GA + D_MODEL
OFF_Q = OFF_GB + D_MODEL
OFF_IQ = OFF_Q + ATT_WIDTH
OFF_HQ = OFF_IQ + N_IDX_HEADS * IDX_DIM
OFF_HF = OFF_HQ + HGRN_WIDTH
OFF_HI = OFF_HF + HGRN_WIDTH
OFF_HG = OFF_HI + HGRN_WIDTH
OFF_K = OFF_HG + HGRN_WIDTH
OFF_V = OFF_K + KV_WIDTH
OFF_IKW = OFF_V + KV_WIDTH
PACK_WIDTH = OFF_IKW + LANES

VMEM_LIMIT = 56 * 1024 * 1024


def _params(semantics):
    return pltpu.CompilerParams(dimension_semantics=semantics, vmem_limit_bytes=VMEM_LIMIT)


def _tile(n, pref):
    t = min(n, pref)
    while n % t:
        t //= 2
    return t


def _ln(x):
    mu = jnp.mean(x, axis=-1, keepdims=True)
    xc = x - mu
    var = jnp.mean(xc * xc, axis=-1, keepdims=True)
    return xc * lax.rsqrt(var + LN_EPS)


def _dot(a, b):
    return jnp.dot(a, b, preferred_element_type=F32)


def _dot_nt(a, b):
    return lax.dot_general(a, b, (((1,), (1,)), ((), ())), preferred_element_type=F32)


def _split3(x):
    x1 = x.astype(BF16)
    r1 = x - x1.astype(F32)
    x2 = r1.astype(BF16)
    x3 = (r1 - x2.astype(F32)).astype(BF16)
    return x1, x2, x3


def _sort_key(s):
    bits = pltpu.bitcast(s, I32)
    return bits ^ ((bits >> 31) & 0x7FFFFFFF)


def _kth_largest_key(count_ge, n_sel, shape):
    def body(i, ans):
        bit = lax.shift_left(jnp.int32(1), jnp.int32(31) - i)
        cand = ans | bit
        cnt = count_ge(cand ^ INT_MIN)
        return jnp.where(cnt >= n_sel, cand, ans)
    ans = lax.fori_loop(0, 32, body, jnp.zeros(shape, I32))
    return ans ^ INT_MIN


def _ada_kernel(c_ref, w_ref, b_ref, o_ref):
    c = c_ref[...]
    a = (c * jax.nn.sigmoid(c)).astype(BF16)
    o_ref[...] = _dot(a, w_ref[...].astype(BF16)) + b_ref[...]


def _ada(c_all, w_ada, b_ada):
    rows = c_all.shape[0]
    width = w_ada.shape[-1]
    tn = _tile(width, 1024)
    return pl.pallas_call(
        _ada_kernel,
        out_shape=jax.ShapeDtypeStruct((DEPTH, rows, width), F32),
        grid=(DEPTH, width // tn),
        in_specs=[pl.BlockSpec((rows, D_MODEL), lambda l, j: (0, 0)),
                  pl.BlockSpec((None, D_MODEL, tn), lambda l, j: (l, 0, j)),
                  pl.BlockSpec((None, 1, tn), lambda l, j: (l, 0, j))],
        out_specs=pl.BlockSpec((None, rows, tn), lambda l, j: (l, 0, j)),
        compiler_params=_params(("parallel", "parallel")),
        name="ada",
    )(c_all, w_ada, b_ada.reshape(DEPTH, 1, width))


def _proj_kernel(x_ref, sh_ref, sc_ref, w_ref, o_ref, h_scr):
    @pl.when(pl.program_id(1) == 0)
    def _():
        h_scr[...] = (_ln(x_ref[...]) * (1.0 + sc_ref[0]) + sh_ref[0]).astype(BF16)
    o_ref[...] = _dot(h_scr[...], w_ref[...])


def _proj(x, shift, scale, w_pack, l, tm):
    n = x.shape[0]
    tiles_per_mod = (n // tm) // shift.shape[0]
    r = shift.shape[1]
    tn = 640
    mod_spec = pl.BlockSpec((1, r, D_MODEL), lambda i, j: (i // tiles_per_mod, 0, 0))
    return pl.pallas_call(
        _proj_kernel,
        out_shape=jax.ShapeDtypeStruct((n, PACK_WIDTH), F32),
        grid=(n // tm, PACK_WIDTH // tn),
        in_specs=[pl.BlockSpec((tm, D_MODEL), lambda i, j: (i, 0)),
                  mod_spec, mod_spec,
                  pl.BlockSpec((None, D_MODEL, tn), lambda i, j: (l, 0, j))],
        out_specs=pl.BlockSpec((tm, tn), lambda i, j: (i, j)),
        scratch_shapes=[pltpu.VMEM((tm, D_MODEL), BF16)],
        compiler_params=_params(("parallel", "arbitrary")),
        name="proj",
    )(x, shift, scale, w_pack)


ROW_CHUNK = 64
ATT_ROWS = 128


def _lane_fold(x, op):
    acc = x[:, :LANES]
    for i in range(1, x.shape[1] // LANES):
        acc = op(acc, x[:, i * LANES:(i + 1) * LANES])
    return acc


def _attn_prompt_kernel(q_ref, iq_ref, iwq_ref, k_ref, v_ref, ikw_ref, o_ref,
                        kb_scr, va_scr, ikb_scr, iqs_scr, qs_scr, key_scr, m_scr, acc_scr, xb_scr,
                        *, tq, n_sel, col_bits):
    qi = pl.program_id(1)
    tk = tq
    n_lane_tiles = tk // LANES

    @pl.when(qi == 0)
    def _():
        kb_scr[...] = k_ref[...].astype(BF16)
        for g in range(N_KV_HEADS):
            va_scr[:, 2 * g * HEAD_DIM:(2 * g + 1) * HEAD_DIM] = (
                v_ref[:, g * HEAD_DIM:(g + 1) * HEAD_DIM].astype(BF16))
            va_scr[:, (2 * g + 1) * HEAD_DIM:(2 * g + 2) * HEAD_DIM] = jnp.ones((va_scr.shape[0], HEAD_DIM), BF16)
        ikb_scr[...] = ikw_ref[:, :IDX_DIM].astype(BF16)

    iq = iq_ref[...]
    for h in range(N_IDX_HEADS):
        iqs_scr[h * tq:(h + 1) * tq, :] = iq[:, h * IDX_DIM:(h + 1) * IDX_DIM].astype(BF16)
    q = q_ref[...]
    for h in range(N_ATT_HEADS):
        qs_scr[h] = q[:, h * HEAD_DIM:(h + 1) * HEAD_DIM].astype(BF16)
    w = iwq_ref[:, IDX_DIM:IDX_DIM + N_IDX_HEADS] * (IDX_DIM ** -0.5 * N_IDX_HEADS ** -0.5)
    row = qi * tq + lax.broadcasted_iota(I32, (tq, tk), 0)
    col0 = lax.broadcasted_iota(I32, (tq, tk), 1)

    def score_block(kb, carry):
        start = pl.multiple_of(kb * tk, tk)
        lg = _dot_nt(iqs_scr[...], ikb_scr[pl.ds(start, tk), :])
        s = jnp.zeros((tq, tk), F32)
        for h in range(N_IDX_HEADS):
            s = s + w[:, h:h + 1] * jnp.maximum(lg[h * tq:(h + 1) * tq, :], 0.0)
        s = jnp.where(kb * tk + col0 <= row, s, MASK_VALUE)
        key_scr[kb] = _sort_key(s)
        return carry

    lax.fori_loop(0, qi + 1, score_block, 0)

    chunks = [slice(c * ROW_CHUNK, (c + 1) * ROW_CHUNK) for c in range(tq // ROW_CHUNK)]

    def count(make_pred):
        parts = []
        for rows in chunks:
            pred = make_pred(rows)

            def body(kb, part, pred=pred, rows=rows):
                hit = jnp.where(pred(kb, key_scr[kb, rows, :]), 1.0, 0.0)
                return part + _lane_fold(hit, jnp.add)

            parts.append(lax.fori_loop(0, qi + 1, body, jnp.zeros((ROW_CHUNK, LANES), F32)))
        part = jnp.concatenate(parts, axis=0)
        return jnp.broadcast_to(jnp.sum(part, axis=1, keepdims=True), (tq, LANES))

    def wide(x):
        return jnp.concatenate([x] * n_lane_tiles, axis=1)

    def bcast(x, rows):
        return wide(x[rows])

    def count_ge(cand):
        return count(lambda rows: (lambda kb, key, c=bcast(cand, rows): key >= c))

    thr = _kth_largest_key(count_ge, float(n_sel), (tq, LANES))

    cnt_gt = count(lambda rows: (lambda kb, key, c=bcast(thr, rows): key > c))
    cnt_ge = count_ge(thr)
    need = float(n_sel) - cnt_gt
    xb_scr[...] = jnp.full(xb_scr.shape, 2 ** 31 - 1, I32)

    @pl.when(jnp.max(jnp.abs(cnt_ge - float(n_sel))) > 0.0)
    def _():
        def body(i, x):
            cand = x | lax.shift_left(jnp.int32(1), jnp.int32(col_bits - 1) - i)

            def make_pred(rows):
                c_thr, c_cand = bcast(thr, rows), bcast(cand, rows)
                c_col = lax.broadcasted_iota(I32, (ROW_CHUNK, tk), 1)
                return lambda kb, key: (key == c_thr) & (kb * tk + c_col < c_cand)

            return jnp.where(count(make_pred) < need, cand, x)

        xb_scr[...] = lax.fori_loop(0, col_bits, body, jnp.zeros((tq, LANES), I32))

    xb_w, thr_w = wide(xb_scr[...]), wide(thr)

    def bias_block(kb, carry):
        key = key_scr[kb]
        col = kb * tk + col0
        sel = ((key > thr_w) | ((key == thr_w) & (col <= xb_w))) & (col <= row)
        key_scr[kb] = pltpu.bitcast(jnp.where(sel, 0.0, -jnp.inf), I32)
        return carry

    lax.fori_loop(0, qi + 1, bias_block, 0)

    m_scr[...] = jnp.full(m_scr.shape, MASK_VALUE, F32)
    acc_scr[...] = jnp.zeros(acc_scr.shape, F32)
    scale = HEAD_DIM ** -0.5

    def att_block(kb, carry):
        start = pl.multiple_of(kb * tk, tk)
        for g in range(N_KV_HEADS):
            kg = kb_scr[pl.ds(start, tk), g * HEAD_DIM:(g + 1) * HEAD_DIM]
            va = va_scr[pl.ds(start, tk), 2 * g * HEAD_DIM:(2 * g + 2) * HEAD_DIM]
            for r in range(KV_GROUP):
                hh = g * KV_GROUP + r
                for c in range(tq // ATT_ROWS):
                    rows = slice(c * ATT_ROWS, (c + 1) * ATT_ROWS)
                    bias = pltpu.bitcast(key_scr[kb, rows, :], F32)
                    s = _dot_nt(qs_scr[hh, rows, :], kg) * scale + bias
                    m_old = m_scr[hh, rows, :]
                    m_new = jnp.maximum(m_old, jnp.max(_lane_fold(s, jnp.maximum), axis=1, keepdims=True))
                    p = jnp.exp(s - jnp.concatenate([m_new] * n_lane_tiles, axis=1))
                    a = jnp.exp(m_old - m_new)
                    acc_scr[hh, rows, :] = (jnp.concatenate([a, a], axis=1) * acc_scr[hh, rows, :]
                                            + _dot(p.astype(BF16), va))
                    m_scr[hh, rows, :] = m_new
        return carry

    lax.fori_loop(0, qi + 1, att_block, 0)
    for h in range(N_ATT_HEADS):
        acc = acc_scr[h]
        o_ref[:, h * HEAD_DIM:(h + 1) * HEAD_DIM] = (acc[:, :HEAD_DIM] / acc[:, HEAD_DIM:]).astype(o_ref.dtype)


def _attn_prompt(proj, b, t):
    tq = _tile(t, 256)
    n_sel = min(TOPK_MAX, t // 4)
    proj3 = proj.reshape(b, t, PACK_WIDTH)
    kern = functools.partial(_attn_prompt_kernel, tq=tq, n_sel=n_sel, col_bits=t.bit_length())
    out = pl.pallas_call(
        kern,
        out_shape=jax.ShapeDtypeStruct((b, t, ATT_WIDTH), BF16),
        grid=(b, t // tq),
        in_specs=[pl.BlockSpec((None, tq, ATT_WIDTH), lambda bi, qi: (bi, qi, OFF_Q // ATT_WIDTH)),
                  pl.BlockSpec((None, tq, ATT_WIDTH), lambda bi, qi: (bi, qi, OFF_IQ // ATT_WIDTH)),
                  pl.BlockSpec((None, tq, LANES), lambda bi, qi: (bi, qi, OFF_IKW // LANES)),
                  pl.BlockSpec((None, t, KV_WIDTH), lambda bi, qi: (bi, 0, OFF_K // KV_WIDTH)),
                  pl.BlockSpec((None, t, KV_WIDTH), lambda bi, qi: (bi, 0, OFF_V // KV_WIDTH)),
                  pl.BlockSpec((None, t, LANES), lambda bi, qi: (bi, 0, OFF_IKW // LANES))],
        out_specs=pl.BlockSpec((None, tq, ATT_WIDTH), lambda bi, qi: (bi, qi, 0)),
        scratch_shapes=[pltpu.VMEM((t, KV_WIDTH), BF16),
                        pltpu.VMEM((t, 2 * KV_WIDTH), BF16),
                        pltpu.VMEM((t, IDX_DIM), BF16),
                        pltpu.VMEM((N_IDX_HEADS * tq, IDX_DIM), BF16),
                        pltpu.VMEM((N_ATT_HEADS, tq, HEAD_DIM), BF16),
                        pltpu.VMEM((t // tq, tq, tq), I32),
                        pltpu.VMEM((N_ATT_HEADS, tq, LANES), F32),
                        pltpu.VMEM((N_ATT_HEADS, tq, 2 * HEAD_DIM), F32),
                        pltpu.VMEM((tq, LANES), I32)],
        compiler_params=_params(("parallel", "arbitrary")),
        name="attn_prompt",
    )(proj3, proj3, proj3, proj3, proj3, proj3)
    return out.reshape(b * t, ATT_WIDTH)


def _hgrn_gates(hq, hf, lb):
    q = hq * jax.nn.sigmoid(hq)
    f = lb + (1.0 - lb) * jax.nn.sigmoid(hf)
    kk = (1.0 - lb) * jax.nn.sigmoid(-hf)
    return q, f, kk


def _hgrn_finish(o, hg, ng):
    o = o * lax.rsqrt(jnp.mean(o * o, axis=-1, keepdims=True) + LN_EPS)
    return o * ng * (hg * jax.nn.sigmoid(hg))


def _hgrn_prompt_kernel(hq_ref, hf_ref, hi_ref, hg_ref, lb_ref, ng_ref, o_ref, s_ref, st_scr, oc_scr,
                        *, n_chunks):
    c_len = HGRN_CHUNK
    sub = HGRN_SUB
    ti = pl.program_id(2)

    @pl.when(ti == 0)
    def _():
        st_scr[...] = jnp.zeros(st_scr.shape, F32)

    lb = lb_ref[...]
    ng = ng_ref[...]
    r_i = lax.broadcasted_iota(I32, (c_len, c_len), 0)
    c_i = lax.broadcasted_iota(I32, (c_len, c_len), 1)
    tri = jnp.where(c_i <= r_i, 1.0, 0.0).astype(BF16)
    ones = jnp.ones((HGRN_DK, HGRN_DV), BF16)

    def chunk(c, carry):
        sl = pl.ds(pl.multiple_of(c * c_len, c_len), c_len)
        q, f, kk = _hgrn_gates(hq_ref[sl, :], hf_ref[sl, :], lb)
        v = hi_ref[sl, :]
        l1, l2, l3 = _split3(jnp.log(f))
        b = _dot(tri, l1) + _dot(tri, l2) + _dot(tri, l3)
        st = st_scr[...]
        oc_scr[...] = _dot_nt((q * jnp.exp(b)).astype(BF16), st.astype(BF16))
        for j in range(c_len // sub):
            lo = j * sub
            nt = c_len - lo
            qt, bt = q[lo:], b[lo:]
            t_idx = lax.broadcasted_iota(I32, (nt, 1), 0)
            ps = []
            for s in range(sub):
                valid = t_idx >= s
                dec = jnp.exp(jnp.where(valid, bt - b[lo + s:lo + s + 1, :], 0.0))
                ps.append(jnp.where(valid, qt * kk[lo + s:lo + s + 1, :] * dec, 0.0).astype(BF16))
            rs = _dot(jnp.concatenate(ps, axis=0), ones)
            acc = jnp.zeros((nt, HGRN_DV), F32)
            for s in range(sub):
                acc = acc + rs[s * nt:(s + 1) * nt, :] * v[lo + s:lo + s + 1, :]
            oc_scr[lo:, :] += acc
        b_last = b[c_len - 1:c_len, :]
        kd = (kk * jnp.exp(b_last - b)).astype(BF16)
        upd = lax.dot_general(v.astype(BF16), kd, (((0,), (0,)), ((), ())), preferred_element_type=F32)
        st_scr[...] = st * jnp.exp(b_last) + upd
        o_ref[sl, :] = _hgrn_finish(oc_scr[...], hg_ref[sl, :], ng).astype(o_ref.dtype)
        return carry

    lax.fori_loop(0, n_chunks, chunk, 0)

    @pl.when(ti == pl.num_programs(2) - 1)
    def _():
        s_ref[...] = st_scr[...].T


def _hgrn_prompt(proj, lb, ng, b, t):
    tc = _tile(t, 512)
    n_chunks = tc // HGRN_CHUNK
    proj3 = proj.reshape(b, t, PACK_WIDTH)

    def col(off):
        return pl.BlockSpec((None, tc, HGRN_DK), lambda bi, h, ti: (bi, ti, off // HGRN_DK + h))

    vec = pl.BlockSpec((None, 1, HGRN_DK), lambda bi, h, ti: (h, 0, 0))
    o, s = pl.pallas_call(
        functools.partial(_hgrn_prompt_kernel, n_chunks=n_chunks),
        out_shape=(jax.ShapeDtypeStruct((b, t, HGRN_WIDTH), BF16),
                   jax.ShapeDtypeStruct((b, N_HGRN_HEADS, HGRN_DK, HGRN_DV), F32)),
        grid=(b, N_HGRN_HEADS, t // tc),
        in_specs=[col(OFF_HQ), col(OFF_HF), col(OFF_HI), col(OFF_HG), vec, vec],
        out_specs=(pl.BlockSpec((None, tc, HGRN_DV), lambda bi, h, ti: (bi, ti, h)),
                   pl.BlockSpec((None, None, HGRN_DK, HGRN_DV), lambda bi, h, ti: (bi, h, 0, 0))),
        scratch_shapes=[pltpu.VMEM((HGRN_DV, HGRN_DK), F32),
                        pltpu.VMEM((HGRN_CHUNK, HGRN_DV), F32)],
        compiler_params=_params(("parallel", "parallel", "arbitrary")),
        name="hgrn_prompt",
    )(proj3, proj3, proj3, proj3, lb.reshape(N_HGRN_HEADS, 1, HGRN_DK), ng.reshape(N_HGRN_HEADS, 1, HGRN_DV))
    return o.reshape(b * t, HGRN_WIDTH), s


def _merge_kernel(oa_ref, ob_ref, ga_ref, gb_ref, wa_ref, wb_ref, o_ref):
    ya = _dot(oa_ref[...].astype(BF16), wa_ref[...])
    yb = _dot(ob_ref[...].astype(BF16), wb_ref[...])
    o_ref[...] = (jax.nn.sigmoid(ga_ref[...]) * ya + jax.nn.sigmoid(gb_ref[...]) * yb).astype(o_ref.dtype)


def _merge(o_a, o_b, proj, w_up_a, w_up_b, l, tm):
    n = o_a.shape[0]
    return pl.pallas_call(
        _merge_kernel,
        out_shape=jax.ShapeDtypeStruct((n, D_MODEL), BF16),
        grid=(n // tm,),
        in_specs=[pl.BlockSpec((tm, ATT_WIDTH), lambda i: (i, 0)),
                  pl.BlockSpec((tm, HGRN_WIDTH), lambda i: (i, 0)),
                  pl.BlockSpec((tm, D_MODEL), lambda i: (i, OFF_GA // D_MODEL)),
                  pl.BlockSpec((tm, D_MODEL), lambda i: (i, OFF_GB // D_MODEL)),
                  pl.BlockSpec((None, ATT_WIDTH, D_MODEL), lambda i: (l, 0, 0)),
                  pl.BlockSpec((None, HGRN_WIDTH, D_MODEL), lambda i: (l, 0, 0))],
        out_specs=pl.BlockSpec((tm, D_MODEL), lambda i: (i, 0)),
        compiler_params=_params(("parallel",)),
        name="merge",
    )(o_a, o_b, proj, proj, w_up_a, w_up_b)


def _post_norm(x, gate, y, g, b):
    return _ln(ALPHA * x + gate * y) * g + b


def _out_kernel(m_ref, w_ref, x_ref, gate_ref, lng_ref, lnb_ref, o_ref):
    y = _dot(m_ref[...], w_ref[...])
    o_ref[...] = _post_norm(x_ref[...], gate_ref[0], y, lng_ref[...], lnb_ref[...])


def _out_proj(merged, w_o, x, gate, ln_g, ln_b, l, tm):
    n = x.shape[0]
    tiles_per_mod = (n // tm) // gate.shape[0]
    vec = pl.BlockSpec((None, 1, D_MODEL), lambda i: (l, 0, 0))
    return pl.pallas_call(
        _out_kernel,
        out_shape=jax.ShapeDtypeStruct((n, D_MODEL), F32),
        grid=(n // tm,),
        in_specs=[pl.BlockSpec((tm, D_MODEL), lambda i: (i, 0)),
                  pl.BlockSpec((None, D_MODEL, D_MODEL), lambda i: (l, 0, 0)),
                  pl.BlockSpec((tm, D_MODEL), lambda i: (i, 0)),
                  pl.BlockSpec((1, gate.shape[1], D_MODEL), lambda i: (i // tiles_per_mod, 0, 0)),
                  vec, vec],
        out_specs=pl.BlockSpec((tm, D_MODEL), lambda i: (i, 0)),
        compiler_params=_params(("parallel",)),
        name="out_proj",
    )(merged, w_o, x, gate, ln_g.reshape(DEPTH, 1, D_MODEL), ln_b.reshape(DEPTH, 1, D_MODEL))


def _route(h, wr, br):
    h1, h2, _ = _split3(h)
    w1, w2, _ = _split3(wr)
    logits = _dot_nt(w1, h1) + _dot_nt(w1, h2) + _dot_nt(w2, h1)
    aff = jax.nn.sigmoid(logits)
    sel = aff + br
    rows = [sel[e:e + 1, :] for e in range(N_EXPERTS)]
    grp = []
    for g in range(N_GROUPS):
        a, b, c, d = rows[g * EXPERTS_PER_GROUP:(g + 1) * EXPERTS_PER_GROUP]
        hi1, lo1 = jnp.maximum(a, b), jnp.minimum(a, b)
        hi2, lo2 = jnp.maximum(c, d), jnp.minimum(c, d)
        grp.append(jnp.maximum(hi1, hi2) + jnp.maximum(jnp.minimum(hi1, hi2), jnp.maximum(lo1, lo2)))
    best = functools.reduce(jnp.maximum, grp)
    taken = jnp.zeros_like(best)
    picked = []
    for g in range(N_GROUPS):
        is_g = jnp.where(grp[g] == best, 1.0, 0.0) * (1.0 - taken)
        taken = taken + is_g
        for e in range(g * EXPERTS_PER_GROUP, (g + 1) * EXPERTS_PER_GROUP):
            rank = jnp.zeros_like(best)
            for o in range(g * EXPERTS_PER_GROUP, (g + 1) * EXPERTS_PER_GROUP):
                if o < e:
                    rank = rank + jnp.where(rows[o] >= rows[e], 1.0, 0.0)
                elif o > e:
                    rank = rank + jnp.where(rows[o] > rows[e], 1.0, 0.0)
            picked.append(is_g * jnp.where(rank < 2.0, 1.0, 0.0))
    picked = jnp.concatenate(picked, axis=0)
    gate = picked * aff
    return picked, gate / jnp.sum(gate, axis=0, keepdims=True)


def _router_kernel(x_ref, sh_ref, sc_ref, wr_ref, br_ref, h_ref, g_ref):
    h = _ln(x_ref[...]) * (1.0 + sc_ref[0]) + sh_ref[0]
    h_ref[...] = h.astype(h_ref.dtype)
    _, g_ref[...] = _route(h, wr_ref[...], br_ref[...])


def _router_sorted_kernel(x_ref, sh_ref, sc_ref, wr_ref, br_ref, h_ref, eid_ref, gw_ref, pos_ref, cnt_ref, base_scr):
    tm = x_ref.shape[0]

    @pl.when(pl.program_id(0) == 0)
    def _():
        base_scr[...] = jnp.zeros(base_scr.shape, F32)

    h = _ln(x_ref[...]) * (1.0 + sc_ref[0]) + sh_ref[0]
    h_ref[...] = h.astype(h_ref.dtype)
    picked, gate = _route(h, wr_ref[...], br_ref[...])
    r = lax.broadcasted_iota(I32, (tm, tm), 0)
    c = lax.broadcasted_iota(I32, (tm, tm), 1)
    earlier = jnp.where(r < c, 1.0, 0.0).astype(BF16)
    base = base_scr[...]
    rank = _dot(picked.astype(BF16), earlier) + jnp.concatenate([base] * (tm // LANES), axis=1)
    base_scr[...] = base + jnp.sum(picked, axis=1, keepdims=True)
    cnt_ref[...] = base_scr[...]
    eio = lax.broadcasted_iota(I32, picked.shape, 0).astype(F32)
    e_lo = jnp.min(jnp.where(picked > 0.0, eio, float(N_EXPERTS)), axis=0, keepdims=True)
    e_hi = jnp.max(jnp.where(picked > 0.0, eio, -1.0), axis=0, keepdims=True)

    def pick(e, x):
        return jnp.sum(jnp.where(eio == e, x, 0.0), axis=0, keepdims=True)

    eid_ref[...] = jnp.concatenate([e_lo, e_hi], axis=0).astype(I32)
    gw_ref[...] = jnp.concatenate([pick(e_lo, gate), pick(e_hi, gate)], axis=0)
    pos_ref[...] = jnp.concatenate([pick(e_lo, rank), pick(e_hi, rank)], axis=0).astype(I32)


def _router_sorted(x, shift, scale, w_router_t, b_router, tm):
    n = x.shape[0]
    tiles_per_mod = (n // tm) // shift.shape[0]
    mod_spec = pl.BlockSpec((1, shift.shape[1], D_MODEL), lambda i: (i // tiles_per_mod, 0, 0))
    pair = pl.BlockSpec((2, tm), lambda i: (0, i))
    return pl.pallas_call(
        _router_sorted_kernel,
        out_shape=(jax.ShapeDtypeStruct((n, D_MODEL), F32),
                   jax.ShapeDtypeStruct((2, n), I32),
                   jax.ShapeDtypeStruct((2, n), F32),
                   jax.ShapeDtypeStruct((2, n), I32),
                   jax.ShapeDtypeStruct((N_EXPERTS, LANES), F32)),
        grid=(n // tm,),
        in_specs=[pl.BlockSpec((tm, D_MODEL), lambda i: (i, 0)),
                  mod_spec, mod_spec,
                  pl.BlockSpec((N_EXPERTS, D_MODEL), lambda i: (0, 0)),
                  pl.BlockSpec((N_EXPERTS, 1), lambda i: (0, 0))],
        out_specs=(pl.BlockSpec((tm, D_MODEL), lambda i: (i, 0)), pair, pair, pair,
                   pl.BlockSpec((N_EXPERTS, LANES), lambda i: (0, 0))),
        scratch_shapes=[pltpu.VMEM((N_EXPERTS, LANES), F32)],
        compiler_params=_params(("arbitrary",)),
        name="router_sorted",
    )(x, shift, scale, w_router_t, b_router.reshape(N_EXPERTS, 1))


MOE_TILE = 256
DMA_UNROLL = 8


def _dispatch_kernel(dst_ref, h_ref, xs_in_ref, xs_ref, sem, *, td, n):
    del xs_in_ref
    first = pl.program_id(0) * td

    def copies(r):
        return [pltpu.make_async_copy(h_ref.at[pl.ds(r, 1)], xs_ref.at[pl.ds(dst_ref[k * n + first + r], 1)], sem)
                for k in range(2)]

    def start(r, carry):
        for cp in copies(r):
            cp.start()
        return carry

    def wait(r, carry):
        for cp in copies(r):
            cp.wait()
        return carry

    lax.fori_loop(0, td, start, 0, unroll=DMA_UNROLL)
    lax.fori_loop(0, td, wait, 0, unroll=DMA_UNROLL)


def _dispatch(h, dst_flat, n_slots, td):
    n = h.shape[0]
    grid_spec = pltpu.PrefetchScalarGridSpec(
        num_scalar_prefetch=1,
        grid=(n // td,),
        in_specs=[pl.BlockSpec((td, D_MODEL), lambda i, dst: (i, 0)),
                  pl.BlockSpec(memory_space=pl.ANY)],
        out_specs=pl.BlockSpec(memory_space=pl.ANY),
        scratch_shapes=[pltpu.SemaphoreType.DMA])
    return pl.pallas_call(
        functools.partial(_dispatch_kernel, td=td, n=n),
        out_shape=jax.ShapeDtypeStruct((n_slots, D_MODEL), F32),
        grid_spec=grid_spec,
        input_output_aliases={2: 0},
        compiler_params=_params(("arbitrary",)),
        name="moe_dispatch",
    )(dst_flat, h, jnp.zeros((n_slots, D_MODEL), F32))


def _experts_kernel(te_ref, tv_ref, xs_ref, w1_ref, w3_ref, w2_ref, ys_ref):
    del te_ref
    valid = tv_ref[pl.program_id(0)] > 0

    @pl.when(valid)
    def _():
        x = xs_ref[...].astype(BF16)
        a = _dot(x, w1_ref[...])
        b = _dot(x, w3_ref[...])
        ys_ref[...] = _dot((a * jax.nn.sigmoid(a) * b).astype(BF16), w2_ref[...])

    @pl.when(jnp.logical_not(valid))
    def _():
        ys_ref[...] = jnp.zeros(ys_ref.shape, F32)


def _experts(xs, tile_expert, tile_valid, w1, w3, w2, l):
    n_tiles = xs.shape[0] // MOE_TILE
    grid_spec = pltpu.PrefetchScalarGridSpec(
        num_scalar_prefetch=2,
        grid=(n_tiles,),
        in_specs=[pl.BlockSpec((MOE_TILE, D_MODEL), lambda i, te, tv: (i, 0)),
                  pl.BlockSpec((None, None, D_MODEL, EXPERT_DFF), lambda i, te, tv: (l, te[i], 0, 0)),
                  pl.BlockSpec((None, None, D_MODEL, EXPERT_DFF), lambda i, te, tv: (l, te[i], 0, 0)),
                  pl.BlockSpec((None, None, EXPERT_DFF, D_MODEL), lambda i, te, tv: (l, te[i], 0, 0))],
        out_specs=pl.BlockSpec((MOE_TILE, D_MODEL), lambda i, te, tv: (i, 0)))
    return pl.pallas_call(
        _experts_kernel,
        out_shape=jax.ShapeDtypeStruct(xs.shape, F32),
        grid_spec=grid_spec,
        compiler_params=_params(("arbitrary",)),
        name="moe_experts",
    )(tile_expert, tile_valid, xs, w1, w3, w2)


def _combine_kernel(dst_ref, ys_ref, gw_ref, x_ref, gate_ref, lng_ref, lnb_ref, o_ref, buf, sem, *, tc, n):
    i = pl.program_id(0)
    slot = i % 2

    def copies(tile, s, r):
        return [pltpu.make_async_copy(ys_ref.at[pl.ds(dst_ref[k * n + tile * tc + r], 1)],
                                      buf.at[s, k, pl.ds(r, 1)], sem.at[s]) for k in range(2)]

    def start_tile(tile, s):
        def body(r, carry):
            for cp in copies(tile, s, r):
                cp.start()
            return carry
        lax.fori_loop(0, tc, body, 0, unroll=DMA_UNROLL)

    def wait_tile(tile, s):
        def body(r, carry):
            for cp in copies(tile, s, r):
                cp.wait()
            return carry
        lax.fori_loop(0, tc, body, 0, unroll=DMA_UNROLL)

    @pl.when(i == 0)
    def _():
        start_tile(0, 0)

    @pl.when(i + 1 < pl.num_programs(0))
    def _():
        start_tile(i + 1, 1 - slot)

    wait_tile(i, slot)
    gw = gw_ref[...]
    y = gw[:, 0:1] * buf[slot, 0] + gw[:, 1:2] * buf[slot, 1]
    o_ref[...] = _post_norm(x_ref[...], gate_ref[0], y, lng_ref[...], lnb_ref[...])


def _combine(ys, dst_flat, gw, x, gate, ln_g, ln_b, l, tc):
    n = x.shape[0]
    tiles_per_mod = (n // tc) // gate.shape[0]
    vec = pl.BlockSpec((None, 1, D_MODEL), lambda i, dst: (l, 0, 0))
    grid_spec = pltpu.PrefetchScalarGridSpec(
        num_scalar_prefetch=1,
        grid=(n // tc,),
        in_specs=[pl.BlockSpec(memory_space=pl.ANY),
                  pl.BlockSpec((tc, 2), lambda i, dst: (i, 0)),
                  pl.BlockSpec((tc, D_MODEL), lambda i, dst: (i, 0)),
                  pl.BlockSpec((1, gate.shape[1], D_MODEL), lambda i, dst: (i // tiles_per_mod, 0, 0)),
                  vec, vec],
        out_specs=pl.BlockSpec((tc, D_MODEL), lambda i, dst: (i, 0)),
        scratch_shapes=[pltpu.VMEM((2, 2, tc, D_MODEL), F32),
                        pltpu.SemaphoreType.DMA((2,))])
    return pl.pallas_call(
        functools.partial(_combine_kernel, tc=tc, n=n),
        out_shape=jax.ShapeDtypeStruct((n, D_MODEL), F32),
        grid_spec=grid_spec,
        compiler_params=_params(("arbitrary",)),
        name="moe_combine",
    )(dst_flat, ys, gw, x, gate, ln_g.reshape(DEPTH, 1, D_MODEL), ln_b.reshape(DEPTH, 1, D_MODEL))


def _moe_grouped(x, shift, scale, gate, w_router_t, b_router, w1, w3, w2, ln_g, ln_b, l, tm):
    n = x.shape[0]
    h, eid, gw, pos, cnt = _router_sorted(x, shift, scale, w_router_t, b_router, tm)
    counts = cnt[:, 0].astype(I32)
    padded = (counts + MOE_TILE - 1) // MOE_TILE * MOE_TILE
    ends = jnp.cumsum(padded)
    dst_flat = ((ends - padded)[eid] + pos).reshape(2 * n)
    n_tiles = 2 * n // MOE_TILE + N_EXPERTS
    tile_start = jnp.arange(n_tiles, dtype=I32) * MOE_TILE
    tile_expert = jnp.minimum(jnp.searchsorted(ends, tile_start, side="right"), N_EXPERTS - 1).astype(I32)
    tile_valid = (tile_start < ends[-1]).astype(I32)
    xs = _dispatch(h, dst_flat, n_tiles * MOE_TILE, tm)
    ys = _experts(xs, tile_expert, tile_valid, w1, w3, w2, l)
    return _combine(ys, dst_flat, gw.T, x, gate, ln_g, ln_b, l, MOE_TILE)


def _router(x, shift, scale, w_router_t, b_router, tm):
    n = x.shape[0]
    tiles_per_mod = (n // tm) // shift.shape[0]
    mod_spec = pl.BlockSpec((1, shift.shape[1], D_MODEL), lambda i: (i // tiles_per_mod, 0, 0))
    return pl.pallas_call(
        _router_kernel,
        out_shape=(jax.ShapeDtypeStruct((n, D_MODEL), BF16),
                   jax.ShapeDtypeStruct((N_EXPERTS, n), F32)),
        grid=(n // tm,),
        in_specs=[pl.BlockSpec((tm, D_MODEL), lambda i: (i, 0)),
                  mod_spec, mod_spec,
                  pl.BlockSpec((N_EXPERTS, D_MODEL), lambda i: (0, 0)),
                  pl.BlockSpec((N_EXPERTS, 1), lambda i: (0, 0))],
        out_specs=(pl.BlockSpec((tm, D_MODEL), lambda i: (i, 0)),
                   pl.BlockSpec((N_EXPERTS, tm), lambda i: (0, i))),
        compiler_params=_params(("parallel",)),
        name="router",
    )(x, shift, scale, w_router_t, b_router.reshape(N_EXPERTS, 1))


def _moe_kernel(h_ref, g_ref, w1_ref, w3_ref, w2_ref, x_ref, gate_ref, lng_ref, lnb_ref, o_ref, acc_scr):
    e = pl.program_id(1)

    @pl.when(e == 0)
    def _():
        acc_scr[...] = jnp.zeros(acc_scr.shape, F32)

    h = h_ref[...]
    a = _dot(h, w1_ref[...])
    b = _dot(h, w3_ref[...])
    gates = g_ref[...]
    lane = lax.broadcasted_iota(I32, gates.shape, 1)
    gcol = jnp.sum(jnp.where(lane == e, gates, 0.0), axis=1, keepdims=True)
    u = (a * jax.nn.sigmoid(a) * b * gcol).astype(BF16)
    acc_scr[...] += _dot(u, w2_ref[...])

    @pl.when(e == pl.num_programs(1) - 1)
    def _():
        o_ref[...] = _post_norm(x_ref[...], gate_ref[0], acc_scr[...], lng_ref[...], lnb_ref[...])


def _moe(h, gates, w1, w3, w2, x, gate, ln_g, ln_b, l, tm):
    n = x.shape[0]
    tiles_per_mod = (n // tm) // gate.shape[0]
    vec = pl.BlockSpec((None, 1, D_MODEL), lambda i, e: (l, 0, 0))
    return pl.pallas_call(
        _moe_kernel,
        out_shape=jax.ShapeDtypeStruct((n, D_MODEL), F32),
        grid=(n // tm, N_EXPERTS),
        in_specs=[pl.BlockSpec((tm, D_MODEL), lambda i, e: (i, 0)),
                  pl.BlockSpec((tm, N_EXPERTS), lambda i, e: (i, 0)),
                  pl.BlockSpec((None, None, D_MODEL, EXPERT_DFF), lambda i, e: (l, e, 0, 0)),
                  pl.BlockSpec((None, None, D_MODEL, EXPERT_DFF), lambda i, e: (l, e, 0, 0)),
                  pl.BlockSpec((None, None, EXPERT_DFF, D_MODEL), lambda i, e: (l, e, 0, 0)),
                  pl.BlockSpec((tm, D_MODEL), lambda i, e: (i, 0)),
                  pl.BlockSpec((1, gate.shape[1], D_MODEL), lambda i, e: (i // tiles_per_mod, 0, 0)),
                  vec, vec],
        out_specs=pl.BlockSpec((tm, D_MODEL), lambda i, e: (i, 0)),
        scratch_shapes=[pltpu.VMEM((tm, D_MODEL), F32)],
        compiler_params=_params(("parallel", "arbitrary")),
        name="moe",
    )(h, gates, w1, w3, w2, x, gate, ln_g.reshape(DEPTH, 1, D_MODEL), ln_b.reshape(DEPTH, 1, D_MODEL))


def _row_to_col(v):
    n = v.shape[1]
    r = lax.broadcasted_iota(I32, (n, n), 0)
    c = lax.broadcasted_iota(I32, (n, n), 1)
    return jnp.sum(jnp.where(r == c, jnp.broadcast_to(v, (n, n)), 0.0), axis=1, keepdims=True)


def _attn_decode_kernel(pt_ref, proj_ref, ck_ref, cv_ref, cik_ref, o_ref,
                        ik_buf, k_buf, v_buf, key_scr, sem_ik, sem_k, sem_v,
                        *, l, n_pages, page, chunk_pages, n_sel):
    b = pl.program_id(0)
    n_chunks = n_pages // chunk_pages
    chunk = chunk_pages * page

    def ik_copy(j):
        return pltpu.make_async_copy(cik_ref.at[l, pt_ref[b, j]], ik_buf.at[pl.ds(j * page, page)], sem_ik)

    def kv_copies(c, j, slot):
        pg = pt_ref[b, c * chunk_pages + j]
        dst = pl.ds(j * page, page)
        copies = []
        for g in range(N_KV_HEADS):
            copies.append(pltpu.make_async_copy(ck_ref.at[l, pg, :, g, :], k_buf.at[slot, g, dst], sem_k.at[slot]))
            copies.append(pltpu.make_async_copy(cv_ref.at[l, pg, :, g, :], v_buf.at[slot, g, dst], sem_v.at[slot]))
        return copies

    def start_chunk(c, slot):
        def body(j, carry):
            for cp in kv_copies(c, j, slot):
                cp.start()
            return carry
        lax.fori_loop(0, chunk_pages, body, 0)

    def wait_chunk(c, slot):
        def body(j, carry):
            for cp in kv_copies(c, j, slot):
                cp.wait()
            return carry
        lax.fori_loop(0, chunk_pages, body, 0)

    def start_ik(j, carry):
        ik_copy(j).start()
        return carry

    def wait_ik(j, carry):
        ik_copy(j).wait()
        return carry

    lax.fori_loop(0, n_pages, start_ik, 0)
    start_chunk(0, 0)
    lax.fori_loop(0, n_pages, wait_ik, 0)

    rowsl = slice(None)
    iq = proj_ref[rowsl, OFF_IQ:OFF_IQ + N_IDX_HEADS * IDX_DIM]
    iq_h = jnp.concatenate([iq[:, h * IDX_DIM:(h + 1) * IDX_DIM] for h in range(N_IDX_HEADS)], axis=0)
    w_row = proj_ref[rowsl, OFF_IKW + IDX_DIM:OFF_IKW + IDX_DIM + N_IDX_HEADS]
    w_col = _row_to_col(w_row) * (IDX_DIM ** -0.5 * N_IDX_HEADS ** -0.5)
    ik_new = proj_ref[rowsl, OFF_IKW:OFF_IKW + IDX_DIM]

    iq_b = iq_h.astype(BF16)
    for c in range(n_chunks):
        lg = _dot_nt(iq_b, ik_buf[c * chunk:(c + 1) * chunk, :].astype(BF16))
        s = jnp.sum(w_col * jnp.maximum(lg, 0.0), axis=0, keepdims=True)
        key_scr[c] = _sort_key(s)
    lg_new = jnp.sum(iq_b.astype(F32) * ik_new.astype(BF16).astype(F32), axis=1, keepdims=True)
    key_new = _sort_key(jnp.sum(w_col * jnp.maximum(lg_new, 0.0), axis=0, keepdims=True))

    def count_ge(cand):
        cnt = jnp.where(key_new >= cand, 1.0, 0.0)
        for c in range(n_chunks):
            cnt = cnt + jnp.sum(jnp.where(key_scr[c] >= cand, 1.0, 0.0), axis=1, keepdims=True)
        return cnt

    thr = _kth_largest_key(count_ge, float(n_sel), (1, 1))

    q = proj_ref[rowsl, OFF_Q:OFF_Q + ATT_WIDTH]
    q_h = jnp.concatenate([q[:, h * HEAD_DIM:(h + 1) * HEAD_DIM] for h in range(N_ATT_HEADS)], axis=0)
    q_b = q_h.astype(BF16)
    k_new = proj_ref[rowsl, OFF_K:OFF_K + KV_WIDTH]
    v_new = proj_ref[rowsl, OFF_V:OFF_V + KV_WIDTH]
    scale = HEAD_DIM ** -0.5

    def att_chunk(c, carry):
        slot = c % 2

        @pl.when(c + 1 < n_chunks)
        def _():
            start_chunk(c + 1, 1 - slot)

        wait_chunk(c, slot)
        sel = key_scr[c] >= thr
        out = []
        for g in range(N_KV_HEADS):
            m_old, l_old, acc_old = carry[g]
            kg = k_buf[slot, g].astype(BF16)
            vg = v_buf[slot, g].astype(BF16)
            s = _dot_nt(q_b[g * KV_GROUP:(g + 1) * KV_GROUP], kg) * scale
            s = jnp.where(sel, s, MASK_VALUE)
            m_new = jnp.maximum(m_old, jnp.max(s, axis=1, keepdims=True))
            p = jnp.where(sel, jnp.exp(s - m_new), 0.0)
            a = jnp.exp(m_old - m_new)
            out.append((m_new, a * l_old + jnp.sum(p, axis=1, keepdims=True),
                        a * acc_old + _dot(p.astype(BF16), vg)))
        return tuple(out)

    init = tuple((jnp.full((KV_GROUP, 1), MASK_VALUE, F32), jnp.zeros((KV_GROUP, 1), F32),
                  jnp.zeros((KV_GROUP, HEAD_DIM), F32)) for _ in range(N_KV_HEADS))
    res = lax.fori_loop(0, n_chunks, att_chunk, init)

    sel_new = key_new >= thr
    for g in range(N_KV_HEADS):
        m_old, l_old, acc_old = res[g]
        kg = k_new[:, g * HEAD_DIM:(g + 1) * HEAD_DIM].astype(BF16).astype(F32)
        vg = v_new[:, g * HEAD_DIM:(g + 1) * HEAD_DIM].astype(BF16).astype(F32)
        qg = q_b[g * KV_GROUP:(g + 1) * KV_GROUP].astype(F32)
        s = jnp.sum(qg * kg, axis=1, keepdims=True) * scale
        s = jnp.where(sel_new, s, MASK_VALUE)
        m_new = jnp.maximum(m_old, s)
        p = jnp.where(sel_new, jnp.exp(s - m_new), 0.0)
        a = jnp.exp(m_old - m_new)
        l_new = a * l_old + p
        acc = a * acc_old + p.astype(BF16).astype(F32) * vg
        o = acc / l_new
        for r in range(KV_GROUP):
            hh = g * KV_GROUP + r
            o_ref[rowsl, hh * HEAD_DIM:(hh + 1) * HEAD_DIM] = o[r:r + 1, :].astype(o_ref.dtype)


def _attn_decode(proj, page_table, cache_k, cache_v, cache_idx_k, l):
    db, n_pages = page_table.shape
    page = cache_k.shape[2]
    past = n_pages * page
    n_sel = min(TOPK_MAX, (past + 1) // 4)
    chunk_pages = _tile(n_pages, 16)
    kern = functools.partial(_attn_decode_kernel, l=l, n_pages=n_pages, page=page,
                             chunk_pages=chunk_pages, n_sel=n_sel)
    grid_spec = pltpu.PrefetchScalarGridSpec(
        num_scalar_prefetch=1,
        grid=(db,),
        in_specs=[pl.BlockSpec((None, 1, PACK_WIDTH), lambda b, pt: (b, 0, 0)),
                  pl.BlockSpec(memory_space=pl.ANY),
                  pl.BlockSpec(memory_space=pl.ANY),
                  pl.BlockSpec(memory_space=pl.ANY)],
        out_specs=pl.BlockSpec((None, 1, ATT_WIDTH), lambda b, pt: (b, 0, 0)),
        scratch_shapes=[pltpu.VMEM((past, IDX_DIM), F32),
                        pltpu.VMEM((2, N_KV_HEADS, chunk_pages * page, HEAD_DIM), F32),
                        pltpu.VMEM((2, N_KV_HEADS, chunk_pages * page, HEAD_DIM), F32),
                        pltpu.VMEM((n_pages // chunk_pages, 1, chunk_pages * page), I32),
                        pltpu.SemaphoreType.DMA,
                        pltpu.SemaphoreType.DMA((2,)),
                        pltpu.SemaphoreType.DMA((2,))])
    return pl.pallas_call(
        kern,
        out_shape=jax.ShapeDtypeStruct((db, 1, ATT_WIDTH), F32),
        grid_spec=grid_spec,
        compiler_params=_params(("arbitrary",)),
        name="attn_decode",
    )(page_table, proj.reshape(db, 1, PACK_WIDTH), cache_k, cache_v, cache_idx_k).reshape(db, ATT_WIDTH)


def _hgrn_decode_kernel(proj_ref, s0_ref, lb_ref, ng_ref, o_ref, s_ref):
    rowsl = slice(None)
    for h in range(N_HGRN_HEADS):
        cols = lambda off: slice(off + h * HGRN_DK, off + (h + 1) * HGRN_DK)
        lb = lb_ref[:, h * HGRN_DK:(h + 1) * HGRN_DK]
        q, f, kk = _hgrn_gates(proj_ref[rowsl, cols(OFF_HQ)], proj_ref[rowsl, cols(OFF_HF)], lb)
        v = proj_ref[rowsl, cols(OFF_HI)]
        s_new = _row_to_col(f) * s0_ref[h] + _row_to_col(kk) * v
        s_ref[h] = s_new
        o = jnp.sum(_row_to_col(q) * s_new, axis=0, keepdims=True)
        ng = ng_ref[:, h * HGRN_DV:(h + 1) * HGRN_DV]
        o_ref[rowsl, h * HGRN_DV:(h + 1) * HGRN_DV] = _hgrn_finish(
            o, proj_ref[rowsl, cols(OFF_HG)], ng).astype(o_ref.dtype)


def _hgrn_decode(proj, state, lb, ng, l):
    db = proj.shape[0]
    st_spec_in = pl.BlockSpec((None, None, N_HGRN_HEADS, HGRN_DK, HGRN_DV), lambda b: (l, b, 0, 0, 0))
    o, s = pl.pallas_call(
        _hgrn_decode_kernel,
        out_shape=(jax.ShapeDtypeStruct((db, 1, HGRN_WIDTH), F32),
                   jax.ShapeDtypeStruct((db, N_HGRN_HEADS, HGRN_DK, HGRN_DV), F32)),
        grid=(db,),
        in_specs=[pl.BlockSpec((None, 1, PACK_WIDTH), lambda b: (b, 0, 0)),
                  st_spec_in,
                  pl.BlockSpec((1, HGRN_WIDTH), lambda b: (0, 0)),
                  pl.BlockSpec((1, HGRN_WIDTH), lambda b: (0, 0))],
        out_specs=(pl.BlockSpec((None, 1, HGRN_WIDTH), lambda b: (b, 0, 0)),
                   pl.BlockSpec((None, N_HGRN_HEADS, HGRN_DK, HGRN_DV), lambda b: (b, 0, 0, 0))),
        compiler_params=_params(("parallel",)),
        name="hgrn_decode",
    )(proj.reshape(db, 1, PACK_WIDTH), state, lb.reshape(1, HGRN_WIDTH), ng.reshape(1, HGRN_WIDTH))
    return o.reshape(db, HGRN_WIDTH), s


def _pack_w_in(w_in):
    offs = [0]
    for s in IN_SPLITS:
        offs.append(offs[-1] + s)
    q, k, v, iq, ik, iw, hq, hf, hi, hg, ga, gb = [w_in[:, :, offs[i]:offs[i + 1]] for i in range(len(IN_SPLITS))]
    pad = jnp.zeros(w_in.shape[:2] + (LANES - IDX_DIM - N_IDX_HEADS,), w_in.dtype)
    return jnp.concatenate([ga, gb, q, iq, hq, hf, hi, hg, k, v, ik, iw, pad], axis=-1).astype(BF16)


def _mods(mod_l, rows, per_row):
    m = mod_l[rows]
    parts = jnp.split(m, 6, axis=-1)
    if per_row:
        return [p[None, :, :] for p in parts]
    return [p[:, None, :] for p in parts]


def kernel(x_prompt, x_sample, c_prompt, c_sample, cache_k, cache_v, cache_idx_k, state_hgrn, page_table,
           w_ada, b_ada, w_in, w_up_a, w_up_b, w_o, hgrn_norm_g, hgrn_lb_logits, ln1_g, ln1_b,
           w_router, b_router, w1, w3, w2, ln2_g, ln2_b):
    bp, t, _ = x_prompt.shape
    db = x_sample.shape[0]
    lbp = jax.nn.softmax(hgrn_lb_logits.astype(F32), axis=0)
    lower_bounds = jnp.cumsum(lbp, axis=0) - lbp[0]

    w_pack = _pack_w_in(w_in)
    w_up_a_b, w_up_b_b, w_o_b = w_up_a.astype(BF16), w_up_b.astype(BF16), w_o.astype(BF16)
    w1_b, w3_b, w2_b = w1.astype(BF16), w3.astype(BF16), w2.astype(BF16)
    w_router_t = w_router.T

    n_c = bp + db
    c_rows = -(-n_c // 8) * 8
    c_all = jnp.concatenate([c_prompt, c_sample, jnp.zeros((c_rows - n_c, D_MODEL), F32)], axis=0)
    mod = _ada(c_all, w_ada, b_ada)

    xp = x_prompt.reshape(bp * t, D_MODEL)
    xs = x_sample.reshape(db, D_MODEL)
    tm_p = _tile(bp * t, 512)
    outs_p = {"k": [], "v": [], "ik": [], "s": []}
    outs_s = {"k": [], "v": [], "ik": [], "s": []}
    for l in range(DEPTH):
        sh1, sc1, g1, sh2, sc2, g2 = _mods(mod[l], slice(0, bp), per_row=False)
        proj = _proj(xp, sh1, sc1, w_pack, l, tm_p)
        o_a = _attn_prompt(proj, bp, t)
        o_b, s_new = _hgrn_prompt(proj, lower_bounds[l], hgrn_norm_g[l], bp, t)
        merged = _merge(o_a, o_b, proj, w_up_a_b, w_up_b_b, l, tm_p)
        xp = _out_proj(merged, w_o_b, xp, g1, ln1_g, ln1_b, l, tm_p)
        xp = _moe_grouped(xp, sh2, sc2, g2, w_router_t, b_router, w1_b, w3_b, w2_b, ln2_g, ln2_b, l, tm_p)
        outs_p["k"].append(proj[:, OFF_K:OFF_K + KV_WIDTH].reshape(bp, t, N_KV_HEADS, HEAD_DIM))
        outs_p["v"].append(proj[:, OFF_V:OFF_V + KV_WIDTH].reshape(bp, t, N_KV_HEADS, HEAD_DIM))
        outs_p["ik"].append(proj[:, OFF_IKW:OFF_IKW + IDX_DIM].reshape(bp, t, IDX_DIM))
        outs_p["s"].append(s_new)

        sh1, sc1, g1, sh2, sc2, g2 = _mods(mod[l], slice(bp, bp + db), per_row=True)
        proj = _proj(xs, sh1, sc1, w_pack, l, db)
        o_a = _attn_decode(proj, page_table, cache_k, cache_v, cache_idx_k, l)
        o_b, s_new = _hgrn_decode(proj, state_hgrn, lower_bounds[l], hgrn_norm_g[l], l)
        merged = _merge(o_a, o_b, proj, w_up_a_b, w_up_b_b, l, db)
        xs = _out_proj(merged, w_o_b, xs, g1, ln1_g, ln1_b, l, db)
        h2, gates = _router(xs, sh2, sc2, w_router_t, b_router, db)
        xs = _moe(h2, gates.T, w1_b, w3_b, w2_b, xs, g2, ln2_g, ln2_b, l, db)
        outs_s["k"].append(proj[:, OFF_K:OFF_K + KV_WIDTH].reshape(db, 1, N_KV_HEADS, HEAD_DIM))
        outs_s["v"].append(proj[:, OFF_V:OFF_V + KV_WIDTH].reshape(db, 1, N_KV_HEADS, HEAD_DIM))
        outs_s["ik"].append(proj[:, OFF_IKW:OFF_IKW + IDX_DIM].reshape(db, 1, IDX_DIM))
        outs_s["s"].append(s_new)

    return (xp.reshape(bp, t, D_MODEL), xs.reshape(db, 1, D_MODEL),
            jnp.stack(outs_p["k"]), jnp.stack(outs_p["v"]), jnp.stack(outs_p["ik"]), jnp.stack(outs_p["s"]),
            jnp.stack(outs_s["k"]), jnp.stack(outs_s["v"]), jnp.stack(outs_s["ik"]), jnp.stack(outs_s["s"]))
```

```python
import functools

import jax
import jax.numpy as jnp
from jax import lax
from jax.experimental import pallas as pl
from jax.experimental.pallas import tpu as pltpu

F32 = jnp.float32
BF16 = jnp.bfloat16
I32 = jnp.int32

DEPTH = 2
D_MODEL = 2048
N_ATT_HEADS = 8
N_KV_HEADS = 2
KV_GROUP = N_ATT_HEADS // N_KV_HEADS
HEAD_DIM = 128
ATT_WIDTH = N_ATT_HEADS * HEAD_DIM
KV_WIDTH = N_KV_HEADS * HEAD_DIM
N_IDX_HEADS = 16
IDX_DIM = 64
TOPK_MAX = 256
N_HGRN_HEADS = 8
HGRN_DK = 128
HGRN_DV = 128
HGRN_WIDTH = N_HGRN_HEADS * HGRN_DK
HGRN_CHUNK = 64
HGRN_SUB = 16
N_EXPERTS = 16
N_GROUPS = 4
EXPERTS_PER_GROUP = N_EXPERTS // N_GROUPS
EXPERT_DFF = 1024
ALPHA = (2 * DEPTH) ** 0.25
LN_EPS = 1e-5
MASK_VALUE = -1e30
INT_MIN = -2 ** 31

IN_SPLITS = (ATT_WIDTH, KV_WIDTH, KV_WIDTH, N_IDX_HEADS * IDX_DIM, IDX_DIM, N_IDX_HEADS,
             HGRN_WIDTH, HGRN_WIDTH, HGRN_WIDTH, HGRN_WIDTH, D_MODEL, D_MODEL)

LANES = 128
OFF_GA = 0
OFF_GB = OFF_GA + D_MODEL
OFF_Q = OFF_GB + D_MODEL
OFF_IQ = OFF_Q + ATT_WIDTH
OFF_HQ = OFF_IQ + N_IDX_HEADS * IDX_DIM
OFF_HF = OFF_HQ + HGRN_WIDTH
OFF_HI = OFF_HF + HGRN_WIDTH
OFF_HG = OFF_HI + HGRN_WIDTH
OFF_K = OFF_HG + HGRN_WIDTH
OFF_V = OFF_K + KV_WIDTH
OFF_IKW = OFF_V + KV_WIDTH
PROJ_TN = 1024
PACK_WIDTH = -(-(OFF_IKW + LANES) // PROJ_TN) * PROJ_TN

VMEM_LIMIT = 56 * 1024 * 1024


def _params(semantics):
    return pltpu.CompilerParams(dimension_semantics=semantics, vmem_limit_bytes=VMEM_LIMIT)


def _tile(n, pref):
    t = min(n, pref)
    while n % t:
        t //= 2
    return t


def _ln(x):
    mu = jnp.mean(x, axis=-1, keepdims=True)
    xc = x - mu
    var = jnp.mean(xc * xc, axis=-1, keepdims=True)
    return xc * lax.rsqrt(var + LN_EPS)


def _dot(a, b):
    return jnp.dot(a, b, preferred_element_type=F32)


def _dot_nt(a, b):
    return lax.dot_general(a, b, (((1,), (1,)), ((), ())), preferred_element_type=F32)


def _split3(x):
    x1 = x.astype(BF16)
    r1 = x - x1.astype(F32)
    x2 = r1.astype(BF16)
    x3 = (r1 - x2.astype(F32)).astype(BF16)
    return x1, x2, x3


def _sort_key(s):
    bits = pltpu.bitcast(s, I32)
    return bits ^ ((bits >> 31) & 0x7FFFFFFF)


def _kth_largest_key(count_ge, n_sel, shape):
    def body(i, ans):
        bit = lax.shift_left(jnp.int32(1), jnp.int32(31) - i)
        cand = ans | bit
        cnt = count_ge(cand ^ INT_MIN)
        return jnp.where(cnt >= n_sel, cand, ans)
    ans = lax.fori_loop(0, 32, body, jnp.zeros(shape, I32))
    return ans ^ INT_MIN


def _ada_kernel(c_ref, w_ref, b_ref, o_ref):
    c = c_ref[...]
    a = (c * jax.nn.sigmoid(c)).astype(BF16)
    o_ref[...] = _dot(a, w_ref[...].astype(BF16)) + b_ref[...]


def _ada(c_all, w_ada, b_ada):
    rows = c_all.shape[0]
    width = w_ada.shape[-1]
    tn = _tile(width, 1024)
    return pl.pallas_call(
        _ada_kernel,
        out_shape=jax.ShapeDtypeStruct((DEPTH, rows, width), F32),
        grid=(DEPTH, width // tn),
        in_specs=[pl.BlockSpec((rows, D_MODEL), lambda l, j: (0, 0)),
                  pl.BlockSpec((None, D_MODEL, tn), lambda l, j: (l, 0, j)),
                  pl.BlockSpec((None, 1, tn), lambda l, j: (l, 0, j))],
        out_specs=pl.BlockSpec((None, rows, tn), lambda l, j: (l, 0, j)),
        compiler_params=_params(("parallel", "parallel")),
        name="ada",
    )(c_all, w_ada, b_ada.reshape(DEPTH, 1, width))


def _proj_kernel(x_ref, sh_ref, sc_ref, w_ref, o_ref, h_scr):
    @pl.when(pl.program_id(1) == 0)
    def _():
        h_scr[...] = (_ln(x_ref[...]) * (1.0 + sc_ref[0]) + sh_ref[0]).astype(BF16)
    o_ref[...] = _dot(h_scr[...], w_ref[...])


def _proj(x, shift, scale, w_pack, l, tm):
    n = x.shape[0]
    tiles_per_mod = (n // tm) // shift.shape[0]
    r = shift.shape[1]
    tn = PROJ_TN
    mod_spec = pl.BlockSpec((1, r, D_MODEL), lambda i, j: (i // tiles_per_mod, 0, 0))
    return pl.pallas_call(
        _proj_kernel,
        out_shape=jax.ShapeDtypeStruct((n, PACK_WIDTH), F32),
        grid=(n // tm, PACK_WIDTH // tn),
        in_specs=[pl.BlockSpec((tm, D_MODEL), lambda i, j: (i, 0)),
                  mod_spec, mod_spec,
                  pl.BlockSpec((None, D_MODEL, tn), lambda i, j: (l, 0, j))],
        out_specs=pl.BlockSpec((tm, tn), lambda i, j: (i, j)),
        scratch_shapes=[pltpu.VMEM((tm, D_MODEL), BF16)],
        compiler_params=_params(("parallel", "arbitrary")),
        name="proj",
    )(x, shift, scale, w_pack)


ROW_CHUNK = 64
ATT_ROWS = 128


def _lane_fold(x, op):
    acc = x[:, :LANES]
    for i in range(1, x.shape[1] // LANES):
        acc = op(acc, x[:, i * LANES:(i + 1) * LANES])
    return acc


def _attn_prompt_kernel(q_ref, iq_ref, iwq_ref, k_ref, v_ref, ikw_ref, o_ref,
                        kb_scr, va_scr, ikb_scr, iqs_scr, qs_scr, key_scr, m_scr, acc_scr, xb_scr,
                        *, tq, n_sel, col_bits):
    qi = pl.program_id(1)
    tk = tq
    n_lane_tiles = tk // LANES

    @pl.when(qi == 0)
    def _():
        kb_scr[...] = k_ref[...].astype(BF16)
        for g in range(N_KV_HEADS):
            va_scr[:, 2 * g * HEAD_DIM:(2 * g + 1) * HEAD_DIM] = (
                v_ref[:, g * HEAD_DIM:(g + 1) * HEAD_DIM].astype(BF16))
            va_scr[:, (2 * g + 1) * HEAD_DIM:(2 * g + 2) * HEAD_DIM] = jnp.ones((va_scr.shape[0], HEAD_DIM), BF16)
        ikb_scr[...] = ikw_ref[:, :IDX_DIM].astype(BF16)

    iq = iq_ref[...]
    for h in range(N_IDX_HEADS):
        iqs_scr[h * tq:(h + 1) * tq, :] = iq[:, h * IDX_DIM:(h + 1) * IDX_DIM].astype(BF16)
    q = q_ref[...]
    for h in range(N_ATT_HEADS):
        qs_scr[h] = q[:, h * HEAD_DIM:(h + 1) * HEAD_DIM].astype(BF16)
    w = iwq_ref[:, IDX_DIM:IDX_DIM + N_IDX_HEADS] * (IDX_DIM ** -0.5 * N_IDX_HEADS ** -0.5)
    row = qi * tq + lax.broadcasted_iota(I32, (tq, tk), 0)
    col0 = lax.broadcasted_iota(I32, (tq, tk), 1)

    def score_block(kb, carry):
        start = pl.multiple_of(kb * tk, tk)
        lg = _dot_nt(iqs_scr[...], ikb_scr[pl.ds(start, tk), :])
        s = jnp.zeros((tq, tk), F32)
        for h in range(N_IDX_HEADS):
            s = s + w[:, h:h + 1] * jnp.maximum(lg[h * tq:(h + 1) * tq, :], 0.0)
        s = jnp.where(kb * tk + col0 <= row, s, MASK_VALUE)
        key_scr[kb] = _sort_key(s)
        return carry

    lax.fori_loop(0, qi + 1, score_block, 0)

    chunks = [slice(c * ROW_CHUNK, (c + 1) * ROW_CHUNK) for c in range(tq // ROW_CHUNK)]

    def count(make_pred):
        parts = []
        for rows in chunks:
            pred = make_pred(rows)

            def body(kb, part, pred=pred, rows=rows):
                hit = jnp.where(pred(kb, key_scr[kb, rows, :]), 1.0, 0.0)
                return part + _lane_fold(hit, jnp.add)

            parts.append(lax.fori_loop(0, qi + 1, body, jnp.zeros((ROW_CHUNK, LANES), F32)))
        part = jnp.concatenate(parts, axis=0)
        return jnp.broadcast_to(jnp.sum(part, axis=1, keepdims=True), (tq, LANES))

    def wide(x):
        return jnp.concatenate([x] * n_lane_tiles, axis=1)

    def bcast(x, rows):
        return wide(x[rows])

    def count_ge(cand):
        return count(lambda rows: (lambda kb, key, c=bcast(cand, rows): key >= c))

    thr = _kth_largest_key(count_ge, float(n_sel), (tq, LANES))

    cnt_gt = count(lambda rows: (lambda kb, key, c=bcast(thr, rows): key > c))
    cnt_ge = count_ge(thr)
    need = float(n_sel) - cnt_gt
    xb_scr[...] = jnp.full(xb_scr.shape, 2 ** 31 - 1, I32)

    @pl.when(jnp.max(jnp.abs(cnt_ge - float(n_sel))) > 0.0)
    def _():
        def body(i, x):
            cand = x | lax.shift_left(jnp.int32(1), jnp.int32(col_bits - 1) - i)

            def make_pred(rows):
                c_thr, c_cand = bcast(thr, rows), bcast(cand, rows)
                c_col = lax.broadcasted_iota(I32, (ROW_CHUNK, tk), 1)
                return lambda kb, key: (key == c_thr) & (kb * tk + c_col < c_cand)

            return jnp.where(count(make_pred) < need, cand, x)

        xb_scr[...] = lax.fori_loop(0, col_bits, body, jnp.zeros((tq, LANES), I32))

    xb_w, thr_w = wide(xb_scr[...]), wide(thr)

    def bias_block(kb, carry):
        key = key_scr[kb]
        col = kb * tk + col0
        sel = ((key > thr_w) | ((key == thr_w) & (col <= xb_w))) & (col <= row)
        key_scr[kb] = pltpu.bitcast(jnp.where(sel, 0.0, -jnp.inf), I32)
        return carry

    lax.fori_loop(0, qi + 1, bias_block, 0)

    m_scr[...] = jnp.full(m_scr.shape, MASK_VALUE, F32)
    acc_scr[...] = jnp.zeros(acc_scr.shape, F32)
    scale = HEAD_DIM ** -0.5

    def att_block(kb, carry):
        start = pl.multiple_of(kb * tk, tk)
        for g in range(N_KV_HEADS):
            kg = kb_scr[pl.ds(start, tk), g * HEAD_DIM:(g + 1) * HEAD_DIM]
            va = va_scr[pl.ds(start, tk), 2 * g * HEAD_DIM:(2 * g + 2) * HEAD_DIM]
            for r in range(KV_GROUP):
                hh = g * KV_GROUP + r
                for c in range(tq // ATT_ROWS):
                    rows = slice(c * ATT_ROWS, (c + 1) * ATT_ROWS)
                    bias = pltpu.bitcast(key_scr[kb, rows, :], F32)
                    s = _dot_nt(qs_scr[hh, rows, :], kg) * scale + bias
                    m_old = m_scr[hh, rows, :]
                    m_new = jnp.maximum(m_old, jnp.max(_lane_fold(s, jnp.maximum), axis=1, keepdims=True))
                    p = jnp.exp(s - jnp.concatenate([m_new] * n_lane_tiles, axis=1))
                    a = jnp.exp(m_old - m_new)
                    acc_scr[hh, rows, :] = (jnp.concatenate([a, a], axis=1) * acc_scr[hh, rows, :]
                                            + _dot(p.astype(BF16), va))
                    m_scr[hh, rows, :] = m_new
        return carry

    lax.fori_loop(0, qi + 1, att_block, 0)
    for h in range(N_ATT_HEADS):
        acc = acc_scr[h]
        o_ref[:, h * HEAD_DIM:(h + 1) * HEAD_DIM] = (acc[:, :HEAD_DIM] / acc[:, HEAD_DIM:]).astype(o_ref.dtype)


def _attn_prompt(proj, b, t):
    tq = _tile(t, 256)
    n_sel = min(TOPK_MAX, t // 4)
    proj3 = proj.reshape(b, t, PACK_WIDTH)
    kern = functools.partial(_attn_prompt_kernel, tq=tq, n_sel=n_sel, col_bits=t.bit_length())
    out = pl.pallas_call(
        kern,
        out_shape=jax.ShapeDtypeStruct((b, t, ATT_WIDTH), BF16),
        grid=(b, t // tq),
        in_specs=[pl.BlockSpec((None, tq, ATT_WIDTH), lambda bi, qi: (bi, qi, OFF_Q // ATT_WIDTH)),
                  pl.BlockSpec((None, tq, ATT_WIDTH), lambda bi, qi: (bi, qi, OFF_IQ // ATT_WIDTH)),
                  pl.BlockSpec((None, tq, LANES), lambda bi, qi: (bi, qi, OFF_IKW // LANES)),
                  pl.BlockSpec((None, t, KV_WIDTH), lambda bi, qi: (bi, 0, OFF_K // KV_WIDTH)),
                  pl.BlockSpec((None, t, KV_WIDTH), lambda bi, qi: (bi, 0, OFF_V // KV_WIDTH)),
                  pl.BlockSpec((None, t, LANES), lambda bi, qi: (bi, 0, OFF_IKW // LANES))],
        out_specs=pl.BlockSpec((None, tq, ATT_WIDTH), lambda bi, qi: (bi, qi, 0)),
        scratch_shapes=[pltpu.VMEM((t, KV_WIDTH), BF16),
                        pltpu.VMEM((t, 2 * KV_WIDTH), BF16),
                        pltpu.VMEM((t, IDX_DIM), BF16),
                        pltpu.VMEM((N_IDX_HEADS * tq, IDX_DIM), BF16),
                        pltpu.VMEM((N_ATT_HEADS, tq, HEAD_DIM), BF16),
                        pltpu.VMEM((t // tq, tq, tq), I32),
                        pltpu.VMEM((N_ATT_HEADS, tq, LANES), F32),
                        pltpu.VMEM((N_ATT_HEADS, tq, 2 * HEAD_DIM), F32),
                        pltpu.VMEM((tq, LANES), I32)],
        compiler_params=_params(("parallel", "arbitrary")),
        name="attn_prompt",
    )(proj3, proj3, proj3, proj3, proj3, proj3)
    return out.reshape(b * t, ATT_WIDTH)


def _hgrn_gates(hq, hf, lb):
    q = hq * jax.nn.sigmoid(hq)
    f = lb + (1.0 - lb) * jax.nn.sigmoid(hf)
    kk = (1.0 - lb) * jax.nn.sigmoid(-hf)
    return q, f, kk


def _hgrn_finish(o, hg, ng):
    o = o * lax.rsqrt(jnp.mean(o * o, axis=-1, keepdims=True) + LN_EPS)
    return o * ng * (hg * jax.nn.sigmoid(hg))


HGRN_HEADS_PER_STEP = 4


def _hgrn_chunk(heads, tri, ones):
    c_len, sub = HGRN_CHUNK, HGRN_SUB
    n_sub = c_len // sub
    t_idx = lax.broadcasted_iota(I32, (sub, 1), 0)
    bs = []
    for q, f, kk, v, st in heads:
        l1, l2, l3 = _split3(jnp.log(f))
        bs.append(_dot(tri, l1) + _dot(tri, l2) + _dot(tri, l3))
    stage2 = []
    for (q, f, kk, v, st), b in zip(heads, bs):
        v_b = v.astype(BF16)
        inter = _dot_nt((q * jnp.exp(b)).astype(BF16), st.astype(BF16))
        rs, cross = [], []
        for j in range(n_sub):
            lo = j * sub
            qt, bt = q[lo:lo + sub], b[lo:lo + sub]
            ps = []
            for s in range(sub):
                valid = t_idx >= s
                dec = jnp.exp(jnp.where(valid, bt - b[lo + s:lo + s + 1, :], 0.0))
                ps.append(jnp.where(valid, qt * kk[lo + s:lo + s + 1, :] * dec, 0.0).astype(BF16))
            rs.append(_dot(jnp.concatenate(ps, axis=0), ones))
            if lo:
                b_edge = b[lo - 1:lo, :]
                qd = (qt * jnp.exp(bt - b_edge)).astype(BF16)
                kd = (kk[:lo] * jnp.exp(b_edge - b[:lo])).astype(BF16)
                cross.append(_dot_nt(qd, kd))
        b_last = b[c_len - 1:c_len, :]
        kd = (kk * jnp.exp(b_last - b)).astype(BF16)
        upd = lax.dot_general(v_b, kd, (((0,), (0,)), ((), ())), preferred_element_type=F32)
        stage2.append((v_b, inter, rs, cross, st * jnp.exp(b_last) + upd))
    out = []
    for (q, f, kk, v, st), (v_b, inter, rs, cross, st_new) in zip(heads, stage2):
        rows = []
        for j in range(n_sub):
            lo = j * sub
            acc = inter[lo:lo + sub]
            if lo:
                acc = acc + _dot(cross[j - 1].astype(BF16), v_b[:lo])
            for s in range(sub):
                acc = acc + rs[j][s * sub:(s + 1) * sub, :] * v[lo + s:lo + s + 1, :]
            rows.append(acc)
        out.append((jnp.concatenate(rows, axis=0), st_new))
    return out


def _hgrn_prompt_kernel(hq_ref, hf_ref, hi_ref, hg_ref, lb_ref, ng_ref, o_ref, s_ref, st_scr, *, n_chunks):
    c_len = HGRN_CHUNK
    ti = pl.program_id(2)

    @pl.when(ti == 0)
    def _():
        st_scr[...] = jnp.zeros(st_scr.shape, F32)

    r_i = lax.broadcasted_iota(I32, (c_len, c_len), 0)
    c_i = lax.broadcasted_iota(I32, (c_len, c_len), 1)
    tri = jnp.where(c_i <= r_i, 1.0, 0.0).astype(BF16)
    ones = jnp.ones((HGRN_DK, HGRN_DV), BF16)

    def chunk(c, carry):
        sl = pl.ds(pl.multiple_of(c * c_len, c_len), c_len)
        cols = [slice(hh * HGRN_DK, (hh + 1) * HGRN_DK) for hh in range(HGRN_HEADS_PER_STEP)]
        heads = []
        for hh, cs in enumerate(cols):
            q, f, kk = _hgrn_gates(hq_ref[sl, cs], hf_ref[sl, cs], lb_ref[:, cs])
            heads.append((q, f, kk, hi_ref[sl, cs], st_scr[hh]))
        for hh, (o, st_new) in enumerate(_hgrn_chunk(heads, tri, ones)):
            st_scr[hh] = st_new
            o_ref[sl, cols[hh]] = _hgrn_finish(o, hg_ref[sl, cols[hh]], ng_ref[:, cols[hh]]).astype(o_ref.dtype)
        return carry

    lax.fori_loop(0, n_chunks, chunk, 0)

    @pl.when(ti == pl.num_programs(2) - 1)
    def _():
        for hh in range(HGRN_HEADS_PER_STEP):
            s_ref[hh] = st_scr[hh].T


def _hgrn_prompt(proj, lb, ng, b, t):
    tc = _tile(t, 512)
    n_chunks = tc // HGRN_CHUNK
    hp = HGRN_HEADS_PER_STEP
    width = hp * HGRN_DK
    proj3 = proj.reshape(b, t, PACK_WIDTH)

    def col(off):
        return pl.BlockSpec((None, tc, width), lambda bi, h, ti: (bi, ti, off // width + h))

    vec = pl.BlockSpec((None, 1, width), lambda bi, h, ti: (h, 0, 0))
    o, s = pl.pallas_call(
        functools.partial(_hgrn_prompt_kernel, n_chunks=n_chunks),
        out_shape=(jax.ShapeDtypeStruct((b, t, HGRN_WIDTH), BF16),
                   jax.ShapeDtypeStruct((b, N_HGRN_HEADS, HGRN_DK, HGRN_DV), F32)),
        grid=(b, N_HGRN_HEADS // hp, t // tc),
        in_specs=[col(OFF_HQ), col(OFF_HF), col(OFF_HI), col(OFF_HG), vec, vec],
        out_specs=(pl.BlockSpec((None, tc, width), lambda bi, h, ti: (bi, ti, h)),
                   pl.BlockSpec((None, hp, HGRN_DK, HGRN_DV), lambda bi, h, ti: (bi, h, 0, 0))),
        scratch_shapes=[pltpu.VMEM((hp, HGRN_DV, HGRN_DK), F32)],
        compiler_params=_params(("parallel", "parallel", "arbitrary")),
        name="hgrn_prompt",
    )(proj3, proj3, proj3, proj3,
      lb.reshape(N_HGRN_HEADS // hp, 1, width), ng.reshape(N_HGRN_HEADS // hp, 1, width))
    return o.reshape(b * t, HGRN_WIDTH), s


def _merge_kernel(oa_ref, ob_ref, ga_ref, gb_ref, wa_ref, wb_ref, o_ref):
    ya = _dot(oa_ref[...].astype(BF16), wa_ref[...])
    yb = _dot(ob_ref[...].astype(BF16), wb_ref[...])
    o_ref[...] = (jax.nn.sigmoid(ga_ref[...]) * ya + jax.nn.sigmoid(gb_ref[...]) * yb).astype(o_ref.dtype)


def _merge(o_a, o_b, proj, w_up_a, w_up_b, l, tm):
    n = o_a.shape[0]
    return pl.pallas_call(
        _merge_kernel,
        out_shape=jax.ShapeDtypeStruct((n, D_MODEL), BF16),
        grid=(n // tm,),
        in_specs=[pl.BlockSpec((tm, ATT_WIDTH), lambda i: (i, 0)),
                  pl.BlockSpec((tm, HGRN_WIDTH), lambda i: (i, 0)),
                  pl.BlockSpec((tm, D_MODEL), lambda i: (i, OFF_GA // D_MODEL)),
                  pl.BlockSpec((tm, D_MODEL), lambda i: (i, OFF_GB // D_MODEL)),
                  pl.BlockSpec((None, ATT_WIDTH, D_MODEL), lambda i: (l, 0, 0)),
                  pl.BlockSpec((None, HGRN_WIDTH, D_MODEL), lambda i: (l, 0, 0))],
        out_specs=pl.BlockSpec((tm, D_MODEL), lambda i: (i, 0)),
        compiler_params=_params(("parallel",)),
        name="merge",
    )(o_a, o_b, proj, proj, w_up_a, w_up_b)


def _post_norm(x, gate, y, g, b):
    return _ln(ALPHA * x + gate * y) * g + b


def _out_kernel(m_ref, w_ref, x_ref, gate_ref, lng_ref, lnb_ref, o_ref):
    y = _dot(m_ref[...], w_ref[...])
    o_ref[...] = _post_norm(x_ref[...], gate_ref[0], y, lng_ref[...], lnb_ref[...])


def _out_proj(merged, w_o, x, gate, ln_g, ln_b, l, tm):
    n = x.shape[0]
    tiles_per_mod = (n // tm) // gate.shape[0]
    vec = pl.BlockSpec((None, 1, D_MODEL), lambda i: (l, 0, 0))
    return pl.pallas_call(
        _out_kernel,
        out_shape=jax.ShapeDtypeStruct((n, D_MODEL), F32),
        grid=(n // tm,),
        in_specs=[pl.BlockSpec((tm, D_MODEL), lambda i: (i, 0)),
                  pl.BlockSpec((None, D_MODEL, D_MODEL), lambda i: (l, 0, 0)),
                  pl.BlockSpec((tm, D_MODEL), lambda i: (i, 0)),
                  pl.BlockSpec((1, gate.shape[1], D_MODEL), lambda i: (i // tiles_per_mod, 0, 0)),
                  vec, vec],
        out_specs=pl.BlockSpec((tm, D_MODEL), lambda i: (i, 0)),
        compiler_params=_params(("parallel",)),
        name="out_proj",
    )(merged, w_o, x, gate, ln_g.reshape(DEPTH, 1, D_MODEL), ln_b.reshape(DEPTH, 1, D_MODEL))


def _route(h, wr, br):
    h1, h2, _ = _split3(h)
    w1, w2, _ = _split3(wr)
    logits = _dot_nt(w1, h1) + _dot_nt(w1, h2) + _dot_nt(w2, h1)
    aff = jax.nn.sigmoid(logits)
    sel = aff + br
    rows = [sel[e:e + 1, :] for e in range(N_EXPERTS)]
    grp = []
    for g in range(N_GROUPS):
        a, b, c, d = rows[g * EXPERTS_PER_GROUP:(g + 1) * EXPERTS_PER_GROUP]
        hi1, lo1 = jnp.maximum(a, b), jnp.minimum(a, b)
        hi2, lo2 = jnp.maximum(c, d), jnp.minimum(c, d)
        grp.append(jnp.maximum(hi1, hi2) + jnp.maximum(jnp.minimum(hi1, hi2), jnp.maximum(lo1, lo2)))
    best = functools.reduce(jnp.maximum, grp)
    taken = jnp.zeros_like(best)
    picked = []
    for g in range(N_GROUPS):
        is_g = jnp.where(grp[g] == best, 1.0, 0.0) * (1.0 - taken)
        taken = taken + is_g
        for e in range(g * EXPERTS_PER_GROUP, (g + 1) * EXPERTS_PER_GROUP):
            rank = jnp.zeros_like(best)
            for o in range(g * EXPERTS_PER_GROUP, (g + 1) * EXPERTS_PER_GROUP):
                if o < e:
                    rank = rank + jnp.where(rows[o] >= rows[e], 1.0, 0.0)
                elif o > e:
                    rank = rank + jnp.where(rows[o] > rows[e], 1.0, 0.0)
            picked.append(is_g * jnp.where(rank < 2.0, 1.0, 0.0))
    picked = jnp.concatenate(picked, axis=0)
    gate = picked * aff
    return picked, gate / jnp.sum(gate, axis=0, keepdims=True)


def _router_kernel(x_ref, sh_ref, sc_ref, wr_ref, br_ref, h_ref, g_ref):
    h = _ln(x_ref[...]) * (1.0 + sc_ref[0]) + sh_ref[0]
    h_ref[...] = h.astype(h_ref.dtype)
    _, g_ref[...] = _route(h, wr_ref[...], br_ref[...])


def _router_sorted_kernel(x_ref, sh_ref, sc_ref, wr_ref, br_ref, h_ref, eid_ref, gw_ref, pos_ref, cnt_ref, base_scr):
    tm = x_ref.shape[0]

    @pl.when(pl.program_id(0) == 0)
    def _():
        base_scr[...] = jnp.zeros(base_scr.shape, F32)

    h = _ln(x_ref[...]) * (1.0 + sc_ref[0]) + sh_ref[0]
    h_ref[...] = h.astype(h_ref.dtype)
    picked, gate = _route(h, wr_ref[...], br_ref[...])
    r = lax.broadcasted_iota(I32, (tm, tm), 0)
    c = lax.broadcasted_iota(I32, (tm, tm), 1)
    earlier = jnp.where(r < c, 1.0, 0.0).astype(BF16)
    base = base_scr[...]
    rank = _dot(picked.astype(BF16), earlier) + jnp.concatenate([base] * (tm // LANES), axis=1)
    base_scr[...] = base + jnp.sum(picked, axis=1, keepdims=True)
    cnt_ref[...] = base_scr[...]
    eio = lax.broadcasted_iota(I32, picked.shape, 0).astype(F32)
    e_lo = jnp.min(jnp.where(picked > 0.0, eio, float(N_EXPERTS)), axis=0, keepdims=True)
    e_hi = jnp.max(jnp.where(picked > 0.0, eio, -1.0), axis=0, keepdims=True)

    def pick(e, x):
        return jnp.sum(jnp.where(eio == e, x, 0.0), axis=0, keepdims=True)

    eid_ref[...] = jnp.concatenate([e_lo, e_hi], axis=0).astype(I32)
    gw_ref[...] = jnp.concatenate([pick(e_lo, gate), pick(e_hi, gate)], axis=0)
    pos_ref[...] = jnp.concatenate([pick(e_lo, rank), pick(e_hi, rank)], axis=0).astype(I32)


def _router_sorted(x, shift, scale, w_router_t, b_router, tm):
    n = x.shape[0]
    tiles_per_mod = (n // tm) // shift.shape[0]
    mod_spec = pl.BlockSpec((1, shift.shape[1], D_MODEL), lambda i: (i // tiles_per_mod, 0, 0))
    pair = pl.BlockSpec((2, tm), lambda i: (0, i))
    return pl.pallas_call(
        _router_sorted_kernel,
        out_shape=(jax.ShapeDtypeStruct((n, D_MODEL), F32),
                   jax.ShapeDtypeStruct((2, n), I32),
                   jax.ShapeDtypeStruct((2, n), F32),
                   jax.ShapeDtypeStruct((2, n), I32),
                   jax.ShapeDtypeStruct((N_EXPERTS, LANES), F32)),
        grid=(n // tm,),
        in_specs=[pl.BlockSpec((tm, D_MODEL), lambda i: (i, 0)),
                  mod_spec, mod_spec,
                  pl.BlockSpec((N_EXPERTS, D_MODEL), lambda i: (0, 0)),
                  pl.BlockSpec((N_EXPERTS, 1), lambda i: (0, 0))],
        out_specs=(pl.BlockSpec((tm, D_MODEL), lambda i: (i, 0)), pair, pair, pair,
                   pl.BlockSpec((N_EXPERTS, LANES), lambda i: (0, 0))),
        scratch_shapes=[pltpu.VMEM((N_EXPERTS, LANES), F32)],
        compiler_params=_params(("arbitrary",)),
        name="router_sorted",
    )(x, shift, scale, w_router_t, b_router.reshape(N_EXPERTS, 1))


MOE_TILE = 256
DMA_UNROLL = 8


def _dispatch_kernel(dst_ref, h_ref, xs_in_ref, xs_ref, sem, *, td, n):
    del xs_in_ref
    first = pl.program_id(0) * td

    def copies(r):
        return [pltpu.make_async_copy(h_ref.at[pl.ds(r, 1)], xs_ref.at[pl.ds(dst_ref[k * n + first + r], 1)], sem)
                for k in range(2)]

    def start(r, carry):
        for cp in copies(r):
            cp.start()
        return carry

    def wait(r, carry):
        for cp in copies(r):
            cp.wait()
        return carry

    lax.fori_loop(0, td, start, 0, unroll=DMA_UNROLL)
    lax.fori_loop(0, td, wait, 0, unroll=DMA_UNROLL)


def _dispatch(h, dst_flat, n_slots, td):
    n = h.shape[0]
    grid_spec = pltpu.PrefetchScalarGridSpec(
        num_scalar_prefetch=1,
        grid=(n // td,),
        in_specs=[pl.BlockSpec((td, D_MODEL), lambda i, dst: (i, 0)),
                  pl.BlockSpec(memory_space=pl.ANY)],
        out_specs=pl.BlockSpec(memory_space=pl.ANY),
        scratch_shapes=[pltpu.SemaphoreType.DMA])
    return pl.pallas_call(
        functools.partial(_dispatch_kernel, td=td, n=n),
        out_shape=jax.ShapeDtypeStruct((n_slots, D_MODEL), F32),
        grid_spec=grid_spec,
        input_output_aliases={2: 0},
        compiler_params=_params(("arbitrary",)),
        name="moe_dispatch",
    )(dst_flat, h, jnp.zeros((n_slots, D_MODEL), F32))


def _experts_kernel(te_ref, tv_ref, xs_ref, w1_ref, w3_ref, w2_ref, ys_ref):
    del te_ref
    valid = tv_ref[pl.program_id(0)] > 0

    @pl.when(valid)
    def _():
        x = xs_ref[...].astype(BF16)
        a = _dot(x, w1_ref[...])
        b = _dot(x, w3_ref[...])
        ys_ref[...] = _dot((a * jax.nn.sigmoid(a) * b).astype(BF16), w2_ref[...])

    @pl.when(jnp.logical_not(valid))
    def _():
        ys_ref[...] = jnp.zeros(ys_ref.shape, F32)


def _experts(xs, tile_expert, tile_valid, w1, w3, w2, l):
    n_tiles = xs.shape[0] // MOE_TILE
    grid_spec = pltpu.PrefetchScalarGridSpec(
        num_scalar_prefetch=2,
        grid=(n_tiles,),
        in_specs=[pl.BlockSpec((MOE_TILE, D_MODEL), lambda i, te, tv: (i, 0)),
                  pl.BlockSpec((None, None, D_MODEL, EXPERT_DFF), lambda i, te, tv: (l, te[i], 0, 0)),
                  pl.BlockSpec((None, None, D_MODEL, EXPERT_DFF), lambda i, te, tv: (l, te[i], 0, 0)),
                  pl.BlockSpec((None, None, EXPERT_DFF, D_MODEL), lambda i, te, tv: (l, te[i], 0, 0))],
        out_specs=pl.BlockSpec((MOE_TILE, D_MODEL), lambda i, te, tv: (i, 0)))
    return pl.pallas_call(
        _experts_kernel,
        out_shape=jax.ShapeDtypeStruct(xs.shape, F32),
        grid_spec=grid_spec,
        compiler_params=_params(("arbitrary",)),
        name="moe_experts",
    )(tile_expert, tile_valid, xs, w1, w3, w2)


def _combine_kernel(dst_ref, ys_ref, gw_ref, x_ref, gate_ref, lng_ref, lnb_ref, o_ref, buf, sem, *, tc, n):
    i = pl.program_id(0)
    slot = i % 2

    def copies(tile, s, r):
        return [pltpu.make_async_copy(ys_ref.at[pl.ds(dst_ref[k * n + tile * tc + r], 1)],
                                      buf.at[s, k, pl.ds(r, 1)], sem.at[s]) for k in range(2)]

    def start_tile(tile, s):
        def body(r, carry):
            for cp in copies(tile, s, r):
                cp.start()
            return carry
        lax.fori_loop(0, tc, body, 0, unroll=DMA_UNROLL)

    def wait_tile(tile, s):
        def body(r, carry):
            for cp in copies(tile, s, r):
                cp.wait()
            return carry
        lax.fori_loop(0, tc, body, 0, unroll=DMA_UNROLL)

    @pl.when(i == 0)
    def _():
        start_tile(0, 0)

    @pl.when(i + 1 < pl.num_programs(0))
    def _():
        start_tile(i + 1, 1 - slot)

    wait_tile(i, slot)
    gw = gw_ref[...]
    y = gw[:, 0:1] * buf[slot, 0] + gw[:, 1:2] * buf[slot, 1]
    o_ref[...] = _post_norm(x_ref[...], gate_ref[0], y, lng_ref[...], lnb_ref[...])


def _combine(ys, dst_flat, gw, x, gate, ln_g, ln_b, l, tc):
    n = x.shape[0]
    tiles_per_mod = (n // tc) // gate.shape[0]
    vec = pl.BlockSpec((None, 1, D_MODEL), lambda i, dst: (l, 0, 0))
    grid_spec = pltpu.PrefetchScalarGridSpec(
        num_scalar_prefetch=1,
        grid=(n // tc,),
        in_specs=[pl.BlockSpec(memory_space=pl.ANY),
                  pl.BlockSpec((tc, 2), lambda i, dst: (i, 0)),
                  pl.BlockSpec((tc, D_MODEL), lambda i, dst: (i, 0)),
                  pl.BlockSpec((1, gate.shape[1], D_MODEL), lambda i, dst: (i // tiles_per_mod, 0, 0)),
                  vec, vec],
        out_specs=pl.BlockSpec((tc, D_MODEL), lambda i, dst: (i, 0)),
        scratch_shapes=[pltpu.VMEM((2, 2, tc, D_MODEL), F32),
                        pltpu.SemaphoreType.DMA((2,))])
    return pl.pallas_call(
        functools.partial(_combine_kernel, tc=tc, n=n),
        out_shape=jax.ShapeDtypeStruct((n, D_MODEL), F32),
        grid_spec=grid_spec,
        compiler_params=_params(("arbitrary",)),
        name="moe_combine",
    )(dst_flat, ys, gw, x, gate, ln_g.reshape(DEPTH, 1, D_MODEL), ln_b.reshape(DEPTH, 1, D_MODEL))


def _moe_grouped(x, shift, scale, gate, w_router_t, b_router, w1, w3, w2, ln_g, ln_b, l, tm):
    n = x.shape[0]
    h, eid, gw, pos, cnt = _router_sorted(x, shift, scale, w_router_t, b_router, tm)
    counts = cnt[:, 0].astype(I32)
    padded = (counts + MOE_TILE - 1) // MOE_TILE * MOE_TILE
    ends = jnp.cumsum(padded)
    experts = jnp.arange(N_EXPERTS, dtype=I32)[:, None, None]
    seg_start = jnp.sum(jnp.where(eid[None] == experts, (ends - padded)[:, None, None], 0), axis=0)
    dst_flat = (seg_start + pos).reshape(2 * n)
    n_tiles = 2 * n // MOE_TILE + N_EXPERTS
    tile_start = jnp.arange(n_tiles, dtype=I32) * MOE_TILE
    tile_expert = jnp.minimum(jnp.sum((tile_start[:, None] >= ends[None, :]).astype(I32), axis=1), N_EXPERTS - 1)
    tile_valid = (tile_start < ends[-1]).astype(I32)
    xs = _dispatch(h, dst_flat, n_tiles * MOE_TILE, tm)
    ys = _experts(xs, tile_expert, tile_valid, w1, w3, w2, l)
    return _combine(ys, dst_flat, gw.T, x, gate, ln_g, ln_b, l, MOE_TILE)


def _router(x, shift, scale, w_router_t, b_router, tm):
    n = x.shape[0]
    tiles_per_mod = (n // tm) // shift.shape[0]
    mod_spec = pl.BlockSpec((1, shift.shape[1], D_MODEL), lambda i: (i // tiles_per_mod, 0, 0))
    return pl.pallas_call(
        _router_kernel,
        out_shape=(jax.ShapeDtypeStruct((n, D_MODEL), BF16),
                   jax.ShapeDtypeStruct((N_EXPERTS, n), F32)),
        grid=(n // tm,),
        in_specs=[pl.BlockSpec((tm, D_MODEL), lambda i: (i, 0)),
                  mod_spec, mod_spec,
                  pl.BlockSpec((N_EXPERTS, D_MODEL), lambda i: (0, 0)),
                  pl.BlockSpec((N_EXPERTS, 1), lambda i: (0, 0))],
        out_specs=(pl.BlockSpec((tm, D_MODEL), lambda i: (i, 0)),
                   pl.BlockSpec((N_EXPERTS, tm), lambda i: (0, i))),
        compiler_params=_params(("parallel",)),
        name="router",
    )(x, shift, scale, w_router_t, b_router.reshape(N_EXPERTS, 1))


def _moe_kernel(h_ref, g_ref, w1_ref, w3_ref, w2_ref, x_ref, gate_ref, lng_ref, lnb_ref, o_ref, acc_scr):
    e = pl.program_id(1)

    @pl.when(e == 0)
    def _():
        acc_scr[...] = jnp.zeros(acc_scr.shape, F32)

    h = h_ref[...]
    a = _dot(h, w1_ref[...])
    b = _dot(h, w3_ref[...])
    gates = g_ref[...]
    lane = lax.broadcasted_iota(I32, gates.shape, 1)
    gcol = jnp.sum(jnp.where(lane == e, gates, 0.0), axis=1, keepdims=True)
    u = (a * jax.nn.sigmoid(a) * b * gcol).astype(BF16)
    acc_scr[...] += _dot(u, w2_ref[...])

    @pl.when(e == pl.num_programs(1) - 1)
    def _():
        o_ref[...] = _post_norm(x_ref[...], gate_ref[0], acc_scr[...], lng_ref[...], lnb_ref[...])


def _moe(h, gates, w1, w3, w2, x, gate, ln_g, ln_b, l, tm):
    n = x.shape[0]
    tiles_per_mod = (n // tm) // gate.shape[0]
    vec = pl.BlockSpec((None, 1, D_MODEL), lambda i, e: (l, 0, 0))
    return pl.pallas_call(
        _moe_kernel,
        out_shape=jax.ShapeDtypeStruct((n, D_MODEL), F32),
        grid=(n // tm, N_EXPERTS),
        in_specs=[pl.BlockSpec((tm, D_MODEL), lambda i, e: (i, 0)),
                  pl.BlockSpec((tm, N_EXPERTS), lambda i, e: (i, 0)),
                  pl.BlockSpec((None, None, D_MODEL, EXPERT_DFF), lambda i, e: (l, e, 0, 0)),
                  pl.BlockSpec((None, None, D_MODEL, EXPERT_DFF), lambda i, e: (l, e, 0, 0)),
                  pl.BlockSpec((None, None, EXPERT_DFF, D_MODEL), lambda i, e: (l, e, 0, 0)),
                  pl.BlockSpec((tm, D_MODEL), lambda i, e: (i, 0)),
                  pl.BlockSpec((1, gate.shape[1], D_MODEL), lambda i, e: (i // tiles_per_mod, 0, 0)),
                  vec, vec],
        out_specs=pl.BlockSpec((tm, D_MODEL), lambda i, e: (i, 0)),
        scratch_shapes=[pltpu.VMEM((tm, D_MODEL), F32)],
        compiler_params=_params(("parallel", "arbitrary")),
        name="moe",
    )(h, gates, w1, w3, w2, x, gate, ln_g.reshape(DEPTH, 1, D_MODEL), ln_b.reshape(DEPTH, 1, D_MODEL))


def _row_to_col(v):
    n = v.shape[1]
    r = lax.broadcasted_iota(I32, (n, n), 0)
    c = lax.broadcasted_iota(I32, (n, n), 1)
    return jnp.sum(jnp.where(r == c, jnp.broadcast_to(v, (n, n)), 0.0), axis=1, keepdims=True)


def _attn_decode_kernel(pt_ref, proj_ref, ck_ref, cv_ref, cik_ref, o_ref,
                        ik_buf, k_buf, v_buf, key_scr, sem_ik, sem_k, sem_v,
                        *, l, n_pages, page, chunk_pages, n_sel):
    b = pl.program_id(0)
    n_chunks = n_pages // chunk_pages
    chunk = chunk_pages * page

    def ik_copy(j):
        return pltpu.make_async_copy(cik_ref.at[l, pt_ref[b, j]], ik_buf.at[pl.ds(j * page, page)], sem_ik)

    def kv_copies(c, j, slot):
        pg = pt_ref[b, c * chunk_pages + j]
        dst = pl.ds(j * page, page)
        copies = []
        for g in range(N_KV_HEADS):
            copies.append(pltpu.make_async_copy(ck_ref.at[l, pg, :, g, :], k_buf.at[slot, g, dst], sem_k.at[slot]))
            copies.append(pltpu.make_async_copy(cv_ref.at[l, pg, :, g, :], v_buf.at[slot, g, dst], sem_v.at[slot]))
        return copies

    def start_chunk(c, slot):
        def body(j, carry):
            for cp in kv_copies(c, j, slot):
                cp.start()
            return carry
        lax.fori_loop(0, chunk_pages, body, 0)

    def wait_chunk(c, slot):
        def body(j, carry):
            for cp in kv_copies(c, j, slot):
                cp.wait()
            return carry
        lax.fori_loop(0, chunk_pages, body, 0)

    def start_ik(j, carry):
        ik_copy(j).start()
        return carry

    def wait_ik(j, carry):
        ik_copy(j).wait()
        return carry

    lax.fori_loop(0, n_pages, start_ik, 0)
    start_chunk(0, 0)
    lax.fori_loop(0, n_pages, wait_ik, 0)

    rowsl = slice(None)
    iq = proj_ref[rowsl, OFF_IQ:OFF_IQ + N_IDX_HEADS * IDX_DIM]
    iq_h = jnp.concatenate([iq[:, h * IDX_DIM:(h + 1) * IDX_DIM] for h in range(N_IDX_HEADS)], axis=0)
    w_row = proj_ref[rowsl, OFF_IKW + IDX_DIM:OFF_IKW + IDX_DIM + N_IDX_HEADS]
    w_col = _row_to_col(w_row) * (IDX_DIM ** -0.5 * N_IDX_HEADS ** -0.5)
    ik_new = proj_ref[rowsl, OFF_IKW:OFF_IKW + IDX_DIM]

    iq_b = iq_h.astype(BF16)
    for c in range(n_chunks):
        lg = _dot_nt(iq_b, ik_buf[c * chunk:(c + 1) * chunk, :].astype(BF16))
        s = jnp.sum(w_col * jnp.maximum(lg, 0.0), axis=0, keepdims=True)
        key_scr[c] = _sort_key(s)
    lg_new = jnp.sum(iq_b.astype(F32) * ik_new.astype(BF16).astype(F32), axis=1, keepdims=True)
    key_new = _sort_key(jnp.sum(w_col * jnp.maximum(lg_new, 0.0), axis=0, keepdims=True))

    def count_ge(cand):
        cnt = jnp.where(key_new >= cand, 1.0, 0.0)
        for c in range(n_chunks):
            cnt = cnt + jnp.sum(jnp.where(key_scr[c] >= cand, 1.0, 0.0), axis=1, keepdims=True)
        return cnt

    thr = _kth_largest_key(count_ge, float(n_sel), (1, 1))

    q = proj_ref[rowsl, OFF_Q:OFF_Q + ATT_WIDTH]
    q_h = jnp.concatenate([q[:, h * HEAD_DIM:(h + 1) * HEAD_DIM] for h in range(N_ATT_HEADS)], axis=0)
    q_b = q_h.astype(BF16)
    k_new = proj_ref[rowsl, OFF_K:OFF_K + KV_WIDTH]
    v_new = proj_ref[rowsl, OFF_V:OFF_V + KV_WIDTH]
    scale = HEAD_DIM ** -0.5

    def att_chunk(c, carry):
        slot = c % 2

        @pl.when(c + 1 < n_chunks)
        def _():
            start_chunk(c + 1, 1 - slot)

        wait_chunk(c, slot)
        sel = key_scr[c] >= thr
        out = []
        for g in range(N_KV_HEADS):
            m_old, l_old, acc_old = carry[g]
            kg = k_buf[slot, g].astype(BF16)
            vg = v_buf[slot, g].astype(BF16)
            s = _dot_nt(q_b[g * KV_GROUP:(g + 1) * KV_GROUP], kg) * scale
            s = jnp.where(sel, s, MASK_VALUE)
            m_new = jnp.maximum(m_old, jnp.max(s, axis=1, keepdims=True))
            p = jnp.where(sel, jnp.exp(s - m_new), 0.0)
            a = jnp.exp(m_old - m_new)
            out.append((m_new, a * l_old + jnp.sum(p, axis=1, keepdims=True),
                        a * acc_old + _dot(p.astype(BF16), vg)))
        return tuple(out)

    init = tuple((jnp.full((KV_GROUP, 1), MASK_VALUE, F32), jnp.zeros((KV_GROUP, 1), F32),
                  jnp.zeros((KV_GROUP, HEAD_DIM), F32)) for _ in range(N_KV_HEADS))
    res = lax.fori_loop(0, n_chunks, att_chunk, init)

    sel_new = key_new >= thr
    for g in range(N_KV_HEADS):
        m_old, l_old, acc_old = res[g]
        kg = k_new[:, g * HEAD_DIM:(g + 1) * HEAD_DIM].astype(BF16).astype(F32)
        vg = v_new[:, g * HEAD_DIM:(g + 1) * HEAD_DIM].astype(BF16).astype(F32)
        qg = q_b[g * KV_GROUP:(g + 1) * KV_GROUP].astype(F32)
        s = jnp.sum(qg * kg, axis=1, keepdims=True) * scale
        s = jnp.where(sel_new, s, MASK_VALUE)
        m_new = jnp.maximum(m_old, s)
        p = jnp.where(sel_new, jnp.exp(s - m_new), 0.0)
        a = jnp.exp(m_old - m_new)
        l_new = a * l_old + p
        acc = a * acc_old + p.astype(BF16).astype(F32) * vg
        o = acc / l_new
        for r in range(KV_GROUP):
            hh = g * KV_GROUP + r
            o_ref[rowsl, hh * HEAD_DIM:(hh + 1) * HEAD_DIM] = o[r:r + 1, :].astype(o_ref.dtype)


def _attn_decode(proj, page_table, cache_k, cache_v, cache_idx_k, l):
    db, n_pages = page_table.shape
    page = cache_k.shape[2]
    past = n_pages * page
    n_sel = min(TOPK_MAX, (past + 1) // 4)
    chunk_pages = _tile(n_pages, 16)
    kern = functools.partial(_attn_decode_kernel, l=l, n_pages=n_pages, page=page,
                             chunk_pages=chunk_pages, n_sel=n_sel)
    grid_spec = pltpu.PrefetchScalarGridSpec(
        num_scalar_prefetch=1,
        grid=(db,),
        in_specs=[pl.BlockSpec((None, 1, PACK_WIDTH), lambda b, pt: (b, 0, 0)),
                  pl.BlockSpec(memory_space=pl.ANY),
                  pl.BlockSpec(memory_space=pl.ANY),
                  pl.BlockSpec(memory_space=pl.ANY)],
        out_specs=pl.BlockSpec((None, 1, ATT_WIDTH), lambda b, pt: (b, 0, 0)),
        scratch_shapes=[pltpu.VMEM((past, IDX_DIM), F32),
                        pltpu.VMEM((2, N_KV_HEADS, chunk_pages * page, HEAD_DIM), F32),
                        pltpu.VMEM((2, N_KV_HEADS, chunk_pages * page, HEAD_DIM), F32),
                        pltpu.VMEM((n_pages // chunk_pages, 1, chunk_pages * page), I32),
                        pltpu.SemaphoreType.DMA,
                        pltpu.SemaphoreType.DMA((2,)),
                        pltpu.SemaphoreType.DMA((2,))])
    return pl.pallas_call(
        kern,
        out_shape=jax.ShapeDtypeStruct((db, 1, ATT_WIDTH), F32),
        grid_spec=grid_spec,
        compiler_params=_params(("arbitrary",)),
        name="attn_decode",
    )(page_table, proj.reshape(db, 1, PACK_WIDTH), cache_k, cache_v, cache_idx_k).reshape(db, ATT_WIDTH)


def _hgrn_decode_kernel(proj_ref, s0_ref, lb_ref, ng_ref, o_ref, s_ref):
    rowsl = slice(None)
    for h in range(N_HGRN_HEADS):
        cols = lambda off: slice(off + h * HGRN_DK, off + (h + 1) * HGRN_DK)
        lb = lb_ref[:, h * HGRN_DK:(h + 1) * HGRN_DK]
        q, f, kk = _hgrn_gates(proj_ref[rowsl, cols(OFF_HQ)], proj_ref[rowsl, cols(OFF_HF)], lb)
        v = proj_ref[rowsl, cols(OFF_HI)]
        s_new = _row_to_col(f) * s0_ref[h] + _row_to_col(kk) * v
        s_ref[h] = s_new
        o = jnp.sum(_row_to_col(q) * s_new, axis=0, keepdims=True)
        ng = ng_ref[:, h * HGRN_DV:(h + 1) * HGRN_DV]
        o_ref[rowsl, h * HGRN_DV:(h + 1) * HGRN_DV] = _hgrn_finish(
            o, proj_ref[rowsl, cols(OFF_HG)], ng).astype(o_ref.dtype)


def _hgrn_decode(proj, state, lb, ng, l):
    db = proj.shape[0]
    st_spec_in = pl.BlockSpec((None, None, N_HGRN_HEADS, HGRN_DK, HGRN_DV), lambda b: (l, b, 0, 0, 0))
    o, s = pl.pallas_call(
        _hgrn_decode_kernel,
        out_shape=(jax.ShapeDtypeStruct((db, 1, HGRN_WIDTH), F32),
                   jax.ShapeDtypeStruct((db, N_HGRN_HEADS, HGRN_DK, HGRN_DV), F32)),
        grid=(db,),
        in_specs=[pl.BlockSpec((None, 1, PACK_WIDTH), lambda b: (b, 0, 0)),
                  st_spec_in,
                  pl.BlockSpec((1, HGRN_WIDTH), lambda b: (0, 0)),
                  pl.BlockSpec((1, HGRN_WIDTH), lambda b: (0, 0))],
        out_specs=(pl.BlockSpec((None, 1, HGRN_WIDTH), lambda b: (b, 0, 0)),
                   pl.BlockSpec((None, N_HGRN_HEADS, HGRN_DK, HGRN_DV), lambda b: (b, 0, 0, 0))),
        compiler_params=_params(("parallel",)),
        name="hgrn_decode",
    )(proj.reshape(db, 1, PACK_WIDTH), state, lb.reshape(1, HGRN_WIDTH), ng.reshape(1, HGRN_WIDTH))
    return o.reshape(db, HGRN_WIDTH), s


def _pack_w_in(w_in):
    offs = [0]
    for s in IN_SPLITS:
        offs.append(offs[-1] + s)
    q, k, v, iq, ik, iw, hq, hf, hi, hg, ga, gb = [w_in[:, :, offs[i]:offs[i + 1]] for i in range(len(IN_SPLITS))]
    pad = jnp.zeros(w_in.shape[:2] + (PACK_WIDTH - OFF_IKW - IDX_DIM - N_IDX_HEADS,), w_in.dtype)
    return jnp.concatenate([ga, gb, q, iq, hq, hf, hi, hg, k, v, ik, iw, pad], axis=-1).astype(BF16)


def _mods(mod_l, rows, per_row):
    m = mod_l[rows]
    parts = jnp.split(m, 6, axis=-1)
    if per_row:
        return [p[None, :, :] for p in parts]
    return [p[:, None, :] for p in parts]


def kernel(x_prompt, x_sample, c_prompt, c_sample, cache_k, cache_v, cache_idx_k, state_hgrn, page_table,
           w_ada, b_ada, w_in, w_up_a, w_up_b, w_o, hgrn_norm_g, hgrn_lb_logits, ln1_g, ln1_b,
           w_router, b_router, w1, w3, w2, ln2_g, ln2_b):
    bp, t, _ = x_prompt.shape
    db = x_sample.shape[0]
    lbp = jax.nn.softmax(hgrn_lb_logits.astype(F32), axis=0)
    lower_bounds = jnp.cumsum(lbp, axis=0) - lbp[0]

    w_pack = _pack_w_in(w_in)
    w_up_a_b, w_up_b_b, w_o_b = w_up_a.astype(BF16), w_up_b.astype(BF16), w_o.astype(BF16)
    w1_b, w3_b, w2_b = w1.astype(BF16), w3.astype(BF16), w2.astype(BF16)
    w_router_t = w_router.T

    n_c = bp + db
    c_rows = -(-n_c // 8) * 8
    c_all = jnp.concatenate([c_prompt, c_sample, jnp.zeros((c_rows - n_c, D_MODEL), F32)], axis=0)
    mod = _ada(c_all, w_ada, b_ada)

    xp = x_prompt.reshape(bp * t, D_MODEL)
    xs = x_sample.reshape(db, D_MODEL)
    tm_p = _tile(bp * t, 512)
    outs_p = {"k": [], "v": [], "ik": [], "s": []}
    outs_s = {"k": [], "v": [], "ik": [], "s": []}
    for l in range(DEPTH):
        sh1, sc1, g1, sh2, sc2, g2 = _mods(mod[l], slice(0, bp), per_row=False)
        proj = _proj(xp, sh1, sc1, w_pack, l, tm_p)
        o_a = _attn_prompt(proj, bp, t)
        o_b, s_new = _hgrn_prompt(proj, lower_bounds[l], hgrn_norm_g[l], bp, t)
        merged = _merge(o_a, o_b, proj, w_up_a_b, w_up_b_b, l, tm_p)
        xp = _out_proj(merged, w_o_b, xp, g1, ln1_g, ln1_b, l, tm_p)
        xp = _moe_grouped(xp, sh2, sc2, g2, w_router_t, b_router, w1_b, w3_b, w2_b, ln2_g, ln2_b, l, tm_p)
        outs_p["k"].append(proj[:, OFF_K:OFF_K + KV_WIDTH].reshape(bp, t, N_KV_HEADS, HEAD_DIM))
        outs_p["v"].append(proj[:, OFF_V:OFF_V + KV_WIDTH].reshape(bp, t, N_KV_HEADS, HEAD_DIM))
        outs_p["ik"].append(proj[:, OFF_IKW:OFF_IKW + IDX_DIM].reshape(bp, t, IDX_DIM))
        outs_p["s"].append(s_new)

        sh1, sc1, g1, sh2, sc2, g2 = _mods(mod[l], slice(bp, bp + db), per_row=True)
        proj = _proj(xs, sh1, sc1, w_pack, l, db)
        o_a = _attn_decode(proj, page_table, cache_k, cache_v, cache_idx_k, l)
        o_b, s_new = _hgrn_decode(proj, state_hgrn, lower_bounds[l], hgrn_norm_g[l], l)
        merged = _merge(o_a, o_b, proj, w_up_a_b, w_up_b_b, l, db)
        xs = _out_proj(merged, w_o_b, xs, g1, ln1_g, ln1_b, l, db)
        h2, gates = _router(xs, sh2, sc2, w_router_t, b_router, db)
        xs = _moe(h2, gates.T, w1_b, w3_b, w2_b, xs, g2, ln2_g, ln2_b, l, db)
        outs_s["k"].append(proj[:, OFF_K:OFF_K + KV_WIDTH].reshape(db, 1, N_KV_HEADS, HEAD_DIM))
        outs_s["v"].append(proj[:, OFF_V:OFF_V + KV_WIDTH].reshape(db, 1, N_KV_HEADS, HEAD_DIM))
        outs_s["ik"].append(proj[:, OFF_IKW:OFF_IKW + IDX_DIM].reshape(db, 1, IDX_DIM))
        outs_s["s"].append(s_new)

    return (xp.reshape(bp, t, D_MODEL), xs.reshape(db, 1, D_MODEL),
            jnp.stack(outs_p["k"]), jnp.stack(outs_p["v"]), jnp.stack(outs_p["ik"]), jnp.stack(outs_p["s"]),
            jnp.stack(outs_s["k"]), jnp.stack(outs_s["v"]), jnp.stack(outs_s["ik"]), jnp.stack(outs_s["s"]))
```

```python
import functools

import jax
import jax.numpy as jnp
from jax import lax
from jax.experimental import pallas as pl
from jax.experimental.pallas import tpu as pltpu

F32 = jnp.float32
BF16 = jnp.bfloat16
I32 = jnp.int32

DEPTH = 2
D_MODEL = 2048
N_ATT_HEADS = 8
N_KV_HEADS = 2
KV_GROUP = N_ATT_HEADS // N_KV_HEADS
HEAD_DIM = 128
ATT_WIDTH = N_ATT_HEADS * HEAD_DIM
KV_WIDTH = N_KV_HEADS * HEAD_DIM
N_IDX_HEADS = 16
IDX_DIM = 64
TOPK_MAX = 256
N_HGRN_HEADS = 8
HGRN_DK = 128
HGRN_DV = 128
HGRN_WIDTH = N_HGRN_HEADS * HGRN_DK
HGRN_CHUNK = 64
HGRN_SUB = 16
N_EXPERTS = 16
N_GROUPS = 4
EXPERTS_PER_GROUP = N_EXPERTS // N_GROUPS
EXPERT_DFF = 1024
ALPHA = (2 * DEPTH) ** 0.25
LN_EPS = 1e-5
MASK_VALUE = -1e30
INT_MIN = -2 ** 31
LOG2_E = 1.4426950408889634

IN_SPLITS = (ATT_WIDTH, KV_WIDTH, KV_WIDTH, N_IDX_HEADS * IDX_DIM, IDX_DIM, N_IDX_HEADS,
             HGRN_WIDTH, HGRN_WIDTH, HGRN_WIDTH, HGRN_WIDTH, D_MODEL, D_MODEL)

LANES = 128
OFF_GA = 0
OFF_GB = OFF_GA + D_MODEL
OFF_Q = OFF_GB + D_MODEL
OFF_IQ = OFF_Q + ATT_WIDTH
OFF_HQ = OFF_IQ + N_IDX_HEADS * IDX_DIM
OFF_HF = OFF_HQ + HGRN_WIDTH
OFF_HI = OFF_HF + HGRN_WIDTH
OFF_HG = OFF_HI + HGRN_WIDTH
OFF_K = OFF_HG + HGRN_WIDTH
OFF_V = OFF_K + KV_WIDTH
OFF_IKW = OFF_V + KV_WIDTH
PROJ_TN = 1024
PACK_WIDTH = -(-(OFF_IKW + LANES) // PROJ_TN) * PROJ_TN

VMEM_LIMIT = 56 * 1024 * 1024


def _params(semantics):
    return pltpu.CompilerParams(dimension_semantics=semantics, vmem_limit_bytes=VMEM_LIMIT)


def _tile(n, pref):
    t = min(n, pref)
    while n % t:
        t //= 2
    return t


def _ln(x):
    mu = jnp.mean(x, axis=-1, keepdims=True)
    xc = x - mu
    var = jnp.mean(xc * xc, axis=-1, keepdims=True)
    return xc * lax.rsqrt(var + LN_EPS)


def _dot(a, b):
    return jnp.dot(a, b, preferred_element_type=F32)


def _dot_nt(a, b):
    return lax.dot_general(a, b, (((1,), (1,)), ((), ())), preferred_element_type=F32)


def _split3(x):
    x1 = x.astype(BF16)
    r1 = x - x1.astype(F32)
    x2 = r1.astype(BF16)
    x3 = (r1 - x2.astype(F32)).astype(BF16)
    return x1, x2, x3


def _sort_key(s):
    bits = pltpu.bitcast(s, I32)
    return bits ^ ((bits >> 31) & 0x7FFFFFFF)


def _kth_largest_key(count_ge, n_sel, shape):
    def body(i, ans):
        bit = lax.shift_left(jnp.int32(1), jnp.int32(31) - i)
        cand = ans | bit
        cnt = count_ge(cand ^ INT_MIN)
        return jnp.where(cnt >= n_sel, cand, ans)
    ans = lax.fori_loop(0, 32, body, jnp.zeros(shape, I32))
    return ans ^ INT_MIN


def _ada_kernel(c_ref, w_ref, b_ref, o_ref):
    c = c_ref[...]
    a = (c * jax.nn.sigmoid(c)).astype(BF16)
    o_ref[...] = _dot(a, w_ref[...].astype(BF16)) + b_ref[...]


def _ada(c_all, w_ada, b_ada):
    rows = c_all.shape[0]
    width = w_ada.shape[-1]
    tn = _tile(width, 1024)
    return pl.pallas_call(
        _ada_kernel,
        out_shape=jax.ShapeDtypeStruct((DEPTH, rows, width), F32),
        grid=(DEPTH, width // tn),
        in_specs=[pl.BlockSpec((rows, D_MODEL), lambda l, j: (0, 0)),
                  pl.BlockSpec((None, D_MODEL, tn), lambda l, j: (l, 0, j)),
                  pl.BlockSpec((None, 1, tn), lambda l, j: (l, 0, j))],
        out_specs=pl.BlockSpec((None, rows, tn), lambda l, j: (l, 0, j)),
        compiler_params=_params(("parallel", "parallel")),
        name="ada",
    )(c_all, w_ada, b_ada.reshape(DEPTH, 1, width))


def _proj_kernel(x_ref, sh_ref, sc_ref, w_ref, o_ref, h_scr):
    @pl.when(pl.program_id(1) == 0)
    def _():
        h_scr[...] = (_ln(x_ref[...]) * (1.0 + sc_ref[0]) + sh_ref[0]).astype(BF16)
    o_ref[...] = _dot(h_scr[...], w_ref[...])


def _proj(x, shift, scale, w_pack, l, tm):
    n = x.shape[0]
    tiles_per_mod = (n // tm) // shift.shape[0]
    r = shift.shape[1]
    tn = PROJ_TN
    mod_spec = pl.BlockSpec((1, r, D_MODEL), lambda i, j: (i // tiles_per_mod, 0, 0))
    return pl.pallas_call(
        _proj_kernel,
        out_shape=jax.ShapeDtypeStruct((n, PACK_WIDTH), F32),
        grid=(n // tm, PACK_WIDTH // tn),
        in_specs=[pl.BlockSpec((tm, D_MODEL), lambda i, j: (i, 0)),
                  mod_spec, mod_spec,
                  pl.BlockSpec((None, D_MODEL, tn), lambda i, j: (l, 0, j))],
        out_specs=pl.BlockSpec((tm, tn), lambda i, j: (i, j)),
        scratch_shapes=[pltpu.VMEM((tm, D_MODEL), BF16)],
        compiler_params=_params(("parallel", "arbitrary")),
        name="proj",
    )(x, shift, scale, w_pack)


ROW_CHUNK = 64
ATT_ROWS = 128


def _lane_fold(x, op):
    acc = x[:, :LANES]
    for i in range(1, x.shape[1] // LANES):
        acc = op(acc, x[:, i * LANES:(i + 1) * LANES])
    return acc


def _attn_prompt_kernel(q_ref, iq_ref, iwq_ref, k_ref, v_ref, ikw_ref, o_ref,
                        kb_scr, va_scr, ikb_scr, iqs_scr, qs_scr, key_scr, m_scr, acc_scr, xb_scr,
                        *, tq, n_sel, col_bits):
    qi = pl.program_id(1)
    tk = tq
    n_lane_tiles = tk // LANES

    @pl.when(qi == 0)
    def _():
        kb_scr[...] = k_ref[...].astype(BF16)
        for g in range(N_KV_HEADS):
            va_scr[:, 2 * g * HEAD_DIM:(2 * g + 1) * HEAD_DIM] = (
                v_ref[:, g * HEAD_DIM:(g + 1) * HEAD_DIM].astype(BF16))
            va_scr[:, (2 * g + 1) * HEAD_DIM:(2 * g + 2) * HEAD_DIM] = jnp.ones((va_scr.shape[0], HEAD_DIM), BF16)
        ikb_scr[...] = ikw_ref[:, :IDX_DIM].astype(BF16)

    iq = iq_ref[...]
    for h in range(N_IDX_HEADS):
        iqs_scr[h * tq:(h + 1) * tq, :] = iq[:, h * IDX_DIM:(h + 1) * IDX_DIM].astype(BF16)
    q = q_ref[...] * (HEAD_DIM ** -0.5 * LOG2_E)
    for h in range(N_ATT_HEADS):
        qs_scr[h * tq:(h + 1) * tq, :] = q[:, h * HEAD_DIM:(h + 1) * HEAD_DIM].astype(BF16)
    w = iwq_ref[:, IDX_DIM:IDX_DIM + N_IDX_HEADS] * (IDX_DIM ** -0.5 * N_IDX_HEADS ** -0.5)
    row = qi * tq + lax.broadcasted_iota(I32, (tq, tk), 0)
    col0 = lax.broadcasted_iota(I32, (tq, tk), 1)

    def score_block(kb, carry):
        start = pl.multiple_of(kb * tk, tk)
        lg = _dot_nt(iqs_scr[...], ikb_scr[pl.ds(start, tk), :])
        s = jnp.zeros((tq, tk), F32)
        for h in range(N_IDX_HEADS):
            s = s + w[:, h:h + 1] * jnp.maximum(lg[h * tq:(h + 1) * tq, :], 0.0)
        s = jnp.where(kb * tk + col0 <= row, s, MASK_VALUE)
        key_scr[kb] = _sort_key(s)
        return carry

    lax.fori_loop(0, qi + 1, score_block, 0)

    chunks = [slice(c * ROW_CHUNK, (c + 1) * ROW_CHUNK) for c in range(tq // ROW_CHUNK)]

    def count(make_pred):
        parts = []
        for rows in chunks:
            pred = make_pred(rows)

            def body(kb, part, pred=pred, rows=rows):
                hit = jnp.where(pred(kb, key_scr[kb, rows, :]), 1.0, 0.0)
                return part + _lane_fold(hit, jnp.add)

            parts.append(lax.fori_loop(0, qi + 1, body, jnp.zeros((ROW_CHUNK, LANES), F32)))
        part = jnp.concatenate(parts, axis=0)
        return jnp.broadcast_to(jnp.sum(part, axis=1, keepdims=True), (tq, LANES))

    def wide(x):
        return jnp.concatenate([x] * n_lane_tiles, axis=1)

    def bcast(x, rows):
        return wide(x[rows])

    def count_ge(cand):
        return count(lambda rows: (lambda kb, key, c=bcast(cand, rows): key >= c))

    thr = _kth_largest_key(count_ge, float(n_sel), (tq, LANES))

    cnt_gt = count(lambda rows: (lambda kb, key, c=bcast(thr, rows): key > c))
    cnt_ge = count_ge(thr)
    need = float(n_sel) - cnt_gt
    xb_scr[...] = jnp.full(xb_scr.shape, 2 ** 31 - 1, I32)

    @pl.when(jnp.max(jnp.abs(cnt_ge - float(n_sel))) > 0.0)
    def _():
        def body(i, x):
            cand = x | lax.shift_left(jnp.int32(1), jnp.int32(col_bits - 1) - i)

            def make_pred(rows):
                c_thr, c_cand = bcast(thr, rows), bcast(cand, rows)
                c_col = lax.broadcasted_iota(I32, (ROW_CHUNK, tk), 1)
                return lambda kb, key: (key == c_thr) & (kb * tk + c_col < c_cand)

            return jnp.where(count(make_pred) < need, cand, x)

        xb_scr[...] = lax.fori_loop(0, col_bits, body, jnp.zeros((tq, LANES), I32))

    xb_w, thr_w = wide(xb_scr[...]), wide(thr)

    def bias_block(kb, carry):
        key = key_scr[kb]
        col = kb * tk + col0
        sel = ((key > thr_w) | ((key == thr_w) & (col <= xb_w))) & (col <= row)
        key_scr[kb] = pltpu.bitcast(jnp.where(sel, 0.0, -jnp.inf), I32)
        return carry

    lax.fori_loop(0, qi + 1, bias_block, 0)

    m_scr[...] = jnp.full(m_scr.shape, MASK_VALUE, F32)
    acc_scr[...] = jnp.zeros(acc_scr.shape, F32)

    def att_block(kb, carry):
        start = pl.multiple_of(kb * tk, tk)
        bias = pltpu.bitcast(key_scr[kb], F32)
        for g in range(N_KV_HEADS):
            kg = kb_scr[pl.ds(start, tk), g * HEAD_DIM:(g + 1) * HEAD_DIM]
            va = va_scr[pl.ds(start, tk), 2 * g * HEAD_DIM:(2 * g + 2) * HEAD_DIM]
            base = g * KV_GROUP * tq
            s_all = _dot_nt(qs_scr[base:base + KV_GROUP * tq, :], kg)
            ps, alphas = [], []
            for c in range(KV_GROUP * tq // ATT_ROWS):
                lo = c * ATT_ROWS
                rows = slice(base + lo, base + lo + ATT_ROWS)
                s = s_all[lo:lo + ATT_ROWS] + bias[lo % tq:lo % tq + ATT_ROWS]
                m_old = m_scr[rows, :]
                m_new = jnp.maximum(m_old, jnp.max(_lane_fold(s, jnp.maximum), axis=1, keepdims=True))
                ps.append(jnp.exp2(s - jnp.concatenate([m_new] * n_lane_tiles, axis=1)).astype(BF16))
                alphas.append(jnp.exp2(m_old - m_new))
                m_scr[rows, :] = m_new
            pv = _dot(jnp.concatenate(ps, axis=0), va)
            for c, a in enumerate(alphas):
                lo = c * ATT_ROWS
                rows = slice(base + lo, base + lo + ATT_ROWS)
                acc_scr[rows, :] = jnp.concatenate([a, a], axis=1) * acc_scr[rows, :] + pv[lo:lo + ATT_ROWS]
        return carry

    lax.fori_loop(0, qi + 1, att_block, 0)
    for h in range(N_ATT_HEADS):
        acc = acc_scr[h * tq:(h + 1) * tq, :]
        o_ref[:, h * HEAD_DIM:(h + 1) * HEAD_DIM] = (acc[:, :HEAD_DIM] / acc[:, HEAD_DIM:]).astype(o_ref.dtype)


def _attn_prompt(proj, b, t):
    tq = _tile(t, 256)
    n_sel = min(TOPK_MAX, t // 4)
    proj3 = proj.reshape(b, t, PACK_WIDTH)
    kern = functools.partial(_attn_prompt_kernel, tq=tq, n_sel=n_sel, col_bits=t.bit_length())
    out = pl.pallas_call(
        kern,
        out_shape=jax.ShapeDtypeStruct((b, t, ATT_WIDTH), BF16),
        grid=(b, t // tq),
        in_specs=[pl.BlockSpec((None, tq, ATT_WIDTH), lambda bi, qi: (bi, qi, OFF_Q // ATT_WIDTH)),
                  pl.BlockSpec((None, tq, ATT_WIDTH), lambda bi, qi: (bi, qi, OFF_IQ // ATT_WIDTH)),
                  pl.BlockSpec((None, tq, LANES), lambda bi, qi: (bi, qi, OFF_IKW // LANES)),
                  pl.BlockSpec((None, t, KV_WIDTH), lambda bi, qi: (bi, 0, OFF_K // KV_WIDTH)),
                  pl.BlockSpec((None, t, KV_WIDTH), lambda bi, qi: (bi, 0, OFF_V // KV_WIDTH)),
                  pl.BlockSpec((None, t, LANES), lambda bi, qi: (bi, 0, OFF_IKW // LANES))],
        out_specs=pl.BlockSpec((None, tq, ATT_WIDTH), lambda bi, qi: (bi, qi, 0)),
        scratch_shapes=[pltpu.VMEM((t, KV_WIDTH), BF16),
                        pltpu.VMEM((t, 2 * KV_WIDTH), BF16),
                        pltpu.VMEM((t, IDX_DIM), BF16),
                        pltpu.VMEM((N_IDX_HEADS * tq, IDX_DIM), BF16),
                        pltpu.VMEM((N_ATT_HEADS * tq, HEAD_DIM), BF16),
                        pltpu.VMEM((t // tq, tq, tq), I32),
                        pltpu.VMEM((N_ATT_HEADS * tq, LANES), F32),
                        pltpu.VMEM((N_ATT_HEADS * tq, 2 * HEAD_DIM), F32),
                        pltpu.VMEM((tq, LANES), I32)],
        compiler_params=_params(("parallel", "arbitrary")),
        name="attn_prompt",
    )(proj3, proj3, proj3, proj3, proj3, proj3)
    return out.reshape(b * t, ATT_WIDTH)


def _hgrn_gates(hq, hf, lb):
    q = hq * jax.nn.sigmoid(hq)
    f = lb + (1.0 - lb) * jax.nn.sigmoid(hf)
    kk = (1.0 - lb) * jax.nn.sigmoid(-hf)
    return q, f, kk


def _hgrn_finish(o, hg, ng):
    o = o * lax.rsqrt(jnp.mean(o * o, axis=-1, keepdims=True) + LN_EPS)
    return o * ng * (hg * jax.nn.sigmoid(hg))


HGRN_HEADS_PER_STEP = 4


def _hgrn_chunk(heads, tri, ones):
    c_len, sub = HGRN_CHUNK, HGRN_SUB
    n_sub = c_len // sub
    t_idx = lax.broadcasted_iota(I32, (sub, 1), 0)
    bs = []
    for q, f, kk, v, st in heads:
        l1, l2, l3 = _split3(jnp.log(f))
        bs.append(_dot(tri, l1) + _dot(tri, l2) + _dot(tri, l3))
    stage2 = []
    for (q, f, kk, v, st), b in zip(heads, bs):
        v_b = v.astype(BF16)
        inter = _dot_nt((q * jnp.exp(b)).astype(BF16), st.astype(BF16))
        rs, cross = [], []
        for j in range(n_sub):
            lo = j * sub
            qt, bt = q[lo:lo + sub], b[lo:lo + sub]
            ps = []
            for s in range(sub):
                valid = t_idx >= s
                dec = jnp.exp(jnp.where(valid, bt - b[lo + s:lo + s + 1, :], 0.0))
                ps.append(jnp.where(valid, qt * kk[lo + s:lo + s + 1, :] * dec, 0.0).astype(BF16))
            rs.append(_dot(jnp.concatenate(ps, axis=0), ones))
            if lo:
                b_edge = b[lo - 1:lo, :]
                qd = (qt * jnp.exp(bt - b_edge)).astype(BF16)
                kd = (kk[:lo] * jnp.exp(b_edge - b[:lo])).astype(BF16)
                cross.append(_dot_nt(qd, kd))
        b_last = b[c_len - 1:c_len, :]
        kd = (kk * jnp.exp(b_last - b)).astype(BF16)
        upd = lax.dot_general(v_b, kd, (((0,), (0,)), ((), ())), preferred_element_type=F32)
        stage2.append((v_b, inter, rs, cross, st * jnp.exp(b_last) + upd))
    out = []
    for (q, f, kk, v, st), (v_b, inter, rs, cross, st_new) in zip(heads, stage2):
        rows = []
        for j in range(n_sub):
            lo = j * sub
            acc = inter[lo:lo + sub]
            if lo:
                acc = acc + _dot(cross[j - 1].astype(BF16), v_b[:lo])
            for s in range(sub):
                acc = acc + rs[j][s * sub:(s + 1) * sub, :] * v[lo + s:lo + s + 1, :]
            rows.append(acc)
        out.append((jnp.concatenate(rows, axis=0), st_new))
    return out


def _hgrn_prompt_kernel(hq_ref, hf_ref, hi_ref, hg_ref, lb_ref, ng_ref, o_ref, s_ref, st_scr, *, n_chunks):
    c_len = HGRN_CHUNK
    ti = pl.program_id(2)

    @pl.when(ti == 0)
    def _():
        st_scr[...] = jnp.zeros(st_scr.shape, F32)

    r_i = lax.broadcasted_iota(I32, (c_len, c_len), 0)
    c_i = lax.broadcasted_iota(I32, (c_len, c_len), 1)
    tri = jnp.where(c_i <= r_i, 1.0, 0.0).astype(BF16)
    ones = jnp.ones((HGRN_DK, HGRN_DV), BF16)

    def chunk(c, carry):
        sl = pl.ds(pl.multiple_of(c * c_len, c_len), c_len)
        cols = [slice(hh * HGRN_DK, (hh + 1) * HGRN_DK) for hh in range(HGRN_HEADS_PER_STEP)]
        heads = []
        for hh, cs in enumerate(cols):
            q, f, kk = _hgrn_gates(hq_ref[sl, cs], hf_ref[sl, cs], lb_ref[:, cs])
            heads.append((q, f, kk, hi_ref[sl, cs], st_scr[hh]))
        for hh, (o, st_new) in enumerate(_hgrn_chunk(heads, tri, ones)):
            st_scr[hh] = st_new
            o_ref[sl, cols[hh]] = _hgrn_finish(o, hg_ref[sl, cols[hh]], ng_ref[:, cols[hh]]).astype(o_ref.dtype)
        return carry

    lax.fori_loop(0, n_chunks, chunk, 0)

    @pl.when(ti == pl.num_programs(2) - 1)
    def _():
        for hh in range(HGRN_HEADS_PER_STEP):
            s_ref[hh] = st_scr[hh].T


def _hgrn_prompt(proj, lb, ng, b, t):
    tc = _tile(t, 512)
    n_chunks = tc // HGRN_CHUNK
    hp = HGRN_HEADS_PER_STEP
    width = hp * HGRN_DK
    proj3 = proj.reshape(b, t, PACK_WIDTH)

    def col(off):
        return pl.BlockSpec((None, tc, width), lambda bi, h, ti: (bi, ti, off // width + h))

    vec = pl.BlockSpec((None, 1, width), lambda bi, h, ti: (h, 0, 0))
    o, s = pl.pallas_call(
        functools.partial(_hgrn_prompt_kernel, n_chunks=n_chunks),
        out_shape=(jax.ShapeDtypeStruct((b, t, HGRN_WIDTH), BF16),
                   jax.ShapeDtypeStruct((b, N_HGRN_HEADS, HGRN_DK, HGRN_DV), F32)),
        grid=(b, N_HGRN_HEADS // hp, t // tc),
        in_specs=[col(OFF_HQ), col(OFF_HF), col(OFF_HI), col(OFF_HG), vec, vec],
        out_specs=(pl.BlockSpec((None, tc, width), lambda bi, h, ti: (bi, ti, h)),
                   pl.BlockSpec((None, hp, HGRN_DK, HGRN_DV), lambda bi, h, ti: (bi, h, 0, 0))),
        scratch_shapes=[pltpu.VMEM((hp, HGRN_DV, HGRN_DK), F32)],
        compiler_params=_params(("parallel", "parallel", "arbitrary")),
        name="hgrn_prompt",
    )(proj3, proj3, proj3, proj3,
      lb.reshape(N_HGRN_HEADS // hp, 1, width), ng.reshape(N_HGRN_HEADS // hp, 1, width))
    return o.reshape(b * t, HGRN_WIDTH), s


def _merge_kernel(oa_ref, ob_ref, ga_ref, gb_ref, wa_ref, wb_ref, o_ref):
    ya = _dot(oa_ref[...].astype(BF16), wa_ref[...])
    yb = _dot(ob_ref[...].astype(BF16), wb_ref[...])
    o_ref[...] = (jax.nn.sigmoid(ga_ref[...]) * ya + jax.nn.sigmoid(gb_ref[...]) * yb).astype(o_ref.dtype)


def _merge(o_a, o_b, proj, w_up_a, w_up_b, l, tm):
    n = o_a.shape[0]
    return pl.pallas_call(
        _merge_kernel,
        out_shape=jax.ShapeDtypeStruct((n, D_MODEL), BF16),
        grid=(n // tm,),
        in_specs=[pl.BlockSpec((tm, ATT_WIDTH), lambda i: (i, 0)),
                  pl.BlockSpec((tm, HGRN_WIDTH), lambda i: (i, 0)),
                  pl.BlockSpec((tm, D_MODEL), lambda i: (i, OFF_GA // D_MODEL)),
                  pl.BlockSpec((tm, D_MODEL), lambda i: (i, OFF_GB // D_MODEL)),
                  pl.BlockSpec((None, ATT_WIDTH, D_MODEL), lambda i: (l, 0, 0)),
                  pl.BlockSpec((None, HGRN_WIDTH, D_MODEL), lambda i: (l, 0, 0))],
        out_specs=pl.BlockSpec((tm, D_MODEL), lambda i: (i, 0)),
        compiler_params=_params(("parallel",)),
        name="merge",
    )(o_a, o_b, proj, proj, w_up_a, w_up_b)


def _post_norm(x, gate, y, g, b):
    return _ln(ALPHA * x + gate * y) * g + b


def _out_kernel(m_ref, w_ref, x_ref, gate_ref, lng_ref, lnb_ref, o_ref):
    y = _dot(m_ref[...], w_ref[...])
    o_ref[...] = _post_norm(x_ref[...], gate_ref[0], y, lng_ref[...], lnb_ref[...])


def _out_proj(merged, w_o, x, gate, ln_g, ln_b, l, tm):
    n = x.shape[0]
    tiles_per_mod = (n // tm) // gate.shape[0]
    vec = pl.BlockSpec((None, 1, D_MODEL), lambda i: (l, 0, 0))
    return pl.pallas_call(
        _out_kernel,
        out_shape=jax.ShapeDtypeStruct((n, D_MODEL), F32),
        grid=(n // tm,),
        in_specs=[pl.BlockSpec((tm, D_MODEL), lambda i: (i, 0)),
                  pl.BlockSpec((None, D_MODEL, D_MODEL), lambda i: (l, 0, 0)),
                  pl.BlockSpec((tm, D_MODEL), lambda i: (i, 0)),
                  pl.BlockSpec((1, gate.shape[1], D_MODEL), lambda i: (i // tiles_per_mod, 0, 0)),
                  vec, vec],
        out_specs=pl.BlockSpec((tm, D_MODEL), lambda i: (i, 0)),
        compiler_params=_params(("parallel",)),
        name="out_proj",
    )(merged, w_o, x, gate, ln_g.reshape(DEPTH, 1, D_MODEL), ln_b.reshape(DEPTH, 1, D_MODEL))


def _route(h, wr, br):
    h1, h2, _ = _split3(h)
    w1, w2, _ = _split3(wr)
    logits = _dot_nt(w1, h1) + _dot_nt(w1, h2) + _dot_nt(w2, h1)
    aff = jax.nn.sigmoid(logits)
    sel = aff + br
    rows = [sel[e:e + 1, :] for e in range(N_EXPERTS)]
    grp = []
    for g in range(N_GROUPS):
        a, b, c, d = rows[g * EXPERTS_PER_GROUP:(g + 1) * EXPERTS_PER_GROUP]
        hi1, lo1 = jnp.maximum(a, b), jnp.minimum(a, b)
        hi2, lo2 = jnp.maximum(c, d), jnp.minimum(c, d)
        grp.append(jnp.maximum(hi1, hi2) + jnp.maximum(jnp.minimum(hi1, hi2), jnp.maximum(lo1, lo2)))
    best = functools.reduce(jnp.maximum, grp)
    taken = jnp.zeros_like(best)
    picked = []
    for g in range(N_GROUPS):
        is_g = jnp.where(grp[g] == best, 1.0, 0.0) * (1.0 - taken)
        taken = taken + is_g
        for e in range(g * EXPERTS_PER_GROUP, (g + 1) * EXPERTS_PER_GROUP):
            rank = jnp.zeros_like(best)
            for o in range(g * EXPERTS_PER_GROUP, (g + 1) * EXPERTS_PER_GROUP):
                if o < e:
                    rank = rank + jnp.where(rows[o] >= rows[e], 1.0, 0.0)
                elif o > e:
                    rank = rank + jnp.where(rows[o] > rows[e], 1.0, 0.0)
            picked.append(is_g * jnp.where(rank < 2.0, 1.0, 0.0))
    picked = jnp.concatenate(picked, axis=0)
    gate = picked * aff
    return picked, gate / jnp.sum(gate, axis=0, keepdims=True)


def _router_kernel(x_ref, sh_ref, sc_ref, wr_ref, br_ref, h_ref, g_ref):
    h = _ln(x_ref[...]) * (1.0 + sc_ref[0]) + sh_ref[0]
    h_ref[...] = h.astype(h_ref.dtype)
    _, g_ref[...] = _route(h, wr_ref[...], br_ref[...])


def _router_sorted_kernel(x_ref, sh_ref, sc_ref, wr_ref, br_ref, h_ref, eid_ref, gw_ref, pos_ref, cnt_ref, base_scr):
    tm = x_ref.shape[0]

    @pl.when(pl.program_id(0) == 0)
    def _():
        base_scr[...] = jnp.zeros(base_scr.shape, F32)

    h = _ln(x_ref[...]) * (1.0 + sc_ref[0]) + sh_ref[0]
    h_ref[...] = h.astype(h_ref.dtype)
    picked, gate = _route(h, wr_ref[...], br_ref[...])
    r = lax.broadcasted_iota(I32, (tm, tm), 0)
    c = lax.broadcasted_iota(I32, (tm, tm), 1)
    earlier = jnp.where(r < c, 1.0, 0.0).astype(BF16)
    base = base_scr[...]
    rank = _dot(picked.astype(BF16), earlier) + jnp.concatenate([base] * (tm // LANES), axis=1)
    base_scr[...] = base + jnp.sum(picked, axis=1, keepdims=True)
    cnt_ref[...] = base_scr[...]
    eio = lax.broadcasted_iota(I32, picked.shape, 0).astype(F32)
    e_lo = jnp.min(jnp.where(picked > 0.0, eio, float(N_EXPERTS)), axis=0, keepdims=True)
    e_hi = jnp.max(jnp.where(picked > 0.0, eio, -1.0), axis=0, keepdims=True)

    def pick(e, x):
        return jnp.sum(jnp.where(eio == e, x, 0.0), axis=0, keepdims=True)

    eid_ref[...] = jnp.concatenate([e_lo, e_hi], axis=0).astype(I32)
    gw_ref[...] = jnp.concatenate([pick(e_lo, gate), pick(e_hi, gate)], axis=0)
    pos_ref[...] = jnp.concatenate([pick(e_lo, rank), pick(e_hi, rank)], axis=0).astype(I32)


def _router_sorted(x, shift, scale, w_router_t, b_router, tm):
    n = x.shape[0]
    tiles_per_mod = (n // tm) // shift.shape[0]
    mod_spec = pl.BlockSpec((1, shift.shape[1], D_MODEL), lambda i: (i // tiles_per_mod, 0, 0))
    pair = pl.BlockSpec((2, tm), lambda i: (0, i))
    return pl.pallas_call(
        _router_sorted_kernel,
        out_shape=(jax.ShapeDtypeStruct((n, D_MODEL), F32),
                   jax.ShapeDtypeStruct((2, n), I32),
                   jax.ShapeDtypeStruct((2, n), F32),
                   jax.ShapeDtypeStruct((2, n), I32),
                   jax.ShapeDtypeStruct((N_EXPERTS, LANES), F32)),
        grid=(n // tm,),
        in_specs=[pl.BlockSpec((tm, D_MODEL), lambda i: (i, 0)),
                  mod_spec, mod_spec,
                  pl.BlockSpec((N_EXPERTS, D_MODEL), lambda i: (0, 0)),
                  pl.BlockSpec((N_EXPERTS, 1), lambda i: (0, 0))],
        out_specs=(pl.BlockSpec((tm, D_MODEL), lambda i: (i, 0)), pair, pair, pair,
                   pl.BlockSpec((N_EXPERTS, LANES), lambda i: (0, 0))),
        scratch_shapes=[pltpu.VMEM((N_EXPERTS, LANES), F32)],
        compiler_params=_params(("arbitrary",)),
        name="router_sorted",
    )(x, shift, scale, w_router_t, b_router.reshape(N_EXPERTS, 1))


MOE_TILE = 256
DMA_UNROLL = 8


def _dispatch_kernel(dst_ref, h_ref, xs_in_ref, xs_ref, sem, *, td, n):
    del xs_in_ref
    first = pl.program_id(0) * td

    def copies(r):
        return [pltpu.make_async_copy(h_ref.at[pl.ds(r, 1)], xs_ref.at[pl.ds(dst_ref[k * n + first + r], 1)], sem)
                for k in range(2)]

    def start(r, carry):
        for cp in copies(r):
            cp.start()
        return carry

    def wait(r, carry):
        for cp in copies(r):
            cp.wait()
        return carry

    lax.fori_loop(0, td, start, 0, unroll=DMA_UNROLL)
    lax.fori_loop(0, td, wait, 0, unroll=DMA_UNROLL)


def _dispatch(h, dst_flat, n_slots, td):
    n = h.shape[0]
    grid_spec = pltpu.PrefetchScalarGridSpec(
        num_scalar_prefetch=1,
        grid=(n // td,),
        in_specs=[pl.BlockSpec((td, D_MODEL), lambda i, dst: (i, 0)),
                  pl.BlockSpec(memory_space=pl.ANY)],
        out_specs=pl.BlockSpec(memory_space=pl.ANY),
        scratch_shapes=[pltpu.SemaphoreType.DMA])
    return pl.pallas_call(
        functools.partial(_dispatch_kernel, td=td, n=n),
        out_shape=jax.ShapeDtypeStruct((n_slots, D_MODEL), F32),
        grid_spec=grid_spec,
        input_output_aliases={2: 0},
        compiler_params=_params(("arbitrary",)),
        name="moe_dispatch",
    )(dst_flat, h, jnp.zeros((n_slots, D_MODEL), F32))


def _experts_kernel(te_ref, tv_ref, xs_ref, w1_ref, w3_ref, w2_ref, ys_ref):
    del te_ref
    valid = tv_ref[pl.program_id(0)] > 0

    @pl.when(valid)
    def _():
        x = xs_ref[...].astype(BF16)
        a = _dot(x, w1_ref[...])
        b = _dot(x, w3_ref[...])
        ys_ref[...] = _dot((a * jax.nn.sigmoid(a) * b).astype(BF16), w2_ref[...])

    @pl.when(jnp.logical_not(valid))
    def _():
        ys_ref[...] = jnp.zeros(ys_ref.shape, F32)


def _experts(xs, tile_expert, tile_valid, w1, w3, w2, l):
    n_tiles = xs.shape[0] // MOE_TILE
    grid_spec = pltpu.PrefetchScalarGridSpec(
        num_scalar_prefetch=2,
        grid=(n_tiles,),
        in_specs=[pl.BlockSpec((MOE_TILE, D_MODEL), lambda i, te, tv: (i, 0)),
                  pl.BlockSpec((None, None, D_MODEL, EXPERT_DFF), lambda i, te, tv: (l, te[i], 0, 0)),
                  pl.BlockSpec((None, None, D_MODEL, EXPERT_DFF), lambda i, te, tv: (l, te[i], 0, 0)),
                  pl.BlockSpec((None, None, EXPERT_DFF, D_MODEL), lambda i, te, tv: (l, te[i], 0, 0))],
        out_specs=pl.BlockSpec((MOE_TILE, D_MODEL), lambda i, te, tv: (i, 0)))
    return pl.pallas_call(
        _experts_kernel,
        out_shape=jax.ShapeDtypeStruct(xs.shape, F32),
        grid_spec=grid_spec,
        compiler_params=_params(("arbitrary",)),
        name="moe_experts",
    )(tile_expert, tile_valid, xs, w1, w3, w2)


def _combine_kernel(dst_ref, ys_ref, gw_ref, x_ref, gate_ref, lng_ref, lnb_ref, o_ref, buf, sem, *, tc, n):
    i = pl.program_id(0)
    slot = i % 2

    def copies(tile, s, r):
        return [pltpu.make_async_copy(ys_ref.at[pl.ds(dst_ref[k * n + tile * tc + r], 1)],
                                      buf.at[s, k, pl.ds(r, 1)], sem.at[s]) for k in range(2)]

    def start_tile(tile, s):
        def body(r, carry):
            for cp in copies(tile, s, r):
                cp.start()
            return carry
        lax.fori_loop(0, tc, body, 0, unroll=DMA_UNROLL)

    def wait_tile(tile, s):
        def body(r, carry):
            for cp in copies(tile, s, r):
                cp.wait()
            return carry
        lax.fori_loop(0, tc, body, 0, unroll=DMA_UNROLL)

    @pl.when(i == 0)
    def _():
        start_tile(0, 0)

    @pl.when(i + 1 < pl.num_programs(0))
    def _():
        start_tile(i + 1, 1 - slot)

    wait_tile(i, slot)
    gw = gw_ref[...]
    y = gw[:, 0:1] * buf[slot, 0] + gw[:, 1:2] * buf[slot, 1]
    o_ref[...] = _post_norm(x_ref[...], gate_ref[0], y, lng_ref[...], lnb_ref[...])


def _combine(ys, dst_flat, gw, x, gate, ln_g, ln_b, l, tc):
    n = x.shape[0]
    tiles_per_mod = (n // tc) // gate.shape[0]
    vec = pl.BlockSpec((None, 1, D_MODEL), lambda i, dst: (l, 0, 0))
    grid_spec = pltpu.PrefetchScalarGridSpec(
        num_scalar_prefetch=1,
        grid=(n // tc,),
        in_specs=[pl.BlockSpec(memory_space=pl.ANY),
                  pl.BlockSpec((tc, 2), lambda i, dst: (i, 0)),
                  pl.BlockSpec((tc, D_MODEL), lambda i, dst: (i, 0)),
                  pl.BlockSpec((1, gate.shape[1], D_MODEL), lambda i, dst: (i // tiles_per_mod, 0, 0)),
                  vec, vec],
        out_specs=pl.BlockSpec((tc, D_MODEL), lambda i, dst: (i, 0)),
        scratch_shapes=[pltpu.VMEM((2, 2, tc, D_MODEL), F32),
                        pltpu.SemaphoreType.DMA((2,))])
    return pl.pallas_call(
        functools.partial(_combine_kernel, tc=tc, n=n),
        out_shape=jax.ShapeDtypeStruct((n, D_MODEL), F32),
        grid_spec=grid_spec,
        compiler_params=_params(("arbitrary",)),
        name="moe_combine",
    )(dst_flat, ys, gw, x, gate, ln_g.reshape(DEPTH, 1, D_MODEL), ln_b.reshape(DEPTH, 1, D_MODEL))


def _moe_grouped(x, shift, scale, gate, w_router_t, b_router, w1, w3, w2, ln_g, ln_b, l, tm):
    n = x.shape[0]
    h, eid, gw, pos, cnt = _router_sorted(x, shift, scale, w_router_t, b_router, tm)
    counts = cnt[:, 0].astype(I32)
    padded = (counts + MOE_TILE - 1) // MOE_TILE * MOE_TILE
    ends = jnp.cumsum(padded)
    experts = jnp.arange(N_EXPERTS, dtype=I32)[:, None, None]
    seg_start = jnp.sum(jnp.where(eid[None] == experts, (ends - padded)[:, None, None], 0), axis=0)
    dst_flat = (seg_start + pos).reshape(2 * n)
    n_tiles = 2 * n // MOE_TILE + N_EXPERTS
    tile_start = jnp.arange(n_tiles, dtype=I32) * MOE_TILE
    tile_expert = jnp.minimum(jnp.sum((tile_start[:, None] >= ends[None, :]).astype(I32), axis=1), N_EXPERTS - 1)
    tile_valid = (tile_start < ends[-1]).astype(I32)
    xs = _dispatch(h, dst_flat, n_tiles * MOE_TILE, tm)
    ys = _experts(xs, tile_expert, tile_valid, w1, w3, w2, l)
    return _combine(ys, dst_flat, gw.T, x, gate, ln_g, ln_b, l, MOE_TILE)


def _router(x, shift, scale, w_router_t, b_router, tm):
    n = x.shape[0]
    tiles_per_mod = (n // tm) // shift.shape[0]
    mod_spec = pl.BlockSpec((1, shift.shape[1], D_MODEL), lambda i: (i // tiles_per_mod, 0, 0))
    return pl.pallas_call(
        _router_kernel,
        out_shape=(jax.ShapeDtypeStruct((n, D_MODEL), BF16),
                   jax.ShapeDtypeStruct((N_EXPERTS, n), F32)),
        grid=(n // tm,),
        in_specs=[pl.BlockSpec((tm, D_MODEL), lambda i: (i, 0)),
                  mod_spec, mod_spec,
                  pl.BlockSpec((N_EXPERTS, D_MODEL), lambda i: (0, 0)),
                  pl.BlockSpec((N_EXPERTS, 1), lambda i: (0, 0))],
        out_specs=(pl.BlockSpec((tm, D_MODEL), lambda i: (i, 0)),
                   pl.BlockSpec((N_EXPERTS, tm), lambda i: (0, i))),
        compiler_params=_params(("parallel",)),
        name="router",
    )(x, shift, scale, w_router_t, b_router.reshape(N_EXPERTS, 1))


def _moe_kernel(h_ref, g_ref, w1_ref, w3_ref, w2_ref, x_ref, gate_ref, lng_ref, lnb_ref, o_ref, acc_scr):
    e = pl.program_id(1)

    @pl.when(e == 0)
    def _():
        acc_scr[...] = jnp.zeros(acc_scr.shape, F32)

    h = h_ref[...]
    a = _dot(h, w1_ref[...])
    b = _dot(h, w3_ref[...])
    gates = g_ref[...]
    lane = lax.broadcasted_iota(I32, gates.shape, 1)
    gcol = jnp.sum(jnp.where(lane == e, gates, 0.0), axis=1, keepdims=True)
    u = (a * jax.nn.sigmoid(a) * b * gcol).astype(BF16)
    acc_scr[...] += _dot(u, w2_ref[...])

    @pl.when(e == pl.num_programs(1) - 1)
    def _():
        o_ref[...] = _post_norm(x_ref[...], gate_ref[0], acc_scr[...], lng_ref[...], lnb_ref[...])


def _moe(h, gates, w1, w3, w2, x, gate, ln_g, ln_b, l, tm):
    n = x.shape[0]
    tiles_per_mod = (n // tm) // gate.shape[0]
    vec = pl.BlockSpec((None, 1, D_MODEL), lambda i, e: (l, 0, 0))
    return pl.pallas_call(
        _moe_kernel,
        out_shape=jax.ShapeDtypeStruct((n, D_MODEL), F32),
        grid=(n // tm, N_EXPERTS),
        in_specs=[pl.BlockSpec((tm, D_MODEL), lambda i, e: (i, 0)),
                  pl.BlockSpec((tm, N_EXPERTS), lambda i, e: (i, 0)),
                  pl.BlockSpec((None, None, D_MODEL, EXPERT_DFF), lambda i, e: (l, e, 0, 0)),
                  pl.BlockSpec((None, None, D_MODEL, EXPERT_DFF), lambda i, e: (l, e, 0, 0)),
                  pl.BlockSpec((None, None, EXPERT_DFF, D_MODEL), lambda i, e: (l, e, 0, 0)),
                  pl.BlockSpec((tm, D_MODEL), lambda i, e: (i, 0)),
                  pl.BlockSpec((1, gate.shape[1], D_MODEL), lambda i, e: (i // tiles_per_mod, 0, 0)),
                  vec, vec],
        out_specs=pl.BlockSpec((tm, D_MODEL), lambda i, e: (i, 0)),
        scratch_shapes=[pltpu.VMEM((tm, D_MODEL), F32)],
        compiler_params=_params(("parallel", "arbitrary")),
        name="moe",
    )(h, gates, w1, w3, w2, x, gate, ln_g.reshape(DEPTH, 1, D_MODEL), ln_b.reshape(DEPTH, 1, D_MODEL))


def _row_to_col(v):
    n = v.shape[1]
    r = lax.broadcasted_iota(I32, (n, n), 0)
    c = lax.broadcasted_iota(I32, (n, n), 1)
    return jnp.sum(jnp.where(r == c, jnp.broadcast_to(v, (n, n)), 0.0), axis=1, keepdims=True)


def _attn_decode_kernel(pt_ref, proj_ref, ck_ref, cv_ref, cik_ref, o_ref,
                        ik_buf, k_buf, v_buf, key_scr, sem_ik, sem_k, sem_v,
                        *, l, n_pages, page, chunk_pages, n_sel):
    b = pl.program_id(0)
    n_chunks = n_pages // chunk_pages
    chunk = chunk_pages * page

    def ik_copy(j):
        return pltpu.make_async_copy(cik_ref.at[l, pt_ref[b, j]], ik_buf.at[pl.ds(j * page, page)], sem_ik)

    def kv_copies(c, j, slot):
        pg = pt_ref[b, c * chunk_pages + j]
        dst = pl.ds(j * page, page)
        copies = []
        for g in range(N_KV_HEADS):
            copies.append(pltpu.make_async_copy(ck_ref.at[l, pg, :, g, :], k_buf.at[slot, g, dst], sem_k.at[slot]))
            copies.append(pltpu.make_async_copy(cv_ref.at[l, pg, :, g, :], v_buf.at[slot, g, dst], sem_v.at[slot]))
        return copies

    def start_chunk(c, slot):
        def body(j, carry):
            for cp in kv_copies(c, j, slot):
                cp.start()
            return carry
        lax.fori_loop(0, chunk_pages, body, 0)

    def wait_chunk(c, slot):
        def body(j, carry):
            for cp in kv_copies(c, j, slot):
                cp.wait()
            return carry
        lax.fori_loop(0, chunk_pages, body, 0)

    def start_ik(j, carry):
        ik_copy(j).start()
        return carry

    def wait_ik(j, carry):
        ik_copy(j).wait()
        return carry

    lax.fori_loop(0, n_pages, start_ik, 0)
    start_chunk(0, 0)
    lax.fori_loop(0, n_pages, wait_ik, 0)

    rowsl = slice(None)
    iq = proj_ref[rowsl, OFF_IQ:OFF_IQ + N_IDX_HEADS * IDX_DIM]
    iq_h = jnp.concatenate([iq[:, h * IDX_DIM:(h + 1) * IDX_DIM] for h in range(N_IDX_HEADS)], axis=0)
    w_row = proj_ref[rowsl, OFF_IKW + IDX_DIM:OFF_IKW + IDX_DIM + N_IDX_HEADS]
    w_col = _row_to_col(w_row) * (IDX_DIM ** -0.5 * N_IDX_HEADS ** -0.5)
    ik_new = proj_ref[rowsl, OFF_IKW:OFF_IKW + IDX_DIM]

    iq_b = iq_h.astype(BF16)
    for c in range(n_chunks):
        lg = _dot_nt(iq_b, ik_buf[c * chunk:(c + 1) * chunk, :].astype(BF16))
        s = jnp.sum(w_col * jnp.maximum(lg, 0.0), axis=0, keepdims=True)
        key_scr[c] = _sort_key(s)
    lg_new = jnp.sum(iq_b.astype(F32) * ik_new.astype(BF16).astype(F32), axis=1, keepdims=True)
    key_new = _sort_key(jnp.sum(w_col * jnp.maximum(lg_new, 0.0), axis=0, keepdims=True))

    def count_ge(cand):
        cnt = jnp.where(key_new >= cand, 1.0, 0.0)
        for c in range(n_chunks):
            cnt = cnt + jnp.sum(jnp.where(key_scr[c] >= cand, 1.0, 0.0), axis=1, keepdims=True)
        return cnt

    thr = _kth_largest_key(count_ge, float(n_sel), (1, 1))

    q = proj_ref[rowsl, OFF_Q:OFF_Q + ATT_WIDTH]
    q_h = jnp.concatenate([q[:, h * HEAD_DIM:(h + 1) * HEAD_DIM] for h in range(N_ATT_HEADS)], axis=0)
    q_b = q_h.astype(BF16)
    k_new = proj_ref[rowsl, OFF_K:OFF_K + KV_WIDTH]
    v_new = proj_ref[rowsl, OFF_V:OFF_V + KV_WIDTH]
    scale = HEAD_DIM ** -0.5

    def att_chunk(c, carry):
        slot = c % 2

        @pl.when(c + 1 < n_chunks)
        def _():
            start_chunk(c + 1, 1 - slot)

        wait_chunk(c, slot)
        sel = key_scr[c] >= thr
        out = []
        for g in range(N_KV_HEADS):
            m_old, l_old, acc_old = carry[g]
            kg = k_buf[slot, g].astype(BF16)
            vg = v_buf[slot, g].astype(BF16)
            s = _dot_nt(q_b[g * KV_GROUP:(g + 1) * KV_GROUP], kg) * scale
            s = jnp.where(sel, s, MASK_VALUE)
            m_new = jnp.maximum(m_old, jnp.max(s, axis=1, keepdims=True))
            p = jnp.where(sel, jnp.exp(s - m_new), 0.0)
            a = jnp.exp(m_old - m_new)
            out.append((m_new, a * l_old + jnp.sum(p, axis=1, keepdims=True),
                        a * acc_old + _dot(p.astype(BF16), vg)))
        return tuple(out)

    init = tuple((jnp.full((KV_GROUP, 1), MASK_VALUE, F32), jnp.zeros((KV_GROUP, 1), F32),
                  jnp.zeros((KV_GROUP, HEAD_DIM), F32)) for _ in range(N_KV_HEADS))
    res = lax.fori_loop(0, n_chunks, att_chunk, init)

    sel_new = key_new >= thr
    for g in range(N_KV_HEADS):
        m_old, l_old, acc_old = res[g]
        kg = k_new[:, g * HEAD_DIM:(g + 1) * HEAD_DIM].astype(BF16).astype(F32)
        vg = v_new[:, g * HEAD_DIM:(g + 1) * HEAD_DIM].astype(BF16).astype(F32)
        qg = q_b[g * KV_GROUP:(g + 1) * KV_GROUP].astype(F32)
        s = jnp.sum(qg * kg, axis=1, keepdims=True) * scale
        s = jnp.where(sel_new, s, MASK_VALUE)
        m_new = jnp.maximum(m_old, s)
        p = jnp.where(sel_new, jnp.exp(s - m_new), 0.0)
        a = jnp.exp(m_old - m_new)
        l_new = a * l_old + p
        acc = a * acc_old + p.astype(BF16).astype(F32) * vg
        o = acc / l_new
        for r in range(KV_GROUP):
            hh = g * KV_GROUP + r
            o_ref[rowsl, hh * HEAD_DIM:(hh + 1) * HEAD_DIM] = o[r:r + 1, :].astype(o_ref.dtype)


def _attn_decode(proj, page_table, cache_k, cache_v, cache_idx_k, l):
    db, n_pages = page_table.shape
    page = cache_k.shape[2]
    past = n_pages * page
    n_sel = min(TOPK_MAX, (past + 1) // 4)
    chunk_pages = _tile(n_pages, 16)
    kern = functools.partial(_attn_decode_kernel, l=l, n_pages=n_pages, page=page,
                             chunk_pages=chunk_pages, n_sel=n_sel)
    grid_spec = pltpu.PrefetchScalarGridSpec(
        num_scalar_prefetch=1,
        grid=(db,),
        in_specs=[pl.BlockSpec((None, 1, PACK_WIDTH), lambda b, pt: (b, 0, 0)),
                  pl.BlockSpec(memory_space=pl.ANY),
                  pl.BlockSpec(memory_space=pl.ANY),
                  pl.BlockSpec(memory_space=pl.ANY)],
        out_specs=pl.BlockSpec((None, 1, ATT_WIDTH), lambda b, pt: (b, 0, 0)),
        scratch_shapes=[pltpu.VMEM((past, IDX_DIM), F32),
                        pltpu.VMEM((2, N_KV_HEADS, chunk_pages * page, HEAD_DIM), F32),
                        pltpu.VMEM((2, N_KV_HEADS, chunk_pages * page, HEAD_DIM), F32),
                        pltpu.VMEM((n_pages // chunk_pages, 1, chunk_pages * page), I32),
                        pltpu.SemaphoreType.DMA,
                        pltpu.SemaphoreType.DMA((2,)),
                        pltpu.SemaphoreType.DMA((2,))])
    return pl.pallas_call(
        kern,
        out_shape=jax.ShapeDtypeStruct((db, 1, ATT_WIDTH), F32),
        grid_spec=grid_spec,
        compiler_params=_params(("arbitrary",)),
        name="attn_decode",
    )(page_table, proj.reshape(db, 1, PACK_WIDTH), cache_k, cache_v, cache_idx_k).reshape(db, ATT_WIDTH)


def _hgrn_decode_kernel(proj_ref, s0_ref, lb_ref, ng_ref, o_ref, s_ref):
    rowsl = slice(None)
    for h in range(N_HGRN_HEADS):
        cols = lambda off: slice(off + h * HGRN_DK, off + (h + 1) * HGRN_DK)
        lb = lb_ref[:, h * HGRN_DK:(h + 1) * HGRN_DK]
        q, f, kk = _hgrn_gates(proj_ref[rowsl, cols(OFF_HQ)], proj_ref[rowsl, cols(OFF_HF)], lb)
        v = proj_ref[rowsl, cols(OFF_HI)]
        s_new = _row_to_col(f) * s0_ref[h] + _row_to_col(kk) * v
        s_ref[h] = s_new
        o = jnp.sum(_row_to_col(q) * s_new, axis=0, keepdims=True)
        ng = ng_ref[:, h * HGRN_DV:(h + 1) * HGRN_DV]
        o_ref[rowsl, h * HGRN_DV:(h + 1) * HGRN_DV] = _hgrn_finish(
            o, proj_ref[rowsl, cols(OFF_HG)], ng).astype(o_ref.dtype)


def _hgrn_decode(proj, state, lb, ng, l):
    db = proj.shape[0]
    st_spec_in = pl.BlockSpec((None, None, N_HGRN_HEADS, HGRN_DK, HGRN_DV), lambda b: (l, b, 0, 0, 0))
    o, s = pl.pallas_call(
        _hgrn_decode_kernel,
        out_shape=(jax.ShapeDtypeStruct((db, 1, HGRN_WIDTH), F32),
                   jax.ShapeDtypeStruct((db, N_HGRN_HEADS, HGRN_DK, HGRN_DV), F32)),
        grid=(db,),
        in_specs=[pl.BlockSpec((None, 1, PACK_WIDTH), lambda b: (b, 0, 0)),
                  st_spec_in,
                  pl.BlockSpec((1, HGRN_WIDTH), lambda b: (0, 0)),
                  pl.BlockSpec((1, HGRN_WIDTH), lambda b: (0, 0))],
        out_specs=(pl.BlockSpec((None, 1, HGRN_WIDTH), lambda b: (b, 0, 0)),
                   pl.BlockSpec((None, N_HGRN_HEADS, HGRN_DK, HGRN_DV), lambda b: (b, 0, 0, 0))),
        compiler_params=_params(("parallel",)),
        name="hgrn_decode",
    )(proj.reshape(db, 1, PACK_WIDTH), state, lb.reshape(1, HGRN_WIDTH), ng.reshape(1, HGRN_WIDTH))
    return o.reshape(db, HGRN_WIDTH), s


def _pack_w_in(w_in):
    offs = [0]
    for s in IN_SPLITS:
        offs.append(offs[-1] + s)
    q, k, v, iq, ik, iw, hq, hf, hi, hg, ga, gb = [w_in[:, :, offs[i]:offs[i + 1]] for i in range(len(IN_SPLITS))]
    pad = jnp.zeros(w_in.shape[:2] + (PACK_WIDTH - OFF_IKW - IDX_DIM - N_IDX_HEADS,), w_in.dtype)
    return jnp.concatenate([ga, gb, q, iq, hq, hf, hi, hg, k, v, ik, iw, pad], axis=-1).astype(BF16)


def _mods(mod_l, rows, per_row):
    m = mod_l[rows]
    parts = jnp.split(m, 6, axis=-1)
    if per_row:
        return [p[None, :, :] for p in parts]
    return [p[:, None, :] for p in parts]


def kernel(x_prompt, x_sample, c_prompt, c_sample, cache_k, cache_v, cache_idx_k, state_hgrn, page_table,
           w_ada, b_ada, w_in, w_up_a, w_up_b, w_o, hgrn_norm_g, hgrn_lb_logits, ln1_g, ln1_b,
           w_router, b_router, w1, w3, w2, ln2_g, ln2_b):
    bp, t, _ = x_prompt.shape
    db = x_sample.shape[0]
    lbp = jax.nn.softmax(hgrn_lb_logits.astype(F32), axis=0)
    lower_bounds = jnp.cumsum(lbp, axis=0) - lbp[0]

    w_pack = _pack_w_in(w_in)
    w_up_a_b, w_up_b_b, w_o_b = w_up_a.astype(BF16), w_up_b.astype(BF16), w_o.astype(BF16)
    w1_b, w3_b, w2_b = w1.astype(BF16), w3.astype(BF16), w2.astype(BF16)
    w_router_t = w_router.T

    n_c = bp + db
    c_rows = -(-n_c // 8) * 8
    c_all = jnp.concatenate([c_prompt, c_sample, jnp.zeros((c_rows - n_c, D_MODEL), F32)], axis=0)
    mod = _ada(c_all, w_ada, b_ada)

    xp = x_prompt.reshape(bp * t, D_MODEL)
    xs = x_sample.reshape(db, D_MODEL)
    tm_p = _tile(bp * t, 512)
    outs_p = {"k": [], "v": [], "ik": [], "s": []}
    outs_s = {"k": [], "v": [], "ik": [], "s": []}
    for l in range(DEPTH):
        sh1, sc1, g1, sh2, sc2, g2 = _mods(mod[l], slice(0, bp), per_row=False)
        proj = _proj(xp, sh1, sc1, w_pack, l, _tile(t, 1024))
        o_a = _attn_prompt(proj, bp, t)
        o_b, s_new = _hgrn_prompt(proj, lower_bounds[l], hgrn_norm_g[l], bp, t)
        merged = _merge(o_a, o_b, proj, w_up_a_b, w_up_b_b, l, tm_p)
        xp = _out_proj(merged, w_o_b, xp, g1, ln1_g, ln1_b, l, tm_p)
        xp = _moe_grouped(xp, sh2, sc2, g2, w_router_t, b_router, w1_b, w3_b, w2_b, ln2_g, ln2_b, l, tm_p)
        outs_p["k"].append(proj[:, OFF_K:OFF_K + KV_WIDTH].reshape(bp, t, N_KV_HEADS, HEAD_DIM))
        outs_p["v"].append(proj[:, OFF_V:OFF_V + KV_WIDTH].reshape(bp, t, N_KV_HEADS, HEAD_DIM))
        outs_p["ik"].append(proj[:, OFF_IKW:OFF_IKW + IDX_DIM].reshape(bp, t, IDX_DIM))
        outs_p["s"].append(s_new)

        sh1, sc1, g1, sh2, sc2, g2 = _mods(mod[l], slice(bp, bp + db), per_row=True)
        proj = _proj(xs, sh1, sc1, w_pack, l, db)
        o_a = _attn_decode(proj, page_table, cache_k, cache_v, cache_idx_k, l)
        o_b, s_new = _hgrn_decode(proj, state_hgrn, lower_bounds[l], hgrn_norm_g[l], l)
        merged = _merge(o_a, o_b, proj, w_up_a_b, w_up_b_b, l, db)
        xs = _out_proj(merged, w_o_b, xs, g1, ln1_g, ln1_b, l, db)
        h2, gates = _router(xs, sh2, sc2, w_router_t, b_router, db)
        xs = _moe(h2, gates.T, w1_b, w3_b, w2_b, xs, g2, ln2_g, ln2_b, l, db)
        outs_s["k"].append(proj[:, OFF_K:OFF_K + KV_WIDTH].reshape(db, 1, N_KV_HEADS, HEAD_DIM))
        outs_s["v"].append(proj[:, OFF_V:OFF_V + KV_WIDTH].reshape(db, 1, N_KV_HEADS, HEAD_DIM))
        outs_s["ik"].append(proj[:, OFF_IKW:OFF_IKW + IDX_DIM].reshape(db, 1, IDX_DIM))
        outs_s["s"].append(s_new)

    return (xp.reshape(bp, t, D_MODEL), xs.reshape(db, 1, D_MODEL),
            jnp.stack(outs_p["k"]), jnp.stack(outs_p["v"]), jnp.stack(outs_p["ik"]), jnp.stack(outs_p["s"]),
            jnp.stack(outs_s["k"]), jnp.stack(outs_s["v"]), jnp.stack(outs_s["ik"]), jnp.stack(outs_s["s"]))
```

```python
import functools

import jax
import jax.numpy as jnp
from jax import lax
from jax.experimental import pallas as pl
from jax.experimental.pallas import tpu as pltpu

F32 = jnp.float32
BF16 = jnp.bfloat16
I32 = jnp.int32
I16 = jnp.int16

DEPTH = 2
D_MODEL = 2048
N_ATT_HEADS = 8
N_KV_HEADS = 2
KV_GROUP = N_ATT_HEADS // N_KV_HEADS
HEAD_DIM = 128
ATT_WIDTH = N_ATT_HEADS * HEAD_DIM
KV_WIDTH = N_KV_HEADS * HEAD_DIM
N_IDX_HEADS = 16
IDX_DIM = 64
TOPK_MAX = 256
N_HGRN_HEADS = 8
HGRN_DK = 128
HGRN_DV = 128
HGRN_WIDTH = N_HGRN_HEADS * HGRN_DK
HGRN_CHUNK = 64
HGRN_SUB = 16
N_EXPERTS = 16
N_GROUPS = 4
EXPERTS_PER_GROUP = N_EXPERTS // N_GROUPS
EXPERT_DFF = 1024
ALPHA = (2 * DEPTH) ** 0.25
LN_EPS = 1e-5
MASK_VALUE = -1e30
INT_MIN = -2 ** 31
LOG2_E = 1.4426950408889634

IN_SPLITS = (ATT_WIDTH, KV_WIDTH, KV_WIDTH, N_IDX_HEADS * IDX_DIM, IDX_DIM, N_IDX_HEADS,
             HGRN_WIDTH, HGRN_WIDTH, HGRN_WIDTH, HGRN_WIDTH, D_MODEL, D_MODEL)

LANES = 128
OFF_GA = 0
OFF_GB = OFF_GA + D_MODEL
OFF_Q = OFF_GB + D_MODEL
OFF_IQ = OFF_Q + ATT_WIDTH
OFF_HQ = OFF_IQ + N_IDX_HEADS * IDX_DIM
OFF_HF = OFF_HQ + HGRN_WIDTH
OFF_HI = OFF_HF + HGRN_WIDTH
OFF_HG = OFF_HI + HGRN_WIDTH
OFF_K = OFF_HG + HGRN_WIDTH
OFF_V = OFF_K + KV_WIDTH
OFF_IKW = OFF_V + KV_WIDTH
PROJ_TN = 1024
PACK_WIDTH = -(-(OFF_IKW + LANES) // PROJ_TN) * PROJ_TN

VMEM_LIMIT = 56 * 1024 * 1024


def _params(semantics):
    return pltpu.CompilerParams(dimension_semantics=semantics, vmem_limit_bytes=VMEM_LIMIT)


def _tile(n, pref):
    t = min(n, pref)
    while n % t:
        t //= 2
    return t


def _ln(x):
    mu = jnp.mean(x, axis=-1, keepdims=True)
    xc = x - mu
    var = jnp.mean(xc * xc, axis=-1, keepdims=True)
    return xc * lax.rsqrt(var + LN_EPS)


def _dot(a, b):
    return jnp.dot(a, b, preferred_element_type=F32)


def _dot_nt(a, b):
    return lax.dot_general(a, b, (((1,), (1,)), ((), ())), preferred_element_type=F32)


def _split3(x):
    x1 = x.astype(BF16)
    r1 = x - x1.astype(F32)
    x2 = r1.astype(BF16)
    x3 = (r1 - x2.astype(F32)).astype(BF16)
    return x1, x2, x3


def _sort_key(s):
    bits = pltpu.bitcast(s, I32)
    return bits ^ ((bits >> 31) & 0x7FFFFFFF)


def _kth_largest_key(count_ge, n_sel, shape):
    def body(i, ans):
        bit = lax.shift_left(jnp.int32(1), jnp.int32(31) - i)
        cand = ans | bit
        cnt = count_ge(cand ^ INT_MIN)
        return jnp.where(cnt >= n_sel, cand, ans)
    ans = lax.fori_loop(0, 32, body, jnp.zeros(shape, I32))
    return ans ^ INT_MIN


def _ada_kernel(c_ref, w_ref, b_ref, o_ref):
    c = c_ref[...]
    a = (c * jax.nn.sigmoid(c)).astype(BF16)
    o_ref[...] = _dot(a, w_ref[...].astype(BF16)) + b_ref[...]


def _ada(c_all, w_ada, b_ada):
    rows = c_all.shape[0]
    width = w_ada.shape[-1]
    tn = _tile(width, 1024)
    return pl.pallas_call(
        _ada_kernel,
        out_shape=jax.ShapeDtypeStruct((DEPTH, rows, width), F32),
        grid=(DEPTH, width // tn),
        in_specs=[pl.BlockSpec((rows, D_MODEL), lambda l, j: (0, 0)),
                  pl.BlockSpec((None, D_MODEL, tn), lambda l, j: (l, 0, j)),
                  pl.BlockSpec((None, 1, tn), lambda l, j: (l, 0, j))],
        out_specs=pl.BlockSpec((None, rows, tn), lambda l, j: (l, 0, j)),
        compiler_params=_params(("parallel", "parallel")),
        name="ada",
    )(c_all, w_ada, b_ada.reshape(DEPTH, 1, width))


def _proj_kernel(x_ref, sh_ref, sc_ref, w_ref, o_ref, h_scr):
    @pl.when(pl.program_id(1) == 0)
    def _():
        h_scr[...] = (_ln(x_ref[...]) * (1.0 + sc_ref[0]) + sh_ref[0]).astype(BF16)
    o_ref[...] = _dot(h_scr[...], w_ref[...])


def _proj(x, shift, scale, w_pack, l, tm):
    n = x.shape[0]
    tiles_per_mod = (n // tm) // shift.shape[0]
    r = shift.shape[1]
    tn = PROJ_TN
    mod_spec = pl.BlockSpec((1, r, D_MODEL), lambda i, j: (i // tiles_per_mod, 0, 0))
    return pl.pallas_call(
        _proj_kernel,
        out_shape=jax.ShapeDtypeStruct((n, PACK_WIDTH), F32),
        grid=(n // tm, PACK_WIDTH // tn),
        in_specs=[pl.BlockSpec((tm, D_MODEL), lambda i, j: (i, 0)),
                  mod_spec, mod_spec,
                  pl.BlockSpec((None, D_MODEL, tn), lambda i, j: (l, 0, j))],
        out_specs=pl.BlockSpec((tm, tn), lambda i, j: (i, j)),
        scratch_shapes=[pltpu.VMEM((tm, D_MODEL), BF16)],
        compiler_params=_params(("parallel", "arbitrary")),
        name="proj",
    )(x, shift, scale, w_pack)


ROW_CHUNK = 64
ATT_ROWS = 128


def _lane_fold(x, op):
    acc = x[:, :LANES]
    for i in range(1, x.shape[1] // LANES):
        acc = op(acc, x[:, i * LANES:(i + 1) * LANES])
    return acc


def _attn_prompt_kernel(q_ref, iq_ref, iwq_ref, k_ref, v_ref, ikw_ref, o_ref,
                        kb_scr, va_scr, ikb_scr, iqs_scr, qs_scr, key_scr, h16_scr, m_scr, acc_scr, xb_scr,
                        *, tq, n_sel, col_bits):
    qi = pl.program_id(1)
    tk = tq
    n_lane_tiles = tk // LANES

    @pl.when(qi == 0)
    def _():
        kb_scr[...] = k_ref[...].astype(BF16)
        for g in range(N_KV_HEADS):
            va_scr[:, 2 * g * HEAD_DIM:(2 * g + 1) * HEAD_DIM] = (
                v_ref[:, g * HEAD_DIM:(g + 1) * HEAD_DIM].astype(BF16))
            va_scr[:, (2 * g + 1) * HEAD_DIM:(2 * g + 2) * HEAD_DIM] = jnp.ones((va_scr.shape[0], HEAD_DIM), BF16)
        ikb_scr[...] = ikw_ref[:, :IDX_DIM].astype(BF16)

    iq = iq_ref[...]
    for h in range(N_IDX_HEADS):
        iqs_scr[h * tq:(h + 1) * tq, :] = iq[:, h * IDX_DIM:(h + 1) * IDX_DIM].astype(BF16)
    q = q_ref[...] * (HEAD_DIM ** -0.5 * LOG2_E)
    for h in range(N_ATT_HEADS):
        qs_scr[h * tq:(h + 1) * tq, :] = q[:, h * HEAD_DIM:(h + 1) * HEAD_DIM].astype(BF16)
    w = iwq_ref[:, IDX_DIM:IDX_DIM + N_IDX_HEADS] * (IDX_DIM ** -0.5 * N_IDX_HEADS ** -0.5)
    row = qi * tq + lax.broadcasted_iota(I32, (tq, tk), 0)
    col0 = lax.broadcasted_iota(I32, (tq, tk), 1)

    def score_block(kb, carry):
        start = pl.multiple_of(kb * tk, tk)
        lg = _dot_nt(iqs_scr[...], ikb_scr[pl.ds(start, tk), :])
        s = jnp.zeros((tq, tk), F32)
        for h in range(N_IDX_HEADS):
            s = s + w[:, h:h + 1] * jnp.maximum(lg[h * tq:(h + 1) * tq, :], 0.0)
        s = jnp.where(kb * tk + col0 <= row, s, MASK_VALUE)
        key = _sort_key(s)
        key_scr[kb] = key
        h16_scr[kb] = (key >> 16).astype(I16)
        return carry

    lax.fori_loop(0, qi + 1, score_block, 0)

    chunks = [slice(c * ROW_CHUNK, (c + 1) * ROW_CHUNK) for c in range(tq // ROW_CHUNK)]

    def count(make_pred):
        parts = []
        for rows in chunks:
            pred = make_pred(rows)

            def body(kb, part, pred=pred, rows=rows):
                hit = jnp.where(pred(kb, key_scr[kb, rows, :]), 1.0, 0.0)
                return part + _lane_fold(hit, jnp.add)

            parts.append(lax.fori_loop(0, qi + 1, body, jnp.zeros((ROW_CHUNK, LANES), F32)))
        part = jnp.concatenate(parts, axis=0)
        return jnp.broadcast_to(jnp.sum(part, axis=1, keepdims=True), (tq, LANES))

    def wide(x):
        return jnp.concatenate([x] * n_lane_tiles, axis=1)

    def bcast(x, rows):
        return wide(x[rows])

    def count_ge(cand):
        return count(lambda rows: (lambda kb, key, c=bcast(cand, rows): key >= c))

    def count16_ge(cand):
        c = wide(cand).astype(I16)

        def body(kb, part):
            hit = jnp.where(h16_scr[kb] >= c, jnp.int16(1), jnp.int16(0))
            return part + _lane_fold(hit, jnp.add)

        part = lax.fori_loop(0, qi + 1, body, jnp.zeros((tq, LANES), I16)).astype(F32)
        return jnp.broadcast_to(jnp.sum(part, axis=1, keepdims=True), (tq, LANES))

    def search16():
        def body(i, u):
            cand = u | lax.shift_left(jnp.int32(1), jnp.int32(15) - i)
            return jnp.where(count16_ge(cand - 32768) >= float(n_sel), cand, u)
        return lax.fori_loop(0, 16, body, jnp.zeros((tq, LANES), I32))

    hi_p = search16() - 32768
    hi_w = wide(hi_p)

    def low_keys(kb, carry):
        key = key_scr[kb]
        hi = key >> 16
        lo = (key & 0xFFFF) - 32768
        h16_scr[kb] = jnp.where(hi > hi_w, 32767, jnp.where(hi < hi_w, -32768, lo)).astype(I16)
        return carry

    lax.fori_loop(0, qi + 1, low_keys, 0)
    thr = hi_p * 65536 + search16()

    cnt_gt = count(lambda rows: (lambda kb, key, c=bcast(thr, rows): key > c))
    cnt_ge = count_ge(thr)
    need = float(n_sel) - cnt_gt
    xb_scr[...] = jnp.full(xb_scr.shape, 2 ** 31 - 1, I32)

    @pl.when(jnp.max(jnp.abs(cnt_ge - float(n_sel))) > 0.0)
    def _():
        def body(i, x):
            cand = x | lax.shift_left(jnp.int32(1), jnp.int32(col_bits - 1) - i)

            def make_pred(rows):
                c_thr, c_cand = bcast(thr, rows), bcast(cand, rows)
                c_col = lax.broadcasted_iota(I32, (ROW_CHUNK, tk), 1)
                return lambda kb, key: (key == c_thr) & (kb * tk + c_col < c_cand)

            return jnp.where(count(make_pred) < need, cand, x)

        xb_scr[...] = lax.fori_loop(0, col_bits, body, jnp.zeros((tq, LANES), I32))

    xb_w, thr_w = wide(xb_scr[...]), wide(thr)

    def bias_block(kb, carry):
        key = key_scr[kb]
        col = kb * tk + col0
        sel = ((key > thr_w) | ((key == thr_w) & (col <= xb_w))) & (col <= row)
        key_scr[kb] = pltpu.bitcast(jnp.where(sel, 0.0, -jnp.inf), I32)
        return carry

    lax.fori_loop(0, qi + 1, bias_block, 0)

    m_scr[...] = jnp.full(m_scr.shape, MASK_VALUE, F32)
    acc_scr[...] = jnp.zeros(acc_scr.shape, F32)

    def att_block(kb, carry):
        start = pl.multiple_of(kb * tk, tk)
        bias = pltpu.bitcast(key_scr[kb], F32)
        for g in range(N_KV_HEADS):
            kg = kb_scr[pl.ds(start, tk), g * HEAD_DIM:(g + 1) * HEAD_DIM]
            va = va_scr[pl.ds(start, tk), 2 * g * HEAD_DIM:(2 * g + 2) * HEAD_DIM]
            base = g * KV_GROUP * tq
            s_all = _dot_nt(qs_scr[base:base + KV_GROUP * tq, :], kg)
            ps, alphas = [], []
            for c in range(KV_GROUP * tq // ATT_ROWS):
                lo = c * ATT_ROWS
                rows = slice(base + lo, base + lo + ATT_ROWS)
                s = s_all[lo:lo + ATT_ROWS] + bias[lo % tq:lo % tq + ATT_ROWS]
                m_old = m_scr[rows, :]
                m_new = jnp.maximum(m_old, jnp.max(_lane_fold(s, jnp.maximum), axis=1, keepdims=True))
                ps.append(jnp.exp2(s - jnp.concatenate([m_new] * n_lane_tiles, axis=1)).astype(BF16))
                alphas.append(jnp.exp2(m_old - m_new))
                m_scr[rows, :] = m_new
            pv = _dot(jnp.concatenate(ps, axis=0), va)
            for c, a in enumerate(alphas):
                lo = c * ATT_ROWS
                rows = slice(base + lo, base + lo + ATT_ROWS)
                acc_scr[rows, :] = jnp.concatenate([a, a], axis=1) * acc_scr[rows, :] + pv[lo:lo + ATT_ROWS]
        return carry

    lax.fori_loop(0, qi + 1, att_block, 0)
    for h in range(N_ATT_HEADS):
        acc = acc_scr[h * tq:(h + 1) * tq, :]
        o_ref[:, h * HEAD_DIM:(h + 1) * HEAD_DIM] = (acc[:, :HEAD_DIM] / acc[:, HEAD_DIM:]).astype(o_ref.dtype)


def _attn_prompt(proj, b, t):
    tq = _tile(t, 256)
    n_sel = min(TOPK_MAX, t // 4)
    proj3 = proj.reshape(b, t, PACK_WIDTH)
    kern = functools.partial(_attn_prompt_kernel, tq=tq, n_sel=n_sel, col_bits=t.bit_length())
    out = pl.pallas_call(
        kern,
        out_shape=jax.ShapeDtypeStruct((b, t, ATT_WIDTH), BF16),
        grid=(b, t // tq),
        in_specs=[pl.BlockSpec((None, tq, ATT_WIDTH), lambda bi, qi: (bi, qi, OFF_Q // ATT_WIDTH)),
                  pl.BlockSpec((None, tq, ATT_WIDTH), lambda bi, qi: (bi, qi, OFF_IQ // ATT_WIDTH)),
                  pl.BlockSpec((None, tq, LANES), lambda bi, qi: (bi, qi, OFF_IKW // LANES)),
                  pl.BlockSpec((None, t, KV_WIDTH), lambda bi, qi: (bi, 0, OFF_K // KV_WIDTH)),
                  pl.BlockSpec((None, t, KV_WIDTH), lambda bi, qi: (bi, 0, OFF_V // KV_WIDTH)),
                  pl.BlockSpec((None, t, LANES), lambda bi, qi: (bi, 0, OFF_IKW // LANES))],
        out_specs=pl.BlockSpec((None, tq, ATT_WIDTH), lambda bi, qi: (bi, qi, 0)),
        scratch_shapes=[pltpu.VMEM((t, KV_WIDTH), BF16),
                        pltpu.VMEM((t, 2 * KV_WIDTH), BF16),
                        pltpu.VMEM((t, IDX_DIM), BF16),
                        pltpu.VMEM((N_IDX_HEADS * tq, IDX_DIM), BF16),
                        pltpu.VMEM((N_ATT_HEADS * tq, HEAD_DIM), BF16),
                        pltpu.VMEM((t // tq, tq, tq), I32),
                        pltpu.VMEM((t // tq, tq, tq), I16),
                        pltpu.VMEM((N_ATT_HEADS * tq, LANES), F32),
                        pltpu.VMEM((N_ATT_HEADS * tq, 2 * HEAD_DIM), F32),
                        pltpu.VMEM((tq, LANES), I32)],
        compiler_params=_params(("parallel", "arbitrary")),
        name="attn_prompt",
    )(proj3, proj3, proj3, proj3, proj3, proj3)
    return out.reshape(b * t, ATT_WIDTH)


def _hgrn_gates(hq, hf, lb):
    q = hq * jax.nn.sigmoid(hq)
    f = lb + (1.0 - lb) * jax.nn.sigmoid(hf)
    kk = (1.0 - lb) * jax.nn.sigmoid(-hf)
    return q, f, kk


def _hgrn_finish(o, hg, ng):
    o = o * lax.rsqrt(jnp.mean(o * o, axis=-1, keepdims=True) + LN_EPS)
    return o * ng * (hg * jax.nn.sigmoid(hg))


HGRN_HEADS_PER_STEP = 4


def _hgrn_chunk(heads, tri, ones):
    c_len, sub = HGRN_CHUNK, HGRN_SUB
    n_sub = c_len // sub
    t_idx = lax.broadcasted_iota(I32, (sub, 1), 0)
    bs = []
    for q, f, kk, v, st in heads:
        l1, l2, l3 = _split3(jnp.log(f))
        bs.append(_dot(tri, l1) + _dot(tri, l2) + _dot(tri, l3))
    stage2 = []
    for (q, f, kk, v, st), b in zip(heads, bs):
        v_b = v.astype(BF16)
        inter = _dot_nt((q * jnp.exp(b)).astype(BF16), st.astype(BF16))
        rs, cross = [], []
        for j in range(n_sub):
            lo = j * sub
            qt, bt = q[lo:lo + sub], b[lo:lo + sub]
            ps = []
            for s in range(sub):
                valid = t_idx >= s
                dec = jnp.exp(jnp.where(valid, bt - b[lo + s:lo + s + 1, :], 0.0))
                ps.append(jnp.where(valid, qt * kk[lo + s:lo + s + 1, :] * dec, 0.0).astype(BF16))
            rs.append(_dot(jnp.concatenate(ps, axis=0), ones))
            if lo:
                b_edge = b[lo - 1:lo, :]
                qd = (qt * jnp.exp(bt - b_edge)).astype(BF16)
                kd = (kk[:lo] * jnp.exp(b_edge - b[:lo])).astype(BF16)
                cross.append(_dot_nt(qd, kd))
        b_last = b[c_len - 1:c_len, :]
        kd = (kk * jnp.exp(b_last - b)).astype(BF16)
        upd = lax.dot_general(v_b, kd, (((0,), (0,)), ((), ())), preferred_element_type=F32)
        stage2.append((v_b, inter, rs, cross, st * jnp.exp(b_last) + upd))
    out = []
    for (q, f, kk, v, st), (v_b, inter, rs, cross, st_new) in zip(heads, stage2):
        rows = []
        for j in range(n_sub):
            lo = j * sub
            acc = inter[lo:lo + sub]
            if lo:
                acc = acc + _dot(cross[j - 1].astype(BF16), v_b[:lo])
            for s in range(sub):
                acc = acc + rs[j][s * sub:(s + 1) * sub, :] * v[lo + s:lo + s + 1, :]
            rows.append(acc)
        out.append((jnp.concatenate(rows, axis=0), st_new))
    return out


def _hgrn_prompt_kernel(hq_ref, hf_ref, hi_ref, hg_ref, lb_ref, ng_ref, o_ref, s_ref, st_scr, *, n_chunks):
    c_len = HGRN_CHUNK
    ti = pl.program_id(2)

    @pl.when(ti == 0)
    def _():
        st_scr[...] = jnp.zeros(st_scr.shape, F32)

    r_i = lax.broadcasted_iota(I32, (c_len, c_len), 0)
    c_i = lax.broadcasted_iota(I32, (c_len, c_len), 1)
    tri = jnp.where(c_i <= r_i, 1.0, 0.0).astype(BF16)
    ones = jnp.ones((HGRN_DK, HGRN_DV), BF16)

    def chunk(c, carry):
        sl = pl.ds(pl.multiple_of(c * c_len, c_len), c_len)
        cols = [slice(hh * HGRN_DK, (hh + 1) * HGRN_DK) for hh in range(HGRN_HEADS_PER_STEP)]
        heads = []
        for hh, cs in enumerate(cols):
            q, f, kk = _hgrn_gates(hq_ref[sl, cs], hf_ref[sl, cs], lb_ref[:, cs])
            heads.append((q, f, kk, hi_ref[sl, cs], st_scr[hh]))
        for hh, (o, st_new) in enumerate(_hgrn_chunk(heads, tri, ones)):
            st_scr[hh] = st_new
            o_ref[sl, cols[hh]] = _hgrn_finish(o, hg_ref[sl, cols[hh]], ng_ref[:, cols[hh]]).astype(o_ref.dtype)
        return carry

    lax.fori_loop(0, n_chunks, chunk, 0)

    @pl.when(ti == pl.num_programs(2) - 1)
    def _():
        for hh in range(HGRN_HEADS_PER_STEP):
            s_ref[hh] = st_scr[hh].T


def _hgrn_prompt(proj, lb, ng, b, t):
    tc = _tile(t, 512)
    n_chunks = tc // HGRN_CHUNK
    hp = HGRN_HEADS_PER_STEP
    width = hp * HGRN_DK
    proj3 = proj.reshape(b, t, PACK_WIDTH)

    def col(off):
        return pl.BlockSpec((None, tc, width), lambda bi, h, ti: (bi, ti, off // width + h))

    vec = pl.BlockSpec((None, 1, width), lambda bi, h, ti: (h, 0, 0))
    o, s = pl.pallas_call(
        functools.partial(_hgrn_prompt_kernel, n_chunks=n_chunks),
        out_shape=(jax.ShapeDtypeStruct((b, t, HGRN_WIDTH), BF16),
                   jax.ShapeDtypeStruct((b, N_HGRN_HEADS, HGRN_DK, HGRN_DV), F32)),
        grid=(b, N_HGRN_HEADS // hp, t // tc),
        in_specs=[col(OFF_HQ), col(OFF_HF), col(OFF_HI), col(OFF_HG), vec, vec],
        out_specs=(pl.BlockSpec((None, tc, width), lambda bi, h, ti: (bi, ti, h)),
                   pl.BlockSpec((None, hp, HGRN_DK, HGRN_DV), lambda bi, h, ti: (bi, h, 0, 0))),
        scratch_shapes=[pltpu.VMEM((hp, HGRN_DV, HGRN_DK), F32)],
        compiler_params=_params(("parallel", "parallel", "arbitrary")),
        name="hgrn_prompt",
    )(proj3, proj3, proj3, proj3,
      lb.reshape(N_HGRN_HEADS // hp, 1, width), ng.reshape(N_HGRN_HEADS // hp, 1, width))
    return o.reshape(b * t, HGRN_WIDTH), s


def _merge_kernel(oa_ref, ob_ref, ga_ref, gb_ref, wa_ref, wb_ref, o_ref):
    ya = _dot(oa_ref[...].astype(BF16), wa_ref[...])
    yb = _dot(ob_ref[...].astype(BF16), wb_ref[...])
    o_ref[...] = (jax.nn.sigmoid(ga_ref[...]) * ya + jax.nn.sigmoid(gb_ref[...]) * yb).astype(o_ref.dtype)


def _merge(o_a, o_b, proj, w_up_a, w_up_b, l, tm):
    n = o_a.shape[0]
    return pl.pallas_call(
        _merge_kernel,
        out_shape=jax.ShapeDtypeStruct((n, D_MODEL), BF16),
        grid=(n // tm,),
        in_specs=[pl.BlockSpec((tm, ATT_WIDTH), lambda i: (i, 0)),
                  pl.BlockSpec((tm, HGRN_WIDTH), lambda i: (i, 0)),
                  pl.BlockSpec((tm, D_MODEL), lambda i: (i, OFF_GA // D_MODEL)),
                  pl.BlockSpec((tm, D_MODEL), lambda i: (i, OFF_GB // D_MODEL)),
                  pl.BlockSpec((None, ATT_WIDTH, D_MODEL), lambda i: (l, 0, 0)),
                  pl.BlockSpec((None, HGRN_WIDTH, D_MODEL), lambda i: (l, 0, 0))],
        out_specs=pl.BlockSpec((tm, D_MODEL), lambda i: (i, 0)),
        compiler_params=_params(("parallel",)),
        name="merge",
    )(o_a, o_b, proj, proj, w_up_a, w_up_b)


def _post_norm(x, gate, y, g, b):
    return _ln(ALPHA * x + gate * y) * g + b


def _out_kernel(m_ref, w_ref, x_ref, gate_ref, lng_ref, lnb_ref, o_ref):
    y = _dot(m_ref[...], w_ref[...])
    o_ref[...] = _post_norm(x_ref[...], gate_ref[0], y, lng_ref[...], lnb_ref[...])


def _out_proj(merged, w_o, x, gate, ln_g, ln_b, l, tm):
    n = x.shape[0]
    tiles_per_mod = (n // tm) // gate.shape[0]
    vec = pl.BlockSpec((None, 1, D_MODEL), lambda i: (l, 0, 0))
    return pl.pallas_call(
        _out_kernel,
        out_shape=jax.ShapeDtypeStruct((n, D_MODEL), F32),
        grid=(n // tm,),
        in_specs=[pl.BlockSpec((tm, D_MODEL), lambda i: (i, 0)),
                  pl.BlockSpec((None, D_MODEL, D_MODEL), lambda i: (l, 0, 0)),
                  pl.BlockSpec((tm, D_MODEL), lambda i: (i, 0)),
                  pl.BlockSpec((1, gate.shape[1], D_MODEL), lambda i: (i // tiles_per_mod, 0, 0)),
                  vec, vec],
        out_specs=pl.BlockSpec((tm, D_MODEL), lambda i: (i, 0)),
        compiler_params=_params(("parallel",)),
        name="out_proj",
    )(merged, w_o, x, gate, ln_g.reshape(DEPTH, 1, D_MODEL), ln_b.reshape(DEPTH, 1, D_MODEL))


def _route(h, wr, br):
    h1, h2, _ = _split3(h)
    w1, w2, _ = _split3(wr)
    logits = _dot_nt(w1, h1) + _dot_nt(w1, h2) + _dot_nt(w2, h1)
    aff = jax.nn.sigmoid(logits)
    sel = aff + br
    rows = [sel[e:e + 1, :] for e in range(N_EXPERTS)]
    grp = []
    for g in range(N_GROUPS):
        a, b, c, d = rows[g * EXPERTS_PER_GROUP:(g + 1) * EXPERTS_PER_GROUP]
        hi1, lo1 = jnp.maximum(a, b), jnp.minimum(a, b)
        hi2, lo2 = jnp.maximum(c, d), jnp.minimum(c, d)
        grp.append(jnp.maximum(hi1, hi2) + jnp.maximum(jnp.minimum(hi1, hi2), jnp.maximum(lo1, lo2)))
    best = functools.reduce(jnp.maximum, grp)
    taken = jnp.zeros_like(best)
    picked = []
    for g in range(N_GROUPS):
        is_g = jnp.where(grp[g] == best, 1.0, 0.0) * (1.0 - taken)
        taken = taken + is_g
        for e in range(g * EXPERTS_PER_GROUP, (g + 1) * EXPERTS_PER_GROUP):
            rank = jnp.zeros_like(best)
            for o in range(g * EXPERTS_PER_GROUP, (g + 1) * EXPERTS_PER_GROUP):
                if o < e:
                    rank = rank + jnp.where(rows[o] >= rows[e], 1.0, 0.0)
                elif o > e:
                    rank = rank + jnp.where(rows[o] > rows[e], 1.0, 0.0)
            picked.append(is_g * jnp.where(rank < 2.0, 1.0, 0.0))
    picked = jnp.concatenate(picked, axis=0)
    gate = picked * aff
    return picked, gate / jnp.sum(gate, axis=0, keepdims=True)


def _router_kernel(x_ref, sh_ref, sc_ref, wr_ref, br_ref, h_ref, g_ref):
    h = _ln(x_ref[...]) * (1.0 + sc_ref[0]) + sh_ref[0]
    h_ref[...] = h.astype(h_ref.dtype)
    _, g_ref[...] = _route(h, wr_ref[...], br_ref[...])


def _router_sorted_kernel(x_ref, sh_ref, sc_ref, wr_ref, br_ref, h_ref, eid_ref, gw_ref, pos_ref, cnt_ref, base_scr):
    tm = x_ref.shape[0]

    @pl.when(pl.program_id(0) == 0)
    def _():
        base_scr[...] = jnp.zeros(base_scr.shape, F32)

    h = _ln(x_ref[...]) * (1.0 + sc_ref[0]) + sh_ref[0]
    h_ref[...] = h.astype(h_ref.dtype)
    picked, gate = _route(h, wr_ref[...], br_ref[...])
    r = lax.broadcasted_iota(I32, (tm, tm), 0)
    c = lax.broadcasted_iota(I32, (tm, tm), 1)
    earlier = jnp.where(r < c, 1.0, 0.0).astype(BF16)
    base = base_scr[...]
    rank = _dot(picked.astype(BF16), earlier) + jnp.concatenate([base] * (tm // LANES), axis=1)
    base_scr[...] = base + jnp.sum(picked, axis=1, keepdims=True)
    cnt_ref[...] = base_scr[...]
    eio = lax.broadcasted_iota(I32, picked.shape, 0).astype(F32)
    e_lo = jnp.min(jnp.where(picked > 0.0, eio, float(N_EXPERTS)), axis=0, keepdims=True)
    e_hi = jnp.max(jnp.where(picked > 0.0, eio, -1.0), axis=0, keepdims=True)

    def pick(e, x):
        return jnp.sum(jnp.where(eio == e, x, 0.0), axis=0, keepdims=True)

    eid_ref[...] = jnp.concatenate([e_lo, e_hi], axis=0).astype(I32)
    gw_ref[...] = jnp.concatenate([pick(e_lo, gate), pick(e_hi, gate)], axis=0)
    pos_ref[...] = jnp.concatenate([pick(e_lo, rank), pick(e_hi, rank)], axis=0).astype(I32)


def _router_sorted(x, shift, scale, w_router_t, b_router, tm):
    n = x.shape[0]
    tiles_per_mod = (n // tm) // shift.shape[0]
    mod_spec = pl.BlockSpec((1, shift.shape[1], D_MODEL), lambda i: (i // tiles_per_mod, 0, 0))
    pair = pl.BlockSpec((2, tm), lambda i: (0, i))
    return pl.pallas_call(
        _router_sorted_kernel,
        out_shape=(jax.ShapeDtypeStruct((n, D_MODEL), F32),
                   jax.ShapeDtypeStruct((2, n), I32),
                   jax.ShapeDtypeStruct((2, n), F32),
                   jax.ShapeDtypeStruct((2, n), I32),
                   jax.ShapeDtypeStruct((N_EXPERTS, LANES), F32)),
        grid=(n // tm,),
        in_specs=[pl.BlockSpec((tm, D_MODEL), lambda i: (i, 0)),
                  mod_spec, mod_spec,
                  pl.BlockSpec((N_EXPERTS, D_MODEL), lambda i: (0, 0)),
                  pl.BlockSpec((N_EXPERTS, 1), lambda i: (0, 0))],
        out_specs=(pl.BlockSpec((tm, D_MODEL), lambda i: (i, 0)), pair, pair, pair,
                   pl.BlockSpec((N_EXPERTS, LANES), lambda i: (0, 0))),
        scratch_shapes=[pltpu.VMEM((N_EXPERTS, LANES), F32)],
        compiler_params=_params(("arbitrary",)),
        name="router_sorted",
    )(x, shift, scale, w_router_t, b_router.reshape(N_EXPERTS, 1))


MOE_TILE = 256
DMA_UNROLL = 8


def _dispatch_kernel(dst_ref, h_ref, xs_in_ref, xs_ref, sem, *, td, n):
    del xs_in_ref
    first = pl.program_id(0) * td

    def copies(r):
        return [pltpu.make_async_copy(h_ref.at[pl.ds(r, 1)], xs_ref.at[pl.ds(dst_ref[k * n + first + r], 1)], sem)
                for k in range(2)]

    def start(r, carry):
        for cp in copies(r):
            cp.start()
        return carry

    def wait(r, carry):
        for cp in copies(r):
            cp.wait()
        return carry

    lax.fori_loop(0, td, start, 0, unroll=DMA_UNROLL)
    lax.fori_loop(0, td, wait, 0, unroll=DMA_UNROLL)


def _dispatch(h, dst_flat, n_slots, td):
    n = h.shape[0]
    grid_spec = pltpu.PrefetchScalarGridSpec(
        num_scalar_prefetch=1,
        grid=(n // td,),
        in_specs=[pl.BlockSpec((td, D_MODEL), lambda i, dst: (i, 0)),
                  pl.BlockSpec(memory_space=pl.ANY)],
        out_specs=pl.BlockSpec(memory_space=pl.ANY),
        scratch_shapes=[pltpu.SemaphoreType.DMA])
    return pl.pallas_call(
        functools.partial(_dispatch_kernel, td=td, n=n),
        out_shape=jax.ShapeDtypeStruct((n_slots, D_MODEL), F32),
        grid_spec=grid_spec,
        input_output_aliases={2: 0},
        compiler_params=_params(("arbitrary",)),
        name="moe_dispatch",
    )(dst_flat, h, jnp.zeros((n_slots, D_MODEL), F32))


def _experts_kernel(te_ref, tv_ref, xs_ref, w1_ref, w3_ref, w2_ref, ys_ref):
    del te_ref
    valid = tv_ref[pl.program_id(0)] > 0

    @pl.when(valid)
    def _():
        x = xs_ref[...].astype(BF16)
        a = _dot(x, w1_ref[...])
        b = _dot(x, w3_ref[...])
        ys_ref[...] = _dot((a * jax.nn.sigmoid(a) * b).astype(BF16), w2_ref[...])

    @pl.when(jnp.logical_not(valid))
    def _():
        ys_ref[...] = jnp.zeros(ys_ref.shape, F32)


def _experts(xs, tile_expert, tile_valid, w1, w3, w2, l):
    n_tiles = xs.shape[0] // MOE_TILE
    grid_spec = pltpu.PrefetchScalarGridSpec(
        num_scalar_prefetch=2,
        grid=(n_tiles,),
        in_specs=[pl.BlockSpec((MOE_TILE, D_MODEL), lambda i, te, tv: (i, 0)),
                  pl.BlockSpec((None, None, D_MODEL, EXPERT_DFF), lambda i, te, tv: (l, te[i], 0, 0)),
                  pl.BlockSpec((None, None, D_MODEL, EXPERT_DFF), lambda i, te, tv: (l, te[i], 0, 0)),
                  pl.BlockSpec((None, None, EXPERT_DFF, D_MODEL), lambda i, te, tv: (l, te[i], 0, 0))],
        out_specs=pl.BlockSpec((MOE_TILE, D_MODEL), lambda i, te, tv: (i, 0)))
    return pl.pallas_call(
        _experts_kernel,
        out_shape=jax.ShapeDtypeStruct(xs.shape, F32),
        grid_spec=grid_spec,
        compiler_params=_params(("arbitrary",)),
        name="moe_experts",
    )(tile_expert, tile_valid, xs, w1, w3, w2)


def _combine_kernel(dst_ref, ys_ref, gw_ref, x_ref, gate_ref, lng_ref, lnb_ref, o_ref, buf, sem, *, tc, n):
    i = pl.program_id(0)
    slot = i % 2

    def copies(tile, s, r):
        return [pltpu.make_async_copy(ys_ref.at[pl.ds(dst_ref[k * n + tile * tc + r], 1)],
                                      buf.at[s, k, pl.ds(r, 1)], sem.at[s]) for k in range(2)]

    def start_tile(tile, s):
        def body(r, carry):
            for cp in copies(tile, s, r):
                cp.start()
            return carry
        lax.fori_loop(0, tc, body, 0, unroll=DMA_UNROLL)

    def wait_tile(tile, s):
        def body(r, carry):
            for cp in copies(tile, s, r):
                cp.wait()
            return carry
        lax.fori_loop(0, tc, body, 0, unroll=DMA_UNROLL)

    @pl.when(i == 0)
    def _():
        start_tile(0, 0)

    @pl.when(i + 1 < pl.num_programs(0))
    def _():
        start_tile(i + 1, 1 - slot)

    wait_tile(i, slot)
    gw = gw_ref[...]
    y = gw[:, 0:1] * buf[slot, 0] + gw[:, 1:2] * buf[slot, 1]
    o_ref[...] = _post_norm(x_ref[...], gate_ref[0], y, lng_ref[...], lnb_ref[...])


def _combine(ys, dst_flat, gw, x, gate, ln_g, ln_b, l, tc):
    n = x.shape[0]
    tiles_per_mod = (n // tc) // gate.shape[0]
    vec = pl.BlockSpec((None, 1, D_MODEL), lambda i, dst: (l, 0, 0))
    grid_spec = pltpu.PrefetchScalarGridSpec(
        num_scalar_prefetch=1,
        grid=(n // tc,),
        in_specs=[pl.BlockSpec(memory_space=pl.ANY),
                  pl.BlockSpec((tc, 2), lambda i, dst: (i, 0)),
                  pl.BlockSpec((tc, D_MODEL), lambda i, dst: (i, 0)),
                  pl.BlockSpec((1, gate.shape[1], D_MODEL), lambda i, dst: (i // tiles_per_mod, 0, 0)),
                  vec, vec],
        out_specs=pl.BlockSpec((tc, D_MODEL), lambda i, dst: (i, 0)),
        scratch_shapes=[pltpu.VMEM((2, 2, tc, D_MODEL), F32),
                        pltpu.SemaphoreType.DMA((2,))])
    return pl.pallas_call(
        functools.partial(_combine_kernel, tc=tc, n=n),
        out_shape=jax.ShapeDtypeStruct((n, D_MODEL), F32),
        grid_spec=grid_spec,
        compiler_params=_params(("arbitrary",)),
        name="moe_combine",
    )(dst_flat, ys, gw, x, gate, ln_g.reshape(DEPTH, 1, D_MODEL), ln_b.reshape(DEPTH, 1, D_MODEL))


def _moe_grouped(x, shift, scale, gate, w_router_t, b_router, w1, w3, w2, ln_g, ln_b, l, tm):
    n = x.shape[0]
    h, eid, gw, pos, cnt = _router_sorted(x, shift, scale, w_router_t, b_router, tm)
    counts = cnt[:, 0].astype(I32)
    padded = (counts + MOE_TILE - 1) // MOE_TILE * MOE_TILE
    ends = jnp.cumsum(padded)
    experts = jnp.arange(N_EXPERTS, dtype=I32)[:, None, None]
    seg_start = jnp.sum(jnp.where(eid[None] == experts, (ends - padded)[:, None, None], 0), axis=0)
    dst_flat = (seg_start + pos).reshape(2 * n)
    n_tiles = 2 * n // MOE_TILE + N_EXPERTS
    tile_start = jnp.arange(n_tiles, dtype=I32) * MOE_TILE
    tile_expert = jnp.minimum(jnp.sum((tile_start[:, None] >= ends[None, :]).astype(I32), axis=1), N_EXPERTS - 1)
    tile_valid = (tile_start < ends[-1]).astype(I32)
    xs = _dispatch(h, dst_flat, n_tiles * MOE_TILE, tm)
    ys = _experts(xs, tile_expert, tile_valid, w1, w3, w2, l)
    return _combine(ys, dst_flat, gw.T, x, gate, ln_g, ln_b, l, MOE_TILE)


def _router(x, shift, scale, w_router_t, b_router, tm):
    n = x.shape[0]
    tiles_per_mod = (n // tm) // shift.shape[0]
    mod_spec = pl.BlockSpec((1, shift.shape[1], D_MODEL), lambda i: (i // tiles_per_mod, 0, 0))
    return pl.pallas_call(
        _router_kernel,
        out_shape=(jax.ShapeDtypeStruct((n, D_MODEL), BF16),
                   jax.ShapeDtypeStruct((N_EXPERTS, n), F32)),
        grid=(n // tm,),
        in_specs=[pl.BlockSpec((tm, D_MODEL), lambda i: (i, 0)),
                  mod_spec, mod_spec,
                  pl.BlockSpec((N_EXPERTS, D_MODEL), lambda i: (0, 0)),
                  pl.BlockSpec((N_EXPERTS, 1), lambda i: (0, 0))],
        out_specs=(pl.BlockSpec((tm, D_MODEL), lambda i: (i, 0)),
                   pl.BlockSpec((N_EXPERTS, tm), lambda i: (0, i))),
        compiler_params=_params(("parallel",)),
        name="router",
    )(x, shift, scale, w_router_t, b_router.reshape(N_EXPERTS, 1))


def _moe_kernel(h_ref, g_ref, w1_ref, w3_ref, w2_ref, x_ref, gate_ref, lng_ref, lnb_ref, o_ref, acc_scr):
    e = pl.program_id(1)

    @pl.when(e == 0)
    def _():
        acc_scr[...] = jnp.zeros(acc_scr.shape, F32)

    h = h_ref[...]
    a = _dot(h, w1_ref[...])
    b = _dot(h, w3_ref[...])
    gates = g_ref[...]
    lane = lax.broadcasted_iota(I32, gates.shape, 1)
    gcol = jnp.sum(jnp.where(lane == e, gates, 0.0), axis=1, keepdims=True)
    u = (a * jax.nn.sigmoid(a) * b * gcol).astype(BF16)
    acc_scr[...] += _dot(u, w2_ref[...])

    @pl.when(e == pl.num_programs(1) - 1)
    def _():
        o_ref[...] = _post_norm(x_ref[...], gate_ref[0], acc_scr[...], lng_ref[...], lnb_ref[...])


def _moe(h, gates, w1, w3, w2, x, gate, ln_g, ln_b, l, tm):
    n = x.shape[0]
    tiles_per_mod = (n // tm) // gate.shape[0]
    vec = pl.BlockSpec((None, 1, D_MODEL), lambda i, e: (l, 0, 0))
    return pl.pallas_call(
        _moe_kernel,
        out_shape=jax.ShapeDtypeStruct((n, D_MODEL), F32),
        grid=(n // tm, N_EXPERTS),
        in_specs=[pl.BlockSpec((tm, D_MODEL), lambda i, e: (i, 0)),
                  pl.BlockSpec((tm, N_EXPERTS), lambda i, e: (i, 0)),
                  pl.BlockSpec((None, None, D_MODEL, EXPERT_DFF), lambda i, e: (l, e, 0, 0)),
                  pl.BlockSpec((None, None, D_MODEL, EXPERT_DFF), lambda i, e: (l, e, 0, 0)),
                  pl.BlockSpec((None, None, EXPERT_DFF, D_MODEL), lambda i, e: (l, e, 0, 0)),
                  pl.BlockSpec((tm, D_MODEL), lambda i, e: (i, 0)),
                  pl.BlockSpec((1, gate.shape[1], D_MODEL), lambda i, e: (i // tiles_per_mod, 0, 0)),
                  vec, vec],
        out_specs=pl.BlockSpec((tm, D_MODEL), lambda i, e: (i, 0)),
        scratch_shapes=[pltpu.VMEM((tm, D_MODEL), F32)],
        compiler_params=_params(("parallel", "arbitrary")),
        name="moe",
    )(h, gates, w1, w3, w2, x, gate, ln_g.reshape(DEPTH, 1, D_MODEL), ln_b.reshape(DEPTH, 1, D_MODEL))


def _row_to_col(v):
    n = v.shape[1]
    r = lax.broadcasted_iota(I32, (n, n), 0)
    c = lax.broadcasted_iota(I32, (n, n), 1)
    return jnp.sum(jnp.where(r == c, jnp.broadcast_to(v, (n, n)), 0.0), axis=1, keepdims=True)


def _attn_decode_kernel(pt_ref, proj_ref, ck_ref, cv_ref, cik_ref, o_ref,
                        ik_buf, k_buf, v_buf, key_scr, sem_ik, sem_k, sem_v,
                        *, l, n_pages, page, chunk_pages, n_sel):
    b = pl.program_id(0)
    n_chunks = n_pages // chunk_pages
    chunk = chunk_pages * page

    def ik_copy(j):
        return pltpu.make_async_copy(cik_ref.at[l, pt_ref[b, j]], ik_buf.at[pl.ds(j * page, page)], sem_ik)

    def kv_copies(c, j, slot):
        pg = pt_ref[b, c * chunk_pages + j]
        dst = pl.ds(j * page, page)
        copies = []
        for g in range(N_KV_HEADS):
            copies.append(pltpu.make_async_copy(ck_ref.at[l, pg, :, g, :], k_buf.at[slot, g, dst], sem_k.at[slot]))
            copies.append(pltpu.make_async_copy(cv_ref.at[l, pg, :, g, :], v_buf.at[slot, g, dst], sem_v.at[slot]))
        return copies

    def start_chunk(c, slot):
        def body(j, carry):
            for cp in kv_copies(c, j, slot):
                cp.start()
            return carry
        lax.fori_loop(0, chunk_pages, body, 0)

    def wait_chunk(c, slot):
        def body(j, carry):
            for cp in kv_copies(c, j, slot):
                cp.wait()
            return carry
        lax.fori_loop(0, chunk_pages, body, 0)

    def start_ik(j, carry):
        ik_copy(j).start()
        return carry

    def wait_ik(j, carry):
        ik_copy(j).wait()
        return carry

    lax.fori_loop(0, n_pages, start_ik, 0)
    start_chunk(0, 0)
    lax.fori_loop(0, n_pages, wait_ik, 0)

    rowsl = slice(None)
    iq = proj_ref[rowsl, OFF_IQ:OFF_IQ + N_IDX_HEADS * IDX_DIM]
    iq_h = jnp.concatenate([iq[:, h * IDX_DIM:(h + 1) * IDX_DIM] for h in range(N_IDX_HEADS)], axis=0)
    w_row = proj_ref[rowsl, OFF_IKW + IDX_DIM:OFF_IKW + IDX_DIM + N_IDX_HEADS]
    w_col = _row_to_col(w_row) * (IDX_DIM ** -0.5 * N_IDX_HEADS ** -0.5)
    ik_new = proj_ref[rowsl, OFF_IKW:OFF_IKW + IDX_DIM]

    iq_b = iq_h.astype(BF16)
    for c in range(n_chunks):
        lg = _dot_nt(iq_b, ik_buf[c * chunk:(c + 1) * chunk, :].astype(BF16))
        s = jnp.sum(w_col * jnp.maximum(lg, 0.0), axis=0, keepdims=True)
        key_scr[c] = _sort_key(s)
    lg_new = jnp.sum(iq_b.astype(F32) * ik_new.astype(BF16).astype(F32), axis=1, keepdims=True)
    key_new = _sort_key(jnp.sum(w_col * jnp.maximum(lg_new, 0.0), axis=0, keepdims=True))

    def count_ge(cand):
        cnt = jnp.where(key_new >= cand, 1.0, 0.0)
        for c in range(n_chunks):
            cnt = cnt + jnp.sum(jnp.where(key_scr[c] >= cand, 1.0, 0.0), axis=1, keepdims=True)
        return cnt

    thr = _kth_largest_key(count_ge, float(n_sel), (1, 1))

    q = proj_ref[rowsl, OFF_Q:OFF_Q + ATT_WIDTH]
    q_h = jnp.concatenate([q[:, h * HEAD_DIM:(h + 1) * HEAD_DIM] for h in range(N_ATT_HEADS)], axis=0)
    q_b = q_h.astype(BF16)
    k_new = proj_ref[rowsl, OFF_K:OFF_K + KV_WIDTH]
    v_new = proj_ref[rowsl, OFF_V:OFF_V + KV_WIDTH]
    scale = HEAD_DIM ** -0.5

    def att_chunk(c, carry):
        slot = c % 2

        @pl.when(c + 1 < n_chunks)
        def _():
            start_chunk(c + 1, 1 - slot)

        wait_chunk(c, slot)
        sel = key_scr[c] >= thr
        out = []
        for g in range(N_KV_HEADS):
            m_old, l_old, acc_old = carry[g]
            kg = k_buf[slot, g].astype(BF16)
            vg = v_buf[slot, g].astype(BF16)
            s = _dot_nt(q_b[g * KV_GROUP:(g + 1) * KV_GROUP], kg) * scale
            s = jnp.where(sel, s, MASK_VALUE)
            m_new = jnp.maximum(m_old, jnp.max(s, axis=1, keepdims=True))
            p = jnp.where(sel, jnp.exp(s - m_new), 0.0)
            a = jnp.exp(m_old - m_new)
            out.append((m_new, a * l_old + jnp.sum(p, axis=1, keepdims=True),
                        a * acc_old + _dot(p.astype(BF16), vg)))
        return tuple(out)

    init = tuple((jnp.full((KV_GROUP, 1), MASK_VALUE, F32), jnp.zeros((KV_GROUP, 1), F32),
                  jnp.zeros((KV_GROUP, HEAD_DIM), F32)) for _ in range(N_KV_HEADS))
    res = lax.fori_loop(0, n_chunks, att_chunk, init)

    sel_new = key_new >= thr
    for g in range(N_KV_HEADS):
        m_old, l_old, acc_old = res[g]
        kg = k_new[:, g * HEAD_DIM:(g + 1) * HEAD_DIM].astype(BF16).astype(F32)
        vg = v_new[:, g * HEAD_DIM:(g + 1) * HEAD_DIM].astype(BF16).astype(F32)
        qg = q_b[g * KV_GROUP:(g + 1) * KV_GROUP].astype(F32)
        s = jnp.sum(qg * kg, axis=1, keepdims=True) * scale
        s = jnp.where(sel_new, s, MASK_VALUE)
        m_new = jnp.maximum(m_old, s)
        p = jnp.where(sel_new, jnp.exp(s - m_new), 0.0)
        a = jnp.exp(m_old - m_new)
        l_new = a * l_old + p
        acc = a * acc_old + p.astype(BF16).astype(F32) * vg
        o = acc / l_new
        for r in range(KV_GROUP):
            hh = g * KV_GROUP + r
            o_ref[rowsl, hh * HEAD_DIM:(hh + 1) * HEAD_DIM] = o[r:r + 1, :].astype(o_ref.dtype)


def _attn_decode(proj, page_table, cache_k, cache_v, cache_idx_k, l):
    db, n_pages = page_table.shape
    page = cache_k.shape[2]
    past = n_pages * page
    n_sel = min(TOPK_MAX, (past + 1) // 4)
    chunk_pages = _tile(n_pages, 16)
    kern = functools.partial(_attn_decode_kernel, l=l, n_pages=n_pages, page=page,
                             chunk_pages=chunk_pages, n_sel=n_sel)
    grid_spec = pltpu.PrefetchScalarGridSpec(
        num_scalar_prefetch=1,
        grid=(db,),
        in_specs=[pl.BlockSpec((None, 1, PACK_WIDTH), lambda b, pt: (b, 0, 0)),
                  pl.BlockSpec(memory_space=pl.ANY),
                  pl.BlockSpec(memory_space=pl.ANY),
                  pl.BlockSpec(memory_space=pl.ANY)],
        out_specs=pl.BlockSpec((None, 1, ATT_WIDTH), lambda b, pt: (b, 0, 0)),
        scratch_shapes=[pltpu.VMEM((past, IDX_DIM), F32),
                        pltpu.VMEM((2, N_KV_HEADS, chunk_pages * page, HEAD_DIM), F32),
                        pltpu.VMEM((2, N_KV_HEADS, chunk_pages * page, HEAD_DIM), F32),
                        pltpu.VMEM((n_pages // chunk_pages, 1, chunk_pages * page), I32),
                        pltpu.SemaphoreType.DMA,
                        pltpu.SemaphoreType.DMA((2,)),
                        pltpu.SemaphoreType.DMA((2,))])
    return pl.pallas_call(
        kern,
        out_shape=jax.ShapeDtypeStruct((db, 1, ATT_WIDTH), F32),
        grid_spec=grid_spec,
        compiler_params=_params(("arbitrary",)),
        name="attn_decode",
    )(page_table, proj.reshape(db, 1, PACK_WIDTH), cache_k, cache_v, cache_idx_k).reshape(db, ATT_WIDTH)


def _hgrn_decode_kernel(proj_ref, s0_ref, lb_ref, ng_ref, o_ref, s_ref):
    rowsl = slice(None)
    for h in range(N_HGRN_HEADS):
        cols = lambda off: slice(off + h * HGRN_DK, off + (h + 1) * HGRN_DK)
        lb = lb_ref[:, h * HGRN_DK:(h + 1) * HGRN_DK]
        q, f, kk = _hgrn_gates(proj_ref[rowsl, cols(OFF_HQ)], proj_ref[rowsl, cols(OFF_HF)], lb)
        v = proj_ref[rowsl, cols(OFF_HI)]
        s_new = _row_to_col(f) * s0_ref[h] + _row_to_col(kk) * v
        s_ref[h] = s_new
        o = jnp.sum(_row_to_col(q) * s_new, axis=0, keepdims=True)
        ng = ng_ref[:, h * HGRN_DV:(h + 1) * HGRN_DV]
        o_ref[rowsl, h * HGRN_DV:(h + 1) * HGRN_DV] = _hgrn_finish(
            o, proj_ref[rowsl, cols(OFF_HG)], ng).astype(o_ref.dtype)


def _hgrn_decode(proj, state, lb, ng, l):
    db = proj.shape[0]
    st_spec_in = pl.BlockSpec((None, None, N_HGRN_HEADS, HGRN_DK, HGRN_DV), lambda b: (l, b, 0, 0, 0))
    o, s = pl.pallas_call(
        _hgrn_decode_kernel,
        out_shape=(jax.ShapeDtypeStruct((db, 1, HGRN_WIDTH), F32),
                   jax.ShapeDtypeStruct((db, N_HGRN_HEADS, HGRN_DK, HGRN_DV), F32)),
        grid=(db,),
        in_specs=[pl.BlockSpec((None, 1, PACK_WIDTH), lambda b: (b, 0, 0)),
                  st_spec_in,
                  pl.BlockSpec((1, HGRN_WIDTH), lambda b: (0, 0)),
                  pl.BlockSpec((1, HGRN_WIDTH), lambda b: (0, 0))],
        out_specs=(pl.BlockSpec((None, 1, HGRN_WIDTH), lambda b: (b, 0, 0)),
                   pl.BlockSpec((None, N_HGRN_HEADS, HGRN_DK, HGRN_DV), lambda b: (b, 0, 0, 0))),
        compiler_params=_params(("parallel",)),
        name="hgrn_decode",
    )(proj.reshape(db, 1, PACK_WIDTH), state, lb.reshape(1, HGRN_WIDTH), ng.reshape(1, HGRN_WIDTH))
    return o.reshape(db, HGRN_WIDTH), s


def _pack_w_in(w_in):
    offs = [0]
    for s in IN_SPLITS:
        offs.append(offs[-1] + s)
    q, k, v, iq, ik, iw, hq, hf, hi, hg, ga, gb = [w_in[:, :, offs[i]:offs[i + 1]] for i in range(len(IN_SPLITS))]
    pad = jnp.zeros(w_in.shape[:2] + (PACK_WIDTH - OFF_IKW - IDX_DIM - N_IDX_HEADS,), w_in.dtype)
    return jnp.concatenate([ga, gb, q, iq, hq, hf, hi, hg, k, v, ik, iw, pad], axis=-1).astype(BF16)


def _mods(mod_l, rows, per_row):
    m = mod_l[rows]
    parts = jnp.split(m, 6, axis=-1)
    if per_row:
        return [p[None, :, :] for p in parts]
    return [p[:, None, :] for p in parts]


def kernel(x_prompt, x_sample, c_prompt, c_sample, cache_k, cache_v, cache_idx_k, state_hgrn, page_table,
           w_ada, b_ada, w_in, w_up_a, w_up_b, w_o, hgrn_norm_g, hgrn_lb_logits, ln1_g, ln1_b,
           w_router, b_router, w1, w3, w2, ln2_g, ln2_b):
    bp, t, _ = x_prompt.shape
    db = x_sample.shape[0]
    lbp = jax.nn.softmax(hgrn_lb_logits.astype(F32), axis=0)
    lower_bounds = jnp.cumsum(lbp, axis=0) - lbp[0]

    w_pack = _pack_w_in(w_in)
    w_up_a_b, w_up_b_b, w_o_b = w_up_a.astype(BF16), w_up_b.astype(BF16), w_o.astype(BF16)
    w1_b, w3_b, w2_b = w1.astype(BF16), w3.astype(BF16), w2.astype(BF16)
    w_router_t = w_router.T

    n_c = bp + db
    c_rows = -(-n_c // 8) * 8
    c_all = jnp.concatenate([c_prompt, c_sample, jnp.zeros((c_rows - n_c, D_MODEL), F32)], axis=0)
    mod = _ada(c_all, w_ada, b_ada)

    xp = x_prompt.reshape(bp * t, D_MODEL)
    xs = x_sample.reshape(db, D_MODEL)
    tm_p = _tile(bp * t, 512)
    outs_p = {"k": [], "v": [], "ik": [], "s": []}
    outs_s = {"k": [], "v": [], "ik": [], "s": []}
    for l in range(DEPTH):
        sh1, sc1, g1, sh2, sc2, g2 = _mods(mod[l], slice(0, bp), per_row=False)
        proj = _proj(xp, sh1, sc1, w_pack, l, _tile(t, 1024))
        o_a = _attn_prompt(proj, bp, t)
        o_b, s_new = _hgrn_prompt(proj, lower_bounds[l], hgrn_norm_g[l], bp, t)
        merged = _merge(o_a, o_b, proj, w_up_a_b, w_up_b_b, l, tm_p)
        xp = _out_proj(merged, w_o_b, xp, g1, ln1_g, ln1_b, l, tm_p)
        xp = _moe_grouped(xp, sh2, sc2, g2, w_router_t, b_router, w1_b, w3_b, w2_b, ln2_g, ln2_b, l, tm_p)
        outs_p["k"].append(proj[:, OFF_K:OFF_K + KV_WIDTH].reshape(bp, t, N_KV_HEADS, HEAD_DIM))
        outs_p["v"].append(proj[:, OFF_V:OFF_V + KV_WIDTH].reshape(bp, t, N_KV_HEADS, HEAD_DIM))
        outs_p["ik"].append(proj[:, OFF_IKW:OFF_IKW + IDX_DIM].reshape(bp, t, IDX_DIM))
        outs_p["s"].append(s_new)

        sh1, sc1, g1, sh2, sc2, g2 = _mods(mod[l], slice(bp, bp + db), per_row=True)
        proj = _proj(xs, sh1, sc1, w_pack, l, db)
        o_a = _attn_decode(proj, page_table, cache_k, cache_v, cache_idx_k, l)
        o_b, s_new = _hgrn_decode(proj, state_hgrn, lower_bounds[l], hgrn_norm_g[l], l)
        merged = _merge(o_a, o_b, proj, w_up_a_b, w_up_b_b, l, db)
        xs = _out_proj(merged, w_o_b, xs, g1, ln1_g, ln1_b, l, db)
        h2, gates = _router(xs, sh2, sc2, w_router_t, b_router, db)
        xs = _moe(h2, gates.T, w1_b, w3_b, w2_b, xs, g2, ln2_g, ln2_b, l, db)
        outs_s["k"].append(proj[:, OFF_K:OFF_K + KV_WIDTH].reshape(db, 1, N_KV_HEADS, HEAD_DIM))
        outs_s["v"].append(proj[:, OFF_V:OFF_V + KV_WIDTH].reshape(db, 1, N_KV_HEADS, HEAD_DIM))
        outs_s["ik"].append(proj[:, OFF_IKW:OFF_IKW + IDX_DIM].reshape(db, 1, IDX_DIM))
        outs_s["s"].append(s_new)

    return (xp.reshape(bp, t, D_MODEL), xs.reshape(db, 1, D_MODEL),
            jnp.stack(outs_p["k"]), jnp.stack(outs_p["v"]), jnp.stack(outs_p["ik"]), jnp.stack(outs_p["s"]),
            jnp.stack(outs_s["k"]), jnp.stack(outs_s["v"]), jnp.stack(outs_s["ik"]), jnp.stack(outs_s["s"]))
```

```python
import functools

import jax
import jax.numpy as jnp
from jax import lax
from jax.experimental import pallas as pl
from jax.experimental.pallas import tpu as pltpu

F32 = jnp.float32
BF16 = jnp.bfloat16
I32 = jnp.int32
I16 = jnp.int16

DEPTH = 2
D_MODEL = 2048
N_ATT_HEADS = 8
N_KV_HEADS = 2
KV_GROUP = N_ATT_HEADS // N_KV_HEADS
HEAD_DIM = 128
ATT_WIDTH = N_ATT_HEADS * HEAD_DIM
KV_WIDTH = N_KV_HEADS * HEAD_DIM
N_IDX_HEADS = 16
IDX_DIM = 64
TOPK_MAX = 256
N_HGRN_HEADS = 8
HGRN_DK = 128
HGRN_DV = 128
HGRN_WIDTH = N_HGRN_HEADS * HGRN_DK
HGRN_CHUNK = 64
HGRN_SUB = 8
N_EXPERTS = 16
N_GROUPS = 4
EXPERTS_PER_GROUP = N_EXPERTS // N_GROUPS
EXPERT_DFF = 1024
ALPHA = (2 * DEPTH) ** 0.25
LN_EPS = 1e-5
MASK_VALUE = -1e30
INT_MIN = -2 ** 31
LOG2_E = 1.4426950408889634

IN_SPLITS = (ATT_WIDTH, KV_WIDTH, KV_WIDTH, N_IDX_HEADS * IDX_DIM, IDX_DIM, N_IDX_HEADS,
             HGRN_WIDTH, HGRN_WIDTH, HGRN_WIDTH, HGRN_WIDTH, D_MODEL, D_MODEL)

LANES = 128
OFF_GA = 0
OFF_GB = OFF_GA + D_MODEL
OFF_Q = OFF_GB + D_MODEL
OFF_IQ = OFF_Q + ATT_WIDTH
OFF_HQ = OFF_IQ + N_IDX_HEADS * IDX_DIM
OFF_HF = OFF_HQ + HGRN_WIDTH
OFF_HI = OFF_HF + HGRN_WIDTH
OFF_HG = OFF_HI + HGRN_WIDTH
OFF_K = OFF_HG + HGRN_WIDTH
OFF_V = OFF_K + KV_WIDTH
OFF_IKW = OFF_V + KV_WIDTH
PROJ_TN = 1024
PACK_WIDTH = -(-(OFF_IKW + LANES) // PROJ_TN) * PROJ_TN

VMEM_LIMIT = 59 * 1024 * 1024


def _params(semantics):
    return pltpu.CompilerParams(dimension_semantics=semantics, vmem_limit_bytes=VMEM_LIMIT)


def _tile(n, pref):
    t = min(n, pref)
    while n % t:
        t //= 2
    return t


def _ln(x):
    mu = jnp.mean(x, axis=-1, keepdims=True)
    xc = x - mu
    var = jnp.mean(xc * xc, axis=-1, keepdims=True)
    return xc * lax.rsqrt(var + LN_EPS)


def _dot(a, b):
    return jnp.dot(a, b, preferred_element_type=F32)


def _dot_nt(a, b):
    return lax.dot_general(a, b, (((1,), (1,)), ((), ())), preferred_element_type=F32)


def _split3(x):
    x1 = x.astype(BF16)
    r1 = x - x1.astype(F32)
    x2 = r1.astype(BF16)
    x3 = (r1 - x2.astype(F32)).astype(BF16)
    return x1, x2, x3


def _sort_key(s):
    bits = pltpu.bitcast(s, I32)
    return bits ^ ((bits >> 31) & 0x7FFFFFFF)


def _kth_largest_key(count_ge, n_sel, shape):
    def body(i, ans):
        bit = lax.shift_left(jnp.int32(1), jnp.int32(31) - i)
        cand = ans | bit
        cnt = count_ge(cand ^ INT_MIN)
        return jnp.where(cnt >= n_sel, cand, ans)
    ans = lax.fori_loop(0, 32, body, jnp.zeros(shape, I32))
    return ans ^ INT_MIN


def _ada_kernel(c_ref, w_ref, b_ref, o_ref):
    c = c_ref[...]
    a = (c * jax.nn.sigmoid(c)).astype(BF16)
    o_ref[...] = _dot(a, w_ref[...].astype(BF16)) + b_ref[...]


def _ada(c_all, w_ada, b_ada):
    rows = c_all.shape[0]
    width = w_ada.shape[-1]
    tn = _tile(width, 1024)
    return pl.pallas_call(
        _ada_kernel,
        out_shape=jax.ShapeDtypeStruct((DEPTH, rows, width), F32),
        grid=(DEPTH, width // tn),
        in_specs=[pl.BlockSpec((rows, D_MODEL), lambda l, j: (0, 0)),
                  pl.BlockSpec((None, D_MODEL, tn), lambda l, j: (l, 0, j)),
                  pl.BlockSpec((None, 1, tn), lambda l, j: (l, 0, j))],
        out_specs=pl.BlockSpec((None, rows, tn), lambda l, j: (l, 0, j)),
        compiler_params=_params(("parallel", "parallel")),
        name="ada",
    )(c_all, w_ada, b_ada.reshape(DEPTH, 1, width))


def _proj_kernel(x_ref, sh_ref, sc_ref, w_ref, o_ref, h_scr):
    @pl.when(pl.program_id(1) == 0)
    def _():
        h_scr[...] = (_ln(x_ref[...]) * (1.0 + sc_ref[0]) + sh_ref[0]).astype(BF16)
    o_ref[...] = _dot(h_scr[...], w_ref[...])


def _proj(x, shift, scale, w_pack, l, tm):
    n = x.shape[0]
    tiles_per_mod = (n // tm) // shift.shape[0]
    r = shift.shape[1]
    tn = PROJ_TN
    mod_spec = pl.BlockSpec((1, r, D_MODEL), lambda i, j: (i // tiles_per_mod, 0, 0))
    return pl.pallas_call(
        _proj_kernel,
        out_shape=jax.ShapeDtypeStruct((n, PACK_WIDTH), F32),
        grid=(n // tm, PACK_WIDTH // tn),
        in_specs=[pl.BlockSpec((tm, D_MODEL), lambda i, j: (i, 0)),
                  mod_spec, mod_spec,
                  pl.BlockSpec((None, D_MODEL, tn), lambda i, j: (l, 0, j))],
        out_specs=pl.BlockSpec((tm, tn), lambda i, j: (i, j)),
        scratch_shapes=[pltpu.VMEM((tm, D_MODEL), BF16)],
        compiler_params=_params(("parallel", "arbitrary")),
        name="proj",
    )(x, shift, scale, w_pack)


KEY_BLOCK = 256
ROW_CHUNK = 64
ATT_ROWS = 128


def _lane_fold(x, op):
    acc = x[:, :LANES]
    for i in range(1, x.shape[1] // LANES):
        acc = op(acc, x[:, i * LANES:(i + 1) * LANES])
    return acc


def _attn_prompt_kernel(q_ref, iq_ref, iwq_ref, k_ref, v_ref, ikw_ref, o_ref,
                        kb_scr, va_scr, ikb_scr, iqs_scr, qs_scr, key_scr, h16_scr, m_scr, acc_scr, xb_scr,
                        *, tq, n_sel, col_bits):
    qi = pl.program_id(1)
    tk = min(tq, KEY_BLOCK)
    n_kb = (qi + 1) * (tq // tk)
    n_lane_tiles = tk // LANES

    @pl.when(qi == 0)
    def _():
        kb_scr[...] = k_ref[...].astype(BF16)
        for g in range(N_KV_HEADS):
            va_scr[:, 2 * g * HEAD_DIM:(2 * g + 1) * HEAD_DIM] = (
                v_ref[:, g * HEAD_DIM:(g + 1) * HEAD_DIM].astype(BF16))
            va_scr[:, (2 * g + 1) * HEAD_DIM:(2 * g + 2) * HEAD_DIM] = jnp.ones((va_scr.shape[0], HEAD_DIM), BF16)
        ikb_scr[...] = ikw_ref[:, :IDX_DIM].astype(BF16)

    iq = iq_ref[...]
    for h in range(N_IDX_HEADS):
        iqs_scr[h * tq:(h + 1) * tq, :] = iq[:, h * IDX_DIM:(h + 1) * IDX_DIM].astype(BF16)
    q = q_ref[...] * (HEAD_DIM ** -0.5 * LOG2_E)
    for h in range(N_ATT_HEADS):
        qs_scr[h * tq:(h + 1) * tq, :] = q[:, h * HEAD_DIM:(h + 1) * HEAD_DIM].astype(BF16)
    w = iwq_ref[:, IDX_DIM:IDX_DIM + N_IDX_HEADS] * (IDX_DIM ** -0.5 * N_IDX_HEADS ** -0.5)
    row = qi * tq + lax.broadcasted_iota(I32, (tq, tk), 0)
    col0 = lax.broadcasted_iota(I32, (tq, tk), 1)

    def score_block(kb, carry):
        start = pl.multiple_of(kb * tk, tk)
        ikb = ikb_scr[pl.ds(start, tk), :]
        s = jnp.zeros((tq, tk), F32)
        for h in range(N_IDX_HEADS):
            lg = _dot_nt(iqs_scr[h * tq:(h + 1) * tq, :], ikb)
            s = s + w[:, h:h + 1] * jnp.maximum(lg, 0.0)
        s = jnp.where(kb * tk + col0 <= row, s, MASK_VALUE)
        key = _sort_key(s)
        key_scr[kb] = key
        h16_scr[kb] = (key >> 16).astype(I16)
        return carry

    lax.fori_loop(0, n_kb, score_block, 0)

    chunks = [slice(c * ROW_CHUNK, (c + 1) * ROW_CHUNK) for c in range(tq // ROW_CHUNK)]

    def count(*make_preds):
        preds = [[mp(rows) for rows in chunks] for mp in make_preds]

        def body(kb, parts):
            new = []
            for ci, rows in enumerate(chunks):
                keys = key_scr[kb, rows, :]
                for pi in range(len(preds)):
                    idx = pi * len(chunks) + ci
                    hit = jnp.where(preds[pi][ci](kb, keys), 1.0, 0.0)
                    new.append((idx, parts[idx] + _lane_fold(hit, jnp.add)))
            return tuple(p for _, p in sorted(new, key=lambda t: t[0]))

        zeros = tuple(jnp.zeros((ROW_CHUNK, LANES), F32) for _ in range(len(preds) * len(chunks)))
        parts = lax.fori_loop(0, n_kb, body, zeros)
        outs = []
        for pi in range(len(preds)):
            part = jnp.concatenate(parts[pi * len(chunks):(pi + 1) * len(chunks)], axis=0)
            outs.append(jnp.broadcast_to(jnp.sum(part, axis=1, keepdims=True), (tq, LANES)))
        return outs[0] if len(outs) == 1 else outs

    def wide(x):
        return jnp.concatenate([x] * n_lane_tiles, axis=1)

    def bcast(x, rows):
        return wide(x[rows])

    def count16_ge(cand):
        c16 = cand.astype(I16)
        cs = [wide(c16[rows]) for rows in chunks]

        def body(kb, parts):
            return tuple(p + _lane_fold(jnp.where(h16_scr[kb, rows, :] >= c, jnp.int16(1), jnp.int16(0)), jnp.add)
                         for p, c, rows in zip(parts, cs, chunks))

        parts = lax.fori_loop(0, n_kb, body, tuple(jnp.zeros((ROW_CHUNK, LANES), I16) for _ in chunks))
        part = jnp.concatenate(parts, axis=0).astype(F32)
        return jnp.broadcast_to(jnp.sum(part, axis=1, keepdims=True), (tq, LANES))

    def search16():
        u = jnp.zeros((tq, LANES), I32)
        for bit in range(15, -1, -1):
            cand = u | (1 << bit)
            u = jnp.where(count16_ge(cand - 32768) >= float(n_sel), cand, u)
        return u

    hi_p = search16() - 32768
    hi_w = wide(hi_p)

    def low_keys(kb, carry):
        key = key_scr[kb]
        hi = key >> 16
        lo = (key & 0xFFFF) - 32768
        h16_scr[kb] = jnp.where(hi > hi_w, 32767, jnp.where(hi < hi_w, -32768, lo)).astype(I16)
        return carry

    lax.fori_loop(0, n_kb, low_keys, 0)
    thr = hi_p * 65536 + search16()

    cnt_gt, cnt_ge = count(lambda rows: (lambda kb, key, c=bcast(thr, rows): key > c),
                           lambda rows: (lambda kb, key, c=bcast(thr, rows): key >= c))
    need = float(n_sel) - cnt_gt
    xb_scr[...] = jnp.full(xb_scr.shape, 2 ** 31 - 1, I32)

    @pl.when(jnp.max(jnp.abs(cnt_ge - float(n_sel))) > 0.0)
    def _():
        def body(i, x):
            cand = x | lax.shift_left(jnp.int32(1), jnp.int32(col_bits - 1) - i)

            def make_pred(rows):
                c_thr, c_cand = bcast(thr, rows), bcast(cand, rows)
                c_col = lax.broadcasted_iota(I32, (ROW_CHUNK, tk), 1)
                return lambda kb, key: (key == c_thr) & (kb * tk + c_col < c_cand)

            return jnp.where(count(make_pred) < need, cand, x)

        xb_scr[...] = lax.fori_loop(0, col_bits, body, jnp.zeros((tq, LANES), I32))

    xb_w, thr_w = wide(xb_scr[...]), wide(thr)

    def bias_block(kb, carry):
        key = key_scr[kb]
        col = kb * tk + col0
        sel = ((key > thr_w) | ((key == thr_w) & (col <= xb_w))) & (col <= row)
        key_scr[kb] = pltpu.bitcast(jnp.where(sel, 0.0, -jnp.inf), I32)
        return carry

    lax.fori_loop(0, n_kb, bias_block, 0)

    m_scr[...] = jnp.full(m_scr.shape, MASK_VALUE, F32)
    acc_scr[...] = jnp.zeros(acc_scr.shape, F32)

    def att_block(kb, carry):
        start = pl.multiple_of(kb * tk, tk)
        bias = pltpu.bitcast(key_scr[kb], F32)
        for g in range(N_KV_HEADS):
            kg = kb_scr[pl.ds(start, tk), g * HEAD_DIM:(g + 1) * HEAD_DIM]
            va = va_scr[pl.ds(start, tk), 2 * g * HEAD_DIM:(2 * g + 2) * HEAD_DIM]
            base = g * KV_GROUP * tq
            s_all = _dot_nt(qs_scr[base:base + KV_GROUP * tq, :], kg)
            ps, alphas = [], []
            for c in range(KV_GROUP * tq // ATT_ROWS):
                lo = c * ATT_ROWS
                rows = slice(base + lo, base + lo + ATT_ROWS)
                s = s_all[lo:lo + ATT_ROWS] + bias[lo % tq:lo % tq + ATT_ROWS]
                m_old = m_scr[rows, :]
                m_new = jnp.maximum(m_old, jnp.max(_lane_fold(s, jnp.maximum), axis=1, keepdims=True))
                ps.append(jnp.exp2(s - jnp.concatenate([m_new] * n_lane_tiles, axis=1)).astype(BF16))
                alphas.append(jnp.exp2(m_old - m_new))
                m_scr[rows, :] = m_new
            pv = _dot(jnp.concatenate(ps, axis=0), va)
            for c, a in enumerate(alphas):
                lo = c * ATT_ROWS
                rows = slice(base + lo, base + lo + ATT_ROWS)
                acc_scr[rows, :] = jnp.concatenate([a, a], axis=1) * acc_scr[rows, :] + pv[lo:lo + ATT_ROWS]
        return carry

    lax.fori_loop(0, n_kb, att_block, 0)
    for h in range(N_ATT_HEADS):
        acc = acc_scr[h * tq:(h + 1) * tq, :]
        o_ref[:, h * HEAD_DIM:(h + 1) * HEAD_DIM] = (acc[:, :HEAD_DIM] / acc[:, HEAD_DIM:]).astype(o_ref.dtype)


def _attn_prompt(proj, b, t):
    tq = _tile(t, 512)
    tk = min(tq, KEY_BLOCK)
    n_sel = min(TOPK_MAX, t // 4)
    proj3 = proj.reshape(b, t, PACK_WIDTH)
    kern = functools.partial(_attn_prompt_kernel, tq=tq, n_sel=n_sel, col_bits=t.bit_length())
    once = dict(pipeline_mode=pl.Buffered(1))
    out = pl.pallas_call(
        kern,
        out_shape=jax.ShapeDtypeStruct((b, t, ATT_WIDTH), BF16),
        grid=(b, t // tq),
        in_specs=[pl.BlockSpec((None, tq, ATT_WIDTH), lambda bi, qi: (bi, qi, OFF_Q // ATT_WIDTH)),
                  pl.BlockSpec((None, tq, ATT_WIDTH), lambda bi, qi: (bi, qi, OFF_IQ // ATT_WIDTH)),
                  pl.BlockSpec((None, tq, LANES), lambda bi, qi: (bi, qi, OFF_IKW // LANES)),
                  pl.BlockSpec((None, t, KV_WIDTH), lambda bi, qi: (bi, 0, OFF_K // KV_WIDTH), **once),
                  pl.BlockSpec((None, t, KV_WIDTH), lambda bi, qi: (bi, 0, OFF_V // KV_WIDTH), **once),
                  pl.BlockSpec((None, t, LANES), lambda bi, qi: (bi, 0, OFF_IKW // LANES), **once)],
        out_specs=pl.BlockSpec((None, tq, ATT_WIDTH), lambda bi, qi: (bi, qi, 0)),
        scratch_shapes=[pltpu.VMEM((t, KV_WIDTH), BF16),
                        pltpu.VMEM((t, 2 * KV_WIDTH), BF16),
                        pltpu.VMEM((t, IDX_DIM), BF16),
                        pltpu.VMEM((N_IDX_HEADS * tq, IDX_DIM), BF16),
                        pltpu.VMEM((N_ATT_HEADS * tq, HEAD_DIM), BF16),
                        pltpu.VMEM((t // tk, tq, tk), I32),
                        pltpu.VMEM((t // tk, tq, tk), I16),
                        pltpu.VMEM((N_ATT_HEADS * tq, LANES), F32),
                        pltpu.VMEM((N_ATT_HEADS * tq, 2 * HEAD_DIM), F32),
                        pltpu.VMEM((tq, LANES), I32)],
        compiler_params=_params(("parallel", "arbitrary")),
        name="attn_prompt",
    )(proj3, proj3, proj3, proj3, proj3, proj3)
    return out.reshape(b * t, ATT_WIDTH)


def _hgrn_gates(hq, hf, lb):
    q = hq * jax.nn.sigmoid(hq)
    f = lb + (1.0 - lb) * jax.nn.sigmoid(hf)
    kk = (1.0 - lb) * jax.nn.sigmoid(-hf)
    return q, f, kk


def _hgrn_finish(o, hg, ng):
    o = o * lax.rsqrt(jnp.mean(o * o, axis=-1, keepdims=True) + LN_EPS)
    return o * ng * (hg * jax.nn.sigmoid(hg))


HGRN_HEADS_PER_STEP = 8


def _hgrn_chunk(heads, tri, ones):
    c_len, sub = HGRN_CHUNK, HGRN_SUB
    n_sub = c_len // sub
    t_idx = lax.broadcasted_iota(I32, (sub, 1), 0)
    bs = []
    for q, f, kk, v, st in heads:
        l1, l2, l3 = _split3(jnp.log(f))
        bs.append(_dot(tri, l1) + _dot(tri, l2) + _dot(tri, l3))
    stage2 = []
    for (q, f, kk, v, st), b in zip(heads, bs):
        v_b = v.astype(BF16)
        inter = _dot_nt((q * jnp.exp(b)).astype(BF16), st.astype(BF16))
        rs, cross = [], []
        for j in range(n_sub):
            lo = j * sub
            qt, bt = q[lo:lo + sub], b[lo:lo + sub]
            ps = []
            for s in range(sub):
                valid = t_idx >= s
                dec = jnp.exp(jnp.where(valid, bt - b[lo + s:lo + s + 1, :], 0.0))
                ps.append(jnp.where(valid, qt * kk[lo + s:lo + s + 1, :] * dec, 0.0))
            rs.append(_dot(jnp.concatenate(ps, axis=0).astype(BF16), ones))
            if lo:
                b_edge = b[lo - 1:lo, :]
                qd = (qt * jnp.exp(bt - b_edge)).astype(BF16)
                kd = (kk[:lo] * jnp.exp(b_edge - b[:lo])).astype(BF16)
                cross.append(_dot_nt(qd, kd))
        b_last = b[c_len - 1:c_len, :]
        kd = (kk * jnp.exp(b_last - b)).astype(BF16)
        upd = lax.dot_general(v_b, kd, (((0,), (0,)), ((), ())), preferred_element_type=F32)
        stage2.append((v_b, inter, rs, cross, st * jnp.exp(b_last) + upd))
    out = []
    for (q, f, kk, v, st), (v_b, inter, rs, cross, st_new) in zip(heads, stage2):
        rows = []
        for j in range(n_sub):
            lo = j * sub
            acc = inter[lo:lo + sub]
            if lo:
                acc = acc + _dot(cross[j - 1].astype(BF16), v_b[:lo])
            for s in range(sub):
                acc = acc + rs[j][s * sub:(s + 1) * sub, :] * v[lo + s:lo + s + 1, :]
            rows.append(acc)
        out.append((jnp.concatenate(rows, axis=0), st_new))
    return out


def _hgrn_prompt_kernel(hq_ref, hf_ref, hi_ref, hg_ref, lb_ref, ng_ref, o_ref, s_ref, st_scr, *, n_chunks):
    c_len = HGRN_CHUNK
    ti = pl.program_id(2)

    @pl.when(ti == 0)
    def _():
        st_scr[...] = jnp.zeros(st_scr.shape, F32)

    r_i = lax.broadcasted_iota(I32, (c_len, c_len), 0)
    c_i = lax.broadcasted_iota(I32, (c_len, c_len), 1)
    tri = jnp.where(c_i <= r_i, 1.0, 0.0).astype(BF16)
    ones = jnp.ones((HGRN_DK, HGRN_DV), BF16)

    def chunk(c, carry):
        sl = pl.ds(pl.multiple_of(c * c_len, c_len), c_len)
        cols = [slice(hh * HGRN_DK, (hh + 1) * HGRN_DK) for hh in range(HGRN_HEADS_PER_STEP)]
        heads = []
        for hh, cs in enumerate(cols):
            q, f, kk = _hgrn_gates(hq_ref[sl, cs], hf_ref[sl, cs], lb_ref[:, cs])
            heads.append((q, f, kk, hi_ref[sl, cs], st_scr[hh]))
        for hh, (o, st_new) in enumerate(_hgrn_chunk(heads, tri, ones)):
            st_scr[hh] = st_new
            o_ref[sl, cols[hh]] = _hgrn_finish(o, hg_ref[sl, cols[hh]], ng_ref[:, cols[hh]]).astype(o_ref.dtype)
        return carry

    lax.fori_loop(0, n_chunks, chunk, 0)

    @pl.when(ti == pl.num_programs(2) - 1)
    def _():
        for hh in range(HGRN_HEADS_PER_STEP):
            s_ref[hh] = st_scr[hh].T


def _hgrn_prompt(proj, lb, ng, b, t):
    tc = _tile(t, 512)
    n_chunks = tc // HGRN_CHUNK
    hp = HGRN_HEADS_PER_STEP
    width = hp * HGRN_DK
    proj3 = proj.reshape(b, t, PACK_WIDTH)

    def col(off):
        return pl.BlockSpec((None, tc, width), lambda bi, h, ti: (bi, ti, off // width + h))

    vec = pl.BlockSpec((None, 1, width), lambda bi, h, ti: (h, 0, 0))
    o, s = pl.pallas_call(
        functools.partial(_hgrn_prompt_kernel, n_chunks=n_chunks),
        out_shape=(jax.ShapeDtypeStruct((b, t, HGRN_WIDTH), BF16),
                   jax.ShapeDtypeStruct((b, N_HGRN_HEADS, HGRN_DK, HGRN_DV), F32)),
        grid=(b, N_HGRN_HEADS // hp, t // tc),
        in_specs=[col(OFF_HQ), col(OFF_HF), col(OFF_HI), col(OFF_HG), vec, vec],
        out_specs=(pl.BlockSpec((None, tc, width), lambda bi, h, ti: (bi, ti, h)),
                   pl.BlockSpec((None, hp, HGRN_DK, HGRN_DV), lambda bi, h, ti: (bi, h, 0, 0))),
        scratch_shapes=[pltpu.VMEM((hp, HGRN_DV, HGRN_DK), F32)],
        compiler_params=_params(("parallel", "parallel", "arbitrary")),
        name="hgrn_prompt",
    )(proj3, proj3, proj3, proj3,
      lb.reshape(N_HGRN_HEADS // hp, 1, width), ng.reshape(N_HGRN_HEADS // hp, 1, width))
    return o.reshape(b * t, HGRN_WIDTH), s


def _merge_kernel(oa_ref, ob_ref, ga_ref, gb_ref, wa_ref, wb_ref, o_ref):
    ya = _dot(oa_ref[...].astype(BF16), wa_ref[...])
    yb = _dot(ob_ref[...].astype(BF16), wb_ref[...])
    o_ref[...] = (jax.nn.sigmoid(ga_ref[...]) * ya + jax.nn.sigmoid(gb_ref[...]) * yb).astype(o_ref.dtype)


def _merge(o_a, o_b, proj, w_up_a, w_up_b, l, tm):
    n = o_a.shape[0]
    return pl.pallas_call(
        _merge_kernel,
        out_shape=jax.ShapeDtypeStruct((n, D_MODEL), BF16),
        grid=(n // tm,),
        in_specs=[pl.BlockSpec((tm, ATT_WIDTH), lambda i: (i, 0)),
                  pl.BlockSpec((tm, HGRN_WIDTH), lambda i: (i, 0)),
                  pl.BlockSpec((tm, D_MODEL), lambda i: (i, OFF_GA // D_MODEL)),
                  pl.BlockSpec((tm, D_MODEL), lambda i: (i, OFF_GB // D_MODEL)),
                  pl.BlockSpec((None, ATT_WIDTH, D_MODEL), lambda i: (l, 0, 0)),
                  pl.BlockSpec((None, HGRN_WIDTH, D_MODEL), lambda i: (l, 0, 0))],
        out_specs=pl.BlockSpec((tm, D_MODEL), lambda i: (i, 0)),
        compiler_params=_params(("parallel",)),
        name="merge",
    )(o_a, o_b, proj, proj, w_up_a, w_up_b)


def _post_norm(x, gate, y, g, b):
    return _ln(ALPHA * x + gate * y) * g + b


def _out_kernel(m_ref, w_ref, x_ref, gate_ref, lng_ref, lnb_ref, o_ref):
    y = _dot(m_ref[...], w_ref[...])
    o_ref[...] = _post_norm(x_ref[...], gate_ref[0], y, lng_ref[...], lnb_ref[...])


def _out_proj(merged, w_o, x, gate, ln_g, ln_b, l, tm):
    n = x.shape[0]
    tiles_per_mod = (n // tm) // gate.shape[0]
    vec = pl.BlockSpec((None, 1, D_MODEL), lambda i: (l, 0, 0))
    return pl.pallas_call(
        _out_kernel,
        out_shape=jax.ShapeDtypeStruct((n, D_MODEL), F32),
        grid=(n // tm,),
        in_specs=[pl.BlockSpec((tm, D_MODEL), lambda i: (i, 0)),
                  pl.BlockSpec((None, D_MODEL, D_MODEL), lambda i: (l, 0, 0)),
                  pl.BlockSpec((tm, D_MODEL), lambda i: (i, 0)),
                  pl.BlockSpec((1, gate.shape[1], D_MODEL), lambda i: (i // tiles_per_mod, 0, 0)),
                  vec, vec],
        out_specs=pl.BlockSpec((tm, D_MODEL), lambda i: (i, 0)),
        compiler_params=_params(("parallel",)),
        name="out_proj",
    )(merged, w_o, x, gate, ln_g.reshape(DEPTH, 1, D_MODEL), ln_b.reshape(DEPTH, 1, D_MODEL))


def _route(h, wr, br):
    h1, h2, _ = _split3(h)
    w1, w2, _ = _split3(wr)
    logits = _dot_nt(w1, h1) + _dot_nt(w1, h2) + _dot_nt(w2, h1)
    aff = jax.nn.sigmoid(logits)
    sel = aff + br
    rows = [sel[e:e + 1, :] for e in range(N_EXPERTS)]
    grp = []
    for g in range(N_GROUPS):
        a, b, c, d = rows[g * EXPERTS_PER_GROUP:(g + 1) * EXPERTS_PER_GROUP]
        hi1, lo1 = jnp.maximum(a, b), jnp.minimum(a, b)
        hi2, lo2 = jnp.maximum(c, d), jnp.minimum(c, d)
        grp.append(jnp.maximum(hi1, hi2) + jnp.maximum(jnp.minimum(hi1, hi2), jnp.maximum(lo1, lo2)))
    best = functools.reduce(jnp.maximum, grp)
    taken = jnp.zeros_like(best)
    picked = []
    for g in range(N_GROUPS):
        is_g = jnp.where(grp[g] == best, 1.0, 0.0) * (1.0 - taken)
        taken = taken + is_g
        for e in range(g * EXPERTS_PER_GROUP, (g + 1) * EXPERTS_PER_GROUP):
            rank = jnp.zeros_like(best)
            for o in range(g * EXPERTS_PER_GROUP, (g + 1) * EXPERTS_PER_GROUP):
                if o < e:
                    rank = rank + jnp.where(rows[o] >= rows[e], 1.0, 0.0)
                elif o > e:
                    rank = rank + jnp.where(rows[o] > rows[e], 1.0, 0.0)
            picked.append(is_g * jnp.where(rank < 2.0, 1.0, 0.0))
    picked = jnp.concatenate(picked, axis=0)
    gate = picked * aff
    return picked, gate / jnp.sum(gate, axis=0, keepdims=True)


def _router_kernel(x_ref, sh_ref, sc_ref, wr_ref, br_ref, h_ref, g_ref):
    h = _ln(x_ref[...]) * (1.0 + sc_ref[0]) + sh_ref[0]
    h_ref[...] = h.astype(h_ref.dtype)
    _, g_ref[...] = _route(h, wr_ref[...], br_ref[...])


def _router_sorted_kernel(x_ref, sh_ref, sc_ref, wr_ref, br_ref, h_ref, eid_ref, gw_ref, pos_ref, cnt_ref, base_scr):
    tm = x_ref.shape[0]

    @pl.when(pl.program_id(0) == 0)
    def _():
        base_scr[...] = jnp.zeros(base_scr.shape, F32)

    h = _ln(x_ref[...]) * (1.0 + sc_ref[0]) + sh_ref[0]
    h_ref[...] = h.astype(h_ref.dtype)
    picked, gate = _route(h, wr_ref[...], br_ref[...])
    r = lax.broadcasted_iota(I32, (tm, tm), 0)
    c = lax.broadcasted_iota(I32, (tm, tm), 1)
    earlier = jnp.where(r < c, 1.0, 0.0).astype(BF16)
    base = base_scr[...]
    rank = _dot(picked.astype(BF16), earlier) + jnp.concatenate([base] * (tm // LANES), axis=1)
    base_scr[...] = base + jnp.sum(picked, axis=1, keepdims=True)
    cnt_ref[...] = base_scr[...]
    eio = lax.broadcasted_iota(I32, picked.shape, 0).astype(F32)
    e_lo = jnp.min(jnp.where(picked > 0.0, eio, float(N_EXPERTS)), axis=0, keepdims=True)
    e_hi = jnp.max(jnp.where(picked > 0.0, eio, -1.0), axis=0, keepdims=True)

    def pick(e, x):
        return jnp.sum(jnp.where(eio == e, x, 0.0), axis=0, keepdims=True)

    eid_ref[...] = jnp.concatenate([e_lo, e_hi], axis=0).astype(I32)
    gw_ref[...] = jnp.concatenate([pick(e_lo, gate), pick(e_hi, gate)], axis=0)
    pos_ref[...] = jnp.concatenate([pick(e_lo, rank), pick(e_hi, rank)], axis=0).astype(I32)


def _router_sorted(x, shift, scale, w_router_t, b_router, tm):
    n = x.shape[0]
    tiles_per_mod = (n // tm) // shift.shape[0]
    mod_spec = pl.BlockSpec((1, shift.shape[1], D_MODEL), lambda i: (i // tiles_per_mod, 0, 0))
    pair = pl.BlockSpec((2, tm), lambda i: (0, i))
    return pl.pallas_call(
        _router_sorted_kernel,
        out_shape=(jax.ShapeDtypeStruct((n, D_MODEL), F32),
                   jax.ShapeDtypeStruct((2, n), I32),
                   jax.ShapeDtypeStruct((2, n), F32),
                   jax.ShapeDtypeStruct((2, n), I32),
                   jax.ShapeDtypeStruct((N_EXPERTS, LANES), F32)),
        grid=(n // tm,),
        in_specs=[pl.BlockSpec((tm, D_MODEL), lambda i: (i, 0)),
                  mod_spec, mod_spec,
                  pl.BlockSpec((N_EXPERTS, D_MODEL), lambda i: (0, 0)),
                  pl.BlockSpec((N_EXPERTS, 1), lambda i: (0, 0))],
        out_specs=(pl.BlockSpec((tm, D_MODEL), lambda i: (i, 0)), pair, pair, pair,
                   pl.BlockSpec((N_EXPERTS, LANES), lambda i: (0, 0))),
        scratch_shapes=[pltpu.VMEM((N_EXPERTS, LANES), F32)],
        compiler_params=_params(("arbitrary",)),
        name="router_sorted",
    )(x, shift, scale, w_router_t, b_router.reshape(N_EXPERTS, 1))


MOE_TILE = 256
DMA_UNROLL = 8


def _dispatch_kernel(dst_ref, h_ref, xs_in_ref, xs_ref, sem, *, td, n):
    del xs_in_ref
    first = pl.program_id(0) * td

    def copies(r):
        return [pltpu.make_async_copy(h_ref.at[pl.ds(r, 1)], xs_ref.at[pl.ds(dst_ref[k * n + first + r], 1)], sem)
                for k in range(2)]

    def start(r, carry):
        for cp in copies(r):
            cp.start()
        return carry

    def wait(r, carry):
        for cp in copies(r):
            cp.wait()
        return carry

    lax.fori_loop(0, td, start, 0, unroll=DMA_UNROLL)
    lax.fori_loop(0, td, wait, 0, unroll=DMA_UNROLL)


def _dispatch(h, dst_flat, n_slots, td):
    n = h.shape[0]
    grid_spec = pltpu.PrefetchScalarGridSpec(
        num_scalar_prefetch=1,
        grid=(n // td,),
        in_specs=[pl.BlockSpec((td, D_MODEL), lambda i, dst: (i, 0)),
                  pl.BlockSpec(memory_space=pl.ANY)],
        out_specs=pl.BlockSpec(memory_space=pl.ANY),
        scratch_shapes=[pltpu.SemaphoreType.DMA])
    return pl.pallas_call(
        functools.partial(_dispatch_kernel, td=td, n=n),
        out_shape=jax.ShapeDtypeStruct((n_slots, D_MODEL), F32),
        grid_spec=grid_spec,
        input_output_aliases={2: 0},
        compiler_params=_params(("arbitrary",)),
        name="moe_dispatch",
    )(dst_flat, h, jnp.zeros((n_slots, D_MODEL), F32))


def _experts_kernel(te_ref, tv_ref, xs_ref, w1_ref, w3_ref, w2_ref, ys_ref):
    del te_ref
    valid = tv_ref[pl.program_id(0)] > 0

    @pl.when(valid)
    def _():
        x = xs_ref[...].astype(BF16)
        a = _dot(x, w1_ref[...])
        b = _dot(x, w3_ref[...])
        ys_ref[...] = _dot((a * jax.nn.sigmoid(a) * b).astype(BF16), w2_ref[...])

    @pl.when(jnp.logical_not(valid))
    def _():
        ys_ref[...] = jnp.zeros(ys_ref.shape, F32)


def _experts(xs, tile_expert, tile_valid, w1, w3, w2, l):
    n_tiles = xs.shape[0] // MOE_TILE
    grid_spec = pltpu.PrefetchScalarGridSpec(
        num_scalar_prefetch=2,
        grid=(n_tiles,),
        in_specs=[pl.BlockSpec((MOE_TILE, D_MODEL), lambda i, te, tv: (i, 0)),
                  pl.BlockSpec((None, None, D_MODEL, EXPERT_DFF), lambda i, te, tv: (l, te[i], 0, 0)),
                  pl.BlockSpec((None, None, D_MODEL, EXPERT_DFF), lambda i, te, tv: (l, te[i], 0, 0)),
                  pl.BlockSpec((None, None, EXPERT_DFF, D_MODEL), lambda i, te, tv: (l, te[i], 0, 0))],
        out_specs=pl.BlockSpec((MOE_TILE, D_MODEL), lambda i, te, tv: (i, 0)))
    return pl.pallas_call(
        _experts_kernel,
        out_shape=jax.ShapeDtypeStruct(xs.shape, F32),
        grid_spec=grid_spec,
        compiler_params=_params(("arbitrary",)),
        name="moe_experts",
    )(tile_expert, tile_valid, xs, w1, w3, w2)


def _combine_kernel(dst_ref, ys_ref, gw_ref, x_ref, gate_ref, lng_ref, lnb_ref, o_ref, buf, sem, *, tc, n):
    i = pl.program_id(0)
    slot = i % 2

    def copies(tile, s, r):
        return [pltpu.make_async_copy(ys_ref.at[pl.ds(dst_ref[k * n + tile * tc + r], 1)],
                                      buf.at[s, k, pl.ds(r, 1)], sem.at[s]) for k in range(2)]

    def start_tile(tile, s):
        def body(r, carry):
            for cp in copies(tile, s, r):
                cp.start()
            return carry
        lax.fori_loop(0, tc, body, 0, unroll=DMA_UNROLL)

    def wait_tile(tile, s):
        def body(r, carry):
            for cp in copies(tile, s, r):
                cp.wait()
            return carry
        lax.fori_loop(0, tc, body, 0, unroll=DMA_UNROLL)

    @pl.when(i == 0)
    def _():
        start_tile(0, 0)

    @pl.when(i + 1 < pl.num_programs(0))
    def _():
        start_tile(i + 1, 1 - slot)

    wait_tile(i, slot)
    gw = gw_ref[...]
    y = gw[:, 0:1] * buf[slot, 0] + gw[:, 1:2] * buf[slot, 1]
    o_ref[...] = _post_norm(x_ref[...], gate_ref[0], y, lng_ref[...], lnb_ref[...])


def _combine(ys, dst_flat, gw, x, gate, ln_g, ln_b, l, tc):
    n = x.shape[0]
    tiles_per_mod = (n // tc) // gate.shape[0]
    vec = pl.BlockSpec((None, 1, D_MODEL), lambda i, dst: (l, 0, 0))
    grid_spec = pltpu.PrefetchScalarGridSpec(
        num_scalar_prefetch=1,
        grid=(n // tc,),
        in_specs=[pl.BlockSpec(memory_space=pl.ANY),
                  pl.BlockSpec((tc, 2), lambda i, dst: (i, 0)),
                  pl.BlockSpec((tc, D_MODEL), lambda i, dst: (i, 0)),
                  pl.BlockSpec((1, gate.shape[1], D_MODEL), lambda i, dst: (i // tiles_per_mod, 0, 0)),
                  vec, vec],
        out_specs=pl.BlockSpec((tc, D_MODEL), lambda i, dst: (i, 0)),
        scratch_shapes=[pltpu.VMEM((2, 2, tc, D_MODEL), F32),
                        pltpu.SemaphoreType.DMA((2,))])
    return pl.pallas_call(
        functools.partial(_combine_kernel, tc=tc, n=n),
        out_shape=jax.ShapeDtypeStruct((n, D_MODEL), F32),
        grid_spec=grid_spec,
        compiler_params=_params(("arbitrary",)),
        name="moe_combine",
    )(dst_flat, ys, gw, x, gate, ln_g.reshape(DEPTH, 1, D_MODEL), ln_b.reshape(DEPTH, 1, D_MODEL))


def _moe_grouped(x, shift, scale, gate, w_router_t, b_router, w1, w3, w2, ln_g, ln_b, l, tm):
    n = x.shape[0]
    h, eid, gw, pos, cnt = _router_sorted(x, shift, scale, w_router_t, b_router, tm)
    counts = cnt[:, 0].astype(I32)
    padded = (counts + MOE_TILE - 1) // MOE_TILE * MOE_TILE
    ends = jnp.cumsum(padded)
    experts = jnp.arange(N_EXPERTS, dtype=I32)[:, None, None]
    seg_start = jnp.sum(jnp.where(eid[None] == experts, (ends - padded)[:, None, None], 0), axis=0)
    dst_flat = (seg_start + pos).reshape(2 * n)
    n_tiles = 2 * n // MOE_TILE + N_EXPERTS
    tile_start = jnp.arange(n_tiles, dtype=I32) * MOE_TILE
    tile_expert = jnp.minimum(jnp.sum((tile_start[:, None] >= ends[None, :]).astype(I32), axis=1), N_EXPERTS - 1)
    tile_valid = (tile_start < ends[-1]).astype(I32)
    xs = _dispatch(h, dst_flat, n_tiles * MOE_TILE, tm)
    ys = _experts(xs, tile_expert, tile_valid, w1, w3, w2, l)
    return _combine(ys, dst_flat, gw.T, x, gate, ln_g, ln_b, l, MOE_TILE)


def _router(x, shift, scale, w_router_t, b_router, tm):
    n = x.shape[0]
    tiles_per_mod = (n // tm) // shift.shape[0]
    mod_spec = pl.BlockSpec((1, shift.shape[1], D_MODEL), lambda i: (i // tiles_per_mod, 0, 0))
    return pl.pallas_call(
        _router_kernel,
        out_shape=(jax.ShapeDtypeStruct((n, D_MODEL), BF16),
                   jax.ShapeDtypeStruct((N_EXPERTS, n), F32)),
        grid=(n // tm,),
        in_specs=[pl.BlockSpec((tm, D_MODEL), lambda i: (i, 0)),
                  mod_spec, mod_spec,
                  pl.BlockSpec((N_EXPERTS, D_MODEL), lambda i: (0, 0)),
                  pl.BlockSpec((N_EXPERTS, 1), lambda i: (0, 0))],
        out_specs=(pl.BlockSpec((tm, D_MODEL), lambda i: (i, 0)),
                   pl.BlockSpec((N_EXPERTS, tm), lambda i: (0, i))),
        compiler_params=_params(("parallel",)),
        name="router",
    )(x, shift, scale, w_router_t, b_router.reshape(N_EXPERTS, 1))


def _moe_kernel(h_ref, g_ref, w1_ref, w3_ref, w2_ref, x_ref, gate_ref, lng_ref, lnb_ref, o_ref, acc_scr):
    e = pl.program_id(1)

    @pl.when(e == 0)
    def _():
        acc_scr[...] = jnp.zeros(acc_scr.shape, F32)

    h = h_ref[...]
    a = _dot(h, w1_ref[...])
    b = _dot(h, w3_ref[...])
    gates = g_ref[...]
    lane = lax.broadcasted_iota(I32, gates.shape, 1)
    gcol = jnp.sum(jnp.where(lane == e, gates, 0.0), axis=1, keepdims=True)
    u = (a * jax.nn.sigmoid(a) * b * gcol).astype(BF16)
    acc_scr[...] += _dot(u, w2_ref[...])

    @pl.when(e == pl.num_programs(1) - 1)
    def _():
        o_ref[...] = _post_norm(x_ref[...], gate_ref[0], acc_scr[...], lng_ref[...], lnb_ref[...])


def _moe(h, gates, w1, w3, w2, x, gate, ln_g, ln_b, l, tm):
    n = x.shape[0]
    tiles_per_mod = (n // tm) // gate.shape[0]
    vec = pl.BlockSpec((None, 1, D_MODEL), lambda i, e: (l, 0, 0))
    return pl.pallas_call(
        _moe_kernel,
        out_shape=jax.ShapeDtypeStruct((n, D_MODEL), F32),
        grid=(n // tm, N_EXPERTS),
        in_specs=[pl.BlockSpec((tm, D_MODEL), lambda i, e: (i, 0)),
                  pl.BlockSpec((tm, N_EXPERTS), lambda i, e: (i, 0)),
                  pl.BlockSpec((None, None, D_MODEL, EXPERT_DFF), lambda i, e: (l, e, 0, 0)),
                  pl.BlockSpec((None, None, D_MODEL, EXPERT_DFF), lambda i, e: (l, e, 0, 0)),
                  pl.BlockSpec((None, None, EXPERT_DFF, D_MODEL), lambda i, e: (l, e, 0, 0)),
                  pl.BlockSpec((tm, D_MODEL), lambda i, e: (i, 0)),
                  pl.BlockSpec((1, gate.shape[1], D_MODEL), lambda i, e: (i // tiles_per_mod, 0, 0)),
                  vec, vec],
        out_specs=pl.BlockSpec((tm, D_MODEL), lambda i, e: (i, 0)),
        scratch_shapes=[pltpu.VMEM((tm, D_MODEL), F32)],
        compiler_params=_params(("parallel", "arbitrary")),
        name="moe",
    )(h, gates, w1, w3, w2, x, gate, ln_g.reshape(DEPTH, 1, D_MODEL), ln_b.reshape(DEPTH, 1, D_MODEL))


def _row_to_col(v):
    n = v.shape[1]
    r = lax.broadcasted_iota(I32, (n, n), 0)
    c = lax.broadcasted_iota(I32, (n, n), 1)
    return jnp.sum(jnp.where(r == c, jnp.broadcast_to(v, (n, n)), 0.0), axis=1, keepdims=True)


def _attn_decode_kernel(pt_ref, proj_ref, ck_ref, cv_ref, cik_ref, o_ref,
                        ik_buf, k_buf, v_buf, key_scr, sem_ik, sem_k, sem_v,
                        *, l, n_pages, page, chunk_pages, n_sel):
    b = pl.program_id(0)
    n_chunks = n_pages // chunk_pages
    chunk = chunk_pages * page

    def ik_copy(j):
        return pltpu.make_async_copy(cik_ref.at[l, pt_ref[b, j]], ik_buf.at[pl.ds(j * page, page)], sem_ik)

    def kv_copies(c, j, slot):
        pg = pt_ref[b, c * chunk_pages + j]
        dst = pl.ds(j * page, page)
        copies = []
        for g in range(N_KV_HEADS):
            copies.append(pltpu.make_async_copy(ck_ref.at[l, pg, :, g, :], k_buf.at[slot, g, dst], sem_k.at[slot]))
            copies.append(pltpu.make_async_copy(cv_ref.at[l, pg, :, g, :], v_buf.at[slot, g, dst], sem_v.at[slot]))
        return copies

    def start_chunk(c, slot):
        def body(j, carry):
            for cp in kv_copies(c, j, slot):
                cp.start()
            return carry
        lax.fori_loop(0, chunk_pages, body, 0)

    def wait_chunk(c, slot):
        def body(j, carry):
            for cp in kv_copies(c, j, slot):
                cp.wait()
            return carry
        lax.fori_loop(0, chunk_pages, body, 0)

    def start_ik(j, carry):
        ik_copy(j).start()
        return carry

    def wait_ik(j, carry):
        ik_copy(j).wait()
        return carry

    lax.fori_loop(0, n_pages, start_ik, 0)
    start_chunk(0, 0)
    lax.fori_loop(0, n_pages, wait_ik, 0)

    rowsl = slice(None)
    iq = proj_ref[rowsl, OFF_IQ:OFF_IQ + N_IDX_HEADS * IDX_DIM]
    iq_h = jnp.concatenate([iq[:, h * IDX_DIM:(h + 1) * IDX_DIM] for h in range(N_IDX_HEADS)], axis=0)
    w_row = proj_ref[rowsl, OFF_IKW + IDX_DIM:OFF_IKW + IDX_DIM + N_IDX_HEADS]
    w_col = _row_to_col(w_row) * (IDX_DIM ** -0.5 * N_IDX_HEADS ** -0.5)
    ik_new = proj_ref[rowsl, OFF_IKW:OFF_IKW + IDX_DIM]

    iq_b = iq_h.astype(BF16)
    for c in range(n_chunks):
        lg = _dot_nt(iq_b, ik_buf[c * chunk:(c + 1) * chunk, :].astype(BF16))
        s = jnp.sum(w_col * jnp.maximum(lg, 0.0), axis=0, keepdims=True)
        key_scr[c] = _sort_key(s)
    lg_new = jnp.sum(iq_b.astype(F32) * ik_new.astype(BF16).astype(F32), axis=1, keepdims=True)
    key_new = _sort_key(jnp.sum(w_col * jnp.maximum(lg_new, 0.0), axis=0, keepdims=True))

    def count_ge(cand):
        cnt = jnp.where(key_new >= cand, 1.0, 0.0)
        for c in range(n_chunks):
            cnt = cnt + jnp.sum(jnp.where(key_scr[c] >= cand, 1.0, 0.0), axis=1, keepdims=True)
        return cnt

    thr = _kth_largest_key(count_ge, float(n_sel), (1, 1))

    q = proj_ref[rowsl, OFF_Q:OFF_Q + ATT_WIDTH]
    q_h = jnp.concatenate([q[:, h * HEAD_DIM:(h + 1) * HEAD_DIM] for h in range(N_ATT_HEADS)], axis=0)
    q_b = q_h.astype(BF16)
    k_new = proj_ref[rowsl, OFF_K:OFF_K + KV_WIDTH]
    v_new = proj_ref[rowsl, OFF_V:OFF_V + KV_WIDTH]
    scale = HEAD_DIM ** -0.5

    def att_chunk(c, carry):
        slot = c % 2

        @pl.when(c + 1 < n_chunks)
        def _():
            start_chunk(c + 1, 1 - slot)

        wait_chunk(c, slot)
        sel = key_scr[c] >= thr
        out = []
        for g in range(N_KV_HEADS):
            m_old, l_old, acc_old = carry[g]
            kg = k_buf[slot, g].astype(BF16)
            vg = v_buf[slot, g].astype(BF16)
            s = _dot_nt(q_b[g * KV_GROUP:(g + 1) * KV_GROUP], kg) * scale
            s = jnp.where(sel, s, MASK_VALUE)
            m_new = jnp.maximum(m_old, jnp.max(s, axis=1, keepdims=True))
            p = jnp.where(sel, jnp.exp(s - m_new), 0.0)
            a = jnp.exp(m_old - m_new)
            out.append((m_new, a * l_old + jnp.sum(p, axis=1, keepdims=True),
                        a * acc_old + _dot(p.astype(BF16), vg)))
        return tuple(out)

    init = tuple((jnp.full((KV_GROUP, 1), MASK_VALUE, F32), jnp.zeros((KV_GROUP, 1), F32),
                  jnp.zeros((KV_GROUP, HEAD_DIM), F32)) for _ in range(N_KV_HEADS))
    res = lax.fori_loop(0, n_chunks, att_chunk, init)

    sel_new = key_new >= thr
    for g in range(N_KV_HEADS):
        m_old, l_old, acc_old = res[g]
        kg = k_new[:, g * HEAD_DIM:(g + 1) * HEAD_DIM].astype(BF16).astype(F32)
        vg = v_new[:, g * HEAD_DIM:(g + 1) * HEAD_DIM].astype(BF16).astype(F32)
        qg = q_b[g * KV_GROUP:(g + 1) * KV_GROUP].astype(F32)
        s = jnp.sum(qg * kg, axis=1, keepdims=True) * scale
        s = jnp.where(sel_new, s, MASK_VALUE)
        m_new = jnp.maximum(m_old, s)
        p = jnp.where(sel_new, jnp.exp(s - m_new), 0.0)
        a = jnp.exp(m_old - m_new)
        l_new = a * l_old + p
        acc = a * acc_old + p.astype(BF16).astype(F32) * vg
        o = acc / l_new
        for r in range(KV_GROUP):
            hh = g * KV_GROUP + r
            o_ref[rowsl, hh * HEAD_DIM:(hh + 1) * HEAD_DIM] = o[r:r + 1, :].astype(o_ref.dtype)


def _attn_decode(proj, page_table, cache_k, cache_v, cache_idx_k, l):
    db, n_pages = page_table.shape
    page = cache_k.shape[2]
    past = n_pages * page
    n_sel = min(TOPK_MAX, (past + 1) // 4)
    chunk_pages = _tile(n_pages, 16)
    kern = functools.partial(_attn_decode_kernel, l=l, n_pages=n_pages, page=page,
                             chunk_pages=chunk_pages, n_sel=n_sel)
    grid_spec = pltpu.PrefetchScalarGridSpec(
        num_scalar_prefetch=1,
        grid=(db,),
        in_specs=[pl.BlockSpec((None, 1, PACK_WIDTH), lambda b, pt: (b, 0, 0)),
                  pl.BlockSpec(memory_space=pl.ANY),
                  pl.BlockSpec(memory_space=pl.ANY),
                  pl.BlockSpec(memory_space=pl.ANY)],
        out_specs=pl.BlockSpec((None, 1, ATT_WIDTH), lambda b, pt: (b, 0, 0)),
        scratch_shapes=[pltpu.VMEM((past, IDX_DIM), F32),
                        pltpu.VMEM((2, N_KV_HEADS, chunk_pages * page, HEAD_DIM), F32),
                        pltpu.VMEM((2, N_KV_HEADS, chunk_pages * page, HEAD_DIM), F32),
                        pltpu.VMEM((n_pages // chunk_pages, 1, chunk_pages * page), I32),
                        pltpu.SemaphoreType.DMA,
                        pltpu.SemaphoreType.DMA((2,)),
                        pltpu.SemaphoreType.DMA((2,))])
    return pl.pallas_call(
        kern,
        out_shape=jax.ShapeDtypeStruct((db, 1, ATT_WIDTH), F32),
        grid_spec=grid_spec,
        compiler_params=_params(("arbitrary",)),
        name="attn_decode",
    )(page_table, proj.reshape(db, 1, PACK_WIDTH), cache_k, cache_v, cache_idx_k).reshape(db, ATT_WIDTH)


def _hgrn_decode_kernel(proj_ref, s0_ref, lb_ref, ng_ref, o_ref, s_ref):
    rowsl = slice(None)
    for h in range(N_HGRN_HEADS):
        cols = lambda off: slice(off + h * HGRN_DK, off + (h + 1) * HGRN_DK)
        lb = lb_ref[:, h * HGRN_DK:(h + 1) * HGRN_DK]
        q, f, kk = _hgrn_gates(proj_ref[rowsl, cols(OFF_HQ)], proj_ref[rowsl, cols(OFF_HF)], lb)
        v = proj_ref[rowsl, cols(OFF_HI)]
        s_new = _row_to_col(f) * s0_ref[h] + _row_to_col(kk) * v
        s_ref[h] = s_new
        o = jnp.sum(_row_to_col(q) * s_new, axis=0, keepdims=True)
        ng = ng_ref[:, h * HGRN_DV:(h + 1) * HGRN_DV]
        o_ref[rowsl, h * HGRN_DV:(h + 1) * HGRN_DV] = _hgrn_finish(
            o, proj_ref[rowsl, cols(OFF_HG)], ng).astype(o_ref.dtype)


def _hgrn_decode(proj, state, lb, ng, l):
    db = proj.shape[0]
    st_spec_in = pl.BlockSpec((None, None, N_HGRN_HEADS, HGRN_DK, HGRN_DV), lambda b: (l, b, 0, 0, 0))
    o, s = pl.pallas_call(
        _hgrn_decode_kernel,
        out_shape=(jax.ShapeDtypeStruct((db, 1, HGRN_WIDTH), F32),
                   jax.ShapeDtypeStruct((db, N_HGRN_HEADS, HGRN_DK, HGRN_DV), F32)),
        grid=(db,),
        in_specs=[pl.BlockSpec((None, 1, PACK_WIDTH), lambda b: (b, 0, 0)),
                  st_spec_in,
                  pl.BlockSpec((1, HGRN_WIDTH), lambda b: (0, 0)),
                  pl.BlockSpec((1, HGRN_WIDTH), lambda b: (0, 0))],
        out_specs=(pl.BlockSpec((None, 1, HGRN_WIDTH), lambda b: (b, 0, 0)),
                   pl.BlockSpec((None, N_HGRN_HEADS, HGRN_DK, HGRN_DV), lambda b: (b, 0, 0, 0))),
        compiler_params=_params(("parallel",)),
        name="hgrn_decode",
    )(proj.reshape(db, 1, PACK_WIDTH), state, lb.reshape(1, HGRN_WIDTH), ng.reshape(1, HGRN_WIDTH))
    return o.reshape(db, HGRN_WIDTH), s


def _pack_w_in(w_in):
    offs = [0]
    for s in IN_SPLITS:
        offs.append(offs[-1] + s)
    q, k, v, iq, ik, iw, hq, hf, hi, hg, ga, gb = [w_in[:, :, offs[i]:offs[i + 1]] for i in range(len(IN_SPLITS))]
    pad = jnp.zeros(w_in.shape[:2] + (PACK_WIDTH - OFF_IKW - IDX_DIM - N_IDX_HEADS,), w_in.dtype)
    return jnp.concatenate([ga, gb, q, iq, hq, hf, hi, hg, k, v, ik, iw, pad], axis=-1).astype(BF16)


def _mods(mod_l, rows, per_row):
    m = mod_l[rows]
    parts = jnp.split(m, 6, axis=-1)
    if per_row:
        return [p[None, :, :] for p in parts]
    return [p[:, None, :] for p in parts]


def kernel(x_prompt, x_sample, c_prompt, c_sample, cache_k, cache_v, cache_idx_k, state_hgrn, page_table,
           w_ada, b_ada, w_in, w_up_a, w_up_b, w_o, hgrn_norm_g, hgrn_lb_logits, ln1_g, ln1_b,
           w_router, b_router, w1, w3, w2, ln2_g, ln2_b):
    bp, t, _ = x_prompt.shape
    db = x_sample.shape[0]
    lbp = jax.nn.softmax(hgrn_lb_logits.astype(F32), axis=0)
    lower_bounds = jnp.cumsum(lbp, axis=0) - lbp[0]

    w_pack = _pack_w_in(w_in)
    w_up_a_b, w_up_b_b, w_o_b = w_up_a.astype(BF16), w_up_b.astype(BF16), w_o.astype(BF16)
    w1_b, w3_b, w2_b = w1.astype(BF16), w3.astype(BF16), w2.astype(BF16)
    w_router_t = w_router.T

    n_c = bp + db
    c_rows = -(-n_c // 8) * 8
    c_all = jnp.concatenate([c_prompt, c_sample, jnp.zeros((c_rows - n_c, D_MODEL), F32)], axis=0)
    mod = _ada(c_all, w_ada, b_ada)

    xp = x_prompt.reshape(bp * t, D_MODEL)
    xs = x_sample.reshape(db, D_MODEL)
    tm_p = _tile(bp * t, 512)
    outs_p = {"k": [], "v": [], "ik": [], "s": []}
    outs_s = {"k": [], "v": [], "ik": [], "s": []}
    for l in range(DEPTH):
        sh1, sc1, g1, sh2, sc2, g2 = _mods(mod[l], slice(0, bp), per_row=False)
        proj = _proj(xp, sh1, sc1, w_pack, l, _tile(t, 1024))
        o_a = _attn_prompt(proj, bp, t)
        o_b, s_new = _hgrn_prompt(proj, lower_bounds[l], hgrn_norm_g[l], bp, t)
        merged = _merge(o_a, o_b, proj, w_up_a_b, w_up_b_b, l, tm_p)
        xp = _out_proj(merged, w_o_b, xp, g1, ln1_g, ln1_b, l, tm_p)
        xp = _moe_grouped(xp, sh2, sc2, g2, w_router_t, b_router, w1_b, w3_b, w2_b, ln2_g, ln2_b, l, tm_p)
        outs_p["k"].append(proj[:, OFF_K:OFF_K + KV_WIDTH].reshape(bp, t, N_KV_HEADS, HEAD_DIM))
        outs_p["v"].append(proj[:, OFF_V:OFF_V + KV_WIDTH].reshape(bp, t, N_KV_HEADS, HEAD_DIM))
        outs_p["ik"].append(proj[:, OFF_IKW:OFF_IKW + IDX_DIM].reshape(bp, t, IDX_DIM))
        outs_p["s"].append(s_new)

        sh1, sc1, g1, sh2, sc2, g2 = _mods(mod[l], slice(bp, bp + db), per_row=True)
        proj = _proj(xs, sh1, sc1, w_pack, l, db)
        o_a = _attn_decode(proj, page_table, cache_k, cache_v, cache_idx_k, l)
        o_b, s_new = _hgrn_decode(proj, state_hgrn, lower_bounds[l], hgrn_norm_g[l], l)
        merged = _merge(o_a, o_b, proj, w_up_a_b, w_up_b_b, l, db)
        xs = _out_proj(merged, w_o_b, xs, g1, ln1_g, ln1_b, l, db)
        h2, gates = _router(xs, sh2, sc2, w_router_t, b_router, db)
        xs = _moe(h2, gates.T, w1_b, w3_b, w2_b, xs, g2, ln2_g, ln2_b, l, db)
        outs_s["k"].append(proj[:, OFF_K:OFF_K + KV_WIDTH].reshape(db, 1, N_KV_HEADS, HEAD_DIM))
        outs_s["v"].append(proj[:, OFF_V:OFF_V + KV_WIDTH].reshape(db, 1, N_KV_HEADS, HEAD_DIM))
        outs_s["ik"].append(proj[:, OFF_IKW:OFF_IKW + IDX_DIM].reshape(db, 1, IDX_DIM))
        outs_s["s"].append(s_new)

    return (xp.reshape(bp, t, D_MODEL), xs.reshape(db, 1, D_MODEL),
            jnp.stack(outs_p["k"]), jnp.stack(outs_p["v"]), jnp.stack(outs_p["ik"]), jnp.stack(outs_p["s"]),
            jnp.stack(outs_s["k"]), jnp.stack(outs_s["v"]), jnp.stack(outs_s["ik"]), jnp.stack(outs_s["s"]))
```

```python
import functools

import jax
import jax.numpy as jnp
from jax import lax
from jax.experimental import pallas as pl
from jax.experimental.pallas import tpu as pltpu

F32 = jnp.float32
BF16 = jnp.bfloat16
I32 = jnp.int32
I16 = jnp.int16

DEPTH = 2
D_MODEL = 2048
N_ATT_HEADS = 8
N_KV_HEADS = 2
KV_GROUP = N_ATT_HEADS // N_KV_HEADS
HEAD_DIM = 128
ATT_WIDTH = N_ATT_HEADS * HEAD_DIM
KV_WIDTH = N_KV_HEADS * HEAD_DIM
N_IDX_HEADS = 16
IDX_DIM = 64
TOPK_MAX = 256
N_HGRN_HEADS = 8
HGRN_DK = 128
HGRN_DV = 128
HGRN_WIDTH = N_HGRN_HEADS * HGRN_DK
HGRN_CHUNK = 64
HGRN_SUB = 8
N_EXPERTS = 16
N_GROUPS = 4
EXPERTS_PER_GROUP = N_EXPERTS // N_GROUPS
EXPERT_DFF = 1024
ALPHA = (2 * DEPTH) ** 0.25
LN_EPS = 1e-5
MASK_VALUE = -1e30
INT_MIN = -2 ** 31
LOG2_E = 1.4426950408889634

IN_SPLITS = (ATT_WIDTH, KV_WIDTH, KV_WIDTH, N_IDX_HEADS * IDX_DIM, IDX_DIM, N_IDX_HEADS,
             HGRN_WIDTH, HGRN_WIDTH, HGRN_WIDTH, HGRN_WIDTH, D_MODEL, D_MODEL)

LANES = 128
PROJ_TN = 1024
IN_WIDTH = sum(IN_SPLITS)
TAIL_START = ATT_WIDTH + 2 * KV_WIDTH + N_IDX_HEADS * IDX_DIM + IDX_DIM + N_IDX_HEADS
TAIL_COLS = IN_WIDTH - TAIL_START
HEAD_COLS = -(-(TAIL_START + LANES - (TAIL_START % LANES)) // PROJ_TN) * PROJ_TN
OFF_HQ = 0
OFF_HF = OFF_HQ + HGRN_WIDTH
OFF_HI = OFF_HF + HGRN_WIDTH
OFF_HG = OFF_HI + HGRN_WIDTH
OFF_GA = OFF_HG + HGRN_WIDTH
OFF_GB = OFF_GA + D_MODEL
OFF_Q = TAIL_COLS
OFF_K = OFF_Q + ATT_WIDTH
OFF_V = OFF_K + KV_WIDTH
OFF_IQ = OFF_V + KV_WIDTH
OFF_IKW = OFF_IQ + N_IDX_HEADS * IDX_DIM
IQ_BLOCK = 512
PACK_WIDTH = TAIL_COLS + HEAD_COLS

VMEM_LIMIT = 59 * 1024 * 1024


def _params(semantics):
    return pltpu.CompilerParams(dimension_semantics=semantics, vmem_limit_bytes=VMEM_LIMIT)


def _tile(n, pref):
    t = min(n, pref)
    while n % t:
        t //= 2
    return t


def _ln(x):
    mu = jnp.mean(x, axis=-1, keepdims=True)
    xc = x - mu
    var = jnp.mean(xc * xc, axis=-1, keepdims=True)
    return xc * lax.rsqrt(var + LN_EPS)


def _dot(a, b):
    return jnp.dot(a, b, preferred_element_type=F32)


def _dot_nt(a, b):
    return lax.dot_general(a, b, (((1,), (1,)), ((), ())), preferred_element_type=F32)


def _split3(x):
    x1 = x.astype(BF16)
    r1 = x - x1.astype(F32)
    x2 = r1.astype(BF16)
    x3 = (r1 - x2.astype(F32)).astype(BF16)
    return x1, x2, x3


def _sort_key(s):
    bits = pltpu.bitcast(s, I32)
    return bits ^ ((bits >> 31) & 0x7FFFFFFF)


def _kth_largest_key(count_ge, n_sel, shape):
    def body(i, ans):
        bit = lax.shift_left(jnp.int32(1), jnp.int32(31) - i)
        cand = ans | bit
        cnt = count_ge(cand ^ INT_MIN)
        return jnp.where(cnt >= n_sel, cand, ans)
    ans = lax.fori_loop(0, 32, body, jnp.zeros(shape, I32))
    return ans ^ INT_MIN


def _ada_kernel(c_ref, w_ref, b_ref, o_ref):
    c = c_ref[...]
    a = (c * jax.nn.sigmoid(c)).astype(BF16)
    o_ref[...] = _dot(a, w_ref[...].astype(BF16)) + b_ref[...]


def _ada(c_all, w_ada, b_ada):
    rows = c_all.shape[0]
    width = w_ada.shape[-1]
    tn = _tile(width, 1024)
    return pl.pallas_call(
        _ada_kernel,
        out_shape=jax.ShapeDtypeStruct((DEPTH, rows, width), F32),
        grid=(DEPTH, width // tn),
        in_specs=[pl.BlockSpec((rows, D_MODEL), lambda l, j: (0, 0)),
                  pl.BlockSpec((None, D_MODEL, tn), lambda l, j: (l, 0, j)),
                  pl.BlockSpec((None, 1, tn), lambda l, j: (l, 0, j))],
        out_specs=pl.BlockSpec((None, rows, tn), lambda l, j: (l, 0, j)),
        compiler_params=_params(("parallel", "parallel")),
        name="ada",
    )(c_all, w_ada, b_ada.reshape(DEPTH, 1, width))


def _proj_kernel(x_ref, sh_ref, sc_ref, wt_ref, wh_ref, o_ref, h_scr, *, tail_tiles):
    j = pl.program_id(1)

    @pl.when(j == 0)
    def _():
        h_scr[...] = (_ln(x_ref[...]) * (1.0 + sc_ref[0]) + sh_ref[0]).astype(BF16)

    @pl.when(j < tail_tiles)
    def _():
        o_ref[...] = _dot(h_scr[...], wt_ref[...])

    @pl.when(j >= tail_tiles)
    def _():
        o_ref[...] = _dot(h_scr[...], wh_ref[...])


def _proj(x, shift, scale, w_tail, w_head, l, tm):
    n = x.shape[0]
    tiles_per_mod = (n // tm) // shift.shape[0]
    r = shift.shape[1]
    tn = PROJ_TN
    tail_tiles = TAIL_COLS // tn
    mod_spec = pl.BlockSpec((1, r, D_MODEL), lambda i, j: (i // tiles_per_mod, 0, 0))
    return pl.pallas_call(
        functools.partial(_proj_kernel, tail_tiles=tail_tiles),
        out_shape=jax.ShapeDtypeStruct((n, PACK_WIDTH), F32),
        grid=(n // tm, PACK_WIDTH // tn),
        in_specs=[pl.BlockSpec((tm, D_MODEL), lambda i, j: (i, 0)),
                  mod_spec, mod_spec,
                  pl.BlockSpec((None, D_MODEL, tn), lambda i, j: (l, 0, jnp.minimum(j, tail_tiles - 1))),
                  pl.BlockSpec((None, D_MODEL, tn), lambda i, j: (l, 0, jnp.maximum(j - tail_tiles, 0)))],
        out_specs=pl.BlockSpec((tm, tn), lambda i, j: (i, j)),
        scratch_shapes=[pltpu.VMEM((tm, D_MODEL), BF16)],
        compiler_params=_params(("parallel", "arbitrary")),
        name="proj",
    )(x, shift, scale, w_tail, w_head)


KEY_BLOCK = 256
ROW_CHUNK = 64
ATT_ROWS = 128


def _lane_fold(x, op):
    acc = x[:, :LANES]
    for i in range(1, x.shape[1] // LANES):
        acc = op(acc, x[:, i * LANES:(i + 1) * LANES])
    return acc


def _attn_prompt_kernel(q_ref, iq0_ref, iq1_ref, iwq_ref, k_ref, v_ref, ikw_ref, o_ref,
                        kb_scr, va_scr, ikb_scr, iqs_scr, qs_scr, key_scr, h16_scr, m_scr, acc_scr, xb_scr,
                        *, tq, n_sel, col_bits):
    qi = pl.program_id(1)
    tk = min(tq, KEY_BLOCK)
    n_kb = (qi + 1) * (tq // tk)
    n_lane_tiles = tk // LANES

    @pl.when(qi == 0)
    def _():
        kb_scr[...] = k_ref[...].astype(BF16)
        for g in range(N_KV_HEADS):
            va_scr[:, 2 * g * HEAD_DIM:(2 * g + 1) * HEAD_DIM] = (
                v_ref[:, g * HEAD_DIM:(g + 1) * HEAD_DIM].astype(BF16))
            va_scr[:, (2 * g + 1) * HEAD_DIM:(2 * g + 2) * HEAD_DIM] = jnp.ones((va_scr.shape[0], HEAD_DIM), BF16)
        ikb_scr[...] = ikw_ref[:, :IDX_DIM].astype(BF16)

    iq = jnp.concatenate([iq0_ref[...], iq1_ref[...]], axis=1)
    for h in range(N_IDX_HEADS):
        iqs_scr[h * tq:(h + 1) * tq, :] = iq[:, h * IDX_DIM:(h + 1) * IDX_DIM].astype(BF16)
    q = q_ref[...] * (HEAD_DIM ** -0.5 * LOG2_E)
    for h in range(N_ATT_HEADS):
        qs_scr[h * tq:(h + 1) * tq, :] = q[:, h * HEAD_DIM:(h + 1) * HEAD_DIM].astype(BF16)
    w = iwq_ref[:, IDX_DIM:IDX_DIM + N_IDX_HEADS] * (IDX_DIM ** -0.5 * N_IDX_HEADS ** -0.5)
    row = qi * tq + lax.broadcasted_iota(I32, (tq, tk), 0)
    col0 = lax.broadcasted_iota(I32, (tq, tk), 1)

    def score_block(kb, carry):
        start = pl.multiple_of(kb * tk, tk)
        ikb = ikb_scr[pl.ds(start, tk), :]
        s = jnp.zeros((tq, tk), F32)
        for h in range(N_IDX_HEADS):
            lg = _dot_nt(iqs_scr[h * tq:(h + 1) * tq, :], ikb)
            s = s + w[:, h:h + 1] * jnp.maximum(lg, 0.0)
        s = jnp.where(kb * tk + col0 <= row, s, MASK_VALUE)
        key = _sort_key(s)
        key_scr[kb] = key
        h16_scr[kb] = (key >> 16).astype(I16)
        return carry

    lax.fori_loop(0, n_kb, score_block, 0)

    chunks = [slice(c * ROW_CHUNK, (c + 1) * ROW_CHUNK) for c in range(tq // ROW_CHUNK)]

    def count(*make_preds):
        preds = [[mp(rows) for rows in chunks] for mp in make_preds]

        def body(kb, parts):
            new = []
            for ci, rows in enumerate(chunks):
                keys = key_scr[kb, rows, :]
                for pi in range(len(preds)):
                    idx = pi * len(chunks) + ci
                    hit = jnp.where(preds[pi][ci](kb, keys), 1.0, 0.0)
                    new.append((idx, parts[idx] + _lane_fold(hit, jnp.add)))
            return tuple(p for _, p in sorted(new, key=lambda t: t[0]))

        zeros = tuple(jnp.zeros((ROW_CHUNK, LANES), F32) for _ in range(len(preds) * len(chunks)))
        parts = lax.fori_loop(0, n_kb, body, zeros)
        outs = []
        for pi in range(len(preds)):
            part = jnp.concatenate(parts[pi * len(chunks):(pi + 1) * len(chunks)], axis=0)
            outs.append(jnp.broadcast_to(jnp.sum(part, axis=1, keepdims=True), (tq, LANES)))
        return outs[0] if len(outs) == 1 else outs

    def wide(x):
        return jnp.concatenate([x] * n_lane_tiles, axis=1)

    def bcast(x, rows):
        return wide(x[rows])

    def count16_ge(cand):
        c16 = cand.astype(I16)
        cs = [wide(c16[rows]) for rows in chunks]

        def body(kb, parts):
            return tuple(p + _lane_fold(jnp.where(h16_scr[kb, rows, :] >= c, jnp.int16(1), jnp.int16(0)), jnp.add)
                         for p, c, rows in zip(parts, cs, chunks))

        parts = lax.fori_loop(0, n_kb, body, tuple(jnp.zeros((ROW_CHUNK, LANES), I16) for _ in chunks))
        part = jnp.concatenate(parts, axis=0).astype(F32)
        return jnp.broadcast_to(jnp.sum(part, axis=1, keepdims=True), (tq, LANES))

    def search16():
        u = jnp.zeros((tq, LANES), I32)
        for bit in range(15, -1, -1):
            cand = u | (1 << bit)
            u = jnp.where(count16_ge(cand - 32768) >= float(n_sel), cand, u)
        return u

    hi_p = search16() - 32768
    hi_w = wide(hi_p)

    def low_keys(kb, carry):
        key = key_scr[kb]
        hi = key >> 16
        lo = (key & 0xFFFF) - 32768
        h16_scr[kb] = jnp.where(hi > hi_w, 32767, jnp.where(hi < hi_w, -32768, lo)).astype(I16)
        return carry

    lax.fori_loop(0, n_kb, low_keys, 0)
    thr = hi_p * 65536 + search16()

    cnt_gt, cnt_ge = count(lambda rows: (lambda kb, key, c=bcast(thr, rows): key > c),
                           lambda rows: (lambda kb, key, c=bcast(thr, rows): key >= c))
    need = float(n_sel) - cnt_gt
    xb_scr[...] = jnp.full(xb_scr.shape, 2 ** 31 - 1, I32)

    @pl.when(jnp.max(jnp.abs(cnt_ge - float(n_sel))) > 0.0)
    def _():
        def body(i, x):
            cand = x | lax.shift_left(jnp.int32(1), jnp.int32(col_bits - 1) - i)

            def make_pred(rows):
                c_thr, c_cand = bcast(thr, rows), bcast(cand, rows)
                c_col = lax.broadcasted_iota(I32, (ROW_CHUNK, tk), 1)
                return lambda kb, key: (key == c_thr) & (kb * tk + c_col < c_cand)

            return jnp.where(count(make_pred) < need, cand, x)

        xb_scr[...] = lax.fori_loop(0, col_bits, body, jnp.zeros((tq, LANES), I32))

    xb_w, thr_w = wide(xb_scr[...]), wide(thr)

    def bias_block(kb, carry):
        key = key_scr[kb]
        col = kb * tk + col0
        sel = ((key > thr_w) | ((key == thr_w) & (col <= xb_w))) & (col <= row)
        key_scr[kb] = pltpu.bitcast(jnp.where(sel, 0.0, -jnp.inf), I32)
        return carry

    lax.fori_loop(0, n_kb, bias_block, 0)

    m_scr[...] = jnp.full(m_scr.shape, MASK_VALUE, F32)
    acc_scr[...] = jnp.zeros(acc_scr.shape, F32)

    def att_block(kb, carry):
        start = pl.multiple_of(kb * tk, tk)
        bias = pltpu.bitcast(key_scr[kb], F32)
        for g in range(N_KV_HEADS):
            kg = kb_scr[pl.ds(start, tk), g * HEAD_DIM:(g + 1) * HEAD_DIM]
            va = va_scr[pl.ds(start, tk), 2 * g * HEAD_DIM:(2 * g + 2) * HEAD_DIM]
            base = g * KV_GROUP * tq
            s_all = _dot_nt(qs_scr[base:base + KV_GROUP * tq, :], kg)
            ps, alphas = [], []
            for c in range(KV_GROUP * tq // ATT_ROWS):
                lo = c * ATT_ROWS
                rows = slice(base + lo, base + lo + ATT_ROWS)
                s = s_all[lo:lo + ATT_ROWS] + bias[lo % tq:lo % tq + ATT_ROWS]
                m_old = m_scr[rows, :]
                m_new = jnp.maximum(m_old, jnp.max(_lane_fold(s, jnp.maximum), axis=1, keepdims=True))
                ps.append(jnp.exp2(s - jnp.concatenate([m_new] * n_lane_tiles, axis=1)).astype(BF16))
                alphas.append(jnp.exp2(m_old - m_new))
                m_scr[rows, :] = m_new
            pv = _dot(jnp.concatenate(ps, axis=0), va)
            for c, a in enumerate(alphas):
                lo = c * ATT_ROWS
                rows = slice(base + lo, base + lo + ATT_ROWS)
                acc_scr[rows, :] = jnp.concatenate([a, a], axis=1) * acc_scr[rows, :] + pv[lo:lo + ATT_ROWS]
        return carry

    lax.fori_loop(0, n_kb, att_block, 0)
    for h in range(N_ATT_HEADS):
        acc = acc_scr[h * tq:(h + 1) * tq, :]
        o_ref[:, h * HEAD_DIM:(h + 1) * HEAD_DIM] = (acc[:, :HEAD_DIM] / acc[:, HEAD_DIM:]).astype(o_ref.dtype)


def _attn_prompt(proj, b, t):
    tq = _tile(t, 512)
    tk = min(tq, KEY_BLOCK)
    n_sel = min(TOPK_MAX, t // 4)
    proj3 = proj.reshape(b, t, PACK_WIDTH)
    kern = functools.partial(_attn_prompt_kernel, tq=tq, n_sel=n_sel, col_bits=t.bit_length())
    once = dict(pipeline_mode=pl.Buffered(1))
    out = pl.pallas_call(
        kern,
        out_shape=jax.ShapeDtypeStruct((b, t, ATT_WIDTH), BF16),
        grid=(b, t // tq),
        in_specs=[pl.BlockSpec((None, tq, ATT_WIDTH), lambda bi, qi: (bi, qi, OFF_Q // ATT_WIDTH)),
                  pl.BlockSpec((None, tq, IQ_BLOCK), lambda bi, qi: (bi, qi, OFF_IQ // IQ_BLOCK)),
                  pl.BlockSpec((None, tq, IQ_BLOCK), lambda bi, qi: (bi, qi, OFF_IQ // IQ_BLOCK + 1)),
                  pl.BlockSpec((None, tq, LANES), lambda bi, qi: (bi, qi, OFF_IKW // LANES)),
                  pl.BlockSpec((None, t, KV_WIDTH), lambda bi, qi: (bi, 0, OFF_K // KV_WIDTH), **once),
                  pl.BlockSpec((None, t, KV_WIDTH), lambda bi, qi: (bi, 0, OFF_V // KV_WIDTH), **once),
                  pl.BlockSpec((None, t, LANES), lambda bi, qi: (bi, 0, OFF_IKW // LANES), **once)],
        out_specs=pl.BlockSpec((None, tq, ATT_WIDTH), lambda bi, qi: (bi, qi, 0)),
        scratch_shapes=[pltpu.VMEM((t, KV_WIDTH), BF16),
                        pltpu.VMEM((t, 2 * KV_WIDTH), BF16),
                        pltpu.VMEM((t, IDX_DIM), BF16),
                        pltpu.VMEM((N_IDX_HEADS * tq, IDX_DIM), BF16),
                        pltpu.VMEM((N_ATT_HEADS * tq, HEAD_DIM), BF16),
                        pltpu.VMEM((t // tk, tq, tk), I32),
                        pltpu.VMEM((t // tk, tq, tk), I16),
                        pltpu.VMEM((N_ATT_HEADS * tq, LANES), F32),
                        pltpu.VMEM((N_ATT_HEADS * tq, 2 * HEAD_DIM), F32),
                        pltpu.VMEM((tq, LANES), I32)],
        compiler_params=_params(("parallel", "arbitrary")),
        name="attn_prompt",
    )(proj3, proj3, proj3, proj3, proj3, proj3, proj3)
    return out.reshape(b * t, ATT_WIDTH)


def _hgrn_gates(hq, hf, lb):
    q = hq * jax.nn.sigmoid(hq)
    f = lb + (1.0 - lb) * jax.nn.sigmoid(hf)
    kk = (1.0 - lb) * jax.nn.sigmoid(-hf)
    return q, f, kk


def _hgrn_finish(o, hg, ng):
    o = o * lax.rsqrt(jnp.mean(o * o, axis=-1, keepdims=True) + LN_EPS)
    return o * ng * (hg * jax.nn.sigmoid(hg))


HGRN_HEADS_PER_STEP = 8


def _hgrn_chunk(heads, tri, ones):
    c_len, sub = HGRN_CHUNK, HGRN_SUB
    n_sub = c_len // sub
    t_idx = lax.broadcasted_iota(I32, (sub, 1), 0)
    bs = []
    for q, f, kk, v, st in heads:
        l1, l2, l3 = _split3(jnp.log(f))
        bs.append(_dot(tri, l1) + _dot(tri, l2) + _dot(tri, l3))
    stage2 = []
    for (q, f, kk, v, st), b in zip(heads, bs):
        v_b = v.astype(BF16)
        inter = _dot_nt((q * jnp.exp(b)).astype(BF16), st.astype(BF16))
        rs, cross = [], []
        for j in range(n_sub):
            lo = j * sub
            qt, bt = q[lo:lo + sub], b[lo:lo + sub]
            ps = []
            for s in range(sub):
                valid = t_idx >= s
                dec = jnp.exp(jnp.where(valid, bt - b[lo + s:lo + s + 1, :], 0.0))
                ps.append(jnp.where(valid, qt * kk[lo + s:lo + s + 1, :] * dec, 0.0))
            rs.append(_dot(jnp.concatenate(ps, axis=0).astype(BF16), ones))
            if lo:
                b_edge = b[lo - 1:lo, :]
                qd = (qt * jnp.exp(bt - b_edge)).astype(BF16)
                kd = (kk[:lo] * jnp.exp(b_edge - b[:lo])).astype(BF16)
                cross.append(_dot_nt(qd, kd))
        b_last = b[c_len - 1:c_len, :]
        kd = (kk * jnp.exp(b_last - b)).astype(BF16)
        upd = lax.dot_general(v_b, kd, (((0,), (0,)), ((), ())), preferred_element_type=F32)
        stage2.append((v_b, inter, rs, cross, st * jnp.exp(b_last) + upd))
    out = []
    for (q, f, kk, v, st), (v_b, inter, rs, cross, st_new) in zip(heads, stage2):
        rows = []
        for j in range(n_sub):
            lo = j * sub
            acc = inter[lo:lo + sub]
            if lo:
                acc = acc + _dot(cross[j - 1].astype(BF16), v_b[:lo])
            for s in range(sub):
                acc = acc + rs[j][s * sub:(s + 1) * sub, :] * v[lo + s:lo + s + 1, :]
            rows.append(acc)
        out.append((jnp.concatenate(rows, axis=0), st_new))
    return out


def _hgrn_prompt_kernel(hq_ref, hf_ref, hi_ref, hg_ref, lb_ref, ng_ref, o_ref, s_ref, st_scr, *, n_chunks):
    c_len = HGRN_CHUNK
    ti = pl.program_id(2)

    @pl.when(ti == 0)
    def _():
        st_scr[...] = jnp.zeros(st_scr.shape, F32)

    r_i = lax.broadcasted_iota(I32, (c_len, c_len), 0)
    c_i = lax.broadcasted_iota(I32, (c_len, c_len), 1)
    tri = jnp.where(c_i <= r_i, 1.0, 0.0).astype(BF16)
    ones = jnp.ones((HGRN_DK, HGRN_DV), BF16)

    def chunk(c, carry):
        sl = pl.ds(pl.multiple_of(c * c_len, c_len), c_len)
        cols = [slice(hh * HGRN_DK, (hh + 1) * HGRN_DK) for hh in range(HGRN_HEADS_PER_STEP)]
        heads = []
        for hh, cs in enumerate(cols):
            q, f, kk = _hgrn_gates(hq_ref[sl, cs], hf_ref[sl, cs], lb_ref[:, cs])
            heads.append((q, f, kk, hi_ref[sl, cs], st_scr[hh]))
        for hh, (o, st_new) in enumerate(_hgrn_chunk(heads, tri, ones)):
            st_scr[hh] = st_new
            o_ref[sl, cols[hh]] = _hgrn_finish(o, hg_ref[sl, cols[hh]], ng_ref[:, cols[hh]]).astype(o_ref.dtype)
        return carry

    lax.fori_loop(0, n_chunks, chunk, 0)

    @pl.when(ti == pl.num_programs(2) - 1)
    def _():
        for hh in range(HGRN_HEADS_PER_STEP):
            s_ref[hh] = st_scr[hh].T


def _hgrn_prompt(proj, lb, ng, b, t):
    tc = _tile(t, 512)
    n_chunks = tc // HGRN_CHUNK
    hp = HGRN_HEADS_PER_STEP
    width = hp * HGRN_DK
    proj3 = proj.reshape(b, t, PACK_WIDTH)

    def col(off):
        return pl.BlockSpec((None, tc, width), lambda bi, h, ti: (bi, ti, off // width + h))

    vec = pl.BlockSpec((None, 1, width), lambda bi, h, ti: (h, 0, 0))
    o, s = pl.pallas_call(
        functools.partial(_hgrn_prompt_kernel, n_chunks=n_chunks),
        out_shape=(jax.ShapeDtypeStruct((b, t, HGRN_WIDTH), BF16),
                   jax.ShapeDtypeStruct((b, N_HGRN_HEADS, HGRN_DK, HGRN_DV), F32)),
        grid=(b, N_HGRN_HEADS // hp, t // tc),
        in_specs=[col(OFF_HQ), col(OFF_HF), col(OFF_HI), col(OFF_HG), vec, vec],
        out_specs=(pl.BlockSpec((None, tc, width), lambda bi, h, ti: (bi, ti, h)),
                   pl.BlockSpec((None, hp, HGRN_DK, HGRN_DV), lambda bi, h, ti: (bi, h, 0, 0))),
        scratch_shapes=[pltpu.VMEM((hp, HGRN_DV, HGRN_DK), F32)],
        compiler_params=_params(("parallel", "parallel", "arbitrary")),
        name="hgrn_prompt",
    )(proj3, proj3, proj3, proj3,
      lb.reshape(N_HGRN_HEADS // hp, 1, width), ng.reshape(N_HGRN_HEADS // hp, 1, width))
    return o.reshape(b * t, HGRN_WIDTH), s


def _merge_kernel(oa_ref, ob_ref, ga_ref, gb_ref, wa_ref, wb_ref, o_ref):
    ya = _dot(oa_ref[...].astype(BF16), wa_ref[...])
    yb = _dot(ob_ref[...].astype(BF16), wb_ref[...])
    o_ref[...] = (jax.nn.sigmoid(ga_ref[...]) * ya + jax.nn.sigmoid(gb_ref[...]) * yb).astype(o_ref.dtype)


def _merge(o_a, o_b, proj, w_up_a, w_up_b, l, tm):
    n = o_a.shape[0]
    return pl.pallas_call(
        _merge_kernel,
        out_shape=jax.ShapeDtypeStruct((n, D_MODEL), BF16),
        grid=(n // tm,),
        in_specs=[pl.BlockSpec((tm, ATT_WIDTH), lambda i: (i, 0)),
                  pl.BlockSpec((tm, HGRN_WIDTH), lambda i: (i, 0)),
                  pl.BlockSpec((tm, D_MODEL), lambda i: (i, OFF_GA // D_MODEL)),
                  pl.BlockSpec((tm, D_MODEL), lambda i: (i, OFF_GB // D_MODEL)),
                  pl.BlockSpec((None, ATT_WIDTH, D_MODEL), lambda i: (l, 0, 0)),
                  pl.BlockSpec((None, HGRN_WIDTH, D_MODEL), lambda i: (l, 0, 0))],
        out_specs=pl.BlockSpec((tm, D_MODEL), lambda i: (i, 0)),
        compiler_params=_params(("parallel",)),
        name="merge",
    )(o_a, o_b, proj, proj, w_up_a, w_up_b)


def _post_norm(x, gate, y, g, b):
    return _ln(ALPHA * x + gate * y) * g + b


def _out_kernel(m_ref, w_ref, x_ref, gate_ref, lng_ref, lnb_ref, o_ref):
    y = _dot(m_ref[...], w_ref[...])
    o_ref[...] = _post_norm(x_ref[...], gate_ref[0], y, lng_ref[...], lnb_ref[...])


def _out_proj(merged, w_o, x, gate, ln_g, ln_b, l, tm):
    n = x.shape[0]
    tiles_per_mod = (n // tm) // gate.shape[0]
    vec = pl.BlockSpec((None, 1, D_MODEL), lambda i: (l, 0, 0))
    return pl.pallas_call(
        _out_kernel,
        out_shape=jax.ShapeDtypeStruct((n, D_MODEL), F32),
        grid=(n // tm,),
        in_specs=[pl.BlockSpec((tm, D_MODEL), lambda i: (i, 0)),
                  pl.BlockSpec((None, D_MODEL, D_MODEL), lambda i: (l, 0, 0)),
                  pl.BlockSpec((tm, D_MODEL), lambda i: (i, 0)),
                  pl.BlockSpec((1, gate.shape[1], D_MODEL), lambda i: (i // tiles_per_mod, 0, 0)),
                  vec, vec],
        out_specs=pl.BlockSpec((tm, D_MODEL), lambda i: (i, 0)),
        compiler_params=_params(("parallel",)),
        name="out_proj",
    )(merged, w_o, x, gate, ln_g.reshape(DEPTH, 1, D_MODEL), ln_b.reshape(DEPTH, 1, D_MODEL))


def _route(h, wr, br):
    h1, h2, _ = _split3(h)
    w1, w2, _ = _split3(wr)
    logits = _dot_nt(w1, h1) + _dot_nt(w1, h2) + _dot_nt(w2, h1)
    aff = jax.nn.sigmoid(logits)
    sel = aff + br
    rows = [sel[e:e + 1, :] for e in range(N_EXPERTS)]
    grp = []
    for g in range(N_GROUPS):
        a, b, c, d = rows[g * EXPERTS_PER_GROUP:(g + 1) * EXPERTS_PER_GROUP]
        hi1, lo1 = jnp.maximum(a, b), jnp.minimum(a, b)
        hi2, lo2 = jnp.maximum(c, d), jnp.minimum(c, d)
        grp.append(jnp.maximum(hi1, hi2) + jnp.maximum(jnp.minimum(hi1, hi2), jnp.maximum(lo1, lo2)))
    best = functools.reduce(jnp.maximum, grp)
    taken = jnp.zeros_like(best)
    picked = []
    for g in range(N_GROUPS):
        is_g = jnp.where(grp[g] == best, 1.0, 0.0) * (1.0 - taken)
        taken = taken + is_g
        for e in range(g * EXPERTS_PER_GROUP, (g + 1) * EXPERTS_PER_GROUP):
            rank = jnp.zeros_like(best)
            for o in range(g * EXPERTS_PER_GROUP, (g + 1) * EXPERTS_PER_GROUP):
                if o < e:
                    rank = rank + jnp.where(rows[o] >= rows[e], 1.0, 0.0)
                elif o > e:
                    rank = rank + jnp.where(rows[o] > rows[e], 1.0, 0.0)
            picked.append(is_g * jnp.where(rank < 2.0, 1.0, 0.0))
    picked = jnp.concatenate(picked, axis=0)
    gate = picked * aff
    return picked, gate / jnp.sum(gate, axis=0, keepdims=True)


def _router_kernel(x_ref, sh_ref, sc_ref, wr_ref, br_ref, h_ref, g_ref):
    h = _ln(x_ref[...]) * (1.0 + sc_ref[0]) + sh_ref[0]
    h_ref[...] = h.astype(h_ref.dtype)
    _, g_ref[...] = _route(h, wr_ref[...], br_ref[...])


def _router_sorted_kernel(x_ref, sh_ref, sc_ref, wr_ref, br_ref, h_ref, eid_ref, gw_ref, pos_ref, cnt_ref, base_scr):
    tm = x_ref.shape[0]

    @pl.when(pl.program_id(0) == 0)
    def _():
        base_scr[...] = jnp.zeros(base_scr.shape, F32)

    h = _ln(x_ref[...]) * (1.0 + sc_ref[0]) + sh_ref[0]
    h_ref[...] = h.astype(h_ref.dtype)
    picked, gate = _route(h, wr_ref[...], br_ref[...])
    r = lax.broadcasted_iota(I32, (tm, tm), 0)
    c = lax.broadcasted_iota(I32, (tm, tm), 1)
    earlier = jnp.where(r < c, 1.0, 0.0).astype(BF16)
    base = base_scr[...]
    rank = _dot(picked.astype(BF16), earlier) + jnp.concatenate([base] * (tm // LANES), axis=1)
    base_scr[...] = base + jnp.sum(picked, axis=1, keepdims=True)
    cnt_ref[...] = base_scr[...]
    eio = lax.broadcasted_iota(I32, picked.shape, 0).astype(F32)
    e_lo = jnp.min(jnp.where(picked > 0.0, eio, float(N_EXPERTS)), axis=0, keepdims=True)
    e_hi = jnp.max(jnp.where(picked > 0.0, eio, -1.0), axis=0, keepdims=True)

    def pick(e, x):
        return jnp.sum(jnp.where(eio == e, x, 0.0), axis=0, keepdims=True)

    eid_ref[...] = jnp.concatenate([e_lo, e_hi], axis=0).astype(I32)
    gw_ref[...] = jnp.concatenate([pick(e_lo, gate), pick(e_hi, gate)], axis=0)
    pos_ref[...] = jnp.concatenate([pick(e_lo, rank), pick(e_hi, rank)], axis=0).astype(I32)


def _router_sorted(x, shift, scale, w_router_t, b_router, tm):
    n = x.shape[0]
    tiles_per_mod = (n // tm) // shift.shape[0]
    mod_spec = pl.BlockSpec((1, shift.shape[1], D_MODEL), lambda i: (i // tiles_per_mod, 0, 0))
    pair = pl.BlockSpec((2, tm), lambda i: (0, i))
    return pl.pallas_call(
        _router_sorted_kernel,
        out_shape=(jax.ShapeDtypeStruct((n, D_MODEL), F32),
                   jax.ShapeDtypeStruct((2, n), I32),
                   jax.ShapeDtypeStruct((2, n), F32),
                   jax.ShapeDtypeStruct((2, n), I32),
                   jax.ShapeDtypeStruct((N_EXPERTS, LANES), F32)),
        grid=(n // tm,),
        in_specs=[pl.BlockSpec((tm, D_MODEL), lambda i: (i, 0)),
                  mod_spec, mod_spec,
                  pl.BlockSpec((N_EXPERTS, D_MODEL), lambda i: (0, 0)),
                  pl.BlockSpec((N_EXPERTS, 1), lambda i: (0, 0))],
        out_specs=(pl.BlockSpec((tm, D_MODEL), lambda i: (i, 0)), pair, pair, pair,
                   pl.BlockSpec((N_EXPERTS, LANES), lambda i: (0, 0))),
        scratch_shapes=[pltpu.VMEM((N_EXPERTS, LANES), F32)],
        compiler_params=_params(("arbitrary",)),
        name="router_sorted",
    )(x, shift, scale, w_router_t, b_router.reshape(N_EXPERTS, 1))


MOE_TILE = 256
DMA_UNROLL = 8


def _dispatch_kernel(dst_ref, ends_ref, h_ref, xs_ref, zero_scr, sem, zsem, *, td, n):
    first = pl.program_id(0) * td

    @pl.when(pl.program_id(0) == 0)
    def _():
        zero_scr[...] = jnp.zeros(zero_scr.shape, F32)
        n_slots = xs_ref.shape[0]

        def zero_copy(start):
            return pltpu.make_async_copy(zero_scr, xs_ref.at[pl.ds(pl.multiple_of(start, MOE_TILE), MOE_TILE)], zsem)

        jobs = []
        for e in range(N_EXPERTS):
            jobs.append((ends_ref[e + 1] > ends_ref[e], ends_ref[e + 1] - MOE_TILE))
            unused = ends_ref[N_EXPERTS] + e * MOE_TILE
            jobs.append((unused < n_slots, jnp.minimum(unused, n_slots - MOE_TILE)))
        for wanted, start in jobs:
            @pl.when(wanted)
            def _(start=start):
                zero_copy(start).start()
        for wanted, start in jobs:
            @pl.when(wanted)
            def _(start=start):
                zero_copy(start).wait()

    def copies(r):
        return [pltpu.make_async_copy(h_ref.at[pl.ds(r, 1)], xs_ref.at[pl.ds(dst_ref[k * n + first + r], 1)], sem)
                for k in range(2)]

    def start(r, carry):
        for cp in copies(r):
            cp.start()
        return carry

    def wait(r, carry):
        for cp in copies(r):
            cp.wait()
        return carry

    lax.fori_loop(0, td, start, 0, unroll=DMA_UNROLL)
    lax.fori_loop(0, td, wait, 0, unroll=DMA_UNROLL)


def _dispatch(h, dst_flat, seg_ends, n_slots, td):
    n = h.shape[0]
    grid_spec = pltpu.PrefetchScalarGridSpec(
        num_scalar_prefetch=2,
        grid=(n // td,),
        in_specs=[pl.BlockSpec((td, D_MODEL), lambda i, dst, ends: (i, 0))],
        out_specs=pl.BlockSpec(memory_space=pl.ANY),
        scratch_shapes=[pltpu.VMEM((MOE_TILE, D_MODEL), F32),
                        pltpu.SemaphoreType.DMA,
                        pltpu.SemaphoreType.DMA])
    return pl.pallas_call(
        functools.partial(_dispatch_kernel, td=td, n=n),
        out_shape=jax.ShapeDtypeStruct((n_slots, D_MODEL), F32),
        grid_spec=grid_spec,
        compiler_params=_params(("arbitrary",)),
        name="moe_dispatch",
    )(dst_flat, seg_ends, h)


def _experts_kernel(te_ref, tv_ref, xs_ref, w1_ref, w3_ref, w2_ref, ys_ref):
    del te_ref
    valid = tv_ref[pl.program_id(0)] > 0

    @pl.when(valid)
    def _():
        x = xs_ref[...].astype(BF16)
        a = _dot(x, w1_ref[...])
        b = _dot(x, w3_ref[...])
        ys_ref[...] = _dot((a * jax.nn.sigmoid(a) * b).astype(BF16), w2_ref[...])

    @pl.when(jnp.logical_not(valid))
    def _():
        ys_ref[...] = jnp.zeros(ys_ref.shape, F32)


def _experts(xs, tile_expert, tile_valid, w1, w3, w2, l):
    n_tiles = xs.shape[0] // MOE_TILE
    grid_spec = pltpu.PrefetchScalarGridSpec(
        num_scalar_prefetch=2,
        grid=(n_tiles,),
        in_specs=[pl.BlockSpec((MOE_TILE, D_MODEL), lambda i, te, tv: (jnp.minimum(i, tv[0] - 1), 0)),
                  pl.BlockSpec((None, None, D_MODEL, EXPERT_DFF), lambda i, te, tv: (l, te[i], 0, 0)),
                  pl.BlockSpec((None, None, D_MODEL, EXPERT_DFF), lambda i, te, tv: (l, te[i], 0, 0)),
                  pl.BlockSpec((None, None, EXPERT_DFF, D_MODEL), lambda i, te, tv: (l, te[i], 0, 0))],
        out_specs=pl.BlockSpec((MOE_TILE, D_MODEL), lambda i, te, tv: (i, 0)))
    return pl.pallas_call(
        _experts_kernel,
        out_shape=jax.ShapeDtypeStruct(xs.shape, F32),
        grid_spec=grid_spec,
        compiler_params=_params(("arbitrary",)),
        name="moe_experts",
    )(tile_expert, tile_valid, xs, w1, w3, w2)


def _combine_kernel(dst_ref, ys_ref, gw_ref, x_ref, gate_ref, lng_ref, lnb_ref, o_ref, buf, sem, *, tc, n):
    i = pl.program_id(0)
    slot = i % 2

    def copies(tile, s, r):
        return [pltpu.make_async_copy(ys_ref.at[pl.ds(dst_ref[k * n + tile * tc + r], 1)],
                                      buf.at[s, k, pl.ds(r, 1)], sem.at[s]) for k in range(2)]

    def start_tile(tile, s):
        def body(r, carry):
            for cp in copies(tile, s, r):
                cp.start()
            return carry
        lax.fori_loop(0, tc, body, 0, unroll=DMA_UNROLL)

    def wait_tile(tile, s):
        def body(r, carry):
            for cp in copies(tile, s, r):
                cp.wait()
            return carry
        lax.fori_loop(0, tc, body, 0, unroll=DMA_UNROLL)

    @pl.when(i == 0)
    def _():
        start_tile(0, 0)

    @pl.when(i + 1 < pl.num_programs(0))
    def _():
        start_tile(i + 1, 1 - slot)

    wait_tile(i, slot)
    gw = gw_ref[...]
    y = gw[:, 0:1] * buf[slot, 0] + gw[:, 1:2] * buf[slot, 1]
    o_ref[...] = _post_norm(x_ref[...], gate_ref[0], y, lng_ref[...], lnb_ref[...])


def _combine(ys, dst_flat, gw, x, gate, ln_g, ln_b, l, tc):
    n = x.shape[0]
    tiles_per_mod = (n // tc) // gate.shape[0]
    vec = pl.BlockSpec((None, 1, D_MODEL), lambda i, dst: (l, 0, 0))
    grid_spec = pltpu.PrefetchScalarGridSpec(
        num_scalar_prefetch=1,
        grid=(n // tc,),
        in_specs=[pl.BlockSpec(memory_space=pl.ANY),
                  pl.BlockSpec((tc, 2), lambda i, dst: (i, 0)),
                  pl.BlockSpec((tc, D_MODEL), lambda i, dst: (i, 0)),
                  pl.BlockSpec((1, gate.shape[1], D_MODEL), lambda i, dst: (i // tiles_per_mod, 0, 0)),
                  vec, vec],
        out_specs=pl.BlockSpec((tc, D_MODEL), lambda i, dst: (i, 0)),
        scratch_shapes=[pltpu.VMEM((2, 2, tc, D_MODEL), F32),
                        pltpu.SemaphoreType.DMA((2,))])
    return pl.pallas_call(
        functools.partial(_combine_kernel, tc=tc, n=n),
        out_shape=jax.ShapeDtypeStruct((n, D_MODEL), F32),
        grid_spec=grid_spec,
        compiler_params=_params(("arbitrary",)),
        name="moe_combine",
    )(dst_flat, ys, gw, x, gate, ln_g.reshape(DEPTH, 1, D_MODEL), ln_b.reshape(DEPTH, 1, D_MODEL))


def _moe_grouped(x, shift, scale, gate, w_router_t, b_router, w1, w3, w2, ln_g, ln_b, l, tm):
    n = x.shape[0]
    h, eid, gw, pos, cnt = _router_sorted(x, shift, scale, w_router_t, b_router, tm)
    counts = cnt[:, 0].astype(I32)
    padded = (counts + MOE_TILE - 1) // MOE_TILE * MOE_TILE
    ends = jnp.cumsum(padded)
    experts = jnp.arange(N_EXPERTS, dtype=I32)[:, None, None]
    seg_start = jnp.sum(jnp.where(eid[None] == experts, (ends - padded)[:, None, None], 0), axis=0)
    dst_flat = (seg_start + pos).reshape(2 * n)
    n_tiles = 2 * n // MOE_TILE + N_EXPERTS
    tile_start = jnp.arange(n_tiles, dtype=I32) * MOE_TILE
    tile_expert = jnp.minimum(jnp.sum((tile_start[:, None] >= ends[None, :]).astype(I32), axis=1), N_EXPERTS - 1)
    tile_valid = jnp.where(tile_start < ends[-1], ends[-1] // MOE_TILE, 0).astype(I32)
    seg_ends = jnp.concatenate([jnp.zeros((1,), I32), ends.astype(I32)])
    xs = _dispatch(h, dst_flat, seg_ends, n_tiles * MOE_TILE, tm)
    ys = _experts(xs, tile_expert, tile_valid, w1, w3, w2, l)
    return _combine(ys, dst_flat, gw.T, x, gate, ln_g, ln_b, l, MOE_TILE)


def _router(x, shift, scale, w_router_t, b_router, tm):
    n = x.shape[0]
    tiles_per_mod = (n // tm) // shift.shape[0]
    mod_spec = pl.BlockSpec((1, shift.shape[1], D_MODEL), lambda i: (i // tiles_per_mod, 0, 0))
    return pl.pallas_call(
        _router_kernel,
        out_shape=(jax.ShapeDtypeStruct((n, D_MODEL), BF16),
                   jax.ShapeDtypeStruct((N_EXPERTS, n), F32)),
        grid=(n // tm,),
        in_specs=[pl.BlockSpec((tm, D_MODEL), lambda i: (i, 0)),
                  mod_spec, mod_spec,
                  pl.BlockSpec((N_EXPERTS, D_MODEL), lambda i: (0, 0)),
                  pl.BlockSpec((N_EXPERTS, 1), lambda i: (0, 0))],
        out_specs=(pl.BlockSpec((tm, D_MODEL), lambda i: (i, 0)),
                   pl.BlockSpec((N_EXPERTS, tm), lambda i: (0, i))),
        compiler_params=_params(("parallel",)),
        name="router",
    )(x, shift, scale, w_router_t, b_router.reshape(N_EXPERTS, 1))


def _moe_kernel(h_ref, g_ref, w1_ref, w3_ref, w2_ref, x_ref, gate_ref, lng_ref, lnb_ref, o_ref, acc_scr):
    e = pl.program_id(1)

    @pl.when(e == 0)
    def _():
        acc_scr[...] = jnp.zeros(acc_scr.shape, F32)

    h = h_ref[...]
    a = _dot(h, w1_ref[...])
    b = _dot(h, w3_ref[...])
    gates = g_ref[...]
    lane = lax.broadcasted_iota(I32, gates.shape, 1)
    gcol = jnp.sum(jnp.where(lane == e, gates, 0.0), axis=1, keepdims=True)
    u = (a * jax.nn.sigmoid(a) * b * gcol).astype(BF16)
    acc_scr[...] += _dot(u, w2_ref[...])

    @pl.when(e == pl.num_programs(1) - 1)
    def _():
        o_ref[...] = _post_norm(x_ref[...], gate_ref[0], acc_scr[...], lng_ref[...], lnb_ref[...])


def _moe(h, gates, w1, w3, w2, x, gate, ln_g, ln_b, l, tm):
    n = x.shape[0]
    tiles_per_mod = (n // tm) // gate.shape[0]
    vec = pl.BlockSpec((None, 1, D_MODEL), lambda i, e: (l, 0, 0))
    return pl.pallas_call(
        _moe_kernel,
        out_shape=jax.ShapeDtypeStruct((n, D_MODEL), F32),
        grid=(n // tm, N_EXPERTS),
        in_specs=[pl.BlockSpec((tm, D_MODEL), lambda i, e: (i, 0)),
                  pl.BlockSpec((tm, N_EXPERTS), lambda i, e: (i, 0)),
                  pl.BlockSpec((None, None, D_MODEL, EXPERT_DFF), lambda i, e: (l, e, 0, 0)),
                  pl.BlockSpec((None, None, D_MODEL, EXPERT_DFF), lambda i, e: (l, e, 0, 0)),
                  pl.BlockSpec((None, None, EXPERT_DFF, D_MODEL), lambda i, e: (l, e, 0, 0)),
                  pl.BlockSpec((tm, D_MODEL), lambda i, e: (i, 0)),
                  pl.BlockSpec((1, gate.shape[1], D_MODEL), lambda i, e: (i // tiles_per_mod, 0, 0)),
                  vec, vec],
        out_specs=pl.BlockSpec((tm, D_MODEL), lambda i, e: (i, 0)),
        scratch_shapes=[pltpu.VMEM((tm, D_MODEL), F32)],
        compiler_params=_params(("parallel", "arbitrary")),
        name="moe",
    )(h, gates, w1, w3, w2, x, gate, ln_g.reshape(DEPTH, 1, D_MODEL), ln_b.reshape(DEPTH, 1, D_MODEL))


def _row_to_col(v):
    n = v.shape[1]
    r = lax.broadcasted_iota(I32, (n, n), 0)
    c = lax.broadcasted_iota(I32, (n, n), 1)
    return jnp.sum(jnp.where(r == c, jnp.broadcast_to(v, (n, n)), 0.0), axis=1, keepdims=True)


def _attn_decode_kernel(pt_ref, proj_ref, ck_ref, cv_ref, cik_ref, o_ref,
                        ik_buf, k_buf, v_buf, key_scr, sem_ik, sem_k, sem_v,
                        *, l, n_pages, page, chunk_pages, n_sel):
    b = pl.program_id(0)
    n_chunks = n_pages // chunk_pages
    chunk = chunk_pages * page

    def ik_copy(j):
        return pltpu.make_async_copy(cik_ref.at[l, pt_ref[b, j]], ik_buf.at[pl.ds(j * page, page)], sem_ik)

    def kv_copies(c, j, slot):
        pg = pt_ref[b, c * chunk_pages + j]
        dst = pl.ds(j * page, page)
        copies = []
        for g in range(N_KV_HEADS):
            copies.append(pltpu.make_async_copy(ck_ref.at[l, pg, :, g, :], k_buf.at[slot, g, dst], sem_k.at[slot]))
            copies.append(pltpu.make_async_copy(cv_ref.at[l, pg, :, g, :], v_buf.at[slot, g, dst], sem_v.at[slot]))
        return copies

    def start_chunk(c, slot):
        def body(j, carry):
            for cp in kv_copies(c, j, slot):
                cp.start()
            return carry
        lax.fori_loop(0, chunk_pages, body, 0)

    def wait_chunk(c, slot):
        def body(j, carry):
            for cp in kv_copies(c, j, slot):
                cp.wait()
            return carry
        lax.fori_loop(0, chunk_pages, body, 0)

    def start_ik(j, carry):
        ik_copy(j).start()
        return carry

    def wait_ik(j, carry):
        ik_copy(j).wait()
        return carry

    lax.fori_loop(0, n_pages, start_ik, 0)
    start_chunk(0, 0)
    lax.fori_loop(0, n_pages, wait_ik, 0)

    rowsl = slice(None)
    iq = proj_ref[rowsl, OFF_IQ:OFF_IQ + N_IDX_HEADS * IDX_DIM]
    iq_h = jnp.concatenate([iq[:, h * IDX_DIM:(h + 1) * IDX_DIM] for h in range(N_IDX_HEADS)], axis=0)
    w_row = proj_ref[rowsl, OFF_IKW + IDX_DIM:OFF_IKW + IDX_DIM + N_IDX_HEADS]
    w_col = _row_to_col(w_row) * (IDX_DIM ** -0.5 * N_IDX_HEADS ** -0.5)
    ik_new = proj_ref[rowsl, OFF_IKW:OFF_IKW + IDX_DIM]

    iq_b = iq_h.astype(BF16)
    for c in range(n_chunks):
        lg = _dot_nt(iq_b, ik_buf[c * chunk:(c + 1) * chunk, :].astype(BF16))
        s = jnp.sum(w_col * jnp.maximum(lg, 0.0), axis=0, keepdims=True)
        key_scr[c] = _sort_key(s)
    lg_new = jnp.sum(iq_b.astype(F32) * ik_new.astype(BF16).astype(F32), axis=1, keepdims=True)
    key_new = _sort_key(jnp.sum(w_col * jnp.maximum(lg_new, 0.0), axis=0, keepdims=True))

    def count_ge(cand):
        cnt = jnp.where(key_new >= cand, 1.0, 0.0)
        for c in range(n_chunks):
            cnt = cnt + jnp.sum(jnp.where(key_scr[c] >= cand, 1.0, 0.0), axis=1, keepdims=True)
        return cnt

    thr = _kth_largest_key(count_ge, float(n_sel), (1, 1))

    q = proj_ref[rowsl, OFF_Q:OFF_Q + ATT_WIDTH]
    q_h = jnp.concatenate([q[:, h * HEAD_DIM:(h + 1) * HEAD_DIM] for h in range(N_ATT_HEADS)], axis=0)
    q_b = q_h.astype(BF16)
    k_new = proj_ref[rowsl, OFF_K:OFF_K + KV_WIDTH]
    v_new = proj_ref[rowsl, OFF_V:OFF_V + KV_WIDTH]
    scale = HEAD_DIM ** -0.5

    def att_chunk(c, carry):
        slot = c % 2

        @pl.when(c + 1 < n_chunks)
        def _():
            start_chunk(c + 1, 1 - slot)

        wait_chunk(c, slot)
        sel = key_scr[c] >= thr
        out = []
        for g in range(N_KV_HEADS):
            m_old, l_old, acc_old = carry[g]
            kg = k_buf[slot, g].astype(BF16)
            vg = v_buf[slot, g].astype(BF16)
            s = _dot_nt(q_b[g * KV_GROUP:(g + 1) * KV_GROUP], kg) * scale
            s = jnp.where(sel, s, MASK_VALUE)
            m_new = jnp.maximum(m_old, jnp.max(s, axis=1, keepdims=True))
            p = jnp.where(sel, jnp.exp(s - m_new), 0.0)
            a = jnp.exp(m_old - m_new)
            out.append((m_new, a * l_old + jnp.sum(p, axis=1, keepdims=True),
                        a * acc_old + _dot(p.astype(BF16), vg)))
        return tuple(out)

    init = tuple((jnp.full((KV_GROUP, 1), MASK_VALUE, F32), jnp.zeros((KV_GROUP, 1), F32),
                  jnp.zeros((KV_GROUP, HEAD_DIM), F32)) for _ in range(N_KV_HEADS))
    res = lax.fori_loop(0, n_chunks, att_chunk, init)

    sel_new = key_new >= thr
    for g in range(N_KV_HEADS):
        m_old, l_old, acc_old = res[g]
        kg = k_new[:, g * HEAD_DIM:(g + 1) * HEAD_DIM].astype(BF16).astype(F32)
        vg = v_new[:, g * HEAD_DIM:(g + 1) * HEAD_DIM].astype(BF16).astype(F32)
        qg = q_b[g * KV_GROUP:(g + 1) * KV_GROUP].astype(F32)
        s = jnp.sum(qg * kg, axis=1, keepdims=True) * scale
        s = jnp.where(sel_new, s, MASK_VALUE)
        m_new = jnp.maximum(m_old, s)
        p = jnp.where(sel_new, jnp.exp(s - m_new), 0.0)
        a = jnp.exp(m_old - m_new)
        l_new = a * l_old + p
        acc = a * acc_old + p.astype(BF16).astype(F32) * vg
        o = acc / l_new
        for r in range(KV_GROUP):
            hh = g * KV_GROUP + r
            o_ref[rowsl, hh * HEAD_DIM:(hh + 1) * HEAD_DIM] = o[r:r + 1, :].astype(o_ref.dtype)


def _attn_decode(proj, page_table, cache_k, cache_v, cache_idx_k, l):
    db, n_pages = page_table.shape
    page = cache_k.shape[2]
    past = n_pages * page
    n_sel = min(TOPK_MAX, (past + 1) // 4)
    chunk_pages = _tile(n_pages, 16)
    kern = functools.partial(_attn_decode_kernel, l=l, n_pages=n_pages, page=page,
                             chunk_pages=chunk_pages, n_sel=n_sel)
    grid_spec = pltpu.PrefetchScalarGridSpec(
        num_scalar_prefetch=1,
        grid=(db,),
        in_specs=[pl.BlockSpec((None, 1, PACK_WIDTH), lambda b, pt: (b, 0, 0)),
                  pl.BlockSpec(memory_space=pl.ANY),
                  pl.BlockSpec(memory_space=pl.ANY),
                  pl.BlockSpec(memory_space=pl.ANY)],
        out_specs=pl.BlockSpec((None, 1, ATT_WIDTH), lambda b, pt: (b, 0, 0)),
        scratch_shapes=[pltpu.VMEM((past, IDX_DIM), F32),
                        pltpu.VMEM((2, N_KV_HEADS, chunk_pages * page, HEAD_DIM), F32),
                        pltpu.VMEM((2, N_KV_HEADS, chunk_pages * page, HEAD_DIM), F32),
                        pltpu.VMEM((n_pages // chunk_pages, 1, chunk_pages * page), I32),
                        pltpu.SemaphoreType.DMA,
                        pltpu.SemaphoreType.DMA((2,)),
                        pltpu.SemaphoreType.DMA((2,))])
    return pl.pallas_call(
        kern,
        out_shape=jax.ShapeDtypeStruct((db, 1, ATT_WIDTH), F32),
        grid_spec=grid_spec,
        compiler_params=_params(("arbitrary",)),
        name="attn_decode",
    )(page_table, proj.reshape(db, 1, PACK_WIDTH), cache_k, cache_v, cache_idx_k).reshape(db, ATT_WIDTH)


def _hgrn_decode_kernel(proj_ref, s0_ref, lb_ref, ng_ref, o_ref, s_ref):
    rowsl = slice(None)
    for h in range(N_HGRN_HEADS):
        cols = lambda off: slice(off + h * HGRN_DK, off + (h + 1) * HGRN_DK)
        lb = lb_ref[:, h * HGRN_DK:(h + 1) * HGRN_DK]
        q, f, kk = _hgrn_gates(proj_ref[rowsl, cols(OFF_HQ)], proj_ref[rowsl, cols(OFF_HF)], lb)
        v = proj_ref[rowsl, cols(OFF_HI)]
        s_new = _row_to_col(f) * s0_ref[h] + _row_to_col(kk) * v
        s_ref[h] = s_new
        o = jnp.sum(_row_to_col(q) * s_new, axis=0, keepdims=True)
        ng = ng_ref[:, h * HGRN_DV:(h + 1) * HGRN_DV]
        o_ref[rowsl, h * HGRN_DV:(h + 1) * HGRN_DV] = _hgrn_finish(
            o, proj_ref[rowsl, cols(OFF_HG)], ng).astype(o_ref.dtype)


def _hgrn_decode(proj, state, lb, ng, l):
    db = proj.shape[0]
    st_spec_in = pl.BlockSpec((None, None, N_HGRN_HEADS, HGRN_DK, HGRN_DV), lambda b: (l, b, 0, 0, 0))
    o, s = pl.pallas_call(
        _hgrn_decode_kernel,
        out_shape=(jax.ShapeDtypeStruct((db, 1, HGRN_WIDTH), F32),
                   jax.ShapeDtypeStruct((db, N_HGRN_HEADS, HGRN_DK, HGRN_DV), F32)),
        grid=(db,),
        in_specs=[pl.BlockSpec((None, 1, PACK_WIDTH), lambda b: (b, 0, 0)),
                  st_spec_in,
                  pl.BlockSpec((1, HGRN_WIDTH), lambda b: (0, 0)),
                  pl.BlockSpec((1, HGRN_WIDTH), lambda b: (0, 0))],
        out_specs=(pl.BlockSpec((None, 1, HGRN_WIDTH), lambda b: (b, 0, 0)),
                   pl.BlockSpec((None, N_HGRN_HEADS, HGRN_DK, HGRN_DV), lambda b: (b, 0, 0, 0))),
        compiler_params=_params(("parallel",)),
        name="hgrn_decode",
    )(proj.reshape(db, 1, PACK_WIDTH), state, lb.reshape(1, HGRN_WIDTH), ng.reshape(1, HGRN_WIDTH))
    return o.reshape(db, HGRN_WIDTH), s


def _split_w_in(w_in):
    return w_in[:, :, TAIL_START:].astype(BF16), w_in[:, :, :HEAD_COLS].astype(BF16)


def _mods(mod_l, rows, per_row):
    m = mod_l[rows]
    parts = jnp.split(m, 6, axis=-1)
    if per_row:
        return [p[None, :, :] for p in parts]
    return [p[:, None, :] for p in parts]


def kernel(x_prompt, x_sample, c_prompt, c_sample, cache_k, cache_v, cache_idx_k, state_hgrn, page_table,
           w_ada, b_ada, w_in, w_up_a, w_up_b, w_o, hgrn_norm_g, hgrn_lb_logits, ln1_g, ln1_b,
           w_router, b_router, w1, w3, w2, ln2_g, ln2_b):
    bp, t, _ = x_prompt.shape
    db = x_sample.shape[0]
    lbp = jax.nn.softmax(hgrn_lb_logits.astype(F32), axis=0)
    lower_bounds = jnp.cumsum(lbp, axis=0) - lbp[0]

    w_tail, w_head = _split_w_in(w_in)
    w_up_a_b, w_up_b_b, w_o_b = w_up_a.astype(BF16), w_up_b.astype(BF16), w_o.astype(BF16)
    w1_b, w3_b, w2_b = w1.astype(BF16), w3.astype(BF16), w2.astype(BF16)
    w_router_t = w_router.T

    n_c = bp + db
    c_rows = -(-n_c // 8) * 8
    c_all = jnp.concatenate([c_prompt, c_sample, jnp.zeros((c_rows - n_c, D_MODEL), F32)], axis=0)
    mod = _ada(c_all, w_ada, b_ada)

    xp = x_prompt.reshape(bp * t, D_MODEL)
    xs = x_sample.reshape(db, D_MODEL)
    tm_p = _tile(bp * t, 512)
    outs_p = {"k": [], "v": [], "ik": [], "s": []}
    outs_s = {"k": [], "v": [], "ik": [], "s": []}
    for l in range(DEPTH):
        sh1, sc1, g1, sh2, sc2, g2 = _mods(mod[l], slice(0, bp), per_row=False)
        proj = _proj(xp, sh1, sc1, w_tail, w_head, l, _tile(t, 1024))
        o_a = _attn_prompt(proj, bp, t)
        o_b, s_new = _hgrn_prompt(proj, lower_bounds[l], hgrn_norm_g[l], bp, t)
        merged = _merge(o_a, o_b, proj, w_up_a_b, w_up_b_b, l, tm_p)
        xp = _out_proj(merged, w_o_b, xp, g1, ln1_g, ln1_b, l, tm_p)
        xp = _moe_grouped(xp, sh2, sc2, g2, w_router_t, b_router, w1_b, w3_b, w2_b, ln2_g, ln2_b, l, tm_p)
        outs_p["k"].append(proj[:, OFF_K:OFF_K + KV_WIDTH].reshape(bp, t, N_KV_HEADS, HEAD_DIM))
        outs_p["v"].append(proj[:, OFF_V:OFF_V + KV_WIDTH].reshape(bp, t, N_KV_HEADS, HEAD_DIM))
        outs_p["ik"].append(proj[:, OFF_IKW:OFF_IKW + IDX_DIM].reshape(bp, t, IDX_DIM))
        outs_p["s"].append(s_new)

        sh1, sc1, g1, sh2, sc2, g2 = _mods(mod[l], slice(bp, bp + db), per_row=True)
        proj = _proj(xs, sh1, sc1, w_tail, w_head, l, db)
        o_a = _attn_decode(proj, page_table, cache_k, cache_v, cache_idx_k, l)
        o_b, s_new = _hgrn_decode(proj, state_hgrn, lower_bounds[l], hgrn_norm_g[l], l)
        merged = _merge(o_a, o_b, proj, w_up_a_b, w_up_b_b, l, db)
        xs = _out_proj(merged, w_o_b, xs, g1, ln1_g, ln1_b, l, db)
        h2, gates = _router(xs, sh2, sc2, w_router_t, b_router, db)
        xs = _moe(h2, gates.T, w1_b, w3_b, w2_b, xs, g2, ln2_g, ln2_b, l, db)
        outs_s["k"].append(proj[:, OFF_K:OFF_K + KV_WIDTH].reshape(db, 1, N_KV_HEADS, HEAD_DIM))
        outs_s["v"].append(proj[:, OFF_V:OFF_V + KV_WIDTH].reshape(db, 1, N_KV_HEADS, HEAD_DIM))
        outs_s["ik"].append(proj[:, OFF_IKW:OFF_IKW + IDX_DIM].reshape(db, 1, IDX_DIM))
        outs_s["s"].append(s_new)

    return (xp.reshape(bp, t, D_MODEL), xs.reshape(db, 1, D_MODEL),
            jnp.stack(outs_p["k"]), jnp.stack(outs_p["v"]), jnp.stack(outs_p["ik"]), jnp.stack(outs_p["s"]),
            jnp.stack(outs_s["k"]), jnp.stack(outs_s["v"]), jnp.stack(outs_s["ik"]), jnp.stack(outs_s["s"]))
```

```python
import functools

import jax
import jax.numpy as jnp
from jax import lax
from jax.experimental import pallas as pl
from jax.experimental.pallas import tpu as pltpu

F32 = jnp.float32
BF16 = jnp.bfloat16
I32 = jnp.int32
I16 = jnp.int16

DEPTH = 2
D_MODEL = 2048
N_ATT_HEADS = 8
N_KV_HEADS = 2
KV_GROUP = N_ATT_HEADS // N_KV_HEADS
HEAD_DIM = 128
ATT_WIDTH = N_ATT_HEADS * HEAD_DIM
KV_WIDTH = N_KV_HEADS * HEAD_DIM
N_IDX_HEADS = 16
IDX_DIM = 64
TOPK_MAX = 256
N_HGRN_HEADS = 8
HGRN_DK = 128
HGRN_DV = 128
HGRN_WIDTH = N_HGRN_HEADS * HGRN_DK
HGRN_CHUNK = 64
HGRN_SUB = 8
N_EXPERTS = 16
N_GROUPS = 4
EXPERTS_PER_GROUP = N_EXPERTS // N_GROUPS
EXPERT_DFF = 1024
ALPHA = (2 * DEPTH) ** 0.25
LN_EPS = 1e-5
MASK_VALUE = -1e30
INT_MIN = -2 ** 31
LOG2_E = 1.4426950408889634

IN_SPLITS = (ATT_WIDTH, KV_WIDTH, KV_WIDTH, N_IDX_HEADS * IDX_DIM, IDX_DIM, N_IDX_HEADS,
             HGRN_WIDTH, HGRN_WIDTH, HGRN_WIDTH, HGRN_WIDTH, D_MODEL, D_MODEL)

LANES = 128
PROJ_TN = 1024
IN_WIDTH = sum(IN_SPLITS)
TAIL_START = ATT_WIDTH + 2 * KV_WIDTH + N_IDX_HEADS * IDX_DIM + IDX_DIM + N_IDX_HEADS
TAIL_COLS = IN_WIDTH - TAIL_START
HEAD_COLS = -(-(TAIL_START + LANES - (TAIL_START % LANES)) // PROJ_TN) * PROJ_TN
OFF_HQ = 0
OFF_HF = OFF_HQ + HGRN_WIDTH
OFF_HI = OFF_HF + HGRN_WIDTH
OFF_HG = OFF_HI + HGRN_WIDTH
OFF_GA = OFF_HG + HGRN_WIDTH
OFF_GB = OFF_GA + D_MODEL
OFF_Q = TAIL_COLS
OFF_K = OFF_Q + ATT_WIDTH
OFF_V = OFF_K + KV_WIDTH
OFF_IQ = OFF_V + KV_WIDTH
OFF_IKW = OFF_IQ + N_IDX_HEADS * IDX_DIM
IQ_BLOCK = 512
PACK_WIDTH = TAIL_COLS + HEAD_COLS

VMEM_LIMIT = 59 * 1024 * 1024


def _params(semantics):
    return pltpu.CompilerParams(dimension_semantics=semantics, vmem_limit_bytes=VMEM_LIMIT)


def _tile(n, pref):
    t = min(n, pref)
    while n % t:
        t //= 2
    return t


def _ln(x):
    mu = jnp.mean(x, axis=-1, keepdims=True)
    xc = x - mu
    var = jnp.mean(xc * xc, axis=-1, keepdims=True)
    return xc * lax.rsqrt(var + LN_EPS)


def _dot(a, b):
    return jnp.dot(a, b, preferred_element_type=F32)


def _dot_nt(a, b):
    return lax.dot_general(a, b, (((1,), (1,)), ((), ())), preferred_element_type=F32)


def _split3(x):
    x1 = x.astype(BF16)
    r1 = x - x1.astype(F32)
    x2 = r1.astype(BF16)
    x3 = (r1 - x2.astype(F32)).astype(BF16)
    return x1, x2, x3


def _sort_key(s):
    bits = pltpu.bitcast(s, I32)
    return bits ^ ((bits >> 31) & 0x7FFFFFFF)


def _kth_largest_key(count_ge, n_sel, shape):
    def body(i, ans):
        bit = lax.shift_left(jnp.int32(1), jnp.int32(31) - i)
        cand = ans | bit
        cnt = count_ge(cand ^ INT_MIN)
        return jnp.where(cnt >= n_sel, cand, ans)
    ans = lax.fori_loop(0, 32, body, jnp.zeros(shape, I32))
    return ans ^ INT_MIN


def _ada_kernel(c_ref, w_ref, b_ref, o_ref):
    c = c_ref[...]
    a = (c * jax.nn.sigmoid(c)).astype(BF16)
    o_ref[...] = _dot(a, w_ref[...].astype(BF16)) + b_ref[...]


def _ada(c_all, w_ada, b_ada):
    rows = c_all.shape[0]
    width = w_ada.shape[-1]
    tn = _tile(width, 1024)
    return pl.pallas_call(
        _ada_kernel,
        out_shape=jax.ShapeDtypeStruct((DEPTH, rows, width), F32),
        grid=(DEPTH, width // tn),
        in_specs=[pl.BlockSpec((rows, D_MODEL), lambda l, j: (0, 0)),
                  pl.BlockSpec((None, D_MODEL, tn), lambda l, j: (l, 0, j)),
                  pl.BlockSpec((None, 1, tn), lambda l, j: (l, 0, j))],
        out_specs=pl.BlockSpec((None, rows, tn), lambda l, j: (l, 0, j)),
        compiler_params=_params(("parallel", "parallel")),
        name="ada",
    )(c_all, w_ada, b_ada.reshape(DEPTH, 1, width))


def _proj_kernel(x_ref, sh_ref, sc_ref, wt_ref, wh_ref, o_ref, h_scr, *, tail_tiles):
    j = pl.program_id(1)

    @pl.when(j == 0)
    def _():
        h_scr[...] = (_ln(x_ref[...]) * (1.0 + sc_ref[0]) + sh_ref[0]).astype(BF16)

    @pl.when(j < tail_tiles)
    def _():
        o_ref[...] = _dot(h_scr[...], wt_ref[...])

    @pl.when(j >= tail_tiles)
    def _():
        o_ref[...] = _dot(h_scr[...], wh_ref[...])


def _proj(x, shift, scale, w_tail, w_head, l, tm):
    n = x.shape[0]
    tiles_per_mod = (n // tm) // shift.shape[0]
    r = shift.shape[1]
    tn = PROJ_TN
    tail_tiles = TAIL_COLS // tn
    mod_spec = pl.BlockSpec((1, r, D_MODEL), lambda i, j: (i // tiles_per_mod, 0, 0))
    return pl.pallas_call(
        functools.partial(_proj_kernel, tail_tiles=tail_tiles),
        out_shape=jax.ShapeDtypeStruct((n, PACK_WIDTH), F32),
        grid=(n // tm, PACK_WIDTH // tn),
        in_specs=[pl.BlockSpec((tm, D_MODEL), lambda i, j: (i, 0)),
                  mod_spec, mod_spec,
                  pl.BlockSpec((None, D_MODEL, tn), lambda i, j: (l, 0, jnp.minimum(j, tail_tiles - 1))),
                  pl.BlockSpec((None, D_MODEL, tn), lambda i, j: (l, 0, jnp.maximum(j - tail_tiles, 0)))],
        out_specs=pl.BlockSpec((tm, tn), lambda i, j: (i, j)),
        scratch_shapes=[pltpu.VMEM((tm, D_MODEL), BF16)],
        compiler_params=_params(("parallel", "arbitrary")),
        name="proj",
    )(x, shift, scale, w_tail, w_head)


KEY_BLOCK = 256
ROW_CHUNK = 64
ATT_ROWS = 128


def _lane_fold(x, op):
    acc = x[:, :LANES]
    for i in range(1, x.shape[1] // LANES):
        acc = op(acc, x[:, i * LANES:(i + 1) * LANES])
    return acc


def _attn_prompt_kernel(q_ref, iq0_ref, iq1_ref, iwq_ref, k_ref, v_ref, ikw_ref, o_ref,
                        kb_scr, va_scr, ikb_scr, iqs_scr, qs_scr, key_scr, h16_scr, m_scr, acc_scr, xb_scr,
                        *, tq, n_sel, col_bits):
    qi = pl.program_id(1)
    tk = min(tq, KEY_BLOCK)
    n_kb = (qi + 1) * (tq // tk)
    n_lane_tiles = tk // LANES

    @pl.when(qi == 0)
    def _():
        kb_scr[...] = k_ref[...].astype(BF16)
        for g in range(N_KV_HEADS):
            va_scr[:, 2 * g * HEAD_DIM:(2 * g + 1) * HEAD_DIM] = (
                v_ref[:, g * HEAD_DIM:(g + 1) * HEAD_DIM].astype(BF16))
            va_scr[:, (2 * g + 1) * HEAD_DIM:(2 * g + 2) * HEAD_DIM] = jnp.ones((va_scr.shape[0], HEAD_DIM), BF16)
        ikb_scr[...] = ikw_ref[:, :IDX_DIM].astype(BF16)

    iq = jnp.concatenate([iq0_ref[...], iq1_ref[...]], axis=1)
    for h in range(N_IDX_HEADS):
        iqs_scr[h * tq:(h + 1) * tq, :] = iq[:, h * IDX_DIM:(h + 1) * IDX_DIM].astype(BF16)
    q = q_ref[...] * (HEAD_DIM ** -0.5 * LOG2_E)
    for h in range(N_ATT_HEADS):
        qs_scr[h * tq:(h + 1) * tq, :] = q[:, h * HEAD_DIM:(h + 1) * HEAD_DIM].astype(BF16)
    w = iwq_ref[:, IDX_DIM:IDX_DIM + N_IDX_HEADS] * (IDX_DIM ** -0.5 * N_IDX_HEADS ** -0.5)
    row = qi * tq + lax.broadcasted_iota(I32, (tq, tk), 0)
    col0 = lax.broadcasted_iota(I32, (tq, tk), 1)

    def score_block(kb, carry):
        start = pl.multiple_of(kb * tk, tk)
        ikb = ikb_scr[pl.ds(start, tk), :]
        s = jnp.zeros((tq, tk), F32)
        for h in range(N_IDX_HEADS):
            lg = _dot_nt(iqs_scr[h * tq:(h + 1) * tq, :], ikb)
            s = s + w[:, h:h + 1] * jnp.maximum(lg, 0.0)
        s = jnp.where(kb * tk + col0 <= row, s, MASK_VALUE)
        key = _sort_key(s)
        key_scr[kb] = key
        h16_scr[kb] = (key >> 16).astype(I16)
        return carry

    lax.fori_loop(0, n_kb, score_block, 0)

    chunks = [slice(c * ROW_CHUNK, (c + 1) * ROW_CHUNK) for c in range(tq // ROW_CHUNK)]

    def count(*make_preds):
        preds = [[mp(rows) for rows in chunks] for mp in make_preds]

        def body(kb, parts):
            new = []
            for ci, rows in enumerate(chunks):
                keys = key_scr[kb, rows, :]
                for pi in range(len(preds)):
                    idx = pi * len(chunks) + ci
                    hit = jnp.where(preds[pi][ci](kb, keys), 1.0, 0.0)
                    new.append((idx, parts[idx] + _lane_fold(hit, jnp.add)))
            return tuple(p for _, p in sorted(new, key=lambda t: t[0]))

        zeros = tuple(jnp.zeros((ROW_CHUNK, LANES), F32) for _ in range(len(preds) * len(chunks)))
        parts = lax.fori_loop(0, n_kb, body, zeros)
        outs = []
        for pi in range(len(preds)):
            part = jnp.concatenate(parts[pi * len(chunks):(pi + 1) * len(chunks)], axis=0)
            outs.append(jnp.broadcast_to(jnp.sum(part, axis=1, keepdims=True), (tq, LANES)))
        return outs[0] if len(outs) == 1 else outs

    def wide(x):
        return jnp.concatenate([x] * n_lane_tiles, axis=1)

    def bcast(x, rows):
        return wide(x[rows])

    def count16_ge(cand):
        c16 = cand.astype(I16)
        cs = [wide(c16[rows]) for rows in chunks]

        def body(kb, parts):
            return tuple(p + _lane_fold(jnp.where(h16_scr[kb, rows, :] >= c, jnp.int16(1), jnp.int16(0)), jnp.add)
                         for p, c, rows in zip(parts, cs, chunks))

        parts = lax.fori_loop(0, n_kb, body, tuple(jnp.zeros((ROW_CHUNK, LANES), I16) for _ in chunks))
        part = jnp.concatenate(parts, axis=0).astype(F32)
        return jnp.broadcast_to(jnp.sum(part, axis=1, keepdims=True), (tq, LANES))

    def search16():
        u = jnp.zeros((tq, LANES), I32)
        for bit in range(15, -1, -1):
            cand = u | (1 << bit)
            u = jnp.where(count16_ge(cand - 32768) >= float(n_sel), cand, u)
        return u

    hi_p = search16() - 32768
    hi_w = wide(hi_p)

    def low_keys(kb, carry):
        key = key_scr[kb]
        hi = key >> 16
        lo = (key & 0xFFFF) - 32768
        h16_scr[kb] = jnp.where(hi > hi_w, 32767, jnp.where(hi < hi_w, -32768, lo)).astype(I16)
        return carry

    lax.fori_loop(0, n_kb, low_keys, 0)
    thr = hi_p * 65536 + search16()

    cnt_gt, cnt_ge = count(lambda rows: (lambda kb, key, c=bcast(thr, rows): key > c),
                           lambda rows: (lambda kb, key, c=bcast(thr, rows): key >= c))
    need = float(n_sel) - cnt_gt
    xb_scr[...] = jnp.full(xb_scr.shape, 2 ** 31 - 1, I32)

    @pl.when(jnp.max(jnp.abs(cnt_ge - float(n_sel))) > 0.0)
    def _():
        def body(i, x):
            cand = x | lax.shift_left(jnp.int32(1), jnp.int32(col_bits - 1) - i)

            def make_pred(rows):
                c_thr, c_cand = bcast(thr, rows), bcast(cand, rows)
                c_col = lax.broadcasted_iota(I32, (ROW_CHUNK, tk), 1)
                return lambda kb, key: (key == c_thr) & (kb * tk + c_col < c_cand)

            return jnp.where(count(make_pred) < need, cand, x)

        xb_scr[...] = lax.fori_loop(0, col_bits, body, jnp.zeros((tq, LANES), I32))

    xb_w, thr_w = wide(xb_scr[...]), wide(thr)

    def bias_block(kb, carry):
        key = key_scr[kb]
        col = kb * tk + col0
        sel = ((key > thr_w) | ((key == thr_w) & (col <= xb_w))) & (col <= row)
        key_scr[kb] = pltpu.bitcast(jnp.where(sel, 0.0, -jnp.inf), I32)
        return carry

    lax.fori_loop(0, n_kb, bias_block, 0)

    m_scr[...] = jnp.full(m_scr.shape, MASK_VALUE, F32)
    acc_scr[...] = jnp.zeros(acc_scr.shape, F32)

    def att_block(kb, carry):
        start = pl.multiple_of(kb * tk, tk)
        bias = pltpu.bitcast(key_scr[kb], F32)
        for g in range(N_KV_HEADS):
            kg = kb_scr[pl.ds(start, tk), g * HEAD_DIM:(g + 1) * HEAD_DIM]
            va = va_scr[pl.ds(start, tk), 2 * g * HEAD_DIM:(2 * g + 2) * HEAD_DIM]
            base = g * KV_GROUP * tq
            s_all = _dot_nt(qs_scr[base:base + KV_GROUP * tq, :], kg)
            ps, alphas = [], []
            for c in range(KV_GROUP * tq // ATT_ROWS):
                lo = c * ATT_ROWS
                rows = slice(base + lo, base + lo + ATT_ROWS)
                s = s_all[lo:lo + ATT_ROWS] + bias[lo % tq:lo % tq + ATT_ROWS]
                m_old = m_scr[rows, :]
                m_new = jnp.maximum(m_old, jnp.max(_lane_fold(s, jnp.maximum), axis=1, keepdims=True))
                ps.append(jnp.exp2(s - jnp.concatenate([m_new] * n_lane_tiles, axis=1)).astype(BF16))
                alphas.append(jnp.exp2(m_old - m_new))
                m_scr[rows, :] = m_new
            pv = _dot(jnp.concatenate(ps, axis=0), va)
            for c, a in enumerate(alphas):
                lo = c * ATT_ROWS
                rows = slice(base + lo, base + lo + ATT_ROWS)
                acc_scr[rows, :] = jnp.concatenate([a, a], axis=1) * acc_scr[rows, :] + pv[lo:lo + ATT_ROWS]
        return carry

    lax.fori_loop(0, n_kb, att_block, 0)
    for h in range(N_ATT_HEADS):
        acc = acc_scr[h * tq:(h + 1) * tq, :]
        o_ref[:, h * HEAD_DIM:(h + 1) * HEAD_DIM] = (acc[:, :HEAD_DIM] / acc[:, HEAD_DIM:]).astype(o_ref.dtype)


def _attn_prompt(proj, b, t):
    tq = _tile(t, 512)
    tk = min(tq, KEY_BLOCK)
    n_sel = min(TOPK_MAX, t // 4)
    proj3 = proj.reshape(b, t, PACK_WIDTH)
    kern = functools.partial(_attn_prompt_kernel, tq=tq, n_sel=n_sel, col_bits=t.bit_length())
    once = dict(pipeline_mode=pl.Buffered(1))
    out = pl.pallas_call(
        kern,
        out_shape=jax.ShapeDtypeStruct((b, t, ATT_WIDTH), BF16),
        grid=(b, t // tq),
        in_specs=[pl.BlockSpec((None, tq, ATT_WIDTH), lambda bi, qi: (bi, qi, OFF_Q // ATT_WIDTH)),
                  pl.BlockSpec((None, tq, IQ_BLOCK), lambda bi, qi: (bi, qi, OFF_IQ // IQ_BLOCK)),
                  pl.BlockSpec((None, tq, IQ_BLOCK), lambda bi, qi: (bi, qi, OFF_IQ // IQ_BLOCK + 1)),
                  pl.BlockSpec((None, tq, LANES), lambda bi, qi: (bi, qi, OFF_IKW // LANES)),
                  pl.BlockSpec((None, t, KV_WIDTH), lambda bi, qi: (bi, 0, OFF_K // KV_WIDTH), **once),
                  pl.BlockSpec((None, t, KV_WIDTH), lambda bi, qi: (bi, 0, OFF_V // KV_WIDTH), **once),
                  pl.BlockSpec((None, t, LANES), lambda bi, qi: (bi, 0, OFF_IKW // LANES), **once)],
        out_specs=pl.BlockSpec((None, tq, ATT_WIDTH), lambda bi, qi: (bi, qi, 0)),
        scratch_shapes=[pltpu.VMEM((t, KV_WIDTH), BF16),
                        pltpu.VMEM((t, 2 * KV_WIDTH), BF16),
                        pltpu.VMEM((t, IDX_DIM), BF16),
                        pltpu.VMEM((N_IDX_HEADS * tq, IDX_DIM), BF16),
                        pltpu.VMEM((N_ATT_HEADS * tq, HEAD_DIM), BF16),
                        pltpu.VMEM((t // tk, tq, tk), I32),
                        pltpu.VMEM((t // tk, tq, tk), I16),
                        pltpu.VMEM((N_ATT_HEADS * tq, LANES), F32),
                        pltpu.VMEM((N_ATT_HEADS * tq, 2 * HEAD_DIM), F32),
                        pltpu.VMEM((tq, LANES), I32)],
        compiler_params=_params(("parallel", "arbitrary")),
        name="attn_prompt",
    )(proj3, proj3, proj3, proj3, proj3, proj3, proj3)
    return out.reshape(b * t, ATT_WIDTH)


def _hgrn_gates(hq, hf, lb):
    q = hq * jax.nn.sigmoid(hq)
    f = lb + (1.0 - lb) * jax.nn.sigmoid(hf)
    kk = (1.0 - lb) * jax.nn.sigmoid(-hf)
    return q, f, kk


def _hgrn_finish(o, hg, ng):
    o = o * lax.rsqrt(jnp.mean(o * o, axis=-1, keepdims=True) + LN_EPS)
    return o * ng * (hg * jax.nn.sigmoid(hg))


HGRN_HEADS_PER_STEP = 8


def _hgrn_chunk(heads, tri, ones):
    c_len, sub = HGRN_CHUNK, HGRN_SUB
    n_sub = c_len // sub
    t_idx = lax.broadcasted_iota(I32, (sub, 1), 0)
    bs = []
    for q, f, kk, v, st in heads:
        l1, l2, l3 = _split3(jnp.log(f))
        bs.append(_dot(tri, l1) + _dot(tri, l2) + _dot(tri, l3))
    stage2 = []
    for (q, f, kk, v, st), b in zip(heads, bs):
        v_b = v.astype(BF16)
        inter = _dot_nt((q * jnp.exp(b)).astype(BF16), st.astype(BF16))
        rs, cross = [], []
        for j in range(n_sub):
            lo = j * sub
            qt, bt = q[lo:lo + sub], b[lo:lo + sub]
            ps = []
            for s in range(sub):
                valid = t_idx >= s
                dec = jnp.exp(jnp.where(valid, bt - b[lo + s:lo + s + 1, :], 0.0))
                ps.append(jnp.where(valid, qt * kk[lo + s:lo + s + 1, :] * dec, 0.0))
            rs.append(_dot(jnp.concatenate(ps, axis=0).astype(BF16), ones))
            if lo:
                b_edge = b[lo - 1:lo, :]
                qd = (qt * jnp.exp(bt - b_edge)).astype(BF16)
                kd = (kk[:lo] * jnp.exp(b_edge - b[:lo])).astype(BF16)
                cross.append(_dot_nt(qd, kd))
        b_last = b[c_len - 1:c_len, :]
        kd = (kk * jnp.exp(b_last - b)).astype(BF16)
        upd = lax.dot_general(v_b, kd, (((0,), (0,)), ((), ())), preferred_element_type=F32)
        stage2.append((v_b, inter, rs, cross, st * jnp.exp(b_last) + upd))
    out = []
    for (q, f, kk, v, st), (v_b, inter, rs, cross, st_new) in zip(heads, stage2):
        rows = []
        for j in range(n_sub):
            lo = j * sub
            acc = inter[lo:lo + sub]
            if lo:
                acc = acc + _dot(cross[j - 1].astype(BF16), v_b[:lo])
            for s in range(sub):
                acc = acc + rs[j][s * sub:(s + 1) * sub, :] * v[lo + s:lo + s + 1, :]
            rows.append(acc)
        out.append((jnp.concatenate(rows, axis=0), st_new))
    return out


def _hgrn_prompt_kernel(hq_ref, hf_ref, hi_ref, hg_ref, lb_ref, ng_ref, o_ref, s_ref, st_scr, *, n_chunks):
    c_len = HGRN_CHUNK
    ti = pl.program_id(2)

    @pl.when(ti == 0)
    def _():
        st_scr[...] = jnp.zeros(st_scr.shape, F32)

    r_i = lax.broadcasted_iota(I32, (c_len, c_len), 0)
    c_i = lax.broadcasted_iota(I32, (c_len, c_len), 1)
    tri = jnp.where(c_i <= r_i, 1.0, 0.0).astype(BF16)
    ones = jnp.ones((HGRN_DK, HGRN_DV), BF16)

    def chunk(c, carry):
        sl = pl.ds(pl.multiple_of(c * c_len, c_len), c_len)
        cols = [slice(hh * HGRN_DK, (hh + 1) * HGRN_DK) for hh in range(HGRN_HEADS_PER_STEP)]
        heads = []
        for hh, cs in enumerate(cols):
            q, f, kk = _hgrn_gates(hq_ref[sl, cs], hf_ref[sl, cs], lb_ref[:, cs])
            heads.append((q, f, kk, hi_ref[sl, cs], st_scr[hh]))
        for hh, (o, st_new) in enumerate(_hgrn_chunk(heads, tri, ones)):
            st_scr[hh] = st_new
            o_ref[sl, cols[hh]] = _hgrn_finish(o, hg_ref[sl, cols[hh]], ng_ref[:, cols[hh]]).astype(o_ref.dtype)
        return carry

    lax.fori_loop(0, n_chunks, chunk, 0)

    @pl.when(ti == pl.num_programs(2) - 1)
    def _():
        for hh in range(HGRN_HEADS_PER_STEP):
            s_ref[hh] = st_scr[hh].T


def _hgrn_prompt(proj, lb, ng, b, t):
    tc = _tile(t, 512)
    n_chunks = tc // HGRN_CHUNK
    hp = HGRN_HEADS_PER_STEP
    width = hp * HGRN_DK
    proj3 = proj.reshape(b, t, PACK_WIDTH)

    def col(off):
        return pl.BlockSpec((None, tc, width), lambda bi, h, ti: (bi, ti, off // width + h))

    vec = pl.BlockSpec((None, 1, width), lambda bi, h, ti: (h, 0, 0))
    o, s = pl.pallas_call(
        functools.partial(_hgrn_prompt_kernel, n_chunks=n_chunks),
        out_shape=(jax.ShapeDtypeStruct((b, t, HGRN_WIDTH), BF16),
                   jax.ShapeDtypeStruct((b, N_HGRN_HEADS, HGRN_DK, HGRN_DV), F32)),
        grid=(b, N_HGRN_HEADS // hp, t // tc),
        in_specs=[col(OFF_HQ), col(OFF_HF), col(OFF_HI), col(OFF_HG), vec, vec],
        out_specs=(pl.BlockSpec((None, tc, width), lambda bi, h, ti: (bi, ti, h)),
                   pl.BlockSpec((None, hp, HGRN_DK, HGRN_DV), lambda bi, h, ti: (bi, h, 0, 0))),
        scratch_shapes=[pltpu.VMEM((hp, HGRN_DV, HGRN_DK), F32)],
        compiler_params=_params(("parallel", "parallel", "arbitrary")),
        name="hgrn_prompt",
    )(proj3, proj3, proj3, proj3,
      lb.reshape(N_HGRN_HEADS // hp, 1, width), ng.reshape(N_HGRN_HEADS // hp, 1, width))
    return o.reshape(b * t, HGRN_WIDTH), s


def _merge_kernel(oa_ref, ob_ref, ga_ref, gb_ref, wa_ref, wb_ref, o_ref):
    ya = _dot(oa_ref[...].astype(BF16), wa_ref[...])
    yb = _dot(ob_ref[...].astype(BF16), wb_ref[...])
    o_ref[...] = (jax.nn.sigmoid(ga_ref[...]) * ya + jax.nn.sigmoid(gb_ref[...]) * yb).astype(o_ref.dtype)


def _merge(o_a, o_b, proj, w_up_a, w_up_b, l, tm):
    n = o_a.shape[0]
    return pl.pallas_call(
        _merge_kernel,
        out_shape=jax.ShapeDtypeStruct((n, D_MODEL), BF16),
        grid=(n // tm,),
        in_specs=[pl.BlockSpec((tm, ATT_WIDTH), lambda i: (i, 0)),
                  pl.BlockSpec((tm, HGRN_WIDTH), lambda i: (i, 0)),
                  pl.BlockSpec((tm, D_MODEL), lambda i: (i, OFF_GA // D_MODEL)),
                  pl.BlockSpec((tm, D_MODEL), lambda i: (i, OFF_GB // D_MODEL)),
                  pl.BlockSpec((None, ATT_WIDTH, D_MODEL), lambda i: (l, 0, 0)),
                  pl.BlockSpec((None, HGRN_WIDTH, D_MODEL), lambda i: (l, 0, 0))],
        out_specs=pl.BlockSpec((tm, D_MODEL), lambda i: (i, 0)),
        compiler_params=_params(("parallel",)),
        name="merge",
    )(o_a, o_b, proj, proj, w_up_a, w_up_b)


def _post_norm(x, gate, y, g, b):
    return _ln(ALPHA * x + gate * y) * g + b


def _out_kernel(m_ref, w_ref, x_ref, gate_ref, lng_ref, lnb_ref, o_ref):
    y = _dot(m_ref[...], w_ref[...])
    o_ref[...] = _post_norm(x_ref[...], gate_ref[0], y, lng_ref[...], lnb_ref[...])


def _out_proj(merged, w_o, x, gate, ln_g, ln_b, l, tm):
    n = x.shape[0]
    tiles_per_mod = (n // tm) // gate.shape[0]
    vec = pl.BlockSpec((None, 1, D_MODEL), lambda i: (l, 0, 0))
    return pl.pallas_call(
        _out_kernel,
        out_shape=jax.ShapeDtypeStruct((n, D_MODEL), F32),
        grid=(n // tm,),
        in_specs=[pl.BlockSpec((tm, D_MODEL), lambda i: (i, 0)),
                  pl.BlockSpec((None, D_MODEL, D_MODEL), lambda i: (l, 0, 0)),
                  pl.BlockSpec((tm, D_MODEL), lambda i: (i, 0)),
                  pl.BlockSpec((1, gate.shape[1], D_MODEL), lambda i: (i // tiles_per_mod, 0, 0)),
                  vec, vec],
        out_specs=pl.BlockSpec((tm, D_MODEL), lambda i: (i, 0)),
        compiler_params=_params(("parallel",)),
        name="out_proj",
    )(merged, w_o, x, gate, ln_g.reshape(DEPTH, 1, D_MODEL), ln_b.reshape(DEPTH, 1, D_MODEL))


def _route(h, wr, br):
    h1, h2, _ = _split3(h)
    w1, w2, _ = _split3(wr)
    logits = _dot_nt(w1, h1) + _dot_nt(w1, h2) + _dot_nt(w2, h1)
    aff = jax.nn.sigmoid(logits)
    sel = aff + br
    rows = [sel[e:e + 1, :] for e in range(N_EXPERTS)]
    grp = []
    for g in range(N_GROUPS):
        a, b, c, d = rows[g * EXPERTS_PER_GROUP:(g + 1) * EXPERTS_PER_GROUP]
        hi1, lo1 = jnp.maximum(a, b), jnp.minimum(a, b)
        hi2, lo2 = jnp.maximum(c, d), jnp.minimum(c, d)
        grp.append(jnp.maximum(hi1, hi2) + jnp.maximum(jnp.minimum(hi1, hi2), jnp.maximum(lo1, lo2)))
    best = functools.reduce(jnp.maximum, grp)
    taken = jnp.zeros_like(best)
    picked = []
    for g in range(N_GROUPS):
        is_g = jnp.where(grp[g] == best, 1.0, 0.0) * (1.0 - taken)
        taken = taken + is_g
        for e in range(g * EXPERTS_PER_GROUP, (g + 1) * EXPERTS_PER_GROUP):
            rank = jnp.zeros_like(best)
            for o in range(g * EXPERTS_PER_GROUP, (g + 1) * EXPERTS_PER_GROUP):
                if o < e:
                    rank = rank + jnp.where(rows[o] >= rows[e], 1.0, 0.0)
                elif o > e:
                    rank = rank + jnp.where(rows[o] > rows[e], 1.0, 0.0)
            picked.append(is_g * jnp.where(rank < 2.0, 1.0, 0.0))
    picked = jnp.concatenate(picked, axis=0)
    gate = picked * aff
    return picked, gate / jnp.sum(gate, axis=0, keepdims=True)


def _router_kernel(x_ref, sh_ref, sc_ref, wr_ref, br_ref, h_ref, g_ref):
    h = _ln(x_ref[...]) * (1.0 + sc_ref[0]) + sh_ref[0]
    h_ref[...] = h.astype(h_ref.dtype)
    _, g_ref[...] = _route(h, wr_ref[...], br_ref[...])


def _router_sorted_kernel(x_ref, sh_ref, sc_ref, wr_ref, br_ref, h_ref, eid_ref, gw_ref, pos_ref, cnt_ref, base_scr):
    tm = x_ref.shape[0]

    @pl.when(pl.program_id(0) == 0)
    def _():
        base_scr[...] = jnp.zeros(base_scr.shape, F32)

    h = _ln(x_ref[...]) * (1.0 + sc_ref[0]) + sh_ref[0]
    h_ref[...] = h.astype(h_ref.dtype)
    picked, gate = _route(h, wr_ref[...], br_ref[...])
    r = lax.broadcasted_iota(I32, (tm, tm), 0)
    c = lax.broadcasted_iota(I32, (tm, tm), 1)
    earlier = jnp.where(r < c, 1.0, 0.0).astype(BF16)
    base = base_scr[...]
    rank = _dot(picked.astype(BF16), earlier) + jnp.concatenate([base] * (tm // LANES), axis=1)
    base_scr[...] = base + jnp.sum(picked, axis=1, keepdims=True)
    cnt_ref[...] = base_scr[...]
    eio = lax.broadcasted_iota(I32, picked.shape, 0).astype(F32)
    e_lo = jnp.min(jnp.where(picked > 0.0, eio, float(N_EXPERTS)), axis=0, keepdims=True)
    e_hi = jnp.max(jnp.where(picked > 0.0, eio, -1.0), axis=0, keepdims=True)

    def pick(e, x):
        return jnp.sum(jnp.where(eio == e, x, 0.0), axis=0, keepdims=True)

    eid_ref[...] = jnp.concatenate([e_lo, e_hi], axis=0).astype(I32)
    gw_ref[...] = jnp.concatenate([pick(e_lo, gate), pick(e_hi, gate)], axis=0)
    pos_ref[...] = jnp.concatenate([pick(e_lo, rank), pick(e_hi, rank)], axis=0).astype(I32)


def _router_sorted(x, shift, scale, w_router_t, b_router, tm):
    n = x.shape[0]
    tiles_per_mod = (n // tm) // shift.shape[0]
    mod_spec = pl.BlockSpec((1, shift.shape[1], D_MODEL), lambda i: (i // tiles_per_mod, 0, 0))
    pair = pl.BlockSpec((2, tm), lambda i: (0, i))
    return pl.pallas_call(
        _router_sorted_kernel,
        out_shape=(jax.ShapeDtypeStruct((n, D_MODEL), F32),
                   jax.ShapeDtypeStruct((2, n), I32),
                   jax.ShapeDtypeStruct((2, n), F32),
                   jax.ShapeDtypeStruct((2, n), I32),
                   jax.ShapeDtypeStruct((N_EXPERTS, LANES), F32)),
        grid=(n // tm,),
        in_specs=[pl.BlockSpec((tm, D_MODEL), lambda i: (i, 0)),
                  mod_spec, mod_spec,
                  pl.BlockSpec((N_EXPERTS, D_MODEL), lambda i: (0, 0)),
                  pl.BlockSpec((N_EXPERTS, 1), lambda i: (0, 0))],
        out_specs=(pl.BlockSpec((tm, D_MODEL), lambda i: (i, 0)), pair, pair, pair,
                   pl.BlockSpec((N_EXPERTS, LANES), lambda i: (0, 0))),
        scratch_shapes=[pltpu.VMEM((N_EXPERTS, LANES), F32)],
        compiler_params=_params(("arbitrary",)),
        name="router_sorted",
    )(x, shift, scale, w_router_t, b_router.reshape(N_EXPERTS, 1))


MOE_TILE = 256
DMA_UNROLL = 32


def _dispatch_kernel(dst_ref, ends_ref, h_ref, xs_ref, zero_scr, sem, zsem, *, td, n):
    first = pl.program_id(0) * td

    @pl.when(pl.program_id(0) == 0)
    def _():
        zero_scr[...] = jnp.zeros(zero_scr.shape, F32)
        n_slots = xs_ref.shape[0]

        def zero_copy(start):
            return pltpu.make_async_copy(zero_scr, xs_ref.at[pl.ds(pl.multiple_of(start, MOE_TILE), MOE_TILE)], zsem)

        jobs = []
        for e in range(N_EXPERTS):
            jobs.append((ends_ref[e + 1] > ends_ref[e], ends_ref[e + 1] - MOE_TILE))
            unused = ends_ref[N_EXPERTS] + e * MOE_TILE
            jobs.append((unused < n_slots, jnp.minimum(unused, n_slots - MOE_TILE)))
        for wanted, start in jobs:
            @pl.when(wanted)
            def _(start=start):
                zero_copy(start).start()
        for wanted, start in jobs:
            @pl.when(wanted)
            def _(start=start):
                zero_copy(start).wait()

    def copies(r):
        return [pltpu.make_async_copy(h_ref.at[pl.ds(r, 1)], xs_ref.at[pl.ds(dst_ref[k * n + first + r], 1)], sem)
                for k in range(2)]

    def start(r, carry):
        for cp in copies(r):
            cp.start()
        return carry

    def wait(r, carry):
        for cp in copies(r):
            cp.wait()
        return carry

    lax.fori_loop(0, td, start, 0, unroll=DMA_UNROLL)
    lax.fori_loop(0, td, wait, 0, unroll=DMA_UNROLL)


def _dispatch(h, dst_flat, seg_ends, n_slots, td):
    n = h.shape[0]
    grid_spec = pltpu.PrefetchScalarGridSpec(
        num_scalar_prefetch=2,
        grid=(n // td,),
        in_specs=[pl.BlockSpec((td, D_MODEL), lambda i, dst, ends: (i, 0))],
        out_specs=pl.BlockSpec(memory_space=pl.ANY),
        scratch_shapes=[pltpu.VMEM((MOE_TILE, D_MODEL), F32),
                        pltpu.SemaphoreType.DMA,
                        pltpu.SemaphoreType.DMA])
    return pl.pallas_call(
        functools.partial(_dispatch_kernel, td=td, n=n),
        out_shape=jax.ShapeDtypeStruct((n_slots, D_MODEL), F32),
        grid_spec=grid_spec,
        compiler_params=_params(("arbitrary",)),
        name="moe_dispatch",
    )(dst_flat, seg_ends, h)


def _experts_kernel(te_ref, tv_ref, xs_ref, w1_ref, w3_ref, w2_ref, ys_ref):
    del te_ref
    valid = tv_ref[pl.program_id(0)] > 0

    @pl.when(valid)
    def _():
        x = xs_ref[...].astype(BF16)
        a = _dot(x, w1_ref[...])
        b = _dot(x, w3_ref[...])
        ys_ref[...] = _dot((a * jax.nn.sigmoid(a) * b).astype(BF16), w2_ref[...])

    @pl.when(jnp.logical_not(valid))
    def _():
        ys_ref[...] = jnp.zeros(ys_ref.shape, F32)


def _experts(xs, tile_expert, tile_valid, w1, w3, w2, l):
    n_tiles = xs.shape[0] // MOE_TILE
    grid_spec = pltpu.PrefetchScalarGridSpec(
        num_scalar_prefetch=2,
        grid=(n_tiles,),
        in_specs=[pl.BlockSpec((MOE_TILE, D_MODEL), lambda i, te, tv: (jnp.minimum(i, tv[0] - 1), 0)),
                  pl.BlockSpec((None, None, D_MODEL, EXPERT_DFF), lambda i, te, tv: (l, te[i], 0, 0)),
                  pl.BlockSpec((None, None, D_MODEL, EXPERT_DFF), lambda i, te, tv: (l, te[i], 0, 0)),
                  pl.BlockSpec((None, None, EXPERT_DFF, D_MODEL), lambda i, te, tv: (l, te[i], 0, 0))],
        out_specs=pl.BlockSpec((MOE_TILE, D_MODEL), lambda i, te, tv: (i, 0)))
    return pl.pallas_call(
        _experts_kernel,
        out_shape=jax.ShapeDtypeStruct(xs.shape, F32),
        grid_spec=grid_spec,
        compiler_params=_params(("arbitrary",)),
        name="moe_experts",
    )(tile_expert, tile_valid, xs, w1, w3, w2)


def _combine_kernel(dst_ref, ys_ref, gw_ref, x_ref, gate_ref, lng_ref, lnb_ref, o_ref, buf, sem, *, tc, n):
    i = pl.program_id(0)
    slot = i % 2

    def copies(tile, s, r):
        return [pltpu.make_async_copy(ys_ref.at[pl.ds(dst_ref[k * n + tile * tc + r], 1)],
                                      buf.at[s, k, pl.ds(r, 1)], sem.at[s]) for k in range(2)]

    def start_tile(tile, s):
        def body(r, carry):
            for cp in copies(tile, s, r):
                cp.start()
            return carry
        lax.fori_loop(0, tc, body, 0, unroll=DMA_UNROLL)

    def wait_tile(tile, s):
        def body(r, carry):
            for cp in copies(tile, s, r):
                cp.wait()
            return carry
        lax.fori_loop(0, tc, body, 0, unroll=DMA_UNROLL)

    @pl.when(i == 0)
    def _():
        start_tile(0, 0)

    @pl.when(i + 1 < pl.num_programs(0))
    def _():
        start_tile(i + 1, 1 - slot)

    wait_tile(i, slot)
    gw = gw_ref[...]
    y = gw[:, 0:1] * buf[slot, 0] + gw[:, 1:2] * buf[slot, 1]
    o_ref[...] = _post_norm(x_ref[...], gate_ref[0], y, lng_ref[...], lnb_ref[...])


def _combine(ys, dst_flat, gw, x, gate, ln_g, ln_b, l, tc):
    n = x.shape[0]
    tiles_per_mod = (n // tc) // gate.shape[0]
    vec = pl.BlockSpec((None, 1, D_MODEL), lambda i, dst: (l, 0, 0))
    grid_spec = pltpu.PrefetchScalarGridSpec(
        num_scalar_prefetch=1,
        grid=(n // tc,),
        in_specs=[pl.BlockSpec(memory_space=pl.ANY),
                  pl.BlockSpec((tc, 2), lambda i, dst: (i, 0)),
                  pl.BlockSpec((tc, D_MODEL), lambda i, dst: (i, 0)),
                  pl.BlockSpec((1, gate.shape[1], D_MODEL), lambda i, dst: (i // tiles_per_mod, 0, 0)),
                  vec, vec],
        out_specs=pl.BlockSpec((tc, D_MODEL), lambda i, dst: (i, 0)),
        scratch_shapes=[pltpu.VMEM((2, 2, tc, D_MODEL), F32),
                        pltpu.SemaphoreType.DMA((2,))])
    return pl.pallas_call(
        functools.partial(_combine_kernel, tc=tc, n=n),
        out_shape=jax.ShapeDtypeStruct((n, D_MODEL), F32),
        grid_spec=grid_spec,
        compiler_params=_params(("arbitrary",)),
        name="moe_combine",
    )(dst_flat, ys, gw, x, gate, ln_g.reshape(DEPTH, 1, D_MODEL), ln_b.reshape(DEPTH, 1, D_MODEL))


def _moe_grouped(x, shift, scale, gate, w_router_t, b_router, w1, w3, w2, ln_g, ln_b, l, tm):
    n = x.shape[0]
    h, eid, gw, pos, cnt = _router_sorted(x, shift, scale, w_router_t, b_router, tm)
    counts = cnt[:, 0].astype(I32)
    padded = (counts + MOE_TILE - 1) // MOE_TILE * MOE_TILE
    ends = jnp.cumsum(padded)
    experts = jnp.arange(N_EXPERTS, dtype=I32)[:, None, None]
    seg_start = jnp.sum(jnp.where(eid[None] == experts, (ends - padded)[:, None, None], 0), axis=0)
    dst_flat = (seg_start + pos).reshape(2 * n)
    n_tiles = 2 * n // MOE_TILE + N_EXPERTS
    tile_start = jnp.arange(n_tiles, dtype=I32) * MOE_TILE
    tile_expert = jnp.minimum(jnp.sum((tile_start[:, None] >= ends[None, :]).astype(I32), axis=1), N_EXPERTS - 1)
    tile_valid = jnp.where(tile_start < ends[-1], ends[-1] // MOE_TILE, 0).astype(I32)
    seg_ends = jnp.concatenate([jnp.zeros((1,), I32), ends.astype(I32)])
    xs = _dispatch(h, dst_flat, seg_ends, n_tiles * MOE_TILE, tm)
    ys = _experts(xs, tile_expert, tile_valid, w1, w3, w2, l)
    return _combine(ys, dst_flat, gw.T, x, gate, ln_g, ln_b, l, MOE_TILE)


def _router(x, shift, scale, w_router_t, b_router, tm):
    n = x.shape[0]
    tiles_per_mod = (n // tm) // shift.shape[0]
    mod_spec = pl.BlockSpec((1, shift.shape[1], D_MODEL), lambda i: (i // tiles_per_mod, 0, 0))
    return pl.pallas_call(
        _router_kernel,
        out_shape=(jax.ShapeDtypeStruct((n, D_MODEL), BF16),
                   jax.ShapeDtypeStruct((N_EXPERTS, n), F32)),
        grid=(n // tm,),
        in_specs=[pl.BlockSpec((tm, D_MODEL), lambda i: (i, 0)),
                  mod_spec, mod_spec,
                  pl.BlockSpec((N_EXPERTS, D_MODEL), lambda i: (0, 0)),
                  pl.BlockSpec((N_EXPERTS, 1), lambda i: (0, 0))],
        out_specs=(pl.BlockSpec((tm, D_MODEL), lambda i: (i, 0)),
                   pl.BlockSpec((N_EXPERTS, tm), lambda i: (0, i))),
        compiler_params=_params(("parallel",)),
        name="router",
    )(x, shift, scale, w_router_t, b_router.reshape(N_EXPERTS, 1))


def _moe_kernel(h_ref, g_ref, w1_ref, w3_ref, w2_ref, x_ref, gate_ref, lng_ref, lnb_ref, o_ref, acc_scr):
    e = pl.program_id(1)

    @pl.when(e == 0)
    def _():
        acc_scr[...] = jnp.zeros(acc_scr.shape, F32)

    h = h_ref[...]
    a = _dot(h, w1_ref[...])
    b = _dot(h, w3_ref[...])
    gates = g_ref[...]
    lane = lax.broadcasted_iota(I32, gates.shape, 1)
    gcol = jnp.sum(jnp.where(lane == e, gates, 0.0), axis=1, keepdims=True)
    u = (a * jax.nn.sigmoid(a) * b * gcol).astype(BF16)
    acc_scr[...] += _dot(u, w2_ref[...])

    @pl.when(e == pl.num_programs(1) - 1)
    def _():
        o_ref[...] = _post_norm(x_ref[...], gate_ref[0], acc_scr[...], lng_ref[...], lnb_ref[...])


def _moe(h, gates, w1, w3, w2, x, gate, ln_g, ln_b, l, tm):
    n = x.shape[0]
    tiles_per_mod = (n // tm) // gate.shape[0]
    vec = pl.BlockSpec((None, 1, D_MODEL), lambda i, e: (l, 0, 0))
    return pl.pallas_call(
        _moe_kernel,
        out_shape=jax.ShapeDtypeStruct((n, D_MODEL), F32),
        grid=(n // tm, N_EXPERTS),
        in_specs=[pl.BlockSpec((tm, D_MODEL), lambda i, e: (i, 0)),
                  pl.BlockSpec((tm, N_EXPERTS), lambda i, e: (i, 0)),
                  pl.BlockSpec((None, None, D_MODEL, EXPERT_DFF), lambda i, e: (l, e, 0, 0)),
                  pl.BlockSpec((None, None, D_MODEL, EXPERT_DFF), lambda i, e: (l, e, 0, 0)),
                  pl.BlockSpec((None, None, EXPERT_DFF, D_MODEL), lambda i, e: (l, e, 0, 0)),
                  pl.BlockSpec((tm, D_MODEL), lambda i, e: (i, 0)),
                  pl.BlockSpec((1, gate.shape[1], D_MODEL), lambda i, e: (i // tiles_per_mod, 0, 0)),
                  vec, vec],
        out_specs=pl.BlockSpec((tm, D_MODEL), lambda i, e: (i, 0)),
        scratch_shapes=[pltpu.VMEM((tm, D_MODEL), F32)],
        compiler_params=_params(("parallel", "arbitrary")),
        name="moe",
    )(h, gates, w1, w3, w2, x, gate, ln_g.reshape(DEPTH, 1, D_MODEL), ln_b.reshape(DEPTH, 1, D_MODEL))


def _row_to_col(v):
    n = v.shape[1]
    r = lax.broadcasted_iota(I32, (n, n), 0)
    c = lax.broadcasted_iota(I32, (n, n), 1)
    return jnp.sum(jnp.where(r == c, jnp.broadcast_to(v, (n, n)), 0.0), axis=1, keepdims=True)


def _attn_decode_kernel(pt_ref, proj_ref, ck_ref, cv_ref, cik_ref, o_ref,
                        ik_buf, k_buf, v_buf, key_scr, xb_scr, sem_ik, sem_k, sem_v,
                        *, l, n_pages, page, chunk_pages, n_sel):
    b = pl.program_id(0)
    n_chunks = n_pages // chunk_pages
    chunk = chunk_pages * page

    def ik_copy(j):
        return pltpu.make_async_copy(cik_ref.at[l, pt_ref[b, j]], ik_buf.at[pl.ds(j * page, page)], sem_ik)

    def kv_copies(c, j, slot):
        pg = pt_ref[b, c * chunk_pages + j]
        dst = pl.ds(j * page, page)
        copies = []
        for g in range(N_KV_HEADS):
            copies.append(pltpu.make_async_copy(ck_ref.at[l, pg, :, g, :], k_buf.at[slot, g, dst], sem_k.at[slot]))
            copies.append(pltpu.make_async_copy(cv_ref.at[l, pg, :, g, :], v_buf.at[slot, g, dst], sem_v.at[slot]))
        return copies

    def start_chunk(c, slot):
        def body(j, carry):
            for cp in kv_copies(c, j, slot):
                cp.start()
            return carry
        lax.fori_loop(0, chunk_pages, body, 0)

    def wait_chunk(c, slot):
        def body(j, carry):
            for cp in kv_copies(c, j, slot):
                cp.wait()
            return carry
        lax.fori_loop(0, chunk_pages, body, 0)

    def start_ik(j, carry):
        ik_copy(j).start()
        return carry

    def wait_ik(j, carry):
        ik_copy(j).wait()
        return carry

    lax.fori_loop(0, n_pages, start_ik, 0)
    start_chunk(0, 0)
    lax.fori_loop(0, n_pages, wait_ik, 0)

    rowsl = slice(None)
    iq = proj_ref[rowsl, OFF_IQ:OFF_IQ + N_IDX_HEADS * IDX_DIM]
    iq_h = jnp.concatenate([iq[:, h * IDX_DIM:(h + 1) * IDX_DIM] for h in range(N_IDX_HEADS)], axis=0)
    w_row = proj_ref[rowsl, OFF_IKW + IDX_DIM:OFF_IKW + IDX_DIM + N_IDX_HEADS]
    w_col = _row_to_col(w_row) * (IDX_DIM ** -0.5 * N_IDX_HEADS ** -0.5)
    ik_new = proj_ref[rowsl, OFF_IKW:OFF_IKW + IDX_DIM]

    iq_b = iq_h.astype(BF16)
    for c in range(n_chunks):
        lg = _dot_nt(iq_b, ik_buf[c * chunk:(c + 1) * chunk, :].astype(BF16))
        s = jnp.sum(w_col * jnp.maximum(lg, 0.0), axis=0, keepdims=True)
        key_scr[c] = _sort_key(s)
    lg_new = jnp.sum(iq_b.astype(F32) * ik_new.astype(BF16).astype(F32), axis=1, keepdims=True)
    key_new = _sort_key(jnp.sum(w_col * jnp.maximum(lg_new, 0.0), axis=0, keepdims=True))

    def count_ge(cand):
        cnt = jnp.where(key_new >= cand, 1.0, 0.0)
        for c in range(n_chunks):
            cnt = cnt + jnp.sum(jnp.where(key_scr[c] >= cand, 1.0, 0.0), axis=1, keepdims=True)
        return cnt

    thr = _kth_largest_key(count_ge, float(n_sel), (1, 1))

    past = n_pages * page
    lane_pos = lax.broadcasted_iota(I32, (1, chunk), 1)

    def count(pred):
        cnt = jnp.where(pred(key_new, past), 1.0, 0.0)
        for c in range(n_chunks):
            cnt = cnt + jnp.sum(jnp.where(pred(key_scr[c], c * chunk + lane_pos), 1.0, 0.0), axis=1, keepdims=True)
        return cnt

    need = float(n_sel) - count(lambda key, pos: key > thr)
    xb_scr[...] = jnp.full(xb_scr.shape, 2 ** 31 - 1, I32)

    @pl.when(jnp.max(jnp.abs(count(lambda key, pos: key >= thr) - float(n_sel))) > 0.0)
    def _():
        x = jnp.zeros((1, 1), I32)
        for bit in range((past + 1).bit_length() - 1, -1, -1):
            cand = x | (1 << bit)
            x = jnp.where(count(lambda key, pos: (key == thr) & (pos < cand)) < need, cand, x)
        xb_scr[...] = jnp.broadcast_to(x, xb_scr.shape)

    xb = xb_scr[:, 0:1]

    def selected(key, pos):
        return (key > thr) | ((key == thr) & (pos <= xb))

    q = proj_ref[rowsl, OFF_Q:OFF_Q + ATT_WIDTH]
    q_h = jnp.concatenate([q[:, h * HEAD_DIM:(h + 1) * HEAD_DIM] for h in range(N_ATT_HEADS)], axis=0)
    q_b = q_h.astype(BF16)
    k_new = proj_ref[rowsl, OFF_K:OFF_K + KV_WIDTH]
    v_new = proj_ref[rowsl, OFF_V:OFF_V + KV_WIDTH]
    scale = HEAD_DIM ** -0.5

    def att_chunk(c, carry):
        slot = c % 2

        @pl.when(c + 1 < n_chunks)
        def _():
            start_chunk(c + 1, 1 - slot)

        wait_chunk(c, slot)
        sel = selected(key_scr[c], c * chunk + lane_pos)
        out = []
        for g in range(N_KV_HEADS):
            m_old, l_old, acc_old = carry[g]
            kg = k_buf[slot, g].astype(BF16)
            vg = v_buf[slot, g].astype(BF16)
            s = _dot_nt(q_b[g * KV_GROUP:(g + 1) * KV_GROUP], kg) * scale
            s = jnp.where(sel, s, MASK_VALUE)
            m_new = jnp.maximum(m_old, jnp.max(s, axis=1, keepdims=True))
            p = jnp.where(sel, jnp.exp(s - m_new), 0.0)
            a = jnp.exp(m_old - m_new)
            out.append((m_new, a * l_old + jnp.sum(p, axis=1, keepdims=True),
                        a * acc_old + _dot(p.astype(BF16), vg)))
        return tuple(out)

    init = tuple((jnp.full((KV_GROUP, 1), MASK_VALUE, F32), jnp.zeros((KV_GROUP, 1), F32),
                  jnp.zeros((KV_GROUP, HEAD_DIM), F32)) for _ in range(N_KV_HEADS))
    res = lax.fori_loop(0, n_chunks, att_chunk, init)

    sel_new = selected(key_new, past)
    for g in range(N_KV_HEADS):
        m_old, l_old, acc_old = res[g]
        kg = k_new[:, g * HEAD_DIM:(g + 1) * HEAD_DIM].astype(BF16).astype(F32)
        vg = v_new[:, g * HEAD_DIM:(g + 1) * HEAD_DIM].astype(BF16).astype(F32)
        qg = q_b[g * KV_GROUP:(g + 1) * KV_GROUP].astype(F32)
        s = jnp.sum(qg * kg, axis=1, keepdims=True) * scale
        s = jnp.where(sel_new, s, MASK_VALUE)
        m_new = jnp.maximum(m_old, s)
        p = jnp.where(sel_new, jnp.exp(s - m_new), 0.0)
        a = jnp.exp(m_old - m_new)
        l_new = a * l_old + p
        acc = a * acc_old + p.astype(BF16).astype(F32) * vg
        o = acc / l_new
        for r in range(KV_GROUP):
            hh = g * KV_GROUP + r
            o_ref[rowsl, hh * HEAD_DIM:(hh + 1) * HEAD_DIM] = o[r:r + 1, :].astype(o_ref.dtype)


def _attn_decode(proj, page_table, cache_k, cache_v, cache_idx_k, l):
    db, n_pages = page_table.shape
    page = cache_k.shape[2]
    past = n_pages * page
    n_sel = min(TOPK_MAX, (past + 1) // 4)
    chunk_pages = _tile(n_pages, 16)
    kern = functools.partial(_attn_decode_kernel, l=l, n_pages=n_pages, page=page,
                             chunk_pages=chunk_pages, n_sel=n_sel)
    grid_spec = pltpu.PrefetchScalarGridSpec(
        num_scalar_prefetch=1,
        grid=(db,),
        in_specs=[pl.BlockSpec((None, 1, PACK_WIDTH), lambda b, pt: (b, 0, 0)),
                  pl.BlockSpec(memory_space=pl.ANY),
                  pl.BlockSpec(memory_space=pl.ANY),
                  pl.BlockSpec(memory_space=pl.ANY)],
        out_specs=pl.BlockSpec((None, 1, ATT_WIDTH), lambda b, pt: (b, 0, 0)),
        scratch_shapes=[pltpu.VMEM((past, IDX_DIM), F32),
                        pltpu.VMEM((2, N_KV_HEADS, chunk_pages * page, HEAD_DIM), F32),
                        pltpu.VMEM((2, N_KV_HEADS, chunk_pages * page, HEAD_DIM), F32),
                        pltpu.VMEM((n_pages // chunk_pages, 1, chunk_pages * page), I32),
                        pltpu.VMEM((1, LANES), I32),
                        pltpu.SemaphoreType.DMA,
                        pltpu.SemaphoreType.DMA((2,)),
                        pltpu.SemaphoreType.DMA((2,))])
    return pl.pallas_call(
        kern,
        out_shape=jax.ShapeDtypeStruct((db, 1, ATT_WIDTH), F32),
        grid_spec=grid_spec,
        compiler_params=_params(("arbitrary",)),
        name="attn_decode",
    )(page_table, proj.reshape(db, 1, PACK_WIDTH), cache_k, cache_v, cache_idx_k).reshape(db, ATT_WIDTH)


def _hgrn_decode_kernel(proj_ref, s0_ref, lb_ref, ng_ref, o_ref, s_ref):
    rowsl = slice(None)
    for h in range(N_HGRN_HEADS):
        cols = lambda off: slice(off + h * HGRN_DK, off + (h + 1) * HGRN_DK)
        lb = lb_ref[:, h * HGRN_DK:(h + 1) * HGRN_DK]
        q, f, kk = _hgrn_gates(proj_ref[rowsl, cols(OFF_HQ)], proj_ref[rowsl, cols(OFF_HF)], lb)
        v = proj_ref[rowsl, cols(OFF_HI)]
        s_new = _row_to_col(f) * s0_ref[h] + _row_to_col(kk) * v
        s_ref[h] = s_new
        o = jnp.sum(_row_to_col(q) * s_new, axis=0, keepdims=True)
        ng = ng_ref[:, h * HGRN_DV:(h + 1) * HGRN_DV]
        o_ref[rowsl, h * HGRN_DV:(h + 1) * HGRN_DV] = _hgrn_finish(
            o, proj_ref[rowsl, cols(OFF_HG)], ng).astype(o_ref.dtype)


def _hgrn_decode(proj, state, lb, ng, l):
    db = proj.shape[0]
    st_spec_in = pl.BlockSpec((None, None, N_HGRN_HEADS, HGRN_DK, HGRN_DV), lambda b: (l, b, 0, 0, 0))
    o, s = pl.pallas_call(
        _hgrn_decode_kernel,
        out_shape=(jax.ShapeDtypeStruct((db, 1, HGRN_WIDTH), F32),
                   jax.ShapeDtypeStruct((db, N_HGRN_HEADS, HGRN_DK, HGRN_DV), F32)),
        grid=(db,),
        in_specs=[pl.BlockSpec((None, 1, PACK_WIDTH), lambda b: (b, 0, 0)),
                  st_spec_in,
                  pl.BlockSpec((1, HGRN_WIDTH), lambda b: (0, 0)),
                  pl.BlockSpec((1, HGRN_WIDTH), lambda b: (0, 0))],
        out_specs=(pl.BlockSpec((None, 1, HGRN_WIDTH), lambda b: (b, 0, 0)),
                   pl.BlockSpec((None, N_HGRN_HEADS, HGRN_DK, HGRN_DV), lambda b: (b, 0, 0, 0))),
        compiler_params=_params(("parallel",)),
        name="hgrn_decode",
    )(proj.reshape(db, 1, PACK_WIDTH), state, lb.reshape(1, HGRN_WIDTH), ng.reshape(1, HGRN_WIDTH))
    return o.reshape(db, HGRN_WIDTH), s


def _split_w_in(w_in):
    return w_in[:, :, TAIL_START:].astype(BF16), w_in[:, :, :HEAD_COLS].astype(BF16)


def _mods(mod_l, rows, per_row):
    m = mod_l[rows]
    parts = jnp.split(m, 6, axis=-1)
    if per_row:
        return [p[None, :, :] for p in parts]
    return [p[:, None, :] for p in parts]


def kernel(x_prompt, x_sample, c_prompt, c_sample, cache_k, cache_v, cache_idx_k, state_hgrn, page_table,
           w_ada, b_ada, w_in, w_up_a, w_up_b, w_o, hgrn_norm_g, hgrn_lb_logits, ln1_g, ln1_b,
           w_router, b_router, w1, w3, w2, ln2_g, ln2_b):
    bp, t, _ = x_prompt.shape
    db = x_sample.shape[0]
    lbp = jax.nn.softmax(hgrn_lb_logits.astype(F32), axis=0)
    lower_bounds = jnp.cumsum(lbp, axis=0) - lbp[0]

    w_tail, w_head = _split_w_in(w_in)
    w_up_a_b, w_up_b_b, w_o_b = w_up_a.astype(BF16), w_up_b.astype(BF16), w_o.astype(BF16)
    w1_b, w3_b, w2_b = w1.astype(BF16), w3.astype(BF16), w2.astype(BF16)
    w_router_t = w_router.T

    n_c = bp + db
    c_rows = -(-n_c // 8) * 8
    c_all = jnp.concatenate([c_prompt, c_sample, jnp.zeros((c_rows - n_c, D_MODEL), F32)], axis=0)
    mod = _ada(c_all, w_ada, b_ada)

    xp = x_prompt.reshape(bp * t, D_MODEL)
    xs = x_sample.reshape(db, D_MODEL)
    tm_p = _tile(bp * t, 512)
    outs_p = {"k": [], "v": [], "ik": [], "s": []}
    outs_s = {"k": [], "v": [], "ik": [], "s": []}
    for l in range(DEPTH):
        sh1, sc1, g1, sh2, sc2, g2 = _mods(mod[l], slice(0, bp), per_row=False)
        proj = _proj(xp, sh1, sc1, w_tail, w_head, l, _tile(t, 1024))
        o_a = _attn_prompt(proj, bp, t)
        o_b, s_new = _hgrn_prompt(proj, lower_bounds[l], hgrn_norm_g[l], bp, t)
        merged = _merge(o_a, o_b, proj, w_up_a_b, w_up_b_b, l, tm_p)
        xp = _out_proj(merged, w_o_b, xp, g1, ln1_g, ln1_b, l, tm_p)
        xp = _moe_grouped(xp, sh2, sc2, g2, w_router_t, b_router, w1_b, w3_b, w2_b, ln2_g, ln2_b, l, tm_p)
        outs_p["k"].append(proj[:, OFF_K:OFF_K + KV_WIDTH].reshape(bp, t, N_KV_HEADS, HEAD_DIM))
        outs_p["v"].append(proj[:, OFF_V:OFF_V + KV_WIDTH].reshape(bp, t, N_KV_HEADS, HEAD_DIM))
        outs_p["ik"].append(proj[:, OFF_IKW:OFF_IKW + IDX_DIM].reshape(bp, t, IDX_DIM))
        outs_p["s"].append(s_new)

        sh1, sc1, g1, sh2, sc2, g2 = _mods(mod[l], slice(bp, bp + db), per_row=True)
        proj = _proj(xs, sh1, sc1, w_tail, w_head, l, db)
        o_a = _attn_decode(proj, page_table, cache_k, cache_v, cache_idx_k, l)
        o_b, s_new = _hgrn_decode(proj, state_hgrn, lower_bounds[l], hgrn_norm_g[l], l)
        merged = _merge(o_a, o_b, proj, w_up_a_b, w_up_b_b, l, db)
        xs = _out_proj(merged, w_o_b, xs, g1, ln1_g, ln1_b, l, db)
        h2, gates = _router(xs, sh2, sc2, w_router_t, b_router, db)
        xs = _moe(h2, gates.T, w1_b, w3_b, w2_b, xs, g2, ln2_g, ln2_b, l, db)
        outs_s["k"].append(proj[:, OFF_K:OFF_K + KV_WIDTH].reshape(db, 1, N_KV_HEADS, HEAD_DIM))
        outs_s["v"].append(proj[:, OFF_V:OFF_V + KV_WIDTH].reshape(db, 1, N_KV_HEADS, HEAD_DIM))
        outs_s["ik"].append(proj[:, OFF_IKW:OFF_IKW + IDX_DIM].reshape(db, 1, IDX_DIM))
        outs_s["s"].append(s_new)

    return (xp.reshape(bp, t, D_MODEL), xs.reshape(db, 1, D_MODEL),
            jnp.stack(outs_p["k"]), jnp.stack(outs_p["v"]), jnp.stack(outs_p["ik"]), jnp.stack(outs_p["s"]),
            jnp.stack(outs_s["k"]), jnp.stack(outs_s["v"]), jnp.stack(outs_s["ik"]), jnp.stack(outs_s["s"]))
```

```python
import functools

import jax
import jax.numpy as jnp
from jax import lax
from jax.experimental import pallas as pl
from jax.experimental.pallas import tpu as pltpu

F32 = jnp.float32
BF16 = jnp.bfloat16
I32 = jnp.int32
I16 = jnp.int16

DEPTH = 2
D_MODEL = 2048
N_ATT_HEADS = 8
N_KV_HEADS = 2
KV_GROUP = N_ATT_HEADS // N_KV_HEADS
HEAD_DIM = 128
ATT_WIDTH = N_ATT_HEADS * HEAD_DIM
KV_WIDTH = N_KV_HEADS * HEAD_DIM
N_IDX_HEADS = 16
IDX_DIM = 64
TOPK_MAX = 256
N_HGRN_HEADS = 8
HGRN_DK = 128
HGRN_DV = 128
HGRN_WIDTH = N_HGRN_HEADS * HGRN_DK
HGRN_CHUNK = 64
HGRN_SUB = 8
N_EXPERTS = 16
N_GROUPS = 4
EXPERTS_PER_GROUP = N_EXPERTS // N_GROUPS
EXPERT_DFF = 1024
ALPHA = (2 * DEPTH) ** 0.25
LN_EPS = 1e-5
MASK_VALUE = -1e30
INT_MIN = -2 ** 31
LOG2_E = 1.4426950408889634

IN_SPLITS = (ATT_WIDTH, KV_WIDTH, KV_WIDTH, N_IDX_HEADS * IDX_DIM, IDX_DIM, N_IDX_HEADS,
             HGRN_WIDTH, HGRN_WIDTH, HGRN_WIDTH, HGRN_WIDTH, D_MODEL, D_MODEL)

LANES = 128
PROJ_TN = 1024
IN_WIDTH = sum(IN_SPLITS)
TAIL_START = ATT_WIDTH + 2 * KV_WIDTH + N_IDX_HEADS * IDX_DIM + IDX_DIM + N_IDX_HEADS
TAIL_COLS = IN_WIDTH - TAIL_START
HEAD_COLS = -(-(TAIL_START + LANES - (TAIL_START % LANES)) // PROJ_TN) * PROJ_TN
OFF_HQ = 0
OFF_HF = OFF_HQ + HGRN_WIDTH
OFF_HI = OFF_HF + HGRN_WIDTH
OFF_HG = OFF_HI + HGRN_WIDTH
OFF_GA = OFF_HG + HGRN_WIDTH
OFF_GB = OFF_GA + D_MODEL
OFF_Q = TAIL_COLS
OFF_K = OFF_Q + ATT_WIDTH
OFF_V = OFF_K + KV_WIDTH
OFF_IQ = OFF_V + KV_WIDTH
OFF_IKW = OFF_IQ + N_IDX_HEADS * IDX_DIM
IQ_BLOCK = 512
PACK_WIDTH = TAIL_COLS + HEAD_COLS

VMEM_LIMIT = 59 * 1024 * 1024


def _params(semantics):
    return pltpu.CompilerParams(dimension_semantics=semantics, vmem_limit_bytes=VMEM_LIMIT)


def _tile(n, pref):
    t = min(n, pref)
    while n % t:
        t //= 2
    return t


def _ln(x):
    mu = jnp.mean(x, axis=-1, keepdims=True)
    xc = x - mu
    var = jnp.mean(xc * xc, axis=-1, keepdims=True)
    return xc * lax.rsqrt(var + LN_EPS)


def _dot(a, b):
    return jnp.dot(a, b, preferred_element_type=F32)


def _dot_nt(a, b):
    return lax.dot_general(a, b, (((1,), (1,)), ((), ())), preferred_element_type=F32)


def _split3(x):
    x1 = x.astype(BF16)
    r1 = x - x1.astype(F32)
    x2 = r1.astype(BF16)
    x3 = (r1 - x2.astype(F32)).astype(BF16)
    return x1, x2, x3


def _sort_key(s):
    bits = pltpu.bitcast(s, I32)
    return bits ^ ((bits >> 31) & 0x7FFFFFFF)


def _kth_largest_key(count_ge, n_sel, shape):
    def body(i, ans):
        bit = lax.shift_left(jnp.int32(1), jnp.int32(31) - i)
        cand = ans | bit
        cnt = count_ge(cand ^ INT_MIN)
        return jnp.where(cnt >= n_sel, cand, ans)
    ans = lax.fori_loop(0, 32, body, jnp.zeros(shape, I32))
    return ans ^ INT_MIN


def _ada_kernel(c_ref, w_ref, b_ref, o_ref):
    c = c_ref[...]
    a = (c * jax.nn.sigmoid(c)).astype(BF16)
    o_ref[...] = _dot(a, w_ref[...].astype(BF16)) + b_ref[...]


def _ada(c_all, w_ada, b_ada):
    rows = c_all.shape[0]
    width = w_ada.shape[-1]
    tn = _tile(width, 1024)
    return pl.pallas_call(
        _ada_kernel,
        out_shape=jax.ShapeDtypeStruct((DEPTH, rows, width), F32),
        grid=(DEPTH, width // tn),
        in_specs=[pl.BlockSpec((rows, D_MODEL), lambda l, j: (0, 0)),
                  pl.BlockSpec((None, D_MODEL, tn), lambda l, j: (l, 0, j)),
                  pl.BlockSpec((None, 1, tn), lambda l, j: (l, 0, j))],
        out_specs=pl.BlockSpec((None, rows, tn), lambda l, j: (l, 0, j)),
        compiler_params=_params(("parallel", "parallel")),
        name="ada",
    )(c_all, w_ada, b_ada.reshape(DEPTH, 1, width))


def _proj_kernel(x_ref, sh_ref, sc_ref, wt_ref, wh_ref, o_ref, h_scr, *, tail_tiles):
    j = pl.program_id(1)

    @pl.when(j == 0)
    def _():
        h_scr[...] = (_ln(x_ref[...]) * (1.0 + sc_ref[0]) + sh_ref[0]).astype(BF16)

    @pl.when(j < tail_tiles)
    def _():
        o_ref[...] = _dot(h_scr[...], wt_ref[...])

    @pl.when(j >= tail_tiles)
    def _():
        o_ref[...] = _dot(h_scr[...], wh_ref[...])


def _proj(x, shift, scale, w_tail, w_head, l, tm):
    n = x.shape[0]
    tiles_per_mod = (n // tm) // shift.shape[0]
    r = shift.shape[1]
    tn = PROJ_TN
    tail_tiles = TAIL_COLS // tn
    mod_spec = pl.BlockSpec((1, r, D_MODEL), lambda i, j: (i // tiles_per_mod, 0, 0))
    return pl.pallas_call(
        functools.partial(_proj_kernel, tail_tiles=tail_tiles),
        out_shape=jax.ShapeDtypeStruct((n, PACK_WIDTH), F32),
        grid=(n // tm, PACK_WIDTH // tn),
        in_specs=[pl.BlockSpec((tm, D_MODEL), lambda i, j: (i, 0)),
                  mod_spec, mod_spec,
                  pl.BlockSpec((None, D_MODEL, tn), lambda i, j: (l, 0, jnp.minimum(j, tail_tiles - 1))),
                  pl.BlockSpec((None, D_MODEL, tn), lambda i, j: (l, 0, jnp.maximum(j - tail_tiles, 0)))],
        out_specs=pl.BlockSpec((tm, tn), lambda i, j: (i, j)),
        scratch_shapes=[pltpu.VMEM((tm, D_MODEL), BF16)],
        compiler_params=_params(("parallel", "arbitrary")),
        name="proj",
    )(x, shift, scale, w_tail, w_head)


KEY_BLOCK = 512
ROW_CHUNK = 32
ATT_ROWS = 128


def _lane_fold(x, op):
    acc = x[:, :LANES]
    for i in range(1, x.shape[1] // LANES):
        acc = op(acc, x[:, i * LANES:(i + 1) * LANES])
    return acc


def _attn_prompt_kernel(q_ref, iq0_ref, iq1_ref, iwq_ref, k_ref, v_ref, ikw_ref, o_ref,
                        kb_scr, va_scr, ikb_scr, iqs_scr, qs_scr, key_scr, h16_scr, m_scr, acc_scr, xb_scr,
                        *, tq, n_sel, col_bits):
    qi = pl.program_id(1)
    tk = min(tq, KEY_BLOCK)
    n_kb = (qi + 1) * (tq // tk)
    n_lane_tiles = tk // LANES

    @pl.when(qi == 0)
    def _():
        kb_scr[...] = k_ref[...].astype(BF16)
        for g in range(N_KV_HEADS):
            va_scr[:, 2 * g * HEAD_DIM:(2 * g + 1) * HEAD_DIM] = (
                v_ref[:, g * HEAD_DIM:(g + 1) * HEAD_DIM].astype(BF16))
            va_scr[:, (2 * g + 1) * HEAD_DIM:(2 * g + 2) * HEAD_DIM] = jnp.ones((va_scr.shape[0], HEAD_DIM), BF16)
        ikb_scr[...] = ikw_ref[:, :IDX_DIM].astype(BF16)

    iq = jnp.concatenate([iq0_ref[...], iq1_ref[...]], axis=1)
    for h in range(N_IDX_HEADS):
        iqs_scr[h * tq:(h + 1) * tq, :] = iq[:, h * IDX_DIM:(h + 1) * IDX_DIM].astype(BF16)
    q = q_ref[...] * (HEAD_DIM ** -0.5 * LOG2_E)
    for h in range(N_ATT_HEADS):
        qs_scr[h * tq:(h + 1) * tq, :] = q[:, h * HEAD_DIM:(h + 1) * HEAD_DIM].astype(BF16)
    w = iwq_ref[:, IDX_DIM:IDX_DIM + N_IDX_HEADS] * (IDX_DIM ** -0.5 * N_IDX_HEADS ** -0.5)
    row = qi * tq + lax.broadcasted_iota(I32, (tq, tk), 0)
    col0 = lax.broadcasted_iota(I32, (tq, tk), 1)

    def score_block(kb, carry):
        start = pl.multiple_of(kb * tk, tk)
        ikb = ikb_scr[pl.ds(start, tk), :]
        s = jnp.zeros((tq, tk), F32)
        for h in range(N_IDX_HEADS):
            lg = _dot_nt(iqs_scr[h * tq:(h + 1) * tq, :], ikb)
            s = s + w[:, h:h + 1] * jnp.maximum(lg, 0.0)
        s = jnp.where(kb * tk + col0 <= row, s, MASK_VALUE)
        key = _sort_key(s)
        key_scr[kb] = key
        h16_scr[kb] = (key >> 16).astype(I16)
        return carry

    lax.fori_loop(0, n_kb, score_block, 0)

    chunks = [slice(c * ROW_CHUNK, (c + 1) * ROW_CHUNK) for c in range(tq // ROW_CHUNK)]

    def count(*make_preds):
        preds = [[mp(rows) for rows in chunks] for mp in make_preds]

        def body(kb, parts):
            new = []
            for ci, rows in enumerate(chunks):
                keys = key_scr[kb, rows, :]
                for pi in range(len(preds)):
                    idx = pi * len(chunks) + ci
                    hit = jnp.where(preds[pi][ci](kb, keys), 1.0, 0.0)
                    new.append((idx, parts[idx] + _lane_fold(hit, jnp.add)))
            return tuple(p for _, p in sorted(new, key=lambda t: t[0]))

        zeros = tuple(jnp.zeros((ROW_CHUNK, LANES), F32) for _ in range(len(preds) * len(chunks)))
        parts = lax.fori_loop(0, n_kb, body, zeros)
        outs = []
        for pi in range(len(preds)):
            part = jnp.concatenate(parts[pi * len(chunks):(pi + 1) * len(chunks)], axis=0)
            outs.append(jnp.broadcast_to(jnp.sum(part, axis=1, keepdims=True), (tq, LANES)))
        return outs[0] if len(outs) == 1 else outs

    def wide(x):
        return jnp.concatenate([x] * n_lane_tiles, axis=1)

    def bcast(x, rows):
        return wide(x[rows])

    def count16_ge(cand):
        c16 = cand.astype(I16)
        cs = [wide(c16[rows]) for rows in chunks]

        def body(kb, parts):
            return tuple(p + _lane_fold(jnp.where(h16_scr[kb, rows, :] >= c, jnp.int16(1), jnp.int16(0)), jnp.add)
                         for p, c, rows in zip(parts, cs, chunks))

        parts = lax.fori_loop(0, n_kb, body, tuple(jnp.zeros((ROW_CHUNK, LANES), I16) for _ in chunks))
        part = jnp.concatenate(parts, axis=0).astype(F32)
        return jnp.broadcast_to(jnp.sum(part, axis=1, keepdims=True), (tq, LANES))

    def search16():
        u = jnp.zeros((tq, LANES), I32)
        for bit in range(15, -1, -1):
            cand = u | (1 << bit)
            u = jnp.where(count16_ge(cand - 32768) >= float(n_sel), cand, u)
        return u

    hi_p = search16() - 32768
    hi_w = wide(hi_p)

    def low_keys(kb, carry):
        key = key_scr[kb]
        hi = key >> 16
        lo = (key & 0xFFFF) - 32768
        h16_scr[kb] = jnp.where(hi > hi_w, 32767, jnp.where(hi < hi_w, -32768, lo)).astype(I16)
        return carry

    lax.fori_loop(0, n_kb, low_keys, 0)
    thr = hi_p * 65536 + search16()

    cnt_gt, cnt_ge = count(lambda rows: (lambda kb, key, c=bcast(thr, rows): key > c),
                           lambda rows: (lambda kb, key, c=bcast(thr, rows): key >= c))
    need = float(n_sel) - cnt_gt
    xb_scr[...] = jnp.full(xb_scr.shape, 2 ** 31 - 1, I32)

    @pl.when(jnp.max(jnp.abs(cnt_ge - float(n_sel))) > 0.0)
    def _():
        def body(i, x):
            cand = x | lax.shift_left(jnp.int32(1), jnp.int32(col_bits - 1) - i)

            def make_pred(rows):
                c_thr, c_cand = bcast(thr, rows), bcast(cand, rows)
                c_col = lax.broadcasted_iota(I32, (ROW_CHUNK, tk), 1)
                return lambda kb, key: (key == c_thr) & (kb * tk + c_col < c_cand)

            return jnp.where(count(make_pred) < need, cand, x)

        xb_scr[...] = lax.fori_loop(0, col_bits, body, jnp.zeros((tq, LANES), I32))

    xb_w, thr_w = wide(xb_scr[...]), wide(thr)

    def bias_block(kb, carry):
        key = key_scr[kb]
        col = kb * tk + col0
        sel = ((key > thr_w) | ((key == thr_w) & (col <= xb_w))) & (col <= row)
        key_scr[kb] = pltpu.bitcast(jnp.where(sel, 0.0, -jnp.inf), I32)
        return carry

    lax.fori_loop(0, n_kb, bias_block, 0)

    m_scr[...] = jnp.full(m_scr.shape, MASK_VALUE, F32)
    acc_scr[...] = jnp.zeros(acc_scr.shape, F32)

    def att_block(kb, carry):
        start = pl.multiple_of(kb * tk, tk)
        bias = pltpu.bitcast(key_scr[kb], F32)
        for g in range(N_KV_HEADS):
            kg = kb_scr[pl.ds(start, tk), g * HEAD_DIM:(g + 1) * HEAD_DIM]
            va = va_scr[pl.ds(start, tk), 2 * g * HEAD_DIM:(2 * g + 2) * HEAD_DIM]
            base = g * KV_GROUP * tq
            s_all = _dot_nt(qs_scr[base:base + KV_GROUP * tq, :], kg)
            ps, alphas = [], []
            for c in range(KV_GROUP * tq // ATT_ROWS):
                lo = c * ATT_ROWS
                rows = slice(base + lo, base + lo + ATT_ROWS)
                s = s_all[lo:lo + ATT_ROWS] + bias[lo % tq:lo % tq + ATT_ROWS]
                m_old = m_scr[rows, :]
                m_new = jnp.maximum(m_old, jnp.max(_lane_fold(s, jnp.maximum), axis=1, keepdims=True))
                ps.append(jnp.exp2(s - jnp.concatenate([m_new] * n_lane_tiles, axis=1)).astype(BF16))
                alphas.append(jnp.exp2(m_old - m_new))
                m_scr[rows, :] = m_new
            pv = _dot(jnp.concatenate(ps, axis=0), va)
            for c, a in enumerate(alphas):
                lo = c * ATT_ROWS
                rows = slice(base + lo, base + lo + ATT_ROWS)
                acc_scr[rows, :] = jnp.concatenate([a, a], axis=1) * acc_scr[rows, :] + pv[lo:lo + ATT_ROWS]
        return carry

    lax.fori_loop(0, n_kb, att_block, 0)
    for h in range(N_ATT_HEADS):
        acc = acc_scr[h * tq:(h + 1) * tq, :]
        o_ref[:, h * HEAD_DIM:(h + 1) * HEAD_DIM] = (acc[:, :HEAD_DIM] / acc[:, HEAD_DIM:]).astype(o_ref.dtype)


def _attn_prompt(proj, b, t):
    tq = _tile(t, 512)
    tk = min(tq, KEY_BLOCK)
    n_sel = min(TOPK_MAX, t // 4)
    proj3 = proj.reshape(b, t, PACK_WIDTH)
    kern = functools.partial(_attn_prompt_kernel, tq=tq, n_sel=n_sel, col_bits=t.bit_length())
    once = dict(pipeline_mode=pl.Buffered(1))
    out = pl.pallas_call(
        kern,
        out_shape=jax.ShapeDtypeStruct((b, t, ATT_WIDTH), BF16),
        grid=(b, t // tq),
        in_specs=[pl.BlockSpec((None, tq, ATT_WIDTH), lambda bi, qi: (bi, qi, OFF_Q // ATT_WIDTH)),
                  pl.BlockSpec((None, tq, IQ_BLOCK), lambda bi, qi: (bi, qi, OFF_IQ // IQ_BLOCK)),
                  pl.BlockSpec((None, tq, IQ_BLOCK), lambda bi, qi: (bi, qi, OFF_IQ // IQ_BLOCK + 1)),
                  pl.BlockSpec((None, tq, LANES), lambda bi, qi: (bi, qi, OFF_IKW // LANES)),
                  pl.BlockSpec((None, t, KV_WIDTH), lambda bi, qi: (bi, 0, OFF_K // KV_WIDTH), **once),
                  pl.BlockSpec((None, t, KV_WIDTH), lambda bi, qi: (bi, 0, OFF_V // KV_WIDTH), **once),
                  pl.BlockSpec((None, t, LANES), lambda bi, qi: (bi, 0, OFF_IKW // LANES), **once)],
        out_specs=pl.BlockSpec((None, tq, ATT_WIDTH), lambda bi, qi: (bi, qi, 0)),
        scratch_shapes=[pltpu.VMEM((t, KV_WIDTH), BF16),
                        pltpu.VMEM((t, 2 * KV_WIDTH), BF16),
                        pltpu.VMEM((t, IDX_DIM), BF16),
                        pltpu.VMEM((N_IDX_HEADS * tq, IDX_DIM), BF16),
                        pltpu.VMEM((N_ATT_HEADS * tq, HEAD_DIM), BF16),
                        pltpu.VMEM((t // tk, tq, tk), I32),
                        pltpu.VMEM((t // tk, tq, tk), I16),
                        pltpu.VMEM((N_ATT_HEADS * tq, LANES), F32),
                        pltpu.VMEM((N_ATT_HEADS * tq, 2 * HEAD_DIM), F32),
                        pltpu.VMEM((tq, LANES), I32)],
        compiler_params=_params(("parallel", "arbitrary")),
        name="attn_prompt",
    )(proj3, proj3, proj3, proj3, proj3, proj3, proj3)
    return out.reshape(b * t, ATT_WIDTH)


def _hgrn_gates(hq, hf, lb):
    q = hq * jax.nn.sigmoid(hq)
    f = lb + (1.0 - lb) * jax.nn.sigmoid(hf)
    kk = (1.0 - lb) * jax.nn.sigmoid(-hf)
    return q, f, kk


def _hgrn_finish(o, hg, ng):
    o = o * lax.rsqrt(jnp.mean(o * o, axis=-1, keepdims=True) + LN_EPS)
    return o * ng * (hg * jax.nn.sigmoid(hg))


HGRN_HEADS_PER_STEP = 8


def _hgrn_chunk(heads, tri, ones):
    c_len, sub = HGRN_CHUNK, HGRN_SUB
    n_sub = c_len // sub
    t_idx = lax.broadcasted_iota(I32, (sub, 1), 0)
    bs = []
    for q, f, kk, v, st in heads:
        l1, l2, l3 = _split3(jnp.log(f))
        bs.append(_dot(tri, l1) + _dot(tri, l2) + _dot(tri, l3))
    stage2 = []
    for (q, f, kk, v, st), b in zip(heads, bs):
        v_b = v.astype(BF16)
        inter = _dot_nt((q * jnp.exp(b)).astype(BF16), st.astype(BF16))
        rs, cross = [], []
        for j in range(n_sub):
            lo = j * sub
            qt, bt = q[lo:lo + sub], b[lo:lo + sub]
            ps = []
            for s in range(sub):
                valid = t_idx >= s
                dec = jnp.exp(jnp.where(valid, bt - b[lo + s:lo + s + 1, :], 0.0))
                ps.append(jnp.where(valid, qt * kk[lo + s:lo + s + 1, :] * dec, 0.0))
            rs.append(_dot(jnp.concatenate(ps, axis=0).astype(BF16), ones))
            if lo:
                b_edge = b[lo - 1:lo, :]
                qd = (qt * jnp.exp(bt - b_edge)).astype(BF16)
                kd = (kk[:lo] * jnp.exp(b_edge - b[:lo])).astype(BF16)
                cross.append(_dot_nt(qd, kd))
        b_last = b[c_len - 1:c_len, :]
        kd = (kk * jnp.exp(b_last - b)).astype(BF16)
        upd = lax.dot_general(v_b, kd, (((0,), (0,)), ((), ())), preferred_element_type=F32)
        stage2.append((v_b, inter, rs, cross, st * jnp.exp(b_last) + upd))
    out = []
    for (q, f, kk, v, st), (v_b, inter, rs, cross, st_new) in zip(heads, stage2):
        rows = []
        for j in range(n_sub):
            lo = j * sub
            acc = inter[lo:lo + sub]
            if lo:
                acc = acc + _dot(cross[j - 1].astype(BF16), v_b[:lo])
            for s in range(sub):
                acc = acc + rs[j][s * sub:(s + 1) * sub, :] * v[lo + s:lo + s + 1, :]
            rows.append(acc)
        out.append((jnp.concatenate(rows, axis=0), st_new))
    return out


def _hgrn_prompt_kernel(hq_ref, hf_ref, hi_ref, hg_ref, lb_ref, ng_ref, o_ref, s_ref, st_scr, *, n_chunks):
    c_len = HGRN_CHUNK
    ti = pl.program_id(2)

    @pl.when(ti == 0)
    def _():
        st_scr[...] = jnp.zeros(st_scr.shape, F32)

    r_i = lax.broadcasted_iota(I32, (c_len, c_len), 0)
    c_i = lax.broadcasted_iota(I32, (c_len, c_len), 1)
    tri = jnp.where(c_i <= r_i, 1.0, 0.0).astype(BF16)
    ones = jnp.ones((HGRN_DK, HGRN_DV), BF16)

    def chunk(c, carry):
        sl = pl.ds(pl.multiple_of(c * c_len, c_len), c_len)
        cols = [slice(hh * HGRN_DK, (hh + 1) * HGRN_DK) for hh in range(HGRN_HEADS_PER_STEP)]
        heads = []
        for hh, cs in enumerate(cols):
            q, f, kk = _hgrn_gates(hq_ref[sl, cs], hf_ref[sl, cs], lb_ref[:, cs])
            heads.append((q, f, kk, hi_ref[sl, cs], st_scr[hh]))
        for hh, (o, st_new) in enumerate(_hgrn_chunk(heads, tri, ones)):
            st_scr[hh] = st_new
            o_ref[sl, cols[hh]] = _hgrn_finish(o, hg_ref[sl, cols[hh]], ng_ref[:, cols[hh]]).astype(o_ref.dtype)
        return carry

    lax.fori_loop(0, n_chunks, chunk, 0)

    @pl.when(ti == pl.num_programs(2) - 1)
    def _():
        for hh in range(HGRN_HEADS_PER_STEP):
            s_ref[hh] = st_scr[hh].T


def _hgrn_prompt(proj, lb, ng, b, t):
    tc = _tile(t, 512)
    n_chunks = tc // HGRN_CHUNK
    hp = HGRN_HEADS_PER_STEP
    width = hp * HGRN_DK
    proj3 = proj.reshape(b, t, PACK_WIDTH)

    def col(off):
        return pl.BlockSpec((None, tc, width), lambda bi, h, ti: (bi, ti, off // width + h))

    vec = pl.BlockSpec((None, 1, width), lambda bi, h, ti: (h, 0, 0))
    o, s = pl.pallas_call(
        functools.partial(_hgrn_prompt_kernel, n_chunks=n_chunks),
        out_shape=(jax.ShapeDtypeStruct((b, t, HGRN_WIDTH), BF16),
                   jax.ShapeDtypeStruct((b, N_HGRN_HEADS, HGRN_DK, HGRN_DV), F32)),
        grid=(b, N_HGRN_HEADS // hp, t // tc),
        in_specs=[col(OFF_HQ), col(OFF_HF), col(OFF_HI), col(OFF_HG), vec, vec],
        out_specs=(pl.BlockSpec((None, tc, width), lambda bi, h, ti: (bi, ti, h)),
                   pl.BlockSpec((None, hp, HGRN_DK, HGRN_DV), lambda bi, h, ti: (bi, h, 0, 0))),
        scratch_shapes=[pltpu.VMEM((hp, HGRN_DV, HGRN_DK), F32)],
        compiler_params=_params(("parallel", "parallel", "arbitrary")),
        name="hgrn_prompt",
    )(proj3, proj3, proj3, proj3,
      lb.reshape(N_HGRN_HEADS // hp, 1, width), ng.reshape(N_HGRN_HEADS // hp, 1, width))
    return o.reshape(b * t, HGRN_WIDTH), s


def _merge_kernel(oa_ref, ob_ref, ga_ref, gb_ref, wa_ref, wb_ref, o_ref):
    ya = _dot(oa_ref[...].astype(BF16), wa_ref[...])
    yb = _dot(ob_ref[...].astype(BF16), wb_ref[...])
    o_ref[...] = (jax.nn.sigmoid(ga_ref[...]) * ya + jax.nn.sigmoid(gb_ref[...]) * yb).astype(o_ref.dtype)


def _merge(o_a, o_b, proj, w_up_a, w_up_b, l, tm):
    n = o_a.shape[0]
    return pl.pallas_call(
        _merge_kernel,
        out_shape=jax.ShapeDtypeStruct((n, D_MODEL), BF16),
        grid=(n // tm,),
        in_specs=[pl.BlockSpec((tm, ATT_WIDTH), lambda i: (i, 0)),
                  pl.BlockSpec((tm, HGRN_WIDTH), lambda i: (i, 0)),
                  pl.BlockSpec((tm, D_MODEL), lambda i: (i, OFF_GA // D_MODEL)),
                  pl.BlockSpec((tm, D_MODEL), lambda i: (i, OFF_GB // D_MODEL)),
                  pl.BlockSpec((None, ATT_WIDTH, D_MODEL), lambda i: (l, 0, 0)),
                  pl.BlockSpec((None, HGRN_WIDTH, D_MODEL), lambda i: (l, 0, 0))],
        out_specs=pl.BlockSpec((tm, D_MODEL), lambda i: (i, 0)),
        compiler_params=_params(("parallel",)),
        name="merge",
    )(o_a, o_b, proj, proj, w_up_a, w_up_b)


def _post_norm(x, gate, y, g, b):
    return _ln(ALPHA * x + gate * y) * g + b


def _out_kernel(m_ref, w_ref, x_ref, gate_ref, lng_ref, lnb_ref, o_ref):
    y = _dot(m_ref[...], w_ref[...])
    o_ref[...] = _post_norm(x_ref[...], gate_ref[0], y, lng_ref[...], lnb_ref[...])


def _out_proj(merged, w_o, x, gate, ln_g, ln_b, l, tm):
    n = x.shape[0]
    tiles_per_mod = (n // tm) // gate.shape[0]
    vec = pl.BlockSpec((None, 1, D_MODEL), lambda i: (l, 0, 0))
    return pl.pallas_call(
        _out_kernel,
        out_shape=jax.ShapeDtypeStruct((n, D_MODEL), F32),
        grid=(n // tm,),
        in_specs=[pl.BlockSpec((tm, D_MODEL), lambda i: (i, 0)),
                  pl.BlockSpec((None, D_MODEL, D_MODEL), lambda i: (l, 0, 0)),
                  pl.BlockSpec((tm, D_MODEL), lambda i: (i, 0)),
                  pl.BlockSpec((1, gate.shape[1], D_MODEL), lambda i: (i // tiles_per_mod, 0, 0)),
                  vec, vec],
        out_specs=pl.BlockSpec((tm, D_MODEL), lambda i: (i, 0)),
        compiler_params=_params(("parallel",)),
        name="out_proj",
    )(merged, w_o, x, gate, ln_g.reshape(DEPTH, 1, D_MODEL), ln_b.reshape(DEPTH, 1, D_MODEL))


def _route(h, wr, br):
    h1, h2, _ = _split3(h)
    w1, w2, _ = _split3(wr)
    logits = _dot_nt(w1, h1) + _dot_nt(w1, h2) + _dot_nt(w2, h1)
    aff = jax.nn.sigmoid(logits)
    sel = aff + br
    rows = [sel[e:e + 1, :] for e in range(N_EXPERTS)]
    grp = []
    for g in range(N_GROUPS):
        a, b, c, d = rows[g * EXPERTS_PER_GROUP:(g + 1) * EXPERTS_PER_GROUP]
        hi1, lo1 = jnp.maximum(a, b), jnp.minimum(a, b)
        hi2, lo2 = jnp.maximum(c, d), jnp.minimum(c, d)
        grp.append(jnp.maximum(hi1, hi2) + jnp.maximum(jnp.minimum(hi1, hi2), jnp.maximum(lo1, lo2)))
    best = functools.reduce(jnp.maximum, grp)
    taken = jnp.zeros_like(best)
    picked = []
    for g in range(N_GROUPS):
        is_g = jnp.where(grp[g] == best, 1.0, 0.0) * (1.0 - taken)
        taken = taken + is_g
        for e in range(g * EXPERTS_PER_GROUP, (g + 1) * EXPERTS_PER_GROUP):
            rank = jnp.zeros_like(best)
            for o in range(g * EXPERTS_PER_GROUP, (g + 1) * EXPERTS_PER_GROUP):
                if o < e:
                    rank = rank + jnp.where(rows[o] >= rows[e], 1.0, 0.0)
                elif o > e:
                    rank = rank + jnp.where(rows[o] > rows[e], 1.0, 0.0)
            picked.append(is_g * jnp.where(rank < 2.0, 1.0, 0.0))
    picked = jnp.concatenate(picked, axis=0)
    gate = picked * aff
    return picked, gate / jnp.sum(gate, axis=0, keepdims=True)


def _router_kernel(x_ref, sh_ref, sc_ref, wr_ref, br_ref, h_ref, g_ref):
    h = _ln(x_ref[...]) * (1.0 + sc_ref[0]) + sh_ref[0]
    h_ref[...] = h.astype(h_ref.dtype)
    _, g_ref[...] = _route(h, wr_ref[...], br_ref[...])


def _router_sorted_kernel(x_ref, sh_ref, sc_ref, wr_ref, br_ref, h_ref, eid_ref, gw_ref, pos_ref, cnt_ref, base_scr):
    tm = x_ref.shape[0]

    @pl.when(pl.program_id(0) == 0)
    def _():
        base_scr[...] = jnp.zeros(base_scr.shape, F32)

    h = _ln(x_ref[...]) * (1.0 + sc_ref[0]) + sh_ref[0]
    h_ref[...] = h.astype(h_ref.dtype)
    picked, gate = _route(h, wr_ref[...], br_ref[...])
    r = lax.broadcasted_iota(I32, (tm, tm), 0)
    c = lax.broadcasted_iota(I32, (tm, tm), 1)
    earlier = jnp.where(r < c, 1.0, 0.0).astype(BF16)
    base = base_scr[...]
    rank = _dot(picked.astype(BF16), earlier) + jnp.concatenate([base] * (tm // LANES), axis=1)
    base_scr[...] = base + jnp.sum(picked, axis=1, keepdims=True)
    cnt_ref[...] = base_scr[...]
    eio = lax.broadcasted_iota(I32, picked.shape, 0).astype(F32)
    e_lo = jnp.min(jnp.where(picked > 0.0, eio, float(N_EXPERTS)), axis=0, keepdims=True)
    e_hi = jnp.max(jnp.where(picked > 0.0, eio, -1.0), axis=0, keepdims=True)

    def pick(e, x):
        return jnp.sum(jnp.where(eio == e, x, 0.0), axis=0, keepdims=True)

    eid_ref[...] = jnp.concatenate([e_lo, e_hi], axis=0).astype(I32)
    gw_ref[...] = jnp.concatenate([pick(e_lo, gate), pick(e_hi, gate)], axis=0)
    pos_ref[...] = jnp.concatenate([pick(e_lo, rank), pick(e_hi, rank)], axis=0).astype(I32)


def _router_sorted(x, shift, scale, w_router_t, b_router, tm):
    n = x.shape[0]
    tiles_per_mod = (n // tm) // shift.shape[0]
    mod_spec = pl.BlockSpec((1, shift.shape[1], D_MODEL), lambda i: (i // tiles_per_mod, 0, 0))
    pair = pl.BlockSpec((2, tm), lambda i: (0, i))
    return pl.pallas_call(
        _router_sorted_kernel,
        out_shape=(jax.ShapeDtypeStruct((n, D_MODEL), F32),
                   jax.ShapeDtypeStruct((2, n), I32),
                   jax.ShapeDtypeStruct((2, n), F32),
                   jax.ShapeDtypeStruct((2, n), I32),
                   jax.ShapeDtypeStruct((N_EXPERTS, LANES), F32)),
        grid=(n // tm,),
        in_specs=[pl.BlockSpec((tm, D_MODEL), lambda i: (i, 0)),
                  mod_spec, mod_spec,
                  pl.BlockSpec((N_EXPERTS, D_MODEL), lambda i: (0, 0)),
                  pl.BlockSpec((N_EXPERTS, 1), lambda i: (0, 0))],
        out_specs=(pl.BlockSpec((tm, D_MODEL), lambda i: (i, 0)), pair, pair, pair,
                   pl.BlockSpec((N_EXPERTS, LANES), lambda i: (0, 0))),
        scratch_shapes=[pltpu.VMEM((N_EXPERTS, LANES), F32)],
        compiler_params=_params(("arbitrary",)),
        name="router_sorted",
    )(x, shift, scale, w_router_t, b_router.reshape(N_EXPERTS, 1))


MOE_TILE = 256
DMA_UNROLL = 32


def _dispatch_kernel(dst_ref, ends_ref, h_ref, xs_ref, zero_scr, sem, zsem, *, td, n):
    first = pl.program_id(0) * td

    @pl.when(pl.program_id(0) == 0)
    def _():
        zero_scr[...] = jnp.zeros(zero_scr.shape, F32)
        n_slots = xs_ref.shape[0]

        def zero_copy(start):
            return pltpu.make_async_copy(zero_scr, xs_ref.at[pl.ds(pl.multiple_of(start, MOE_TILE), MOE_TILE)], zsem)

        jobs = []
        for e in range(N_EXPERTS):
            jobs.append((ends_ref[e + 1] > ends_ref[e], ends_ref[e + 1] - MOE_TILE))
            unused = ends_ref[N_EXPERTS] + e * MOE_TILE
            jobs.append((unused < n_slots, jnp.minimum(unused, n_slots - MOE_TILE)))
        for wanted, start in jobs:
            @pl.when(wanted)
            def _(start=start):
                zero_copy(start).start()
        for wanted, start in jobs:
            @pl.when(wanted)
            def _(start=start):
                zero_copy(start).wait()

    def copies(r):
        return [pltpu.make_async_copy(h_ref.at[pl.ds(r, 1)], xs_ref.at[pl.ds(dst_ref[k * n + first + r], 1)], sem)
                for k in range(2)]

    def start(r, carry):
        for cp in copies(r):
            cp.start()
        return carry

    def wait(r, carry):
        for cp in copies(r):
            cp.wait()
        return carry

    lax.fori_loop(0, td, start, 0, unroll=DMA_UNROLL)
    lax.fori_loop(0, td, wait, 0, unroll=DMA_UNROLL)


def _dispatch(h, dst_flat, seg_ends, n_slots, td):
    n = h.shape[0]
    grid_spec = pltpu.PrefetchScalarGridSpec(
        num_scalar_prefetch=2,
        grid=(n // td,),
        in_specs=[pl.BlockSpec((td, D_MODEL), lambda i, dst, ends: (i, 0))],
        out_specs=pl.BlockSpec(memory_space=pl.ANY),
        scratch_shapes=[pltpu.VMEM((MOE_TILE, D_MODEL), F32),
                        pltpu.SemaphoreType.DMA,
                        pltpu.SemaphoreType.DMA])
    return pl.pallas_call(
        functools.partial(_dispatch_kernel, td=td, n=n),
        out_shape=jax.ShapeDtypeStruct((n_slots, D_MODEL), F32),
        grid_spec=grid_spec,
        compiler_params=_params(("arbitrary",)),
        name="moe_dispatch",
    )(dst_flat, seg_ends, h)


def _experts_kernel(te_ref, tv_ref, xs_ref, w1_ref, w3_ref, w2_ref, ys_ref):
    del te_ref
    valid = tv_ref[pl.program_id(0)] > 0

    @pl.when(valid)
    def _():
        x = xs_ref[...].astype(BF16)
        a = _dot(x, w1_ref[...])
        b = _dot(x, w3_ref[...])
        ys_ref[...] = _dot((a * jax.nn.sigmoid(a) * b).astype(BF16), w2_ref[...])

    @pl.when(jnp.logical_not(valid))
    def _():
        ys_ref[...] = jnp.zeros(ys_ref.shape, F32)


def _experts(xs, tile_expert, tile_valid, w1, w3, w2, l):
    n_tiles = xs.shape[0] // MOE_TILE
    grid_spec = pltpu.PrefetchScalarGridSpec(
        num_scalar_prefetch=2,
        grid=(n_tiles,),
        in_specs=[pl.BlockSpec((MOE_TILE, D_MODEL), lambda i, te, tv: (jnp.minimum(i, tv[0] - 1), 0)),
                  pl.BlockSpec((None, None, D_MODEL, EXPERT_DFF), lambda i, te, tv: (l, te[i], 0, 0)),
                  pl.BlockSpec((None, None, D_MODEL, EXPERT_DFF), lambda i, te, tv: (l, te[i], 0, 0)),
                  pl.BlockSpec((None, None, EXPERT_DFF, D_MODEL), lambda i, te, tv: (l, te[i], 0, 0))],
        out_specs=pl.BlockSpec((MOE_TILE, D_MODEL), lambda i, te, tv: (i, 0)))
    return pl.pallas_call(
        _experts_kernel,
        out_shape=jax.ShapeDtypeStruct(xs.shape, F32),
        grid_spec=grid_spec,
        compiler_params=_params(("arbitrary",)),
        name="moe_experts",
    )(tile_expert, tile_valid, xs, w1, w3, w2)


def _combine_kernel(dst_ref, ys_ref, gw_ref, x_ref, gate_ref, lng_ref, lnb_ref, o_ref, buf, sem, *, tc, n):
    i = pl.program_id(0)
    slot = i % 2

    def copies(tile, s, r):
        return [pltpu.make_async_copy(ys_ref.at[pl.ds(dst_ref[k * n + tile * tc + r], 1)],
                                      buf.at[s, k, pl.ds(r, 1)], sem.at[s]) for k in range(2)]

    def start_tile(tile, s):
        def body(r, carry):
            for cp in copies(tile, s, r):
                cp.start()
            return carry
        lax.fori_loop(0, tc, body, 0, unroll=DMA_UNROLL)

    def wait_tile(tile, s):
        def body(r, carry):
            for cp in copies(tile, s, r):
                cp.wait()
            return carry
        lax.fori_loop(0, tc, body, 0, unroll=DMA_UNROLL)

    @pl.when(i == 0)
    def _():
        start_tile(0, 0)

    @pl.when(i + 1 < pl.num_programs(0))
    def _():
        start_tile(i + 1, 1 - slot)

    wait_tile(i, slot)
    gw = gw_ref[...]
    y = gw[:, 0:1] * buf[slot, 0] + gw[:, 1:2] * buf[slot, 1]
    o_ref[...] = _post_norm(x_ref[...], gate_ref[0], y, lng_ref[...], lnb_ref[...])


def _combine(ys, dst_flat, gw, x, gate, ln_g, ln_b, l, tc):
    n = x.shape[0]
    tiles_per_mod = (n // tc) // gate.shape[0]
    vec = pl.BlockSpec((None, 1, D_MODEL), lambda i, dst: (l, 0, 0))
    grid_spec = pltpu.PrefetchScalarGridSpec(
        num_scalar_prefetch=1,
        grid=(n // tc,),
        in_specs=[pl.BlockSpec(memory_space=pl.ANY),
                  pl.BlockSpec((tc, 2), lambda i, dst: (i, 0)),
                  pl.BlockSpec((tc, D_MODEL), lambda i, dst: (i, 0)),
                  pl.BlockSpec((1, gate.shape[1], D_MODEL), lambda i, dst: (i // tiles_per_mod, 0, 0)),
                  vec, vec],
        out_specs=pl.BlockSpec((tc, D_MODEL), lambda i, dst: (i, 0)),
        scratch_shapes=[pltpu.VMEM((2, 2, tc, D_MODEL), F32),
                        pltpu.SemaphoreType.DMA((2,))])
    return pl.pallas_call(
        functools.partial(_combine_kernel, tc=tc, n=n),
        out_shape=jax.ShapeDtypeStruct((n, D_MODEL), F32),
        grid_spec=grid_spec,
        compiler_params=_params(("arbitrary",)),
        name="moe_combine",
    )(dst_flat, ys, gw, x, gate, ln_g.reshape(DEPTH, 1, D_MODEL), ln_b.reshape(DEPTH, 1, D_MODEL))


def _moe_grouped(x, shift, scale, gate, w_router_t, b_router, w1, w3, w2, ln_g, ln_b, l, tm):
    n = x.shape[0]
    h, eid, gw, pos, cnt = _router_sorted(x, shift, scale, w_router_t, b_router, tm)
    counts = cnt[:, 0].astype(I32)
    padded = (counts + MOE_TILE - 1) // MOE_TILE * MOE_TILE
    ends = jnp.cumsum(padded)
    experts = jnp.arange(N_EXPERTS, dtype=I32)[:, None, None]
    seg_start = jnp.sum(jnp.where(eid[None] == experts, (ends - padded)[:, None, None], 0), axis=0)
    dst_flat = (seg_start + pos).reshape(2 * n)
    n_tiles = 2 * n // MOE_TILE + N_EXPERTS
    tile_start = jnp.arange(n_tiles, dtype=I32) * MOE_TILE
    tile_expert = jnp.minimum(jnp.sum((tile_start[:, None] >= ends[None, :]).astype(I32), axis=1), N_EXPERTS - 1)
    tile_valid = jnp.where(tile_start < ends[-1], ends[-1] // MOE_TILE, 0).astype(I32)
    seg_ends = jnp.concatenate([jnp.zeros((1,), I32), ends.astype(I32)])
    xs = _dispatch(h, dst_flat, seg_ends, n_tiles * MOE_TILE, tm)
    ys = _experts(xs, tile_expert, tile_valid, w1, w3, w2, l)
    return _combine(ys, dst_flat, gw.T, x, gate, ln_g, ln_b, l, MOE_TILE)


def _router(x, shift, scale, w_router_t, b_router, tm):
    n = x.shape[0]
    tiles_per_mod = (n // tm) // shift.shape[0]
    mod_spec = pl.BlockSpec((1, shift.shape[1], D_MODEL), lambda i: (i // tiles_per_mod, 0, 0))
    return pl.pallas_call(
        _router_kernel,
        out_shape=(jax.ShapeDtypeStruct((n, D_MODEL), BF16),
                   jax.ShapeDtypeStruct((N_EXPERTS, n), F32)),
        grid=(n // tm,),
        in_specs=[pl.BlockSpec((tm, D_MODEL), lambda i: (i, 0)),
                  mod_spec, mod_spec,
                  pl.BlockSpec((N_EXPERTS, D_MODEL), lambda i: (0, 0)),
                  pl.BlockSpec((N_EXPERTS, 1), lambda i: (0, 0))],
        out_specs=(pl.BlockSpec((tm, D_MODEL), lambda i: (i, 0)),
                   pl.BlockSpec((N_EXPERTS, tm), lambda i: (0, i))),
        compiler_params=_params(("parallel",)),
        name="router",
    )(x, shift, scale, w_router_t, b_router.reshape(N_EXPERTS, 1))


def _moe_kernel(h_ref, g_ref, w1_ref, w3_ref, w2_ref, x_ref, gate_ref, lng_ref, lnb_ref, o_ref, acc_scr):
    e = pl.program_id(1)

    @pl.when(e == 0)
    def _():
        acc_scr[...] = jnp.zeros(acc_scr.shape, F32)

    h = h_ref[...]
    a = _dot(h, w1_ref[...])
    b = _dot(h, w3_ref[...])
    gates = g_ref[...]
    lane = lax.broadcasted_iota(I32, gates.shape, 1)
    gcol = jnp.sum(jnp.where(lane == e, gates, 0.0), axis=1, keepdims=True)
    u = (a * jax.nn.sigmoid(a) * b * gcol).astype(BF16)
    acc_scr[...] += _dot(u, w2_ref[...])

    @pl.when(e == pl.num_programs(1) - 1)
    def _():
        o_ref[...] = _post_norm(x_ref[...], gate_ref[0], acc_scr[...], lng_ref[...], lnb_ref[...])


def _moe(h, gates, w1, w3, w2, x, gate, ln_g, ln_b, l, tm):
    n = x.shape[0]
    tiles_per_mod = (n // tm) // gate.shape[0]
    vec = pl.BlockSpec((None, 1, D_MODEL), lambda i, e: (l, 0, 0))
    return pl.pallas_call(
        _moe_kernel,
        out_shape=jax.ShapeDtypeStruct((n, D_MODEL), F32),
        grid=(n // tm, N_EXPERTS),
        in_specs=[pl.BlockSpec((tm, D_MODEL), lambda i, e: (i, 0)),
                  pl.BlockSpec((tm, N_EXPERTS), lambda i, e: (i, 0)),
                  pl.BlockSpec((None, None, D_MODEL, EXPERT_DFF), lambda i, e: (l, e, 0, 0)),
                  pl.BlockSpec((None, None, D_MODEL, EXPERT_DFF), lambda i, e: (l, e, 0, 0)),
                  pl.BlockSpec((None, None, EXPERT_DFF, D_MODEL), lambda i, e: (l, e, 0, 0)),
                  pl.BlockSpec((tm, D_MODEL), lambda i, e: (i, 0)),
                  pl.BlockSpec((1, gate.shape[1], D_MODEL), lambda i, e: (i // tiles_per_mod, 0, 0)),
                  vec, vec],
        out_specs=pl.BlockSpec((tm, D_MODEL), lambda i, e: (i, 0)),
        scratch_shapes=[pltpu.VMEM((tm, D_MODEL), F32)],
        compiler_params=_params(("parallel", "arbitrary")),
        name="moe",
    )(h, gates, w1, w3, w2, x, gate, ln_g.reshape(DEPTH, 1, D_MODEL), ln_b.reshape(DEPTH, 1, D_MODEL))


def _row_to_col(v):
    n = v.shape[1]
    r = lax.broadcasted_iota(I32, (n, n), 0)
    c = lax.broadcasted_iota(I32, (n, n), 1)
    return jnp.sum(jnp.where(r == c, jnp.broadcast_to(v, (n, n)), 0.0), axis=1, keepdims=True)


def _attn_decode_kernel(pt_ref, proj_ref, ck_ref, cv_ref, cik_ref, o_ref,
                        ik_buf, k_buf, v_buf, key_scr, xb_scr, sem_ik, sem_k, sem_v,
                        *, l, n_pages, page, chunk_pages, n_sel):
    b = pl.program_id(0)
    n_chunks = n_pages // chunk_pages
    chunk = chunk_pages * page

    def ik_copy(j):
        return pltpu.make_async_copy(cik_ref.at[l, pt_ref[b, j]], ik_buf.at[pl.ds(j * page, page)], sem_ik)

    def kv_copies(c, j, slot):
        pg = pt_ref[b, c * chunk_pages + j]
        dst = pl.ds(j * page, page)
        copies = []
        for g in range(N_KV_HEADS):
            copies.append(pltpu.make_async_copy(ck_ref.at[l, pg, :, g, :], k_buf.at[slot, g, dst], sem_k.at[slot]))
            copies.append(pltpu.make_async_copy(cv_ref.at[l, pg, :, g, :], v_buf.at[slot, g, dst], sem_v.at[slot]))
        return copies

    def start_chunk(c, slot):
        def body(j, carry):
            for cp in kv_copies(c, j, slot):
                cp.start()
            return carry
        lax.fori_loop(0, chunk_pages, body, 0)

    def wait_chunk(c, slot):
        def body(j, carry):
            for cp in kv_copies(c, j, slot):
                cp.wait()
            return carry
        lax.fori_loop(0, chunk_pages, body, 0)

    def start_ik(j, carry):
        ik_copy(j).start()
        return carry

    def wait_ik(j, carry):
        ik_copy(j).wait()
        return carry

    lax.fori_loop(0, n_pages, start_ik, 0)
    start_chunk(0, 0)
    lax.fori_loop(0, n_pages, wait_ik, 0)

    rowsl = slice(None)
    iq = proj_ref[rowsl, OFF_IQ:OFF_IQ + N_IDX_HEADS * IDX_DIM]
    iq_h = jnp.concatenate([iq[:, h * IDX_DIM:(h + 1) * IDX_DIM] for h in range(N_IDX_HEADS)], axis=0)
    w_row = proj_ref[rowsl, OFF_IKW + IDX_DIM:OFF_IKW + IDX_DIM + N_IDX_HEADS]
    w_col = _row_to_col(w_row) * (IDX_DIM ** -0.5 * N_IDX_HEADS ** -0.5)
    ik_new = proj_ref[rowsl, OFF_IKW:OFF_IKW + IDX_DIM]

    iq_b = iq_h.astype(BF16)
    for c in range(n_chunks):
        lg = _dot_nt(iq_b, ik_buf[c * chunk:(c + 1) * chunk, :].astype(BF16))
        s = jnp.sum(w_col * jnp.maximum(lg, 0.0), axis=0, keepdims=True)
        key_scr[c] = _sort_key(s)
    lg_new = jnp.sum(iq_b.astype(F32) * ik_new.astype(BF16).astype(F32), axis=1, keepdims=True)
    key_new = _sort_key(jnp.sum(w_col * jnp.maximum(lg_new, 0.0), axis=0, keepdims=True))

    def count_ge(cand):
        cnt = jnp.where(key_new >= cand, 1.0, 0.0)
        for c in range(n_chunks):
            cnt = cnt + jnp.sum(jnp.where(key_scr[c] >= cand, 1.0, 0.0), axis=1, keepdims=True)
        return cnt

    thr = _kth_largest_key(count_ge, float(n_sel), (1, 1))

    past = n_pages * page
    lane_pos = lax.broadcasted_iota(I32, (1, chunk), 1)

    def count(pred):
        cnt = jnp.where(pred(key_new, past), 1.0, 0.0)
        for c in range(n_chunks):
            cnt = cnt + jnp.sum(jnp.where(pred(key_scr[c], c * chunk + lane_pos), 1.0, 0.0), axis=1, keepdims=True)
        return cnt

    need = float(n_sel) - count(lambda key, pos: key > thr)
    xb_scr[...] = jnp.full(xb_scr.shape, 2 ** 31 - 1, I32)

    @pl.when(jnp.max(jnp.abs(count(lambda key, pos: key >= thr) - float(n_sel))) > 0.0)
    def _():
        x = jnp.zeros((1, 1), I32)
        for bit in range((past + 1).bit_length() - 1, -1, -1):
            cand = x | (1 << bit)
            x = jnp.where(count(lambda key, pos: (key == thr) & (pos < cand)) < need, cand, x)
        xb_scr[...] = jnp.broadcast_to(x, xb_scr.shape)

    xb = xb_scr[:, 0:1]

    def selected(key, pos):
        return (key > thr) | ((key == thr) & (pos <= xb))

    q = proj_ref[rowsl, OFF_Q:OFF_Q + ATT_WIDTH]
    q_h = jnp.concatenate([q[:, h * HEAD_DIM:(h + 1) * HEAD_DIM] for h in range(N_ATT_HEADS)], axis=0)
    q_b = q_h.astype(BF16)
    k_new = proj_ref[rowsl, OFF_K:OFF_K + KV_WIDTH]
    v_new = proj_ref[rowsl, OFF_V:OFF_V + KV_WIDTH]
    scale = HEAD_DIM ** -0.5

    def att_chunk(c, carry):
        slot = c % 2

        @pl.when(c + 1 < n_chunks)
        def _():
            start_chunk(c + 1, 1 - slot)

        wait_chunk(c, slot)
        sel = selected(key_scr[c], c * chunk + lane_pos)
        out = []
        for g in range(N_KV_HEADS):
            m_old, l_old, acc_old = carry[g]
            kg = k_buf[slot, g].astype(BF16)
            vg = v_buf[slot, g].astype(BF16)
            s = _dot_nt(q_b[g * KV_GROUP:(g + 1) * KV_GROUP], kg) * scale
            s = jnp.where(sel, s, MASK_VALUE)
            m_new = jnp.maximum(m_old, jnp.max(s, axis=1, keepdims=True))
            p = jnp.where(sel, jnp.exp(s - m_new), 0.0)
            a = jnp.exp(m_old - m_new)
            out.append((m_new, a * l_old + jnp.sum(p, axis=1, keepdims=True),
                        a * acc_old + _dot(p.astype(BF16), vg)))
        return tuple(out)

    init = tuple((jnp.full((KV_GROUP, 1), MASK_VALUE, F32), jnp.zeros((KV_GROUP, 1), F32),
                  jnp.zeros((KV_GROUP, HEAD_DIM), F32)) for _ in range(N_KV_HEADS))
    res = lax.fori_loop(0, n_chunks, att_chunk, init)

    sel_new = selected(key_new, past)
    for g in range(N_KV_HEADS):
        m_old, l_old, acc_old = res[g]
        kg = k_new[:, g * HEAD_DIM:(g + 1) * HEAD_DIM].astype(BF16).astype(F32)
        vg = v_new[:, g * HEAD_DIM:(g + 1) * HEAD_DIM].astype(BF16).astype(F32)
        qg = q_b[g * KV_GROUP:(g + 1) * KV_GROUP].astype(F32)
        s = jnp.sum(qg * kg, axis=1, keepdims=True) * scale
        s = jnp.where(sel_new, s, MASK_VALUE)
        m_new = jnp.maximum(m_old, s)
        p = jnp.where(sel_new, jnp.exp(s - m_new), 0.0)
        a = jnp.exp(m_old - m_new)
        l_new = a * l_old + p
        acc = a * acc_old + p.astype(BF16).astype(F32) * vg
        o = acc / l_new
        for r in range(KV_GROUP):
            hh = g * KV_GROUP + r
            o_ref[rowsl, hh * HEAD_DIM:(hh + 1) * HEAD_DIM] = o[r:r + 1, :].astype(o_ref.dtype)


def _attn_decode(proj, page_table, cache_k, cache_v, cache_idx_k, l):
    db, n_pages = page_table.shape
    page = cache_k.shape[2]
    past = n_pages * page
    n_sel = min(TOPK_MAX, (past + 1) // 4)
    chunk_pages = _tile(n_pages, 16)
    kern = functools.partial(_attn_decode_kernel, l=l, n_pages=n_pages, page=page,
                             chunk_pages=chunk_pages, n_sel=n_sel)
    grid_spec = pltpu.PrefetchScalarGridSpec(
        num_scalar_prefetch=1,
        grid=(db,),
        in_specs=[pl.BlockSpec((None, 1, PACK_WIDTH), lambda b, pt: (b, 0, 0)),
                  pl.BlockSpec(memory_space=pl.ANY),
                  pl.BlockSpec(memory_space=pl.ANY),
                  pl.BlockSpec(memory_space=pl.ANY)],
        out_specs=pl.BlockSpec((None, 1, ATT_WIDTH), lambda b, pt: (b, 0, 0)),
        scratch_shapes=[pltpu.VMEM((past, IDX_DIM), F32),
                        pltpu.VMEM((2, N_KV_HEADS, chunk_pages * page, HEAD_DIM), F32),
                        pltpu.VMEM((2, N_KV_HEADS, chunk_pages * page, HEAD_DIM), F32),
                        pltpu.VMEM((n_pages // chunk_pages, 1, chunk_pages * page), I32),
                        pltpu.VMEM((1, LANES), I32),
                        pltpu.SemaphoreType.DMA,
                        pltpu.SemaphoreType.DMA((2,)),
                        pltpu.SemaphoreType.DMA((2,))])
    return pl.pallas_call(
        kern,
        out_shape=jax.ShapeDtypeStruct((db, 1, ATT_WIDTH), F32),
        grid_spec=grid_spec,
        compiler_params=_params(("arbitrary",)),
        name="attn_decode",
    )(page_table, proj.reshape(db, 1, PACK_WIDTH), cache_k, cache_v, cache_idx_k).reshape(db, ATT_WIDTH)


def _hgrn_decode_kernel(proj_ref, s0_ref, lb_ref, ng_ref, o_ref, s_ref):
    rowsl = slice(None)
    for h in range(N_HGRN_HEADS):
        cols = lambda off: slice(off + h * HGRN_DK, off + (h + 1) * HGRN_DK)
        lb = lb_ref[:, h * HGRN_DK:(h + 1) * HGRN_DK]
        q, f, kk = _hgrn_gates(proj_ref[rowsl, cols(OFF_HQ)], proj_ref[rowsl, cols(OFF_HF)], lb)
        v = proj_ref[rowsl, cols(OFF_HI)]
        s_new = _row_to_col(f) * s0_ref[h] + _row_to_col(kk) * v
        s_ref[h] = s_new
        o = jnp.sum(_row_to_col(q) * s_new, axis=0, keepdims=True)
        ng = ng_ref[:, h * HGRN_DV:(h + 1) * HGRN_DV]
        o_ref[rowsl, h * HGRN_DV:(h + 1) * HGRN_DV] = _hgrn_finish(
            o, proj_ref[rowsl, cols(OFF_HG)], ng).astype(o_ref.dtype)


def _hgrn_decode(proj, state, lb, ng, l):
    db = proj.shape[0]
    st_spec_in = pl.BlockSpec((None, None, N_HGRN_HEADS, HGRN_DK, HGRN_DV), lambda b: (l, b, 0, 0, 0))
    o, s = pl.pallas_call(
        _hgrn_decode_kernel,
        out_shape=(jax.ShapeDtypeStruct((db, 1, HGRN_WIDTH), F32),
                   jax.ShapeDtypeStruct((db, N_HGRN_HEADS, HGRN_DK, HGRN_DV), F32)),
        grid=(db,),
        in_specs=[pl.BlockSpec((None, 1, PACK_WIDTH), lambda b: (b, 0, 0)),
                  st_spec_in,
                  pl.BlockSpec((1, HGRN_WIDTH), lambda b: (0, 0)),
                  pl.BlockSpec((1, HGRN_WIDTH), lambda b: (0, 0))],
        out_specs=(pl.BlockSpec((None, 1, HGRN_WIDTH), lambda b: (b, 0, 0)),
                   pl.BlockSpec((None, N_HGRN_HEADS, HGRN_DK, HGRN_DV), lambda b: (b, 0, 0, 0))),
        compiler_params=_params(("parallel",)),
        name="hgrn_decode",
    )(proj.reshape(db, 1, PACK_WIDTH), state, lb.reshape(1, HGRN_WIDTH), ng.reshape(1, HGRN_WIDTH))
    return o.reshape(db, HGRN_WIDTH), s


def _split_w_in(w_in):
    return w_in[:, :, TAIL_START:].astype(BF16), w_in[:, :, :HEAD_COLS].astype(BF16)


def _mods(mod_l, rows, per_row):
    m = mod_l[rows]
    parts = jnp.split(m, 6, axis=-1)
    if per_row:
        return [p[None, :, :] for p in parts]
    return [p[:, None, :] for p in parts]


def kernel(x_prompt, x_sample, c_prompt, c_sample, cache_k, cache_v, cache_idx_k, state_hgrn, page_table,
           w_ada, b_ada, w_in, w_up_a, w_up_b, w_o, hgrn_norm_g, hgrn_lb_logits, ln1_g, ln1_b,
           w_router, b_router, w1, w3, w2, ln2_g, ln2_b):
    bp, t, _ = x_prompt.shape
    db = x_sample.shape[0]
    lbp = jax.nn.softmax(hgrn_lb_logits.astype(F32), axis=0)
    lower_bounds = jnp.cumsum(lbp, axis=0) - lbp[0]

    w_tail, w_head = _split_w_in(w_in)
    w_up_a_b, w_up_b_b, w_o_b = w_up_a.astype(BF16), w_up_b.astype(BF16), w_o.astype(BF16)
    w1_b, w3_b, w2_b = w1.astype(BF16), w3.astype(BF16), w2.astype(BF16)
    w_router_t = w_router.T

    n_c = bp + db
    c_rows = -(-n_c // 8) * 8
    c_all = jnp.concatenate([c_prompt, c_sample, jnp.zeros((c_rows - n_c, D_MODEL), F32)], axis=0)
    mod = _ada(c_all, w_ada, b_ada)

    xp = x_prompt.reshape(bp * t, D_MODEL)
    xs = x_sample.reshape(db, D_MODEL)
    tm_p = _tile(bp * t, 512)
    outs_p = {"k": [], "v": [], "ik": [], "s": []}
    outs_s = {"k": [], "v": [], "ik": [], "s": []}
    for l in range(DEPTH):
        sh1, sc1, g1, sh2, sc2, g2 = _mods(mod[l], slice(0, bp), per_row=False)
        proj = _proj(xp, sh1, sc1, w_tail, w_head, l, _tile(t, 1024))
        o_a = _attn_prompt(proj, bp, t)
        o_b, s_new = _hgrn_prompt(proj, lower_bounds[l], hgrn_norm_g[l], bp, t)
        merged = _merge(o_a, o_b, proj, w_up_a_b, w_up_b_b, l, tm_p)
        xp = _out_proj(merged, w_o_b, xp, g1, ln1_g, ln1_b, l, tm_p)
        xp = _moe_grouped(xp, sh2, sc2, g2, w_router_t, b_router, w1_b, w3_b, w2_b, ln2_g, ln2_b, l, tm_p)
        outs_p["k"].append(proj[:, OFF_K:OFF_K + KV_WIDTH].reshape(bp, t, N_KV_HEADS, HEAD_DIM))
        outs_p["v"].append(proj[:, OFF_V:OFF_V + KV_WIDTH].reshape(bp, t, N_KV_HEADS, HEAD_DIM))
        outs_p["ik"].append(proj[:, OFF_IKW:OFF_IKW + IDX_DIM].reshape(bp, t, IDX_DIM))
        outs_p["s"].append(s_new)

        sh1, sc1, g1, sh2, sc2, g2 = _mods(mod[l], slice(bp, bp + db), per_row=True)
        proj = _proj(xs, sh1, sc1, w_tail, w_head, l, db)
        o_a = _attn_decode(proj, page_table, cache_k, cache_v, cache_idx_k, l)
        o_b, s_new = _hgrn_decode(proj, state_hgrn, lower_bounds[l], hgrn_norm_g[l], l)
        merged = _merge(o_a, o_b, proj, w_up_a_b, w_up_b_b, l, db)
        xs = _out_proj(merged, w_o_b, xs, g1, ln1_g, ln1_b, l, db)
        h2, gates = _router(xs, sh2, sc2, w_router_t, b_router, db)
        xs = _moe(h2, gates.T, w1_b, w3_b, w2_b, xs, g2, ln2_g, ln2_b, l, db)
        outs_s["k"].append(proj[:, OFF_K:OFF_K + KV_WIDTH].reshape(db, 1, N_KV_HEADS, HEAD_DIM))
        outs_s["v"].append(proj[:, OFF_V:OFF_V + KV_WIDTH].reshape(db, 1, N_KV_HEADS, HEAD_DIM))
        outs_s["ik"].append(proj[:, OFF_IKW:OFF_IKW + IDX_DIM].reshape(db, 1, IDX_DIM))
        outs_s["s"].append(s_new)

    return (xp.reshape(bp, t, D_MODEL), xs.reshape(db, 1, D_MODEL),
            jnp.stack(outs_p["k"]), jnp.stack(outs_p["v"]), jnp.stack(outs_p["ik"]), jnp.stack(outs_p["s"]),
            jnp.stack(outs_s["k"]), jnp.stack(outs_s["v"]), jnp.stack(outs_s["ik"]), jnp.stack(outs_s["s"]))
```

```python
import functools

import jax
import jax.numpy as jnp
from jax import lax
from jax.experimental import pallas as pl
from jax.experimental.pallas import tpu as pltpu

F32 = jnp.float32
BF16 = jnp.bfloat16
I32 = jnp.int32
I16 = jnp.int16

DEPTH = 2
D_MODEL = 2048
N_ATT_HEADS = 8
N_KV_HEADS = 2
KV_GROUP = N_ATT_HEADS // N_KV_HEADS
HEAD_DIM = 128
ATT_WIDTH = N_ATT_HEADS * HEAD_DIM
KV_WIDTH = N_KV_HEADS * HEAD_DIM
N_IDX_HEADS = 16
IDX_DIM = 64
TOPK_MAX = 256
N_HGRN_HEADS = 8
HGRN_DK = 128
HGRN_DV = 128
HGRN_WIDTH = N_HGRN_HEADS * HGRN_DK
HGRN_CHUNK = 64
HGRN_SUB = 8
N_EXPERTS = 16
N_GROUPS = 4
EXPERTS_PER_GROUP = N_EXPERTS // N_GROUPS
EXPERT_DFF = 1024
ALPHA = (2 * DEPTH) ** 0.25
LN_EPS = 1e-5
MASK_VALUE = -1e30
INT_MIN = -2 ** 31
LOG2_E = 1.4426950408889634

IN_SPLITS = (ATT_WIDTH, KV_WIDTH, KV_WIDTH, N_IDX_HEADS * IDX_DIM, IDX_DIM, N_IDX_HEADS,
             HGRN_WIDTH, HGRN_WIDTH, HGRN_WIDTH, HGRN_WIDTH, D_MODEL, D_MODEL)

LANES = 128
PROJ_TN = 1024
IN_WIDTH = sum(IN_SPLITS)
TAIL_START = ATT_WIDTH + 2 * KV_WIDTH + N_IDX_HEADS * IDX_DIM + IDX_DIM + N_IDX_HEADS
TAIL_COLS = IN_WIDTH - TAIL_START
HEAD_COLS = -(-(TAIL_START + LANES - (TAIL_START % LANES)) // PROJ_TN) * PROJ_TN
OFF_HQ = 0
OFF_HF = OFF_HQ + HGRN_WIDTH
OFF_HI = OFF_HF + HGRN_WIDTH
OFF_HG = OFF_HI + HGRN_WIDTH
OFF_GA = OFF_HG + HGRN_WIDTH
OFF_GB = OFF_GA + D_MODEL
OFF_Q = TAIL_COLS
OFF_K = OFF_Q + ATT_WIDTH
OFF_V = OFF_K + KV_WIDTH
OFF_IQ = OFF_V + KV_WIDTH
OFF_IKW = OFF_IQ + N_IDX_HEADS * IDX_DIM
IQ_BLOCK = 512
PACK_WIDTH = TAIL_COLS + HEAD_COLS

VMEM_LIMIT = 59 * 1024 * 1024


def _params(semantics):
    return pltpu.CompilerParams(dimension_semantics=semantics, vmem_limit_bytes=VMEM_LIMIT)


def _tile(n, pref):
    t = min(n, pref)
    while n % t:
        t //= 2
    return t


def _ln(x):
    mu = jnp.mean(x, axis=-1, keepdims=True)
    xc = x - mu
    var = jnp.mean(xc * xc, axis=-1, keepdims=True)
    return xc * lax.rsqrt(var + LN_EPS)


def _dot(a, b):
    return jnp.dot(a, b, preferred_element_type=F32)


def _dot_nt(a, b):
    return lax.dot_general(a, b, (((1,), (1,)), ((), ())), preferred_element_type=F32)


def _split3(x):
    x1 = x.astype(BF16)
    r1 = x - x1.astype(F32)
    x2 = r1.astype(BF16)
    x3 = (r1 - x2.astype(F32)).astype(BF16)
    return x1, x2, x3


def _sort_key(s):
    bits = pltpu.bitcast(s, I32)
    return bits ^ ((bits >> 31) & 0x7FFFFFFF)


def _kth_largest_key(count_ge, n_sel, shape):
    def body(i, ans):
        bit = lax.shift_left(jnp.int32(1), jnp.int32(31) - i)
        cand = ans | bit
        cnt = count_ge(cand ^ INT_MIN)
        return jnp.where(cnt >= n_sel, cand, ans)
    ans = lax.fori_loop(0, 32, body, jnp.zeros(shape, I32))
    return ans ^ INT_MIN


def _ada_kernel(c_ref, w_ref, b_ref, o_ref):
    c = c_ref[...]
    a = (c * jax.nn.sigmoid(c)).astype(BF16)
    o_ref[...] = _dot(a, w_ref[...].astype(BF16)) + b_ref[...]


def _ada(c_all, w_ada, b_ada):
    rows = c_all.shape[0]
    width = w_ada.shape[-1]
    tn = _tile(width, 1024)
    return pl.pallas_call(
        _ada_kernel,
        out_shape=jax.ShapeDtypeStruct((DEPTH, rows, width), F32),
        grid=(DEPTH, width // tn),
        in_specs=[pl.BlockSpec((rows, D_MODEL), lambda l, j: (0, 0)),
                  pl.BlockSpec((None, D_MODEL, tn), lambda l, j: (l, 0, j)),
                  pl.BlockSpec((None, 1, tn), lambda l, j: (l, 0, j))],
        out_specs=pl.BlockSpec((None, rows, tn), lambda l, j: (l, 0, j)),
        compiler_params=_params(("parallel", "parallel")),
        name="ada",
    )(c_all, w_ada, b_ada.reshape(DEPTH, 1, width))


def _proj_kernel(x_ref, sh_ref, sc_ref, wt_ref, wh_ref, o_ref, h_scr, *, tail_tiles):
    j = pl.program_id(1)

    @pl.when(j == 0)
    def _():
        h_scr[...] = (_ln(x_ref[...]) * (1.0 + sc_ref[0]) + sh_ref[0]).astype(BF16)

    @pl.when(j < tail_tiles)
    def _():
        o_ref[...] = _dot(h_scr[...], wt_ref[...])

    @pl.when(j >= tail_tiles)
    def _():
        o_ref[...] = _dot(h_scr[...], wh_ref[...])


def _proj(x, shift, scale, w_tail, w_head, l, tm):
    n = x.shape[0]
    tiles_per_mod = (n // tm) // shift.shape[0]
    r = shift.shape[1]
    tn = PROJ_TN
    tail_tiles = TAIL_COLS // tn
    mod_spec = pl.BlockSpec((1, r, D_MODEL), lambda i, j: (i // tiles_per_mod, 0, 0))
    return pl.pallas_call(
        functools.partial(_proj_kernel, tail_tiles=tail_tiles),
        out_shape=jax.ShapeDtypeStruct((n, PACK_WIDTH), F32),
        grid=(n // tm, PACK_WIDTH // tn),
        in_specs=[pl.BlockSpec((tm, D_MODEL), lambda i, j: (i, 0)),
                  mod_spec, mod_spec,
                  pl.BlockSpec((None, D_MODEL, tn), lambda i, j: (l, 0, jnp.minimum(j, tail_tiles - 1))),
                  pl.BlockSpec((None, D_MODEL, tn), lambda i, j: (l, 0, jnp.maximum(j - tail_tiles, 0)))],
        out_specs=pl.BlockSpec((tm, tn), lambda i, j: (i, j)),
        scratch_shapes=[pltpu.VMEM((tm, D_MODEL), BF16)],
        compiler_params=_params(("parallel", "arbitrary")),
        name="proj",
    )(x, shift, scale, w_tail, w_head)


KEY_BLOCK = 512
ROW_CHUNK = 32
ATT_ROWS = 128


def _lane_fold(x, op):
    acc = x[:, :LANES]
    for i in range(1, x.shape[1] // LANES):
        acc = op(acc, x[:, i * LANES:(i + 1) * LANES])
    return acc


def _attn_prompt_kernel(q_ref, iq0_ref, iq1_ref, iwq_ref, k_ref, v_ref, ikw_ref, o_ref,
                        kb_scr, va_scr, ikb_scr, iqs_scr, qs_scr, key_scr, h16_scr, m_scr, acc_scr, xb_scr,
                        *, tq, n_sel, col_bits):
    qi = pl.program_id(1)
    tk = min(tq, KEY_BLOCK)
    n_kb = (qi + 1) * (tq // tk)
    n_lane_tiles = tk // LANES

    @pl.when(qi == 0)
    def _():
        kb_scr[...] = k_ref[...].astype(BF16)
        for g in range(N_KV_HEADS):
            va_scr[:, 2 * g * HEAD_DIM:(2 * g + 1) * HEAD_DIM] = (
                v_ref[:, g * HEAD_DIM:(g + 1) * HEAD_DIM].astype(BF16))
            va_scr[:, (2 * g + 1) * HEAD_DIM:(2 * g + 2) * HEAD_DIM] = jnp.ones((va_scr.shape[0], HEAD_DIM), BF16)
        ikb_scr[...] = ikw_ref[:, :IDX_DIM].astype(BF16)

    iq = jnp.concatenate([iq0_ref[...], iq1_ref[...]], axis=1)
    for h in range(N_IDX_HEADS):
        iqs_scr[h * tq:(h + 1) * tq, :] = iq[:, h * IDX_DIM:(h + 1) * IDX_DIM].astype(BF16)
    q = q_ref[...] * (HEAD_DIM ** -0.5 * LOG2_E)
    for h in range(N_ATT_HEADS):
        qs_scr[h * tq:(h + 1) * tq, :] = q[:, h * HEAD_DIM:(h + 1) * HEAD_DIM].astype(BF16)
    w = iwq_ref[:, IDX_DIM:IDX_DIM + N_IDX_HEADS] * (IDX_DIM ** -0.5 * N_IDX_HEADS ** -0.5)
    row = qi * tq + lax.broadcasted_iota(I32, (tq, tk), 0)
    col0 = lax.broadcasted_iota(I32, (tq, tk), 1)

    def score_block(kb, carry):
        start = pl.multiple_of(kb * tk, tk)
        ikb = ikb_scr[pl.ds(start, tk), :]
        s = jnp.zeros((tq, tk), F32)
        for h in range(N_IDX_HEADS):
            lg = _dot_nt(iqs_scr[h * tq:(h + 1) * tq, :], ikb)
            s = s + w[:, h:h + 1] * jnp.maximum(lg, 0.0)
        s = jnp.where(kb * tk + col0 <= row, s, MASK_VALUE)
        key = _sort_key(s)
        key_scr[kb] = key
        h16_scr[kb] = (key >> 16).astype(I16)
        return carry

    lax.fori_loop(0, n_kb, score_block, 0)

    chunks = [slice(c * ROW_CHUNK, (c + 1) * ROW_CHUNK) for c in range(tq // ROW_CHUNK)]

    def count(*make_preds):
        preds = [[mp(rows) for rows in chunks] for mp in make_preds]

        def body(kb, parts):
            new = []
            for ci, rows in enumerate(chunks):
                keys = key_scr[kb, rows, :]
                for pi in range(len(preds)):
                    idx = pi * len(chunks) + ci
                    hit = jnp.where(preds[pi][ci](kb, keys), 1.0, 0.0)
                    new.append((idx, parts[idx] + _lane_fold(hit, jnp.add)))
            return tuple(p for _, p in sorted(new, key=lambda t: t[0]))

        zeros = tuple(jnp.zeros((ROW_CHUNK, LANES), F32) for _ in range(len(preds) * len(chunks)))
        parts = lax.fori_loop(0, n_kb, body, zeros)
        outs = []
        for pi in range(len(preds)):
            part = jnp.concatenate(parts[pi * len(chunks):(pi + 1) * len(chunks)], axis=0)
            outs.append(jnp.broadcast_to(jnp.sum(part, axis=1, keepdims=True), (tq, LANES)))
        return outs[0] if len(outs) == 1 else outs

    def wide(x):
        return jnp.concatenate([x] * n_lane_tiles, axis=1)

    def bcast(x, rows):
        return wide(x[rows])

    def count16_ge(cand):
        c16 = cand.astype(I16)
        cs = [wide(c16[rows]) for rows in chunks]

        def body(kb, parts):
            return tuple(p + _lane_fold(jnp.where(h16_scr[kb, rows, :] >= c, jnp.int16(1), jnp.int16(0)), jnp.add)
                         for p, c, rows in zip(parts, cs, chunks))

        parts = lax.fori_loop(0, n_kb, body, tuple(jnp.zeros((ROW_CHUNK, LANES), I16) for _ in chunks))
        part = jnp.concatenate(parts, axis=0).astype(F32)
        return jnp.broadcast_to(jnp.sum(part, axis=1, keepdims=True), (tq, LANES))

    def search16():
        u = jnp.zeros((tq, LANES), I32)
        for bit in range(15, -1, -1):
            cand = u | (1 << bit)
            u = jnp.where(count16_ge(cand - 32768) >= float(n_sel), cand, u)
        return u

    hi_p = search16() - 32768
    hi_w = wide(hi_p)

    def low_keys(kb, carry):
        key = key_scr[kb]
        hi = key >> 16
        lo = (key & 0xFFFF) - 32768
        h16_scr[kb] = jnp.where(hi > hi_w, 32767, jnp.where(hi < hi_w, -32768, lo)).astype(I16)
        return carry

    lax.fori_loop(0, n_kb, low_keys, 0)
    thr = hi_p * 65536 + search16()

    cnt_gt, cnt_ge = count(lambda rows: (lambda kb, key, c=bcast(thr, rows): key > c),
                           lambda rows: (lambda kb, key, c=bcast(thr, rows): key >= c))
    need = float(n_sel) - cnt_gt
    xb_scr[...] = jnp.full(xb_scr.shape, 2 ** 31 - 1, I32)

    @pl.when(jnp.max(jnp.abs(cnt_ge - float(n_sel))) > 0.0)
    def _():
        def body(i, x):
            cand = x | lax.shift_left(jnp.int32(1), jnp.int32(col_bits - 1) - i)

            def make_pred(rows):
                c_thr, c_cand = bcast(thr, rows), bcast(cand, rows)
                c_col = lax.broadcasted_iota(I32, (ROW_CHUNK, tk), 1)
                return lambda kb, key: (key == c_thr) & (kb * tk + c_col < c_cand)

            return jnp.where(count(make_pred) < need, cand, x)

        xb_scr[...] = lax.fori_loop(0, col_bits, body, jnp.zeros((tq, LANES), I32))

    xb_w, thr_w = wide(xb_scr[...]), wide(thr)

    def bias_block(kb, carry):
        key = key_scr[kb]
        col = kb * tk + col0
        sel = ((key > thr_w) | ((key == thr_w) & (col <= xb_w))) & (col <= row)
        key_scr[kb] = pltpu.bitcast(jnp.where(sel, 0.0, -jnp.inf), I32)
        return carry

    lax.fori_loop(0, n_kb, bias_block, 0)

    m_scr[...] = jnp.full(m_scr.shape, MASK_VALUE, F32)
    acc_scr[...] = jnp.zeros(acc_scr.shape, F32)

    def att_block(kb, carry):
        start = pl.multiple_of(kb * tk, tk)
        bias = pltpu.bitcast(key_scr[kb], F32)
        for g in range(N_KV_HEADS):
            kg = kb_scr[pl.ds(start, tk), g * HEAD_DIM:(g + 1) * HEAD_DIM]
            va = va_scr[pl.ds(start, tk), 2 * g * HEAD_DIM:(2 * g + 2) * HEAD_DIM]
            base = g * KV_GROUP * tq
            s_all = _dot_nt(qs_scr[base:base + KV_GROUP * tq, :], kg)
            ps, alphas = [], []
            for c in range(KV_GROUP * tq // ATT_ROWS):
                lo = c * ATT_ROWS
                rows = slice(base + lo, base + lo + ATT_ROWS)
                s = s_all[lo:lo + ATT_ROWS] + bias[lo % tq:lo % tq + ATT_ROWS]
                m_old = m_scr[rows, :]
                m_new = jnp.maximum(m_old, jnp.max(_lane_fold(s, jnp.maximum), axis=1, keepdims=True))
                ps.append(jnp.exp2(s - jnp.concatenate([m_new] * n_lane_tiles, axis=1)).astype(BF16))
                alphas.append(jnp.exp2(m_old - m_new))
                m_scr[rows, :] = m_new
            pv = _dot(jnp.concatenate(ps, axis=0), va)
            for c, a in enumerate(alphas):
                lo = c * ATT_ROWS
                rows = slice(base + lo, base + lo + ATT_ROWS)
                acc_scr[rows, :] = jnp.concatenate([a, a], axis=1) * acc_scr[rows, :] + pv[lo:lo + ATT_ROWS]
        return carry

    lax.fori_loop(0, n_kb, att_block, 0)
    for h in range(N_ATT_HEADS):
        acc = acc_scr[h * tq:(h + 1) * tq, :]
        o_ref[:, h * HEAD_DIM:(h + 1) * HEAD_DIM] = (acc[:, :HEAD_DIM] / acc[:, HEAD_DIM:]).astype(o_ref.dtype)


def _attn_prompt(proj, b, t):
    tq = _tile(t, 512)
    tk = min(tq, KEY_BLOCK)
    n_sel = min(TOPK_MAX, t // 4)
    proj3 = proj.reshape(b, t, PACK_WIDTH)
    kern = functools.partial(_attn_prompt_kernel, tq=tq, n_sel=n_sel, col_bits=t.bit_length())
    once = dict(pipeline_mode=pl.Buffered(1))
    out = pl.pallas_call(
        kern,
        out_shape=jax.ShapeDtypeStruct((b, t, ATT_WIDTH), BF16),
        grid=(b, t // tq),
        in_specs=[pl.BlockSpec((None, tq, ATT_WIDTH), lambda bi, qi: (bi, qi, OFF_Q // ATT_WIDTH)),
                  pl.BlockSpec((None, tq, IQ_BLOCK), lambda bi, qi: (bi, qi, OFF_IQ // IQ_BLOCK)),
                  pl.BlockSpec((None, tq, IQ_BLOCK), lambda bi, qi: (bi, qi, OFF_IQ // IQ_BLOCK + 1)),
                  pl.BlockSpec((None, tq, LANES), lambda bi, qi: (bi, qi, OFF_IKW // LANES)),
                  pl.BlockSpec((None, t, KV_WIDTH), lambda bi, qi: (bi, 0, OFF_K // KV_WIDTH), **once),
                  pl.BlockSpec((None, t, KV_WIDTH), lambda bi, qi: (bi, 0, OFF_V // KV_WIDTH), **once),
                  pl.BlockSpec((None, t, LANES), lambda bi, qi: (bi, 0, OFF_IKW // LANES), **once)],
        out_specs=pl.BlockSpec((None, tq, ATT_WIDTH), lambda bi, qi: (bi, qi, 0)),
        scratch_shapes=[pltpu.VMEM((t, KV_WIDTH), BF16),
                        pltpu.VMEM((t, 2 * KV_WIDTH), BF16),
                        pltpu.VMEM((t, IDX_DIM), BF16),
                        pltpu.VMEM((N_IDX_HEADS * tq, IDX_DIM), BF16),
                        pltpu.VMEM((N_ATT_HEADS * tq, HEAD_DIM), BF16),
                        pltpu.VMEM((t // tk, tq, tk), I32),
                        pltpu.VMEM((t // tk, tq, tk), I16),
                        pltpu.VMEM((N_ATT_HEADS * tq, LANES), F32),
                        pltpu.VMEM((N_ATT_HEADS * tq, 2 * HEAD_DIM), F32),
                        pltpu.VMEM((tq, LANES), I32)],
        compiler_params=_params(("parallel", "arbitrary")),
        name="attn_prompt",
    )(proj3, proj3, proj3, proj3, proj3, proj3, proj3)
    return out.reshape(b * t, ATT_WIDTH)


def _hgrn_gates(hq, hf, lb):
    q = hq * jax.nn.sigmoid(hq)
    f = lb + (1.0 - lb) * jax.nn.sigmoid(hf)
    kk = (1.0 - lb) * jax.nn.sigmoid(-hf)
    return q, f, kk


def _hgrn_finish(o, hg, ng):
    o = o * lax.rsqrt(jnp.mean(o * o, axis=-1, keepdims=True) + LN_EPS)
    return o * ng * (hg * jax.nn.sigmoid(hg))


HGRN_HEADS_PER_STEP = 8


def _hgrn_chunk(heads, tri, ones):
    c_len, sub = HGRN_CHUNK, HGRN_SUB
    n_sub = c_len // sub
    t_idx = lax.broadcasted_iota(I32, (sub, 1), 0)
    bs = []
    for q, f, kk, v, st in heads:
        l1, l2, l3 = _split3(jnp.log(f))
        bs.append(_dot(tri, l1) + _dot(tri, l2) + _dot(tri, l3))
    stage2 = []
    for (q, f, kk, v, st), b in zip(heads, bs):
        v_b = v.astype(BF16)
        inter = _dot_nt((q * jnp.exp(b)).astype(BF16), st.astype(BF16))
        rs, cross = [], []
        for j in range(n_sub):
            lo = j * sub
            qt, bt = q[lo:lo + sub], b[lo:lo + sub]
            ps = []
            for s in range(sub):
                valid = t_idx >= s
                dec = jnp.exp(jnp.where(valid, bt - b[lo + s:lo + s + 1, :], 0.0))
                ps.append(jnp.where(valid, qt * kk[lo + s:lo + s + 1, :] * dec, 0.0))
            rs.append(_dot(jnp.concatenate(ps, axis=0).astype(BF16), ones))
            if lo:
                b_edge = b[lo - 1:lo, :]
                qd = (qt * jnp.exp(bt - b_edge)).astype(BF16)
                kd = (kk[:lo] * jnp.exp(b_edge - b[:lo])).astype(BF16)
                cross.append(_dot_nt(qd, kd))
        b_last = b[c_len - 1:c_len, :]
        kd = (kk * jnp.exp(b_last - b)).astype(BF16)
        upd = lax.dot_general(v_b, kd, (((0,), (0,)), ((), ())), preferred_element_type=F32)
        stage2.append((v_b, inter, rs, cross, st * jnp.exp(b_last) + upd))
    out = []
    for (q, f, kk, v, st), (v_b, inter, rs, cross, st_new) in zip(heads, stage2):
        rows = []
        for j in range(n_sub):
            lo = j * sub
            acc = inter[lo:lo + sub]
            if lo:
                acc = acc + _dot(cross[j - 1].astype(BF16), v_b[:lo])
            for s in range(sub):
                acc = acc + rs[j][s * sub:(s + 1) * sub, :] * v[lo + s:lo + s + 1, :]
            rows.append(acc)
        out.append((jnp.concatenate(rows, axis=0), st_new))
    return out


def _hgrn_prompt_kernel(hq_ref, hf_ref, hi_ref, hg_ref, lb_ref, ng_ref, o_ref, s_ref, st_scr, *, n_chunks):
    c_len = HGRN_CHUNK
    ti = pl.program_id(2)

    @pl.when(ti == 0)
    def _():
        st_scr[...] = jnp.zeros(st_scr.shape, F32)

    r_i = lax.broadcasted_iota(I32, (c_len, c_len), 0)
    c_i = lax.broadcasted_iota(I32, (c_len, c_len), 1)
    tri = jnp.where(c_i <= r_i, 1.0, 0.0).astype(BF16)
    ones = jnp.ones((HGRN_DK, HGRN_DV), BF16)

    def chunk(c, carry):
        sl = pl.ds(pl.multiple_of(c * c_len, c_len), c_len)
        cols = [slice(hh * HGRN_DK, (hh + 1) * HGRN_DK) for hh in range(HGRN_HEADS_PER_STEP)]
        heads = []
        for hh, cs in enumerate(cols):
            q, f, kk = _hgrn_gates(hq_ref[sl, cs], hf_ref[sl, cs], lb_ref[:, cs])
            heads.append((q, f, kk, hi_ref[sl, cs], st_scr[hh]))
        for hh, (o, st_new) in enumerate(_hgrn_chunk(heads, tri, ones)):
            st_scr[hh] = st_new
            o_ref[sl, cols[hh]] = _hgrn_finish(o, hg_ref[sl, cols[hh]], ng_ref[:, cols[hh]]).astype(o_ref.dtype)
        return carry

    lax.fori_loop(0, n_chunks, chunk, 0)

    @pl.when(ti == pl.num_programs(2) - 1)
    def _():
        for hh in range(HGRN_HEADS_PER_STEP):
            s_ref[hh] = st_scr[hh].T


def _hgrn_prompt(proj, lb, ng, b, t):
    tc = _tile(t, 512)
    n_chunks = tc // HGRN_CHUNK
    hp = HGRN_HEADS_PER_STEP
    width = hp * HGRN_DK
    proj3 = proj.reshape(b, t, PACK_WIDTH)

    def col(off):
        return pl.BlockSpec((None, tc, width), lambda bi, h, ti: (bi, ti, off // width + h))

    vec = pl.BlockSpec((None, 1, width), lambda bi, h, ti: (h, 0, 0))
    o, s = pl.pallas_call(
        functools.partial(_hgrn_prompt_kernel, n_chunks=n_chunks),
        out_shape=(jax.ShapeDtypeStruct((b, t, HGRN_WIDTH), BF16),
                   jax.ShapeDtypeStruct((b, N_HGRN_HEADS, HGRN_DK, HGRN_DV), F32)),
        grid=(b, N_HGRN_HEADS // hp, t // tc),
        in_specs=[col(OFF_HQ), col(OFF_HF), col(OFF_HI), col(OFF_HG), vec, vec],
        out_specs=(pl.BlockSpec((None, tc, width), lambda bi, h, ti: (bi, ti, h)),
                   pl.BlockSpec((None, hp, HGRN_DK, HGRN_DV), lambda bi, h, ti: (bi, h, 0, 0))),
        scratch_shapes=[pltpu.VMEM((hp, HGRN_DV, HGRN_DK), F32)],
        compiler_params=_params(("parallel", "parallel", "arbitrary")),
        name="hgrn_prompt",
    )(proj3, proj3, proj3, proj3,
      lb.reshape(N_HGRN_HEADS // hp, 1, width), ng.reshape(N_HGRN_HEADS // hp, 1, width))
    return o.reshape(b * t, HGRN_WIDTH), s


def _merge_kernel(oa_ref, ob_ref, ga_ref, gb_ref, wa_ref, wb_ref, o_ref):
    ya = _dot(oa_ref[...].astype(BF16), wa_ref[...])
    yb = _dot(ob_ref[...].astype(BF16), wb_ref[...])
    o_ref[...] = (jax.nn.sigmoid(ga_ref[...]) * ya + jax.nn.sigmoid(gb_ref[...]) * yb).astype(o_ref.dtype)


def _merge(o_a, o_b, proj, w_up_a, w_up_b, l, tm):
    n = o_a.shape[0]
    return pl.pallas_call(
        _merge_kernel,
        out_shape=jax.ShapeDtypeStruct((n, D_MODEL), BF16),
        grid=(n // tm,),
        in_specs=[pl.BlockSpec((tm, ATT_WIDTH), lambda i: (i, 0)),
                  pl.BlockSpec((tm, HGRN_WIDTH), lambda i: (i, 0)),
                  pl.BlockSpec((tm, D_MODEL), lambda i: (i, OFF_GA // D_MODEL)),
                  pl.BlockSpec((tm, D_MODEL), lambda i: (i, OFF_GB // D_MODEL)),
                  pl.BlockSpec((None, ATT_WIDTH, D_MODEL), lambda i: (l, 0, 0)),
                  pl.BlockSpec((None, HGRN_WIDTH, D_MODEL), lambda i: (l, 0, 0))],
        out_specs=pl.BlockSpec((tm, D_MODEL), lambda i: (i, 0)),
        compiler_params=_params(("parallel",)),
        name="merge",
    )(o_a, o_b, proj, proj, w_up_a, w_up_b)


def _post_norm(x, gate, y, g, b):
    return _ln(ALPHA * x + gate * y) * g + b


def _out_kernel(m_ref, w_ref, x_ref, gate_ref, lng_ref, lnb_ref, o_ref):
    y = _dot(m_ref[...], w_ref[...])
    o_ref[...] = _post_norm(x_ref[...], gate_ref[0], y, lng_ref[...], lnb_ref[...])


def _out_proj(merged, w_o, x, gate, ln_g, ln_b, l, tm):
    n = x.shape[0]
    tiles_per_mod = (n // tm) // gate.shape[0]
    vec = pl.BlockSpec((None, 1, D_MODEL), lambda i: (l, 0, 0))
    return pl.pallas_call(
        _out_kernel,
        out_shape=jax.ShapeDtypeStruct((n, D_MODEL), F32),
        grid=(n // tm,),
        in_specs=[pl.BlockSpec((tm, D_MODEL), lambda i: (i, 0)),
                  pl.BlockSpec((None, D_MODEL, D_MODEL), lambda i: (l, 0, 0)),
                  pl.BlockSpec((tm, D_MODEL), lambda i: (i, 0)),
                  pl.BlockSpec((1, gate.shape[1], D_MODEL), lambda i: (i // tiles_per_mod, 0, 0)),
                  vec, vec],
        out_specs=pl.BlockSpec((tm, D_MODEL), lambda i: (i, 0)),
        compiler_params=_params(("parallel",)),
        name="out_proj",
    )(merged, w_o, x, gate, ln_g.reshape(DEPTH, 1, D_MODEL), ln_b.reshape(DEPTH, 1, D_MODEL))


def _route(h, wr, br):
    h1, h2, _ = _split3(h)
    w1, w2, _ = _split3(wr)
    logits = _dot_nt(w1, h1) + _dot_nt(w1, h2) + _dot_nt(w2, h1)
    aff = jax.nn.sigmoid(logits)
    sel = aff + br
    rows = [sel[e:e + 1, :] for e in range(N_EXPERTS)]
    grp = []
    for g in range(N_GROUPS):
        a, b, c, d = rows[g * EXPERTS_PER_GROUP:(g + 1) * EXPERTS_PER_GROUP]
        hi1, lo1 = jnp.maximum(a, b), jnp.minimum(a, b)
        hi2, lo2 = jnp.maximum(c, d), jnp.minimum(c, d)
        grp.append(jnp.maximum(hi1, hi2) + jnp.maximum(jnp.minimum(hi1, hi2), jnp.maximum(lo1, lo2)))
    best = functools.reduce(jnp.maximum, grp)
    taken = jnp.zeros_like(best)
    picked = []
    for g in range(N_GROUPS):
        is_g = jnp.where(grp[g] == best, 1.0, 0.0) * (1.0 - taken)
        taken = taken + is_g
        for e in range(g * EXPERTS_PER_GROUP, (g + 1) * EXPERTS_PER_GROUP):
            rank = jnp.zeros_like(best)
            for o in range(g * EXPERTS_PER_GROUP, (g + 1) * EXPERTS_PER_GROUP):
                if o < e:
                    rank = rank + jnp.where(rows[o] >= rows[e], 1.0, 0.0)
                elif o > e:
                    rank = rank + jnp.where(rows[o] > rows[e], 1.0, 0.0)
            picked.append(is_g * jnp.where(rank < 2.0, 1.0, 0.0))
    picked = jnp.concatenate(picked, axis=0)
    gate = picked * aff
    return picked, gate / jnp.sum(gate, axis=0, keepdims=True)


def _router_kernel(x_ref, sh_ref, sc_ref, wr_ref, br_ref, h_ref, g_ref):
    h = _ln(x_ref[...]) * (1.0 + sc_ref[0]) + sh_ref[0]
    h_ref[...] = h.astype(h_ref.dtype)
    _, g_ref[...] = _route(h, wr_ref[...], br_ref[...])


def _router_sorted_kernel(x_ref, sh_ref, sc_ref, wr_ref, br_ref, h_ref, eid_ref, gw_ref, pos_ref, cnt_ref, base_scr):
    tm = x_ref.shape[0]

    @pl.when(pl.program_id(0) == 0)
    def _():
        base_scr[...] = jnp.zeros(base_scr.shape, F32)

    h = _ln(x_ref[...]) * (1.0 + sc_ref[0]) + sh_ref[0]
    h_ref[...] = h.astype(h_ref.dtype)
    picked, gate = _route(h, wr_ref[...], br_ref[...])
    r = lax.broadcasted_iota(I32, (tm, tm), 0)
    c = lax.broadcasted_iota(I32, (tm, tm), 1)
    earlier = jnp.where(r < c, 1.0, 0.0).astype(BF16)
    base = base_scr[...]
    rank = _dot(picked.astype(BF16), earlier) + jnp.concatenate([base] * (tm // LANES), axis=1)
    base_scr[...] = base + jnp.sum(picked, axis=1, keepdims=True)
    cnt_ref[...] = base_scr[...]
    eio = lax.broadcasted_iota(I32, picked.shape, 0).astype(F32)
    e_lo = jnp.min(jnp.where(picked > 0.0, eio, float(N_EXPERTS)), axis=0, keepdims=True)
    e_hi = jnp.max(jnp.where(picked > 0.0, eio, -1.0), axis=0, keepdims=True)

    def pick(e, x):
        return jnp.sum(jnp.where(eio == e, x, 0.0), axis=0, keepdims=True)

    eid_ref[...] = jnp.concatenate([e_lo, e_hi], axis=0).astype(I32)
    gw_ref[...] = jnp.concatenate([pick(e_lo, gate), pick(e_hi, gate)], axis=0)
    pos_ref[...] = jnp.concatenate([pick(e_lo, rank), pick(e_hi, rank)], axis=0).astype(I32)


def _router_sorted(x, shift, scale, w_router_t, b_router, tm):
    n = x.shape[0]
    tiles_per_mod = (n // tm) // shift.shape[0]
    mod_spec = pl.BlockSpec((1, shift.shape[1], D_MODEL), lambda i: (i // tiles_per_mod, 0, 0))
    pair = pl.BlockSpec((2, tm), lambda i: (0, i))
    return pl.pallas_call(
        _router_sorted_kernel,
        out_shape=(jax.ShapeDtypeStruct((n, D_MODEL), F32),
                   jax.ShapeDtypeStruct((2, n), I32),
                   jax.ShapeDtypeStruct((2, n), F32),
                   jax.ShapeDtypeStruct((2, n), I32),
                   jax.ShapeDtypeStruct((N_EXPERTS, LANES), F32)),
        grid=(n // tm,),
        in_specs=[pl.BlockSpec((tm, D_MODEL), lambda i: (i, 0)),
                  mod_spec, mod_spec,
                  pl.BlockSpec((N_EXPERTS, D_MODEL), lambda i: (0, 0)),
                  pl.BlockSpec((N_EXPERTS, 1), lambda i: (0, 0))],
        out_specs=(pl.BlockSpec((tm, D_MODEL), lambda i: (i, 0)), pair, pair, pair,
                   pl.BlockSpec((N_EXPERTS, LANES), lambda i: (0, 0))),
        scratch_shapes=[pltpu.VMEM((N_EXPERTS, LANES), F32)],
        compiler_params=_params(("arbitrary",)),
        name="router_sorted",
    )(x, shift, scale, w_router_t, b_router.reshape(N_EXPERTS, 1))


MOE_TILE = 256
DMA_UNROLL = 32


def _dispatch_kernel(dst_ref, ends_ref, h_ref, xs_ref, zero_scr, sem, zsem, *, td, n):
    first = pl.program_id(0) * td

    @pl.when(pl.program_id(0) == 0)
    def _():
        zero_scr[...] = jnp.zeros(zero_scr.shape, F32)
        n_slots = xs_ref.shape[0]

        def zero_copy(start):
            return pltpu.make_async_copy(zero_scr, xs_ref.at[pl.ds(pl.multiple_of(start, MOE_TILE), MOE_TILE)], zsem)

        jobs = []
        for e in range(N_EXPERTS):
            jobs.append((ends_ref[e + 1] > ends_ref[e], ends_ref[e + 1] - MOE_TILE))
            unused = ends_ref[N_EXPERTS] + e * MOE_TILE
            jobs.append((unused < n_slots, jnp.minimum(unused, n_slots - MOE_TILE)))
        for wanted, start in jobs:
            @pl.when(wanted)
            def _(start=start):
                zero_copy(start).start()
        for wanted, start in jobs:
            @pl.when(wanted)
            def _(start=start):
                zero_copy(start).wait()

    def copies(r):
        return [pltpu.make_async_copy(h_ref.at[pl.ds(r, 1)], xs_ref.at[pl.ds(dst_ref[k * n + first + r], 1)], sem)
                for k in range(2)]

    def start(r, carry):
        for cp in copies(r):
            cp.start()
        return carry

    def wait(r, carry):
        for cp in copies(r):
            cp.wait()
        return carry

    lax.fori_loop(0, td, start, 0, unroll=DMA_UNROLL)
    lax.fori_loop(0, td, wait, 0, unroll=DMA_UNROLL)


def _dispatch(h, dst_flat, seg_ends, n_slots, td):
    n = h.shape[0]
    grid_spec = pltpu.PrefetchScalarGridSpec(
        num_scalar_prefetch=2,
        grid=(n // td,),
        in_specs=[pl.BlockSpec((td, D_MODEL), lambda i, dst, ends: (i, 0))],
        out_specs=pl.BlockSpec(memory_space=pl.ANY),
        scratch_shapes=[pltpu.VMEM((MOE_TILE, D_MODEL), F32),
                        pltpu.SemaphoreType.DMA,
                        pltpu.SemaphoreType.DMA])
    return pl.pallas_call(
        functools.partial(_dispatch_kernel, td=td, n=n),
        out_shape=jax.ShapeDtypeStruct((n_slots, D_MODEL), F32),
        grid_spec=grid_spec,
        compiler_params=_params(("arbitrary",)),
        name="moe_dispatch",
    )(dst_flat, seg_ends, h)


def _experts_kernel(te_ref, tv_ref, xs_ref, w1_ref, w3_ref, w2_ref, ys_ref):
    del te_ref
    valid = tv_ref[pl.program_id(0)] > 0

    @pl.when(valid)
    def _():
        x = xs_ref[...].astype(BF16)
        a = _dot(x, w1_ref[...])
        b = _dot(x, w3_ref[...])
        ys_ref[...] = _dot((a * jax.nn.sigmoid(a) * b).astype(BF16), w2_ref[...])

    @pl.when(jnp.logical_not(valid))
    def _():
        ys_ref[...] = jnp.zeros(ys_ref.shape, F32)


def _experts(xs, tile_expert, tile_valid, w1, w3, w2, l):
    n_tiles = xs.shape[0] // MOE_TILE
    grid_spec = pltpu.PrefetchScalarGridSpec(
        num_scalar_prefetch=2,
        grid=(n_tiles,),
        in_specs=[pl.BlockSpec((MOE_TILE, D_MODEL), lambda i, te, tv: (jnp.minimum(i, tv[0] - 1), 0)),
                  pl.BlockSpec((None, None, D_MODEL, EXPERT_DFF), lambda i, te, tv: (l, te[i], 0, 0)),
                  pl.BlockSpec((None, None, D_MODEL, EXPERT_DFF), lambda i, te, tv: (l, te[i], 0, 0)),
                  pl.BlockSpec((None, None, EXPERT_DFF, D_MODEL), lambda i, te, tv: (l, te[i], 0, 0))],
        out_specs=pl.BlockSpec((MOE_TILE, D_MODEL), lambda i, te, tv: (i, 0)))
    return pl.pallas_call(
        _experts_kernel,
        out_shape=jax.ShapeDtypeStruct(xs.shape, F32),
        grid_spec=grid_spec,
        compiler_params=_params(("arbitrary",)),
        name="moe_experts",
    )(tile_expert, tile_valid, xs, w1, w3, w2)


def _combine_kernel(dst_ref, ys_ref, gw_ref, x_ref, gate_ref, lng_ref, lnb_ref, o_ref, buf, sem, *, tc, n):
    i = pl.program_id(0)
    slot = i % 2

    def copies(tile, s, r):
        return [pltpu.make_async_copy(ys_ref.at[pl.ds(dst_ref[k * n + tile * tc + r], 1)],
                                      buf.at[s, k, pl.ds(r, 1)], sem.at[s]) for k in range(2)]

    def start_tile(tile, s):
        def body(r, carry):
            for cp in copies(tile, s, r):
                cp.start()
            return carry
        lax.fori_loop(0, tc, body, 0, unroll=DMA_UNROLL)

    def wait_tile(tile, s):
        def body(r, carry):
            for cp in copies(tile, s, r):
                cp.wait()
            return carry
        lax.fori_loop(0, tc, body, 0, unroll=DMA_UNROLL)

    @pl.when(i == 0)
    def _():
        start_tile(0, 0)

    @pl.when(i + 1 < pl.num_programs(0))
    def _():
        start_tile(i + 1, 1 - slot)

    wait_tile(i, slot)
    gw = gw_ref[...]
    y = gw[:, 0:1] * buf[slot, 0] + gw[:, 1:2] * buf[slot, 1]
    o_ref[...] = _post_norm(x_ref[...], gate_ref[0], y, lng_ref[...], lnb_ref[...])


def _combine(ys, dst_flat, gw, x, gate, ln_g, ln_b, l, tc):
    n = x.shape[0]
    tiles_per_mod = (n // tc) // gate.shape[0]
    vec = pl.BlockSpec((None, 1, D_MODEL), lambda i, dst: (l, 0, 0))
    grid_spec = pltpu.PrefetchScalarGridSpec(
        num_scalar_prefetch=1,
        grid=(n // tc,),
        in_specs=[pl.BlockSpec(memory_space=pl.ANY),
                  pl.BlockSpec((tc, 2), lambda i, dst: (i, 0)),
                  pl.BlockSpec((tc, D_MODEL), lambda i, dst: (i, 0)),
                  pl.BlockSpec((1, gate.shape[1], D_MODEL), lambda i, dst: (i // tiles_per_mod, 0, 0)),
                  vec, vec],
        out_specs=pl.BlockSpec((tc, D_MODEL), lambda i, dst: (i, 0)),
        scratch_shapes=[pltpu.VMEM((2, 2, tc, D_MODEL), F32),
                        pltpu.SemaphoreType.DMA((2,))])
    return pl.pallas_call(
        functools.partial(_combine_kernel, tc=tc, n=n),
        out_shape=jax.ShapeDtypeStruct((n, D_MODEL), F32),
        grid_spec=grid_spec,
        compiler_params=_params(("arbitrary",)),
        name="moe_combine",
    )(dst_flat, ys, gw, x, gate, ln_g.reshape(DEPTH, 1, D_MODEL), ln_b.reshape(DEPTH, 1, D_MODEL))


def _moe_grouped(x, shift, scale, gate, w_router_t, b_router, w1, w3, w2, ln_g, ln_b, l, tm):
    n = x.shape[0]
    h, eid, gw, pos, cnt = _router_sorted(x, shift, scale, w_router_t, b_router, tm)
    counts = cnt[:, 0].astype(I32)
    padded = (counts + MOE_TILE - 1) // MOE_TILE * MOE_TILE
    ends = jnp.cumsum(padded)
    experts = jnp.arange(N_EXPERTS, dtype=I32)[:, None, None]
    seg_start = jnp.sum(jnp.where(eid[None] == experts, (ends - padded)[:, None, None], 0), axis=0)
    dst_flat = (seg_start + pos).reshape(2 * n)
    n_tiles = 2 * n // MOE_TILE + N_EXPERTS
    tile_start = jnp.arange(n_tiles, dtype=I32) * MOE_TILE
    tile_expert = jnp.minimum(jnp.sum((tile_start[:, None] >= ends[None, :]).astype(I32), axis=1), N_EXPERTS - 1)
    tile_valid = jnp.where(tile_start < ends[-1], ends[-1] // MOE_TILE, 0).astype(I32)
    seg_ends = jnp.concatenate([jnp.zeros((1,), I32), ends.astype(I32)])
    xs = _dispatch(h, dst_flat, seg_ends, n_tiles * MOE_TILE, tm)
    ys = _experts(xs, tile_expert, tile_valid, w1, w3, w2, l)
    return _combine(ys, dst_flat, gw.T, x, gate, ln_g, ln_b, l, MOE_TILE)


def _router(x, shift, scale, w_router_t, b_router, tm):
    n = x.shape[0]
    tiles_per_mod = (n // tm) // shift.shape[0]
    mod_spec = pl.BlockSpec((1, shift.shape[1], D_MODEL), lambda i: (i // tiles_per_mod, 0, 0))
    return pl.pallas_call(
        _router_kernel,
        out_shape=(jax.ShapeDtypeStruct((n, D_MODEL), BF16),
                   jax.ShapeDtypeStruct((N_EXPERTS, n), F32)),
        grid=(n // tm,),
        in_specs=[pl.BlockSpec((tm, D_MODEL), lambda i: (i, 0)),
                  mod_spec, mod_spec,
                  pl.BlockSpec((N_EXPERTS, D_MODEL), lambda i: (0, 0)),
                  pl.BlockSpec((N_EXPERTS, 1), lambda i: (0, 0))],
        out_specs=(pl.BlockSpec((tm, D_MODEL), lambda i: (i, 0)),
                   pl.BlockSpec((N_EXPERTS, tm), lambda i: (0, i))),
        compiler_params=_params(("parallel",)),
        name="router",
    )(x, shift, scale, w_router_t, b_router.reshape(N_EXPERTS, 1))


def _moe_kernel(h_ref, g_ref, w1_ref, w3_ref, w2_ref, x_ref, gate_ref, lng_ref, lnb_ref, o_ref, acc_scr):
    e = pl.program_id(1)

    @pl.when(e == 0)
    def _():
        acc_scr[...] = jnp.zeros(acc_scr.shape, F32)

    h = h_ref[...]
    a = _dot(h, w1_ref[...])
    b = _dot(h, w3_ref[...])
    gates = g_ref[...]
    lane = lax.broadcasted_iota(I32, gates.shape, 1)
    gcol = jnp.sum(jnp.where(lane == e, gates, 0.0), axis=1, keepdims=True)
    u = (a * jax.nn.sigmoid(a) * b * gcol).astype(BF16)
    acc_scr[...] += _dot(u, w2_ref[...])

    @pl.when(e == pl.num_programs(1) - 1)
    def _():
        o_ref[...] = _post_norm(x_ref[...], gate_ref[0], acc_scr[...], lng_ref[...], lnb_ref[...])


def _moe(h, gates, w1, w3, w2, x, gate, ln_g, ln_b, l, tm):
    n = x.shape[0]
    tiles_per_mod = (n // tm) // gate.shape[0]
    vec = pl.BlockSpec((None, 1, D_MODEL), lambda i, e: (l, 0, 0))
    return pl.pallas_call(
        _moe_kernel,
        out_shape=jax.ShapeDtypeStruct((n, D_MODEL), F32),
        grid=(n // tm, N_EXPERTS),
        in_specs=[pl.BlockSpec((tm, D_MODEL), lambda i, e: (i, 0)),
                  pl.BlockSpec((tm, N_EXPERTS), lambda i, e: (i, 0)),
                  pl.BlockSpec((None, None, D_MODEL, EXPERT_DFF), lambda i, e: (l, e, 0, 0)),
                  pl.BlockSpec((None, None, D_MODEL, EXPERT_DFF), lambda i, e: (l, e, 0, 0)),
                  pl.BlockSpec((None, None, EXPERT_DFF, D_MODEL), lambda i, e: (l, e, 0, 0)),
                  pl.BlockSpec((tm, D_MODEL), lambda i, e: (i, 0)),
                  pl.BlockSpec((1, gate.shape[1], D_MODEL), lambda i, e: (i // tiles_per_mod, 0, 0)),
                  vec, vec],
        out_specs=pl.BlockSpec((tm, D_MODEL), lambda i, e: (i, 0)),
        scratch_shapes=[pltpu.VMEM((tm, D_MODEL), F32)],
        compiler_params=_params(("parallel", "arbitrary")),
        name="moe",
    )(h, gates, w1, w3, w2, x, gate, ln_g.reshape(DEPTH, 1, D_MODEL), ln_b.reshape(DEPTH, 1, D_MODEL))


def _row_to_col(v):
    n = v.shape[1]
    r = lax.broadcasted_iota(I32, (n, n), 0)
    c = lax.broadcasted_iota(I32, (n, n), 1)
    return jnp.sum(jnp.where(r == c, jnp.broadcast_to(v, (n, n)), 0.0), axis=1, keepdims=True)


def _attn_decode_kernel(pt_ref, proj_ref, ck_ref, cv_ref, cik_ref, o_ref,
                        ik_buf, k_buf, v_buf, key_scr, xb_scr, sem_ik, sem_k, sem_v,
                        *, l, n_pages, page, chunk_pages, n_sel):
    b = pl.program_id(0)
    n_chunks = n_pages // chunk_pages
    chunk = chunk_pages * page

    def ik_copy(j):
        return pltpu.make_async_copy(cik_ref.at[l, pt_ref[b, j]], ik_buf.at[pl.ds(j * page, page)], sem_ik)

    def kv_copies(c, j, slot):
        pg = pt_ref[b, c * chunk_pages + j]
        dst = pl.ds(j * page, page)
        copies = []
        for g in range(N_KV_HEADS):
            copies.append(pltpu.make_async_copy(ck_ref.at[l, pg, :, g, :], k_buf.at[slot, g, dst], sem_k.at[slot]))
            copies.append(pltpu.make_async_copy(cv_ref.at[l, pg, :, g, :], v_buf.at[slot, g, dst], sem_v.at[slot]))
        return copies

    def start_chunk(c, slot):
        def body(j, carry):
            for cp in kv_copies(c, j, slot):
                cp.start()
            return carry
        lax.fori_loop(0, chunk_pages, body, 0)

    def wait_chunk(c, slot):
        def body(j, carry):
            for cp in kv_copies(c, j, slot):
                cp.wait()
            return carry
        lax.fori_loop(0, chunk_pages, body, 0)

    def start_ik(j, carry):
        ik_copy(j).start()
        return carry

    def wait_ik(j, carry):
        ik_copy(j).wait()
        return carry

    lax.fori_loop(0, n_pages, start_ik, 0)
    start_chunk(0, 0)
    lax.fori_loop(0, n_pages, wait_ik, 0)

    rowsl = slice(None)
    iq = proj_ref[rowsl, OFF_IQ:OFF_IQ + N_IDX_HEADS * IDX_DIM]
    iq_h = jnp.concatenate([iq[:, h * IDX_DIM:(h + 1) * IDX_DIM] for h in range(N_IDX_HEADS)], axis=0)
    w_row = proj_ref[rowsl, OFF_IKW + IDX_DIM:OFF_IKW + IDX_DIM + N_IDX_HEADS]
    w_col = _row_to_col(w_row) * (IDX_DIM ** -0.5 * N_IDX_HEADS ** -0.5)
    ik_new = proj_ref[rowsl, OFF_IKW:OFF_IKW + IDX_DIM]

    iq_b = iq_h.astype(BF16)
    for c in range(n_chunks):
        lg = _dot_nt(iq_b, ik_buf[c * chunk:(c + 1) * chunk, :].astype(BF16))
        s = jnp.sum(w_col * jnp.maximum(lg, 0.0), axis=0, keepdims=True)
        key_scr[c] = _sort_key(s)
    lg_new = jnp.sum(iq_b.astype(F32) * ik_new.astype(BF16).astype(F32), axis=1, keepdims=True)
    key_new = _sort_key(jnp.sum(w_col * jnp.maximum(lg_new, 0.0), axis=0, keepdims=True))

    def count_ge(cand):
        cnt = jnp.where(key_new >= cand, 1.0, 0.0)
        for c in range(n_chunks):
            cnt = cnt + jnp.sum(jnp.where(key_scr[c] >= cand, 1.0, 0.0), axis=1, keepdims=True)
        return cnt

    thr = _kth_largest_key(count_ge, float(n_sel), (1, 1))

    past = n_pages * page
    lane_pos = lax.broadcasted_iota(I32, (1, chunk), 1)

    def count(pred):
        cnt = jnp.where(pred(key_new, past), 1.0, 0.0)
        for c in range(n_chunks):
            cnt = cnt + jnp.sum(jnp.where(pred(key_scr[c], c * chunk + lane_pos), 1.0, 0.0), axis=1, keepdims=True)
        return cnt

    need = float(n_sel) - count(lambda key, pos: key > thr)
    xb_scr[...] = jnp.full(xb_scr.shape, 2 ** 31 - 1, I32)

    @pl.when(jnp.max(jnp.abs(count(lambda key, pos: key >= thr) - float(n_sel))) > 0.0)
    def _():
        x = jnp.zeros((1, 1), I32)
        for bit in range((past + 1).bit_length() - 1, -1, -1):
            cand = x | (1 << bit)
            x = jnp.where(count(lambda key, pos: (key == thr) & (pos < cand)) < need, cand, x)
        xb_scr[...] = jnp.broadcast_to(x, xb_scr.shape)

    xb = xb_scr[:, 0:1]

    def selected(key, pos):
        return (key > thr) | ((key == thr) & (pos <= xb))

    q = proj_ref[rowsl, OFF_Q:OFF_Q + ATT_WIDTH]
    q_h = jnp.concatenate([q[:, h * HEAD_DIM:(h + 1) * HEAD_DIM] for h in range(N_ATT_HEADS)], axis=0)
    q_b = q_h.astype(BF16)
    k_new = proj_ref[rowsl, OFF_K:OFF_K + KV_WIDTH]
    v_new = proj_ref[rowsl, OFF_V:OFF_V + KV_WIDTH]
    scale = HEAD_DIM ** -0.5

    def att_chunk(c, carry):
        slot = c % 2

        @pl.when(c + 1 < n_chunks)
        def _():
            start_chunk(c + 1, 1 - slot)

        wait_chunk(c, slot)
        sel = selected(key_scr[c], c * chunk + lane_pos)
        out = []
        for g in range(N_KV_HEADS):
            m_old, l_old, acc_old = carry[g]
            kg = k_buf[slot, g].astype(BF16)
            vg = v_buf[slot, g].astype(BF16)
            s = _dot_nt(q_b[g * KV_GROUP:(g + 1) * KV_GROUP], kg) * scale
            s = jnp.where(sel, s, MASK_VALUE)
            m_new = jnp.maximum(m_old, jnp.max(s, axis=1, keepdims=True))
            p = jnp.where(sel, jnp.exp(s - m_new), 0.0)
            a = jnp.exp(m_old - m_new)
            out.append((m_new, a * l_old + jnp.sum(p, axis=1, keepdims=True),
                        a * acc_old + _dot(p.astype(BF16), vg)))
        return tuple(out)

    init = tuple((jnp.full((KV_GROUP, 1), MASK_VALUE, F32), jnp.zeros((KV_GROUP, 1), F32),
                  jnp.zeros((KV_GROUP, HEAD_DIM), F32)) for _ in range(N_KV_HEADS))
    res = lax.fori_loop(0, n_chunks, att_chunk, init)

    sel_new = selected(key_new, past)
    for g in range(N_KV_HEADS):
        m_old, l_old, acc_old = res[g]
        kg = k_new[:, g * HEAD_DIM:(g + 1) * HEAD_DIM].astype(BF16).astype(F32)
        vg = v_new[:, g * HEAD_DIM:(g + 1) * HEAD_DIM].astype(BF16).astype(F32)
        qg = q_b[g * KV_GROUP:(g + 1) * KV_GROUP].astype(F32)
        s = jnp.sum(qg * kg, axis=1, keepdims=True) * scale
        s = jnp.where(sel_new, s, MASK_VALUE)
        m_new = jnp.maximum(m_old, s)
        p = jnp.where(sel_new, jnp.exp(s - m_new), 0.0)
        a = jnp.exp(m_old - m_new)
        l_new = a * l_old + p
        acc = a * acc_old + p.astype(BF16).astype(F32) * vg
        o = acc / l_new
        for r in range(KV_GROUP):
            hh = g * KV_GROUP + r
            o_ref[rowsl, hh * HEAD_DIM:(hh + 1) * HEAD_DIM] = o[r:r + 1, :].astype(o_ref.dtype)


def _attn_decode(proj, page_table, cache_k, cache_v, cache_idx_k, l):
    db, n_pages = page_table.shape
    page = cache_k.shape[2]
    past = n_pages * page
    n_sel = min(TOPK_MAX, (past + 1) // 4)
    chunk_pages = _tile(n_pages, 32)
    kern = functools.partial(_attn_decode_kernel, l=l, n_pages=n_pages, page=page,
                             chunk_pages=chunk_pages, n_sel=n_sel)
    grid_spec = pltpu.PrefetchScalarGridSpec(
        num_scalar_prefetch=1,
        grid=(db,),
        in_specs=[pl.BlockSpec((None, 1, PACK_WIDTH), lambda b, pt: (b, 0, 0)),
                  pl.BlockSpec(memory_space=pl.ANY),
                  pl.BlockSpec(memory_space=pl.ANY),
                  pl.BlockSpec(memory_space=pl.ANY)],
        out_specs=pl.BlockSpec((None, 1, ATT_WIDTH), lambda b, pt: (b, 0, 0)),
        scratch_shapes=[pltpu.VMEM((past, IDX_DIM), F32),
                        pltpu.VMEM((2, N_KV_HEADS, chunk_pages * page, HEAD_DIM), F32),
                        pltpu.VMEM((2, N_KV_HEADS, chunk_pages * page, HEAD_DIM), F32),
                        pltpu.VMEM((n_pages // chunk_pages, 1, chunk_pages * page), I32),
                        pltpu.VMEM((1, LANES), I32),
                        pltpu.SemaphoreType.DMA,
                        pltpu.SemaphoreType.DMA((2,)),
                        pltpu.SemaphoreType.DMA((2,))])
    return pl.pallas_call(
        kern,
        out_shape=jax.ShapeDtypeStruct((db, 1, ATT_WIDTH), F32),
        grid_spec=grid_spec,
        compiler_params=_params(("arbitrary",)),
        name="attn_decode",
    )(page_table, proj.reshape(db, 1, PACK_WIDTH), cache_k, cache_v, cache_idx_k).reshape(db, ATT_WIDTH)


def _hgrn_decode_kernel(proj_ref, s0_ref, lb_ref, ng_ref, o_ref, s_ref):
    rowsl = slice(None)
    for h in range(N_HGRN_HEADS):
        cols = lambda off: slice(off + h * HGRN_DK, off + (h + 1) * HGRN_DK)
        lb = lb_ref[:, h * HGRN_DK:(h + 1) * HGRN_DK]
        q, f, kk = _hgrn_gates(proj_ref[rowsl, cols(OFF_HQ)], proj_ref[rowsl, cols(OFF_HF)], lb)
        v = proj_ref[rowsl, cols(OFF_HI)]
        s_new = _row_to_col(f) * s0_ref[h] + _row_to_col(kk) * v
        s_ref[h] = s_new
        o = jnp.sum(_row_to_col(q) * s_new, axis=0, keepdims=True)
        ng = ng_ref[:, h * HGRN_DV:(h + 1) * HGRN_DV]
        o_ref[rowsl, h * HGRN_DV:(h + 1) * HGRN_DV] = _hgrn_finish(
            o, proj_ref[rowsl, cols(OFF_HG)], ng).astype(o_ref.dtype)


def _hgrn_decode(proj, state, lb, ng, l):
    db = proj.shape[0]
    st_spec_in = pl.BlockSpec((None, None, N_HGRN_HEADS, HGRN_DK, HGRN_DV), lambda b: (l, b, 0, 0, 0))
    o, s = pl.pallas_call(
        _hgrn_decode_kernel,
        out_shape=(jax.ShapeDtypeStruct((db, 1, HGRN_WIDTH), F32),
                   jax.ShapeDtypeStruct((db, N_HGRN_HEADS, HGRN_DK, HGRN_DV), F32)),
        grid=(db,),
        in_specs=[pl.BlockSpec((None, 1, PACK_WIDTH), lambda b: (b, 0, 0)),
                  st_spec_in,
                  pl.BlockSpec((1, HGRN_WIDTH), lambda b: (0, 0)),
                  pl.BlockSpec((1, HGRN_WIDTH), lambda b: (0, 0))],
        out_specs=(pl.BlockSpec((None, 1, HGRN_WIDTH), lambda b: (b, 0, 0)),
                   pl.BlockSpec((None, N_HGRN_HEADS, HGRN_DK, HGRN_DV), lambda b: (b, 0, 0, 0))),
        compiler_params=_params(("parallel",)),
        name="hgrn_decode",
    )(proj.reshape(db, 1, PACK_WIDTH), state, lb.reshape(1, HGRN_WIDTH), ng.reshape(1, HGRN_WIDTH))
    return o.reshape(db, HGRN_WIDTH), s


def _split_w_in(w_in):
    return w_in[:, :, TAIL_START:].astype(BF16), w_in[:, :, :HEAD_COLS].astype(BF16)


def _mods(mod_l, rows, per_row):
    m = mod_l[rows]
    parts = jnp.split(m, 6, axis=-1)
    if per_row:
        return [p[None, :, :] for p in parts]
    return [p[:, None, :] for p in parts]


def kernel(x_prompt, x_sample, c_prompt, c_sample, cache_k, cache_v, cache_idx_k, state_hgrn, page_table,
           w_ada, b_ada, w_in, w_up_a, w_up_b, w_o, hgrn_norm_g, hgrn_lb_logits, ln1_g, ln1_b,
           w_router, b_router, w1, w3, w2, ln2_g, ln2_b):
    bp, t, _ = x_prompt.shape
    db = x_sample.shape[0]
    lbp = jax.nn.softmax(hgrn_lb_logits.astype(F32), axis=0)
    lower_bounds = jnp.cumsum(lbp, axis=0) - lbp[0]

    w_tail, w_head = _split_w_in(w_in)
    w_up_a_b, w_up_b_b, w_o_b = w_up_a.astype(BF16), w_up_b.astype(BF16), w_o.astype(BF16)
    w1_b, w3_b, w2_b = w1.astype(BF16), w3.astype(BF16), w2.astype(BF16)
    w_router_t = w_router.T

    n_c = bp + db
    c_rows = -(-n_c // 8) * 8
    c_all = jnp.concatenate([c_prompt, c_sample, jnp.zeros((c_rows - n_c, D_MODEL), F32)], axis=0)
    mod = _ada(c_all, w_ada, b_ada)

    xp = x_prompt.reshape(bp * t, D_MODEL)
    xs = x_sample.reshape(db, D_MODEL)
    tm_p = _tile(bp * t, 512)
    outs_p = {"k": [], "v": [], "ik": [], "s": []}
    outs_s = {"k": [], "v": [], "ik": [], "s": []}
    for l in range(DEPTH):
        sh1, sc1, g1, sh2, sc2, g2 = _mods(mod[l], slice(0, bp), per_row=False)
        proj = _proj(xp, sh1, sc1, w_tail, w_head, l, _tile(t, 1024))
        o_a = _attn_prompt(proj, bp, t)
        o_b, s_new = _hgrn_prompt(proj, lower_bounds[l], hgrn_norm_g[l], bp, t)
        merged = _merge(o_a, o_b, proj, w_up_a_b, w_up_b_b, l, tm_p)
        xp = _out_proj(merged, w_o_b, xp, g1, ln1_g, ln1_b, l, tm_p)
        xp = _moe_grouped(xp, sh2, sc2, g2, w_router_t, b_router, w1_b, w3_b, w2_b, ln2_g, ln2_b, l, tm_p)
        outs_p["k"].append(proj[:, OFF_K:OFF_K + KV_WIDTH].reshape(bp, t, N_KV_HEADS, HEAD_DIM))
        outs_p["v"].append(proj[:, OFF_V:OFF_V + KV_WIDTH].reshape(bp, t, N_KV_HEADS, HEAD_DIM))
        outs_p["ik"].append(proj[:, OFF_IKW:OFF_IKW + IDX_DIM].reshape(bp, t, IDX_DIM))
        outs_p["s"].append(s_new)

        sh1, sc1, g1, sh2, sc2, g2 = _mods(mod[l], slice(bp, bp + db), per_row=True)
        proj = _proj(xs, sh1, sc1, w_tail, w_head, l, db)
        o_a = _attn_decode(proj, page_table, cache_k, cache_v, cache_idx_k, l)
        o_b, s_new = _hgrn_decode(proj, state_hgrn, lower_bounds[l], hgrn_norm_g[l], l)
        merged = _merge(o_a, o_b, proj, w_up_a_b, w_up_b_b, l, db)
        xs = _out_proj(merged, w_o_b, xs, g1, ln1_g, ln1_b, l, db)
        h2, gates = _router(xs, sh2, sc2, w_router_t, b_router, db)
        xs = _moe(h2, gates.T, w1_b, w3_b, w2_b, xs, g2, ln2_g, ln2_b, l, db)
        outs_s["k"].append(proj[:, OFF_K:OFF_K + KV_WIDTH].reshape(db, 1, N_KV_HEADS, HEAD_DIM))
        outs_s["v"].append(proj[:, OFF_V:OFF_V + KV_WIDTH].reshape(db, 1, N_KV_HEADS, HEAD_DIM))
        outs_s["ik"].append(proj[:, OFF_IKW:OFF_IKW + IDX_DIM].reshape(db, 1, IDX_DIM))
        outs_s["s"].append(s_new)

    return (xp.reshape(bp, t, D_MODEL), xs.reshape(db, 1, D_MODEL),
            jnp.stack(outs_p["k"]), jnp.stack(outs_p["v"]), jnp.stack(outs_p["ik"]), jnp.stack(outs_p["s"]),
            jnp.stack(outs_s["k"]), jnp.stack(outs_s["v"]), jnp.stack(outs_s["ik"]), jnp.stack(outs_s["s"]))
```

```python
import functools

import jax
import jax.numpy as jnp
from jax import lax
from jax.experimental import pallas as pl
from jax.experimental.pallas import tpu as pltpu

F32 = jnp.float32
BF16 = jnp.bfloat16
I32 = jnp.int32
I16 = jnp.int16

DEPTH = 2
D_MODEL = 2048
N_ATT_HEADS = 8
N_KV_HEADS = 2
KV_GROUP = N_ATT_HEADS // N_KV_HEADS
HEAD_DIM = 128
ATT_WIDTH = N_ATT_HEADS * HEAD_DIM
KV_WIDTH = N_KV_HEADS * HEAD_DIM
N_IDX_HEADS = 16
IDX_DIM = 64
TOPK_MAX = 256
N_HGRN_HEADS = 8
HGRN_DK = 128
HGRN_DV = 128
HGRN_WIDTH = N_HGRN_HEADS * HGRN_DK
HGRN_CHUNK = 64
HGRN_SUB = 8
N_EXPERTS = 16
N_GROUPS = 4
EXPERTS_PER_GROUP = N_EXPERTS // N_GROUPS
EXPERT_DFF = 1024
ALPHA = (2 * DEPTH) ** 0.25
LN_EPS = 1e-5
MASK_VALUE = -1e30
INT_MIN = -2 ** 31
LOG2_E = 1.4426950408889634

IN_SPLITS = (ATT_WIDTH, KV_WIDTH, KV_WIDTH, N_IDX_HEADS * IDX_DIM, IDX_DIM, N_IDX_HEADS,
             HGRN_WIDTH, HGRN_WIDTH, HGRN_WIDTH, HGRN_WIDTH, D_MODEL, D_MODEL)

LANES = 128
PROJ_TN = 1024
IN_WIDTH = sum(IN_SPLITS)
TAIL_START = ATT_WIDTH + 2 * KV_WIDTH + N_IDX_HEADS * IDX_DIM + IDX_DIM + N_IDX_HEADS
TAIL_COLS = IN_WIDTH - TAIL_START
HEAD_COLS = -(-(TAIL_START + LANES - (TAIL_START % LANES)) // PROJ_TN) * PROJ_TN
OFF_HQ = 0
OFF_HF = OFF_HQ + HGRN_WIDTH
OFF_HI = OFF_HF + HGRN_WIDTH
OFF_HG = OFF_HI + HGRN_WIDTH
OFF_GA = OFF_HG + HGRN_WIDTH
OFF_GB = OFF_GA + D_MODEL
OFF_Q = TAIL_COLS
OFF_K = OFF_Q + ATT_WIDTH
OFF_V = OFF_K + KV_WIDTH
OFF_IQ = OFF_V + KV_WIDTH
OFF_IKW = OFF_IQ + N_IDX_HEADS * IDX_DIM
IQ_BLOCK = 512
PACK_WIDTH = TAIL_COLS + HEAD_COLS

VMEM_LIMIT = 59 * 1024 * 1024


def _params(semantics):
    return pltpu.CompilerParams(dimension_semantics=semantics, vmem_limit_bytes=VMEM_LIMIT)


def _tile(n, pref):
    t = min(n, pref)
    while n % t:
        t //= 2
    return t


def _ln(x):
    mu = jnp.mean(x, axis=-1, keepdims=True)
    xc = x - mu
    var = jnp.mean(xc * xc, axis=-1, keepdims=True)
    return xc * lax.rsqrt(var + LN_EPS)


def _dot(a, b):
    return jnp.dot(a, b, preferred_element_type=F32)


def _dot_nt(a, b):
    return lax.dot_general(a, b, (((1,), (1,)), ((), ())), preferred_element_type=F32)


def _split3(x):
    x1 = x.astype(BF16)
    r1 = x - x1.astype(F32)
    x2 = r1.astype(BF16)
    x3 = (r1 - x2.astype(F32)).astype(BF16)
    return x1, x2, x3


def _sort_key(s):
    bits = pltpu.bitcast(s, I32)
    return bits ^ ((bits >> 31) & 0x7FFFFFFF)


def _kth_largest_key(count_ge, n_sel, shape):
    def body(i, ans):
        bit = lax.shift_left(jnp.int32(1), jnp.int32(31) - i)
        cand = ans | bit
        cnt = count_ge(cand ^ INT_MIN)
        return jnp.where(cnt >= n_sel, cand, ans)
    ans = lax.fori_loop(0, 32, body, jnp.zeros(shape, I32))
    return ans ^ INT_MIN


def _ada_kernel(c_ref, w_ref, b_ref, o_ref):
    c = c_ref[...]
    a = (c * jax.nn.sigmoid(c)).astype(BF16)
    o_ref[...] = _dot(a, w_ref[...].astype(BF16)) + b_ref[...]


def _ada(c_all, w_ada, b_ada):
    rows = c_all.shape[0]
    width = w_ada.shape[-1]
    tn = _tile(width, 1024)
    return pl.pallas_call(
        _ada_kernel,
        out_shape=jax.ShapeDtypeStruct((DEPTH, rows, width), F32),
        grid=(DEPTH, width // tn),
        in_specs=[pl.BlockSpec((rows, D_MODEL), lambda l, j: (0, 0)),
                  pl.BlockSpec((None, D_MODEL, tn), lambda l, j: (l, 0, j)),
                  pl.BlockSpec((None, 1, tn), lambda l, j: (l, 0, j))],
        out_specs=pl.BlockSpec((None, rows, tn), lambda l, j: (l, 0, j)),
        compiler_params=_params(("parallel", "parallel")),
        name="ada",
    )(c_all, w_ada, b_ada.reshape(DEPTH, 1, width))


def _proj_kernel(x_ref, sh_ref, sc_ref, wt_ref, wh_ref, o_ref, h_scr, *, tail_tiles):
    j = pl.program_id(1)

    @pl.when(j == 0)
    def _():
        h_scr[...] = (_ln(x_ref[...]) * (1.0 + sc_ref[0]) + sh_ref[0]).astype(BF16)

    @pl.when(j < tail_tiles)
    def _():
        o_ref[...] = _dot(h_scr[...], wt_ref[...])

    @pl.when(j >= tail_tiles)
    def _():
        o_ref[...] = _dot(h_scr[...], wh_ref[...])


def _proj(x, shift, scale, w_tail, w_head, l, tm):
    n = x.shape[0]
    tiles_per_mod = (n // tm) // shift.shape[0]
    r = shift.shape[1]
    tn = PROJ_TN
    tail_tiles = TAIL_COLS // tn
    mod_spec = pl.BlockSpec((1, r, D_MODEL), lambda i, j: (i // tiles_per_mod, 0, 0))
    return pl.pallas_call(
        functools.partial(_proj_kernel, tail_tiles=tail_tiles),
        out_shape=jax.ShapeDtypeStruct((n, PACK_WIDTH), F32),
        grid=(n // tm, PACK_WIDTH // tn),
        in_specs=[pl.BlockSpec((tm, D_MODEL), lambda i, j: (i, 0)),
                  mod_spec, mod_spec,
                  pl.BlockSpec((None, D_MODEL, tn), lambda i, j: (l, 0, jnp.minimum(j, tail_tiles - 1))),
                  pl.BlockSpec((None, D_MODEL, tn), lambda i, j: (l, 0, jnp.maximum(j - tail_tiles, 0)))],
        out_specs=pl.BlockSpec((tm, tn), lambda i, j: (i, j)),
        scratch_shapes=[pltpu.VMEM((tm, D_MODEL), BF16)],
        compiler_params=_params(("parallel", "arbitrary")),
        name="proj",
    )(x, shift, scale, w_tail, w_head)


KEY_BLOCK = 512
ROW_CHUNK = 32
ATT_ROWS = 128


def _lane_fold(x, op):
    acc = x[:, :LANES]
    for i in range(1, x.shape[1] // LANES):
        acc = op(acc, x[:, i * LANES:(i + 1) * LANES])
    return acc


def _attn_prompt_kernel(q_ref, iq0_ref, iq1_ref, iwq_ref, k_ref, v_ref, ikw_ref, o_ref,
                        kb_scr, va_scr, ikb_scr, iqs_scr, qs_scr, key_scr, h16_scr, m_scr, acc_scr, xb_scr,
                        *, tq, n_sel, col_bits):
    qi = pl.program_id(1)
    tk = min(tq, KEY_BLOCK)
    n_kb = (qi + 1) * (tq // tk)
    n_lane_tiles = tk // LANES

    @pl.when(qi == 0)
    def _():
        kb_scr[...] = k_ref[...].astype(BF16)
        for g in range(N_KV_HEADS):
            va_scr[:, 2 * g * HEAD_DIM:(2 * g + 1) * HEAD_DIM] = (
                v_ref[:, g * HEAD_DIM:(g + 1) * HEAD_DIM].astype(BF16))
            va_scr[:, (2 * g + 1) * HEAD_DIM:(2 * g + 2) * HEAD_DIM] = jnp.ones((va_scr.shape[0], HEAD_DIM), BF16)
        ikb_scr[...] = ikw_ref[:, :IDX_DIM].astype(BF16)

    iq = jnp.concatenate([iq0_ref[...], iq1_ref[...]], axis=1)
    for h in range(N_IDX_HEADS):
        iqs_scr[h * tq:(h + 1) * tq, :] = iq[:, h * IDX_DIM:(h + 1) * IDX_DIM].astype(BF16)
    q = q_ref[...] * (HEAD_DIM ** -0.5 * LOG2_E)
    for h in range(N_ATT_HEADS):
        qs_scr[h * tq:(h + 1) * tq, :] = q[:, h * HEAD_DIM:(h + 1) * HEAD_DIM].astype(BF16)
    w = iwq_ref[:, IDX_DIM:IDX_DIM + N_IDX_HEADS] * (IDX_DIM ** -0.5 * N_IDX_HEADS ** -0.5)
    row = qi * tq + lax.broadcasted_iota(I32, (tq, tk), 0)
    col0 = lax.broadcasted_iota(I32, (tq, tk), 1)

    def score_block(kb, carry):
        start = pl.multiple_of(kb * tk, tk)
        ikb = ikb_scr[pl.ds(start, tk), :]
        s = jnp.zeros((tq, tk), F32)
        for h in range(N_IDX_HEADS):
            lg = _dot_nt(iqs_scr[h * tq:(h + 1) * tq, :], ikb)
            s = s + w[:, h:h + 1] * jnp.maximum(lg, 0.0)
        s = jnp.where(kb * tk + col0 <= row, s, MASK_VALUE)
        key = _sort_key(s)
        key_scr[kb] = key
        h16_scr[kb] = (key >> 16).astype(I16)
        return carry

    lax.fori_loop(0, n_kb, score_block, 0)

    chunks = [slice(c * ROW_CHUNK, (c + 1) * ROW_CHUNK) for c in range(tq // ROW_CHUNK)]

    def count(*make_preds):
        preds = [[mp(rows) for rows in chunks] for mp in make_preds]

        def body(kb, parts):
            new = []
            for ci, rows in enumerate(chunks):
                keys = key_scr[kb, rows, :]
                for pi in range(len(preds)):
                    idx = pi * len(chunks) + ci
                    hit = jnp.where(preds[pi][ci](kb, keys), 1.0, 0.0)
                    new.append((idx, parts[idx] + _lane_fold(hit, jnp.add)))
            return tuple(p for _, p in sorted(new, key=lambda t: t[0]))

        zeros = tuple(jnp.zeros((ROW_CHUNK, LANES), F32) for _ in range(len(preds) * len(chunks)))
        parts = lax.fori_loop(0, n_kb, body, zeros)
        outs = []
        for pi in range(len(preds)):
            part = jnp.concatenate(parts[pi * len(chunks):(pi + 1) * len(chunks)], axis=0)
            outs.append(jnp.broadcast_to(jnp.sum(part, axis=1, keepdims=True), (tq, LANES)))
        return outs[0] if len(outs) == 1 else outs

    def wide(x):
        return jnp.concatenate([x] * n_lane_tiles, axis=1)

    def bcast(x, rows):
        return wide(x[rows])

    def count16_ge(cand):
        c16 = cand.astype(I16)
        cs = [wide(c16[rows]) for rows in chunks]

        def body(kb, parts):
            return tuple(p + _lane_fold(jnp.where(h16_scr[kb, rows, :] >= c, jnp.int16(1), jnp.int16(0)), jnp.add)
                         for p, c, rows in zip(parts, cs, chunks))

        parts = lax.fori_loop(0, n_kb, body, tuple(jnp.zeros((ROW_CHUNK, LANES), I16) for _ in chunks))
        part = jnp.concatenate(parts, axis=0).astype(F32)
        return jnp.broadcast_to(jnp.sum(part, axis=1, keepdims=True), (tq, LANES))

    def search16():
        u = jnp.zeros((tq, LANES), I32)
        for bit in range(15, -1, -1):
            cand = u | (1 << bit)
            u = jnp.where(count16_ge(cand - 32768) >= float(n_sel), cand, u)
        return u

    hi_p = search16() - 32768
    hi_w = wide(hi_p)

    def low_keys(kb, carry):
        key = key_scr[kb]
        hi = key >> 16
        lo = (key & 0xFFFF) - 32768
        h16_scr[kb] = jnp.where(hi > hi_w, 32767, jnp.where(hi < hi_w, -32768, lo)).astype(I16)
        return carry

    lax.fori_loop(0, n_kb, low_keys, 0)
    thr = hi_p * 65536 + search16()

    cnt_gt, cnt_ge = count(lambda rows: (lambda kb, key, c=bcast(thr, rows): key > c),
                           lambda rows: (lambda kb, key, c=bcast(thr, rows): key >= c))
    need = float(n_sel) - cnt_gt
    xb_scr[...] = jnp.full(xb_scr.shape, 2 ** 31 - 1, I32)

    @pl.when(jnp.max(jnp.abs(cnt_ge - float(n_sel))) > 0.0)
    def _():
        def body(i, x):
            cand = x | lax.shift_left(jnp.int32(1), jnp.int32(col_bits - 1) - i)

            def make_pred(rows):
                c_thr, c_cand = bcast(thr, rows), bcast(cand, rows)
                c_col = lax.broadcasted_iota(I32, (ROW_CHUNK, tk), 1)
                return lambda kb, key: (key == c_thr) & (kb * tk + c_col < c_cand)

            return jnp.where(count(make_pred) < need, cand, x)

        xb_scr[...] = lax.fori_loop(0, col_bits, body, jnp.zeros((tq, LANES), I32))

    xb_w, thr_w = wide(xb_scr[...]), wide(thr)

    def bias_block(kb, carry):
        key = key_scr[kb]
        col = kb * tk + col0
        sel = ((key > thr_w) | ((key == thr_w) & (col <= xb_w))) & (col <= row)
        key_scr[kb] = pltpu.bitcast(jnp.where(sel, 0.0, -jnp.inf), I32)
        return carry

    lax.fori_loop(0, n_kb, bias_block, 0)

    m_scr[...] = jnp.full(m_scr.shape, MASK_VALUE, F32)
    acc_scr[...] = jnp.zeros(acc_scr.shape, F32)

    def att_block(kb, carry):
        start = pl.multiple_of(kb * tk, tk)
        bias = pltpu.bitcast(key_scr[kb], F32)
        for g in range(N_KV_HEADS):
            kg = kb_scr[pl.ds(start, tk), g * HEAD_DIM:(g + 1) * HEAD_DIM]
            va = va_scr[pl.ds(start, tk), 2 * g * HEAD_DIM:(2 * g + 2) * HEAD_DIM]
            base = g * KV_GROUP * tq
            s_all = _dot_nt(qs_scr[base:base + KV_GROUP * tq, :], kg)
            ps, alphas = [], []
            for c in range(KV_GROUP * tq // ATT_ROWS):
                lo = c * ATT_ROWS
                rows = slice(base + lo, base + lo + ATT_ROWS)
                s = s_all[lo:lo + ATT_ROWS] + bias[lo % tq:lo % tq + ATT_ROWS]
                m_old = m_scr[rows, :]
                m_new = jnp.maximum(m_old, jnp.max(_lane_fold(s, jnp.maximum), axis=1, keepdims=True))
                ps.append(jnp.exp2(s - jnp.concatenate([m_new] * n_lane_tiles, axis=1)).astype(BF16))
                alphas.append(jnp.exp2(m_old - m_new))
                m_scr[rows, :] = m_new
            pv = _dot(jnp.concatenate(ps, axis=0), va)
            for c, a in enumerate(alphas):
                lo = c * ATT_ROWS
                rows = slice(base + lo, base + lo + ATT_ROWS)
                acc_scr[rows, :] = jnp.concatenate([a, a], axis=1) * acc_scr[rows, :] + pv[lo:lo + ATT_ROWS]
        return carry

    lax.fori_loop(0, n_kb, att_block, 0)
    for h in range(N_ATT_HEADS):
        acc = acc_scr[h * tq:(h + 1) * tq, :]
        o_ref[:, h * HEAD_DIM:(h + 1) * HEAD_DIM] = (acc[:, :HEAD_DIM] / acc[:, HEAD_DIM:]).astype(o_ref.dtype)


def _attn_prompt(proj, b, t):
    tq = _tile(t, 512)
    tk = min(tq, KEY_BLOCK)
    n_sel = min(TOPK_MAX, t // 4)
    proj3 = proj.reshape(b, t, PACK_WIDTH)
    kern = functools.partial(_attn_prompt_kernel, tq=tq, n_sel=n_sel, col_bits=t.bit_length())
    once = dict(pipeline_mode=pl.Buffered(1))
    out = pl.pallas_call(
        kern,
        out_shape=jax.ShapeDtypeStruct((b, t, ATT_WIDTH), BF16),
        grid=(b, t // tq),
        in_specs=[pl.BlockSpec((None, tq, ATT_WIDTH), lambda bi, qi: (bi, qi, OFF_Q // ATT_WIDTH)),
                  pl.BlockSpec((None, tq, IQ_BLOCK), lambda bi, qi: (bi, qi, OFF_IQ // IQ_BLOCK)),
                  pl.BlockSpec((None, tq, IQ_BLOCK), lambda bi, qi: (bi, qi, OFF_IQ // IQ_BLOCK + 1)),
                  pl.BlockSpec((None, tq, LANES), lambda bi, qi: (bi, qi, OFF_IKW // LANES)),
                  pl.BlockSpec((None, t, KV_WIDTH), lambda bi, qi: (bi, 0, OFF_K // KV_WIDTH), **once),
                  pl.BlockSpec((None, t, KV_WIDTH), lambda bi, qi: (bi, 0, OFF_V // KV_WIDTH), **once),
                  pl.BlockSpec((None, t, LANES), lambda bi, qi: (bi, 0, OFF_IKW // LANES), **once)],
        out_specs=pl.BlockSpec((None, tq, ATT_WIDTH), lambda bi, qi: (bi, qi, 0)),
        scratch_shapes=[pltpu.VMEM((t, KV_WIDTH), BF16),
                        pltpu.VMEM((t, 2 * KV_WIDTH), BF16),
                        pltpu.VMEM((t, IDX_DIM), BF16),
                        pltpu.VMEM((N_IDX_HEADS * tq, IDX_DIM), BF16),
                        pltpu.VMEM((N_ATT_HEADS * tq, HEAD_DIM), BF16),
                        pltpu.VMEM((t // tk, tq, tk), I32),
                        pltpu.VMEM((t // tk, tq, tk), I16),
                        pltpu.VMEM((N_ATT_HEADS * tq, LANES), F32),
                        pltpu.VMEM((N_ATT_HEADS * tq, 2 * HEAD_DIM), F32),
                        pltpu.VMEM((tq, LANES), I32)],
        compiler_params=_params(("parallel", "arbitrary")),
        name="attn_prompt",
    )(proj3, proj3, proj3, proj3, proj3, proj3, proj3)
    return out.reshape(b * t, ATT_WIDTH)


def _hgrn_gates(hq, hf, lb):
    q = hq * jax.nn.sigmoid(hq)
    f = lb + (1.0 - lb) * jax.nn.sigmoid(hf)
    kk = (1.0 - lb) * jax.nn.sigmoid(-hf)
    return q, f, kk


def _hgrn_finish(o, hg, ng):
    o = o * lax.rsqrt(jnp.mean(o * o, axis=-1, keepdims=True) + LN_EPS)
    return o * ng * (hg * jax.nn.sigmoid(hg))


HGRN_HEADS_PER_STEP = 8


def _hgrn_chunk(heads, tri, ones):
    c_len, sub = HGRN_CHUNK, HGRN_SUB
    n_sub = c_len // sub
    t_idx = lax.broadcasted_iota(I32, (sub, 1), 0)
    bs = []
    for q, f, kk, v, st in heads:
        l1, l2, l3 = _split3(jnp.log(f))
        bs.append(_dot(tri, l1) + _dot(tri, l2) + _dot(tri, l3))
    stage2 = []
    for (q, f, kk, v, st), b in zip(heads, bs):
        v_b = v.astype(BF16)
        inter = _dot_nt((q * jnp.exp(b)).astype(BF16), st.astype(BF16))
        rs, cross = [], []
        for j in range(n_sub):
            lo = j * sub
            qt, bt = q[lo:lo + sub], b[lo:lo + sub]
            ps = []
            for s in range(sub):
                valid = t_idx >= s
                dec = jnp.exp(jnp.where(valid, bt - b[lo + s:lo + s + 1, :], 0.0))
                ps.append(jnp.where(valid, qt * kk[lo + s:lo + s + 1, :] * dec, 0.0))
            rs.append(_dot(jnp.concatenate(ps, axis=0).astype(BF16), ones))
            if lo:
                b_edge = b[lo - 1:lo, :]
                qd = (qt * jnp.exp(bt - b_edge)).astype(BF16)
                kd = (kk[:lo] * jnp.exp(b_edge - b[:lo])).astype(BF16)
                cross.append(_dot_nt(qd, kd))
        b_last = b[c_len - 1:c_len, :]
        kd = (kk * jnp.exp(b_last - b)).astype(BF16)
        upd = lax.dot_general(v_b, kd, (((0,), (0,)), ((), ())), preferred_element_type=F32)
        stage2.append((v_b, inter, rs, cross, st * jnp.exp(b_last) + upd))
    out = []
    for (q, f, kk, v, st), (v_b, inter, rs, cross, st_new) in zip(heads, stage2):
        rows = []
        for j in range(n_sub):
            lo = j * sub
            acc = inter[lo:lo + sub]
            if lo:
                acc = acc + _dot(cross[j - 1].astype(BF16), v_b[:lo])
            for s in range(sub):
                acc = acc + rs[j][s * sub:(s + 1) * sub, :] * v[lo + s:lo + s + 1, :]
            rows.append(acc)
        out.append((jnp.concatenate(rows, axis=0), st_new))
    return out


def _hgrn_prompt_kernel(hq_ref, hf_ref, hi_ref, hg_ref, lb_ref, ng_ref, o_ref, s_ref, st_scr, *, n_chunks):
    c_len = HGRN_CHUNK
    ti = pl.program_id(2)

    @pl.when(ti == 0)
    def _():
        st_scr[...] = jnp.zeros(st_scr.shape, F32)

    r_i = lax.broadcasted_iota(I32, (c_len, c_len), 0)
    c_i = lax.broadcasted_iota(I32, (c_len, c_len), 1)
    tri = jnp.where(c_i <= r_i, 1.0, 0.0).astype(BF16)
    ones = jnp.ones((HGRN_DK, HGRN_DV), BF16)

    def chunk(c, carry):
        sl = pl.ds(pl.multiple_of(c * c_len, c_len), c_len)
        cols = [slice(hh * HGRN_DK, (hh + 1) * HGRN_DK) for hh in range(HGRN_HEADS_PER_STEP)]
        heads = []
        for hh, cs in enumerate(cols):
            q, f, kk = _hgrn_gates(hq_ref[sl, cs], hf_ref[sl, cs], lb_ref[:, cs])
            heads.append((q, f, kk, hi_ref[sl, cs], st_scr[hh]))
        for hh, (o, st_new) in enumerate(_hgrn_chunk(heads, tri, ones)):
            st_scr[hh] = st_new
            o_ref[sl, cols[hh]] = _hgrn_finish(o, hg_ref[sl, cols[hh]], ng_ref[:, cols[hh]]).astype(o_ref.dtype)
        return carry

    lax.fori_loop(0, n_chunks, chunk, 0)

    @pl.when(ti == pl.num_programs(2) - 1)
    def _():
        for hh in range(HGRN_HEADS_PER_STEP):
            s_ref[hh] = st_scr[hh].T


def _hgrn_prompt(proj, lb, ng, b, t):
    tc = _tile(t, 512)
    n_chunks = tc // HGRN_CHUNK
    hp = HGRN_HEADS_PER_STEP
    width = hp * HGRN_DK
    proj3 = proj.reshape(b, t, PACK_WIDTH)

    def col(off):
        return pl.BlockSpec((None, tc, width), lambda bi, h, ti: (bi, ti, off // width + h))

    vec = pl.BlockSpec((None, 1, width), lambda bi, h, ti: (h, 0, 0))
    o, s = pl.pallas_call(
        functools.partial(_hgrn_prompt_kernel, n_chunks=n_chunks),
        out_shape=(jax.ShapeDtypeStruct((b, t, HGRN_WIDTH), BF16),
                   jax.ShapeDtypeStruct((b, N_HGRN_HEADS, HGRN_DK, HGRN_DV), F32)),
        grid=(b, N_HGRN_HEADS // hp, t // tc),
        in_specs=[col(OFF_HQ), col(OFF_HF), col(OFF_HI), col(OFF_HG), vec, vec],
        out_specs=(pl.BlockSpec((None, tc, width), lambda bi, h, ti: (bi, ti, h)),
                   pl.BlockSpec((None, hp, HGRN_DK, HGRN_DV), lambda bi, h, ti: (bi, h, 0, 0))),
        scratch_shapes=[pltpu.VMEM((hp, HGRN_DV, HGRN_DK), F32)],
        compiler_params=_params(("parallel", "parallel", "arbitrary")),
        name="hgrn_prompt",
    )(proj3, proj3, proj3, proj3,
      lb.reshape(N_HGRN_HEADS // hp, 1, width), ng.reshape(N_HGRN_HEADS // hp, 1, width))
    return o.reshape(b * t, HGRN_WIDTH), s


def _merge_kernel(oa_ref, ob_ref, ga_ref, gb_ref, wa_ref, wb_ref, o_ref):
    ya = _dot(oa_ref[...].astype(BF16), wa_ref[...])
    yb = _dot(ob_ref[...].astype(BF16), wb_ref[...])
    o_ref[...] = (jax.nn.sigmoid(ga_ref[...]) * ya + jax.nn.sigmoid(gb_ref[...]) * yb).astype(o_ref.dtype)


def _merge(o_a, o_b, proj, w_up_a, w_up_b, l, tm):
    n = o_a.shape[0]
    return pl.pallas_call(
        _merge_kernel,
        out_shape=jax.ShapeDtypeStruct((n, D_MODEL), BF16),
        grid=(n // tm,),
        in_specs=[pl.BlockSpec((tm, ATT_WIDTH), lambda i: (i, 0)),
                  pl.BlockSpec((tm, HGRN_WIDTH), lambda i: (i, 0)),
                  pl.BlockSpec((tm, D_MODEL), lambda i: (i, OFF_GA // D_MODEL)),
                  pl.BlockSpec((tm, D_MODEL), lambda i: (i, OFF_GB // D_MODEL)),
                  pl.BlockSpec((None, ATT_WIDTH, D_MODEL), lambda i: (l, 0, 0)),
                  pl.BlockSpec((None, HGRN_WIDTH, D_MODEL), lambda i: (l, 0, 0))],
        out_specs=pl.BlockSpec((tm, D_MODEL), lambda i: (i, 0)),
        compiler_params=_params(("parallel",)),
        name="merge",
    )(o_a, o_b, proj, proj, w_up_a, w_up_b)


def _post_norm(x, gate, y, g, b):
    return _ln(ALPHA * x + gate * y) * g + b


def _out_kernel(m_ref, w_ref, x_ref, gate_ref, lng_ref, lnb_ref, o_ref):
    y = _dot(m_ref[...], w_ref[...])
    o_ref[...] = _post_norm(x_ref[...], gate_ref[0], y, lng_ref[...], lnb_ref[...])


def _out_proj(merged, w_o, x, gate, ln_g, ln_b, l, tm):
    n = x.shape[0]
    tiles_per_mod = (n // tm) // gate.shape[0]
    vec = pl.BlockSpec((None, 1, D_MODEL), lambda i: (l, 0, 0))
    return pl.pallas_call(
        _out_kernel,
        out_shape=jax.ShapeDtypeStruct((n, D_MODEL), F32),
        grid=(n // tm,),
        in_specs=[pl.BlockSpec((tm, D_MODEL), lambda i: (i, 0)),
                  pl.BlockSpec((None, D_MODEL, D_MODEL), lambda i: (l, 0, 0)),
                  pl.BlockSpec((tm, D_MODEL), lambda i: (i, 0)),
                  pl.BlockSpec((1, gate.shape[1], D_MODEL), lambda i: (i // tiles_per_mod, 0, 0)),
                  vec, vec],
        out_specs=pl.BlockSpec((tm, D_MODEL), lambda i: (i, 0)),
        compiler_params=_params(("parallel",)),
        name="out_proj",
    )(merged, w_o, x, gate, ln_g.reshape(DEPTH, 1, D_MODEL), ln_b.reshape(DEPTH, 1, D_MODEL))


def _route(h, wr, br):
    h1, h2, _ = _split3(h)
    w1, w2, _ = _split3(wr)
    logits = _dot_nt(w1, h1) + _dot_nt(w1, h2) + _dot_nt(w2, h1)
    aff = jax.nn.sigmoid(logits)
    sel = aff + br
    rows = [sel[e:e + 1, :] for e in range(N_EXPERTS)]
    grp = []
    for g in range(N_GROUPS):
        a, b, c, d = rows[g * EXPERTS_PER_GROUP:(g + 1) * EXPERTS_PER_GROUP]
        hi1, lo1 = jnp.maximum(a, b), jnp.minimum(a, b)
        hi2, lo2 = jnp.maximum(c, d), jnp.minimum(c, d)
        grp.append(jnp.maximum(hi1, hi2) + jnp.maximum(jnp.minimum(hi1, hi2), jnp.maximum(lo1, lo2)))
    best = functools.reduce(jnp.maximum, grp)
    taken = jnp.zeros_like(best)
    picked = []
    for g in range(N_GROUPS):
        is_g = jnp.where(grp[g] == best, 1.0, 0.0) * (1.0 - taken)
        taken = taken + is_g
        for e in range(g * EXPERTS_PER_GROUP, (g + 1) * EXPERTS_PER_GROUP):
            rank = jnp.zeros_like(best)
            for o in range(g * EXPERTS_PER_GROUP, (g + 1) * EXPERTS_PER_GROUP):
                if o < e:
                    rank = rank + jnp.where(rows[o] >= rows[e], 1.0, 0.0)
                elif o > e:
                    rank = rank + jnp.where(rows[o] > rows[e], 1.0, 0.0)
            picked.append(is_g * jnp.where(rank < 2.0, 1.0, 0.0))
    picked = jnp.concatenate(picked, axis=0)
    gate = picked * aff
    return picked, gate / jnp.sum(gate, axis=0, keepdims=True)


def _router_kernel(x_ref, sh_ref, sc_ref, wr_ref, br_ref, h_ref, g_ref):
    h = _ln(x_ref[...]) * (1.0 + sc_ref[0]) + sh_ref[0]
    h_ref[...] = h.astype(h_ref.dtype)
    _, g_ref[...] = _route(h, wr_ref[...], br_ref[...])


def _router_sorted_kernel(x_ref, sh_ref, sc_ref, wr_ref, br_ref, h_ref, eid_ref, gw_ref, pos_ref, cnt_ref, base_scr):
    tm = x_ref.shape[0]

    @pl.when(pl.program_id(0) == 0)
    def _():
        base_scr[...] = jnp.zeros(base_scr.shape, F32)

    h = _ln(x_ref[...]) * (1.0 + sc_ref[0]) + sh_ref[0]
    h_ref[...] = h.astype(h_ref.dtype)
    picked, gate = _route(h, wr_ref[...], br_ref[...])
    r = lax.broadcasted_iota(I32, (tm, tm), 0)
    c = lax.broadcasted_iota(I32, (tm, tm), 1)
    earlier = jnp.where(r < c, 1.0, 0.0).astype(BF16)
    base = base_scr[...]
    rank = _dot(picked.astype(BF16), earlier) + jnp.concatenate([base] * (tm // LANES), axis=1)
    base_scr[...] = base + jnp.sum(picked, axis=1, keepdims=True)
    cnt_ref[...] = base_scr[...]
    eio = lax.broadcasted_iota(I32, picked.shape, 0).astype(F32)
    e_lo = jnp.min(jnp.where(picked > 0.0, eio, float(N_EXPERTS)), axis=0, keepdims=True)
    e_hi = jnp.max(jnp.where(picked > 0.0, eio, -1.0), axis=0, keepdims=True)

    def pick(e, x):
        return jnp.sum(jnp.where(eio == e, x, 0.0), axis=0, keepdims=True)

    eid_ref[...] = jnp.concatenate([e_lo, e_hi], axis=0).astype(I32)
    gw_ref[...] = jnp.concatenate([pick(e_lo, gate), pick(e_hi, gate)], axis=0)
    pos_ref[...] = jnp.concatenate([pick(e_lo, rank), pick(e_hi, rank)], axis=0).astype(I32)


def _router_sorted(x, shift, scale, w_router_t, b_router, tm):
    n = x.shape[0]
    tiles_per_mod = (n // tm) // shift.shape[0]
    mod_spec = pl.BlockSpec((1, shift.shape[1], D_MODEL), lambda i: (i // tiles_per_mod, 0, 0))
    pair = pl.BlockSpec((2, tm), lambda i: (0, i))
    return pl.pallas_call(
        _router_sorted_kernel,
        out_shape=(jax.ShapeDtypeStruct((n, D_MODEL), F32),
                   jax.ShapeDtypeStruct((2, n), I32),
                   jax.ShapeDtypeStruct((2, n), F32),
                   jax.ShapeDtypeStruct((2, n), I32),
                   jax.ShapeDtypeStruct((N_EXPERTS, LANES), F32)),
        grid=(n // tm,),
        in_specs=[pl.BlockSpec((tm, D_MODEL), lambda i: (i, 0)),
                  mod_spec, mod_spec,
                  pl.BlockSpec((N_EXPERTS, D_MODEL), lambda i: (0, 0)),
                  pl.BlockSpec((N_EXPERTS, 1), lambda i: (0, 0))],
        out_specs=(pl.BlockSpec((tm, D_MODEL), lambda i: (i, 0)), pair, pair, pair,
                   pl.BlockSpec((N_EXPERTS, LANES), lambda i: (0, 0))),
        scratch_shapes=[pltpu.VMEM((N_EXPERTS, LANES), F32)],
        compiler_params=_params(("arbitrary",)),
        name="router_sorted",
    )(x, shift, scale, w_router_t, b_router.reshape(N_EXPERTS, 1))


MOE_TILE = 256
DMA_UNROLL = 32


def _dispatch_kernel(dst_ref, ends_ref, h_ref, xs_ref, zero_scr, sem, zsem, *, td, n):
    first = pl.program_id(0) * td

    @pl.when(pl.program_id(0) == 0)
    def _():
        zero_scr[...] = jnp.zeros(zero_scr.shape, F32)
        n_slots = xs_ref.shape[0]

        def zero_copy(start):
            return pltpu.make_async_copy(zero_scr, xs_ref.at[pl.ds(pl.multiple_of(start, MOE_TILE), MOE_TILE)], zsem)

        jobs = []
        for e in range(N_EXPERTS):
            jobs.append((ends_ref[e + 1] > ends_ref[e], ends_ref[e + 1] - MOE_TILE))
            unused = ends_ref[N_EXPERTS] + e * MOE_TILE
            jobs.append((unused < n_slots, jnp.minimum(unused, n_slots - MOE_TILE)))
        for wanted, start in jobs:
            @pl.when(wanted)
            def _(start=start):
                zero_copy(start).start()
        for wanted, start in jobs:
            @pl.when(wanted)
            def _(start=start):
                zero_copy(start).wait()

    def copies(r):
        return [pltpu.make_async_copy(h_ref.at[pl.ds(r, 1)], xs_ref.at[pl.ds(dst_ref[k * n + first + r], 1)], sem)
                for k in range(2)]

    def start(r, carry):
        for k, cp in enumerate(copies(r)):
            cp.start(priority=k)
        return carry

    def wait(r, carry):
        for cp in copies(r):
            cp.wait()
        return carry

    lax.fori_loop(0, td, start, 0, unroll=DMA_UNROLL)
    lax.fori_loop(0, td, wait, 0, unroll=DMA_UNROLL)


def _dispatch(h, dst_flat, seg_ends, n_slots, td):
    n = h.shape[0]
    grid_spec = pltpu.PrefetchScalarGridSpec(
        num_scalar_prefetch=2,
        grid=(n // td,),
        in_specs=[pl.BlockSpec((td, D_MODEL), lambda i, dst, ends: (i, 0))],
        out_specs=pl.BlockSpec(memory_space=pl.ANY),
        scratch_shapes=[pltpu.VMEM((MOE_TILE, D_MODEL), F32),
                        pltpu.SemaphoreType.DMA,
                        pltpu.SemaphoreType.DMA])
    return pl.pallas_call(
        functools.partial(_dispatch_kernel, td=td, n=n),
        out_shape=jax.ShapeDtypeStruct((n_slots, D_MODEL), F32),
        grid_spec=grid_spec,
        compiler_params=_params(("arbitrary",)),
        name="moe_dispatch",
    )(dst_flat, seg_ends, h)


def _experts_kernel(te_ref, tv_ref, xs_ref, w1_ref, w3_ref, w2_ref, ys_ref):
    del te_ref
    valid = tv_ref[pl.program_id(0)] > 0

    @pl.when(valid)
    def _():
        x = xs_ref[...].astype(BF16)
        a = _dot(x, w1_ref[...])
        b = _dot(x, w3_ref[...])
        ys_ref[...] = _dot((a * jax.nn.sigmoid(a) * b).astype(BF16), w2_ref[...])

    @pl.when(jnp.logical_not(valid))
    def _():
        ys_ref[...] = jnp.zeros(ys_ref.shape, F32)


def _experts(xs, tile_expert, tile_valid, w1, w3, w2, l):
    n_tiles = xs.shape[0] // MOE_TILE
    grid_spec = pltpu.PrefetchScalarGridSpec(
        num_scalar_prefetch=2,
        grid=(n_tiles,),
        in_specs=[pl.BlockSpec((MOE_TILE, D_MODEL), lambda i, te, tv: (jnp.minimum(i, tv[0] - 1), 0)),
                  pl.BlockSpec((None, None, D_MODEL, EXPERT_DFF), lambda i, te, tv: (l, te[i], 0, 0)),
                  pl.BlockSpec((None, None, D_MODEL, EXPERT_DFF), lambda i, te, tv: (l, te[i], 0, 0)),
                  pl.BlockSpec((None, None, EXPERT_DFF, D_MODEL), lambda i, te, tv: (l, te[i], 0, 0))],
        out_specs=pl.BlockSpec((MOE_TILE, D_MODEL), lambda i, te, tv: (i, 0)))
    return pl.pallas_call(
        _experts_kernel,
        out_shape=jax.ShapeDtypeStruct(xs.shape, F32),
        grid_spec=grid_spec,
        compiler_params=_params(("arbitrary",)),
        name="moe_experts",
    )(tile_expert, tile_valid, xs, w1, w3, w2)


def _combine_kernel(dst_ref, ys_ref, gw_ref, x_ref, gate_ref, lng_ref, lnb_ref, o_ref, buf, sem, *, tc, n):
    i = pl.program_id(0)
    slot = i % 2

    def copies(tile, s, r):
        return [pltpu.make_async_copy(ys_ref.at[pl.ds(dst_ref[k * n + tile * tc + r], 1)],
                                      buf.at[s, k, pl.ds(r, 1)], sem.at[s]) for k in range(2)]

    def start_tile(tile, s):
        def body(r, carry):
            for k, cp in enumerate(copies(tile, s, r)):
                cp.start(priority=k)
            return carry
        lax.fori_loop(0, tc, body, 0, unroll=DMA_UNROLL)

    def wait_tile(tile, s):
        def body(r, carry):
            for cp in copies(tile, s, r):
                cp.wait()
            return carry
        lax.fori_loop(0, tc, body, 0, unroll=DMA_UNROLL)

    @pl.when(i == 0)
    def _():
        start_tile(0, 0)

    @pl.when(i + 1 < pl.num_programs(0))
    def _():
        start_tile(i + 1, 1 - slot)

    wait_tile(i, slot)
    gw = gw_ref[...]
    y = gw[:, 0:1] * buf[slot, 0] + gw[:, 1:2] * buf[slot, 1]
    o_ref[...] = _post_norm(x_ref[...], gate_ref[0], y, lng_ref[...], lnb_ref[...])


def _combine(ys, dst_flat, gw, x, gate, ln_g, ln_b, l, tc):
    n = x.shape[0]
    tiles_per_mod = (n // tc) // gate.shape[0]
    vec = pl.BlockSpec((None, 1, D_MODEL), lambda i, dst: (l, 0, 0))
    grid_spec = pltpu.PrefetchScalarGridSpec(
        num_scalar_prefetch=1,
        grid=(n // tc,),
        in_specs=[pl.BlockSpec(memory_space=pl.ANY),
                  pl.BlockSpec((tc, 2), lambda i, dst: (i, 0)),
                  pl.BlockSpec((tc, D_MODEL), lambda i, dst: (i, 0)),
                  pl.BlockSpec((1, gate.shape[1], D_MODEL), lambda i, dst: (i // tiles_per_mod, 0, 0)),
                  vec, vec],
        out_specs=pl.BlockSpec((tc, D_MODEL), lambda i, dst: (i, 0)),
        scratch_shapes=[pltpu.VMEM((2, 2, tc, D_MODEL), F32),
                        pltpu.SemaphoreType.DMA((2,))])
    return pl.pallas_call(
        functools.partial(_combine_kernel, tc=tc, n=n),
        out_shape=jax.ShapeDtypeStruct((n, D_MODEL), F32),
        grid_spec=grid_spec,
        compiler_params=_params(("arbitrary",)),
        name="moe_combine",
    )(dst_flat, ys, gw, x, gate, ln_g.reshape(DEPTH, 1, D_MODEL), ln_b.reshape(DEPTH, 1, D_MODEL))


def _moe_grouped(x, shift, scale, gate, w_router_t, b_router, w1, w3, w2, ln_g, ln_b, l, tm):
    n = x.shape[0]
    h, eid, gw, pos, cnt = _router_sorted(x, shift, scale, w_router_t, b_router, tm)
    counts = cnt[:, 0].astype(I32)
    padded = (counts + MOE_TILE - 1) // MOE_TILE * MOE_TILE
    ends = jnp.cumsum(padded)
    experts = jnp.arange(N_EXPERTS, dtype=I32)[:, None, None]
    seg_start = jnp.sum(jnp.where(eid[None] == experts, (ends - padded)[:, None, None], 0), axis=0)
    dst_flat = (seg_start + pos).reshape(2 * n)
    n_tiles = 2 * n // MOE_TILE + N_EXPERTS
    tile_start = jnp.arange(n_tiles, dtype=I32) * MOE_TILE
    tile_expert = jnp.minimum(jnp.sum((tile_start[:, None] >= ends[None, :]).astype(I32), axis=1), N_EXPERTS - 1)
    tile_valid = jnp.where(tile_start < ends[-1], ends[-1] // MOE_TILE, 0).astype(I32)
    seg_ends = jnp.concatenate([jnp.zeros((1,), I32), ends.astype(I32)])
    xs = _dispatch(h, dst_flat, seg_ends, n_tiles * MOE_TILE, tm)
    ys = _experts(xs, tile_expert, tile_valid, w1, w3, w2, l)
    return _combine(ys, dst_flat, gw.T, x, gate, ln_g, ln_b, l, MOE_TILE)


def _router(x, shift, scale, w_router_t, b_router, tm):
    n = x.shape[0]
    tiles_per_mod = (n // tm) // shift.shape[0]
    mod_spec = pl.BlockSpec((1, shift.shape[1], D_MODEL), lambda i: (i // tiles_per_mod, 0, 0))
    return pl.pallas_call(
        _router_kernel,
        out_shape=(jax.ShapeDtypeStruct((n, D_MODEL), BF16),
                   jax.ShapeDtypeStruct((N_EXPERTS, n), F32)),
        grid=(n // tm,),
        in_specs=[pl.BlockSpec((tm, D_MODEL), lambda i: (i, 0)),
                  mod_spec, mod_spec,
                  pl.BlockSpec((N_EXPERTS, D_MODEL), lambda i: (0, 0)),
                  pl.BlockSpec((N_EXPERTS, 1), lambda i: (0, 0))],
        out_specs=(pl.BlockSpec((tm, D_MODEL), lambda i: (i, 0)),
                   pl.BlockSpec((N_EXPERTS, tm), lambda i: (0, i))),
        compiler_params=_params(("parallel",)),
        name="router",
    )(x, shift, scale, w_router_t, b_router.reshape(N_EXPERTS, 1))


def _moe_kernel(h_ref, g_ref, w1_ref, w3_ref, w2_ref, x_ref, gate_ref, lng_ref, lnb_ref, o_ref, acc_scr):
    e = pl.program_id(1)

    @pl.when(e == 0)
    def _():
        acc_scr[...] = jnp.zeros(acc_scr.shape, F32)

    h = h_ref[...]
    a = _dot(h, w1_ref[...])
    b = _dot(h, w3_ref[...])
    gates = g_ref[...]
    lane = lax.broadcasted_iota(I32, gates.shape, 1)
    gcol = jnp.sum(jnp.where(lane == e, gates, 0.0), axis=1, keepdims=True)
    u = (a * jax.nn.sigmoid(a) * b * gcol).astype(BF16)
    acc_scr[...] += _dot(u, w2_ref[...])

    @pl.when(e == pl.num_programs(1) - 1)
    def _():
        o_ref[...] = _post_norm(x_ref[...], gate_ref[0], acc_scr[...], lng_ref[...], lnb_ref[...])


def _moe(h, gates, w1, w3, w2, x, gate, ln_g, ln_b, l, tm):
    n = x.shape[0]
    tiles_per_mod = (n // tm) // gate.shape[0]
    vec = pl.BlockSpec((None, 1, D_MODEL), lambda i, e: (l, 0, 0))
    return pl.pallas_call(
        _moe_kernel,
        out_shape=jax.ShapeDtypeStruct((n, D_MODEL), F32),
        grid=(n // tm, N_EXPERTS),
        in_specs=[pl.BlockSpec((tm, D_MODEL), lambda i, e: (i, 0)),
                  pl.BlockSpec((tm, N_EXPERTS), lambda i, e: (i, 0)),
                  pl.BlockSpec((None, None, D_MODEL, EXPERT_DFF), lambda i, e: (l, e, 0, 0)),
                  pl.BlockSpec((None, None, D_MODEL, EXPERT_DFF), lambda i, e: (l, e, 0, 0)),
                  pl.BlockSpec((None, None, EXPERT_DFF, D_MODEL), lambda i, e: (l, e, 0, 0)),
                  pl.BlockSpec((tm, D_MODEL), lambda i, e: (i, 0)),
                  pl.BlockSpec((1, gate.shape[1], D_MODEL), lambda i, e: (i // tiles_per_mod, 0, 0)),
                  vec, vec],
        out_specs=pl.BlockSpec((tm, D_MODEL), lambda i, e: (i, 0)),
        scratch_shapes=[pltpu.VMEM((tm, D_MODEL), F32)],
        compiler_params=_params(("parallel", "arbitrary")),
        name="moe",
    )(h, gates, w1, w3, w2, x, gate, ln_g.reshape(DEPTH, 1, D_MODEL), ln_b.reshape(DEPTH, 1, D_MODEL))


def _row_to_col(v):
    n = v.shape[1]
    r = lax.broadcasted_iota(I32, (n, n), 0)
    c = lax.broadcasted_iota(I32, (n, n), 1)
    return jnp.sum(jnp.where(r == c, jnp.broadcast_to(v, (n, n)), 0.0), axis=1, keepdims=True)


def _attn_decode_kernel(pt_ref, proj_ref, ck_ref, cv_ref, cik_ref, o_ref,
                        ik_buf, k_buf, v_buf, key_scr, xb_scr, sem_ik, sem_k, sem_v,
                        *, l, n_pages, page, chunk_pages, n_sel):
    b = pl.program_id(0)
    n_chunks = n_pages // chunk_pages
    chunk = chunk_pages * page

    def ik_copy(j):
        return pltpu.make_async_copy(cik_ref.at[l, pt_ref[b, j]], ik_buf.at[pl.ds(j * page, page)], sem_ik)

    def kv_copies(c, j, slot):
        pg = pt_ref[b, c * chunk_pages + j]
        dst = pl.ds(j * page, page)
        copies = []
        for g in range(N_KV_HEADS):
            copies.append(pltpu.make_async_copy(ck_ref.at[l, pg, :, g, :], k_buf.at[slot, g, dst], sem_k.at[slot]))
            copies.append(pltpu.make_async_copy(cv_ref.at[l, pg, :, g, :], v_buf.at[slot, g, dst], sem_v.at[slot]))
        return copies

    def start_chunk(c, slot):
        def body(j, carry):
            for cp in kv_copies(c, j, slot):
                cp.start()
            return carry
        lax.fori_loop(0, chunk_pages, body, 0)

    def wait_chunk(c, slot):
        def body(j, carry):
            for cp in kv_copies(c, j, slot):
                cp.wait()
            return carry
        lax.fori_loop(0, chunk_pages, body, 0)

    def start_ik(j, carry):
        ik_copy(j).start()
        return carry

    def wait_ik(j, carry):
        ik_copy(j).wait()
        return carry

    lax.fori_loop(0, n_pages, start_ik, 0)
    start_chunk(0, 0)
    lax.fori_loop(0, n_pages, wait_ik, 0)

    rowsl = slice(None)
    iq = proj_ref[rowsl, OFF_IQ:OFF_IQ + N_IDX_HEADS * IDX_DIM]
    iq_h = jnp.concatenate([iq[:, h * IDX_DIM:(h + 1) * IDX_DIM] for h in range(N_IDX_HEADS)], axis=0)
    w_row = proj_ref[rowsl, OFF_IKW + IDX_DIM:OFF_IKW + IDX_DIM + N_IDX_HEADS]
    w_col = _row_to_col(w_row) * (IDX_DIM ** -0.5 * N_IDX_HEADS ** -0.5)
    ik_new = proj_ref[rowsl, OFF_IKW:OFF_IKW + IDX_DIM]

    iq_b = iq_h.astype(BF16)
    for c in range(n_chunks):
        lg = _dot_nt(iq_b, ik_buf[c * chunk:(c + 1) * chunk, :].astype(BF16))
        s = jnp.sum(w_col * jnp.maximum(lg, 0.0), axis=0, keepdims=True)
        key_scr[c] = _sort_key(s)
    lg_new = jnp.sum(iq_b.astype(F32) * ik_new.astype(BF16).astype(F32), axis=1, keepdims=True)
    key_new = _sort_key(jnp.sum(w_col * jnp.maximum(lg_new, 0.0), axis=0, keepdims=True))

    def count_ge(cand):
        cnt = jnp.where(key_new >= cand, 1.0, 0.0)
        for c in range(n_chunks):
            cnt = cnt + jnp.sum(jnp.where(key_scr[c] >= cand, 1.0, 0.0), axis=1, keepdims=True)
        return cnt

    thr = _kth_largest_key(count_ge, float(n_sel), (1, 1))

    past = n_pages * page
    lane_pos = lax.broadcasted_iota(I32, (1, chunk), 1)

    def count(pred):
        cnt = jnp.where(pred(key_new, past), 1.0, 0.0)
        for c in range(n_chunks):
            cnt = cnt + jnp.sum(jnp.where(pred(key_scr[c], c * chunk + lane_pos), 1.0, 0.0), axis=1, keepdims=True)
        return cnt

    need = float(n_sel) - count(lambda key, pos: key > thr)
    xb_scr[...] = jnp.full(xb_scr.shape, 2 ** 31 - 1, I32)

    @pl.when(jnp.max(jnp.abs(count(lambda key, pos: key >= thr) - float(n_sel))) > 0.0)
    def _():
        x = jnp.zeros((1, 1), I32)
        for bit in range((past + 1).bit_length() - 1, -1, -1):
            cand = x | (1 << bit)
            x = jnp.where(count(lambda key, pos: (key == thr) & (pos < cand)) < need, cand, x)
        xb_scr[...] = jnp.broadcast_to(x, xb_scr.shape)

    xb = xb_scr[:, 0:1]

    def selected(key, pos):
        return (key > thr) | ((key == thr) & (pos <= xb))

    q = proj_ref[rowsl, OFF_Q:OFF_Q + ATT_WIDTH]
    q_h = jnp.concatenate([q[:, h * HEAD_DIM:(h + 1) * HEAD_DIM] for h in range(N_ATT_HEADS)], axis=0)
    q_b = q_h.astype(BF16)
    k_new = proj_ref[rowsl, OFF_K:OFF_K + KV_WIDTH]
    v_new = proj_ref[rowsl, OFF_V:OFF_V + KV_WIDTH]
    scale = HEAD_DIM ** -0.5

    def att_chunk(c, carry):
        slot = c % 2

        @pl.when(c + 1 < n_chunks)
        def _():
            start_chunk(c + 1, 1 - slot)

        wait_chunk(c, slot)
        sel = selected(key_scr[c], c * chunk + lane_pos)
        out = []
        for g in range(N_KV_HEADS):
            m_old, l_old, acc_old = carry[g]
            kg = k_buf[slot, g].astype(BF16)
            vg = v_buf[slot, g].astype(BF16)
            s = _dot_nt(q_b[g * KV_GROUP:(g + 1) * KV_GROUP], kg) * scale
            s = jnp.where(sel, s, MASK_VALUE)
            m_new = jnp.maximum(m_old, jnp.max(s, axis=1, keepdims=True))
            p = jnp.where(sel, jnp.exp(s - m_new), 0.0)
            a = jnp.exp(m_old - m_new)
            out.append((m_new, a * l_old + jnp.sum(p, axis=1, keepdims=True),
                        a * acc_old + _dot(p.astype(BF16), vg)))
        return tuple(out)

    init = tuple((jnp.full((KV_GROUP, 1), MASK_VALUE, F32), jnp.zeros((KV_GROUP, 1), F32),
                  jnp.zeros((KV_GROUP, HEAD_DIM), F32)) for _ in range(N_KV_HEADS))
    res = lax.fori_loop(0, n_chunks, att_chunk, init)

    sel_new = selected(key_new, past)
    for g in range(N_KV_HEADS):
        m_old, l_old, acc_old = res[g]
        kg = k_new[:, g * HEAD_DIM:(g + 1) * HEAD_DIM].astype(BF16).astype(F32)
        vg = v_new[:, g * HEAD_DIM:(g + 1) * HEAD_DIM].astype(BF16).astype(F32)
        qg = q_b[g * KV_GROUP:(g + 1) * KV_GROUP].astype(F32)
        s = jnp.sum(qg * kg, axis=1, keepdims=True) * scale
        s = jnp.where(sel_new, s, MASK_VALUE)
        m_new = jnp.maximum(m_old, s)
        p = jnp.where(sel_new, jnp.exp(s - m_new), 0.0)
        a = jnp.exp(m_old - m_new)
        l_new = a * l_old + p
        acc = a * acc_old + p.astype(BF16).astype(F32) * vg
        o = acc / l_new
        for r in range(KV_GROUP):
            hh = g * KV_GROUP + r
            o_ref[rowsl, hh * HEAD_DIM:(hh + 1) * HEAD_DIM] = o[r:r + 1, :].astype(o_ref.dtype)


def _attn_decode(proj, page_table, cache_k, cache_v, cache_idx_k, l):
    db, n_pages = page_table.shape
    page = cache_k.shape[2]
    past = n_pages * page
    n_sel = min(TOPK_MAX, (past + 1) // 4)
    chunk_pages = _tile(n_pages, 32)
    kern = functools.partial(_attn_decode_kernel, l=l, n_pages=n_pages, page=page,
                             chunk_pages=chunk_pages, n_sel=n_sel)
    grid_spec = pltpu.PrefetchScalarGridSpec(
        num_scalar_prefetch=1,
        grid=(db,),
        in_specs=[pl.BlockSpec((None, 1, PACK_WIDTH), lambda b, pt: (b, 0, 0)),
                  pl.BlockSpec(memory_space=pl.ANY),
                  pl.BlockSpec(memory_space=pl.ANY),
                  pl.BlockSpec(memory_space=pl.ANY)],
        out_specs=pl.BlockSpec((None, 1, ATT_WIDTH), lambda b, pt: (b, 0, 0)),
        scratch_shapes=[pltpu.VMEM((past, IDX_DIM), F32),
                        pltpu.VMEM((2, N_KV_HEADS, chunk_pages * page, HEAD_DIM), F32),
                        pltpu.VMEM((2, N_KV_HEADS, chunk_pages * page, HEAD_DIM), F32),
                        pltpu.VMEM((n_pages // chunk_pages, 1, chunk_pages * page), I32),
                        pltpu.VMEM((1, LANES), I32),
                        pltpu.SemaphoreType.DMA,
                        pltpu.SemaphoreType.DMA((2,)),
                        pltpu.SemaphoreType.DMA((2,))])
    return pl.pallas_call(
        kern,
        out_shape=jax.ShapeDtypeStruct((db, 1, ATT_WIDTH), F32),
        grid_spec=grid_spec,
        compiler_params=_params(("arbitrary",)),
        name="attn_decode",
    )(page_table, proj.reshape(db, 1, PACK_WIDTH), cache_k, cache_v, cache_idx_k).reshape(db, ATT_WIDTH)


def _hgrn_decode_kernel(proj_ref, s0_ref, lb_ref, ng_ref, o_ref, s_ref):
    rowsl = slice(None)
    for h in range(N_HGRN_HEADS):
        cols = lambda off: slice(off + h * HGRN_DK, off + (h + 1) * HGRN_DK)
        lb = lb_ref[:, h * HGRN_DK:(h + 1) * HGRN_DK]
        q, f, kk = _hgrn_gates(proj_ref[rowsl, cols(OFF_HQ)], proj_ref[rowsl, cols(OFF_HF)], lb)
        v = proj_ref[rowsl, cols(OFF_HI)]
        s_new = _row_to_col(f) * s0_ref[h] + _row_to_col(kk) * v
        s_ref[h] = s_new
        o = jnp.sum(_row_to_col(q) * s_new, axis=0, keepdims=True)
        ng = ng_ref[:, h * HGRN_DV:(h + 1) * HGRN_DV]
        o_ref[rowsl, h * HGRN_DV:(h + 1) * HGRN_DV] = _hgrn_finish(
            o, proj_ref[rowsl, cols(OFF_HG)], ng).astype(o_ref.dtype)


def _hgrn_decode(proj, state, lb, ng, l):
    db = proj.shape[0]
    st_spec_in = pl.BlockSpec((None, None, N_HGRN_HEADS, HGRN_DK, HGRN_DV), lambda b: (l, b, 0, 0, 0))
    o, s = pl.pallas_call(
        _hgrn_decode_kernel,
        out_shape=(jax.ShapeDtypeStruct((db, 1, HGRN_WIDTH), F32),
                   jax.ShapeDtypeStruct((db, N_HGRN_HEADS, HGRN_DK, HGRN_DV), F32)),
        grid=(db,),
        in_specs=[pl.BlockSpec((None, 1, PACK_WIDTH), lambda b: (b, 0, 0)),
                  st_spec_in,
                  pl.BlockSpec((1, HGRN_WIDTH), lambda b: (0, 0)),
                  pl.BlockSpec((1, HGRN_WIDTH), lambda b: (0, 0))],
        out_specs=(pl.BlockSpec((None, 1, HGRN_WIDTH), lambda b: (b, 0, 0)),
                   pl.BlockSpec((None, N_HGRN_HEADS, HGRN_DK, HGRN_DV), lambda b: (b, 0, 0, 0))),
        compiler_params=_params(("parallel",)),
        name="hgrn_decode",
    )(proj.reshape(db, 1, PACK_WIDTH), state, lb.reshape(1, HGRN_WIDTH), ng.reshape(1, HGRN_WIDTH))
    return o.reshape(db, HGRN_WIDTH), s


def _split_w_in(w_in):
    return w_in[:, :, TAIL_START:].astype(BF16), w_in[:, :, :HEAD_COLS].astype(BF16)


def _mods(mod_l, rows, per_row):
    m = mod_l[rows]
    parts = jnp.split(m, 6, axis=-1)
    if per_row:
        return [p[None, :, :] for p in parts]
    return [p[:, None, :] for p in parts]


def kernel(x_prompt, x_sample, c_prompt, c_sample, cache_k, cache_v, cache_idx_k, state_hgrn, page_table,
           w_ada, b_ada, w_in, w_up_a, w_up_b, w_o, hgrn_norm_g, hgrn_lb_logits, ln1_g, ln1_b,
           w_router, b_router, w1, w3, w2, ln2_g, ln2_b):
    bp, t, _ = x_prompt.shape
    db = x_sample.shape[0]
    lbp = jax.nn.softmax(hgrn_lb_logits.astype(F32), axis=0)
    lower_bounds = jnp.cumsum(lbp, axis=0) - lbp[0]

    w_tail, w_head = _split_w_in(w_in)
    w_up_a_b, w_up_b_b, w_o_b = w_up_a.astype(BF16), w_up_b.astype(BF16), w_o.astype(BF16)
    w1_b, w3_b, w2_b = w1.astype(BF16), w3.astype(BF16), w2.astype(BF16)
    w_router_t = w_router.T

    n_c = bp + db
    c_rows = -(-n_c // 8) * 8
    c_all = jnp.concatenate([c_prompt, c_sample, jnp.zeros((c_rows - n_c, D_MODEL), F32)], axis=0)
    mod = _ada(c_all, w_ada, b_ada)

    xp = x_prompt.reshape(bp * t, D_MODEL)
    xs = x_sample.reshape(db, D_MODEL)
    tm_p = _tile(bp * t, 512)
    outs_p = {"k": [], "v": [], "ik": [], "s": []}
    outs_s = {"k": [], "v": [], "ik": [], "s": []}
    for l in range(DEPTH):
        sh1, sc1, g1, sh2, sc2, g2 = _mods(mod[l], slice(0, bp), per_row=False)
        proj = _proj(xp, sh1, sc1, w_tail, w_head, l, _tile(t, 1024))
        o_a = _attn_prompt(proj, bp, t)
        o_b, s_new = _hgrn_prompt(proj, lower_bounds[l], hgrn_norm_g[l], bp, t)
        merged = _merge(o_a, o_b, proj, w_up_a_b, w_up_b_b, l, tm_p)
        xp = _out_proj(merged, w_o_b, xp, g1, ln1_g, ln1_b, l, tm_p)
        xp = _moe_grouped(xp, sh2, sc2, g2, w_router_t, b_router, w1_b, w3_b, w2_b, ln2_g, ln2_b, l, tm_p)
        outs_p["k"].append(proj[:, OFF_K:OFF_K + KV_WIDTH].reshape(bp, t, N_KV_HEADS, HEAD_DIM))
        outs_p["v"].append(proj[:, OFF_V:OFF_V + KV_WIDTH].reshape(bp, t, N_KV_HEADS, HEAD_DIM))
        outs_p["ik"].append(proj[:, OFF_IKW:OFF_IKW + IDX_DIM].reshape(bp, t, IDX_DIM))
        outs_p["s"].append(s_new)

        sh1, sc1, g1, sh2, sc2, g2 = _mods(mod[l], slice(bp, bp + db), per_row=True)
        proj = _proj(xs, sh1, sc1, w_tail, w_head, l, db)
        o_a = _attn_decode(proj, page_table, cache_k, cache_v, cache_idx_k, l)
        o_b, s_new = _hgrn_decode(proj, state_hgrn, lower_bounds[l], hgrn_norm_g[l], l)
        merged = _merge(o_a, o_b, proj, w_up_a_b, w_up_b_b, l, db)
        xs = _out_proj(merged, w_o_b, xs, g1, ln1_g, ln1_b, l, db)
        h2, gates = _router(xs, sh2, sc2, w_router_t, b_router, db)
        xs = _moe(h2, gates.T, w1_b, w3_b, w2_b, xs, g2, ln2_g, ln2_b, l, db)
        outs_s["k"].append(proj[:, OFF_K:OFF_K + KV_WIDTH].reshape(db, 1, N_KV_HEADS, HEAD_DIM))
        outs_s["v"].append(proj[:, OFF_V:OFF_V + KV_WIDTH].reshape(db, 1, N_KV_HEADS, HEAD_DIM))
        outs_s["ik"].append(proj[:, OFF_IKW:OFF_IKW + IDX_DIM].reshape(db, 1, IDX_DIM))
        outs_s["s"].append(s_new)

    return (xp.reshape(bp, t, D_MODEL), xs.reshape(db, 1, D_MODEL),
            jnp.stack(outs_p["k"]), jnp.stack(outs_p["v"]), jnp.stack(outs_p["ik"]), jnp.stack(outs_p["s"]),
            jnp.stack(outs_s["k"]), jnp.stack(outs_s["v"]), jnp.stack(outs_s["ik"]), jnp.stack(outs_s["s"]))
```
